```python
import math
import jax, jax.numpy as jnp
from jax import lax
import numpy as np

D_MODEL = 1024
BATCH = 8
SEQ = 4096
DEPTH = 2
DEC_BATCH = 8
DEC_SEQ = 64
PAST_LEN = 4096

CHUNK = 64
QBLOCK = 128
HEAD_DIM = 64
N_HEADS_MIX = D_MODEL // HEAD_DIM
H_A = (3 * N_HEADS_MIX) // 8
H_B = (3 * N_HEADS_MIX) // 8
H_C = N_HEADS_MIX - H_A - H_B
Q_LORA = 192
KV_LORA = 128
NOPE = HEAD_DIM
ROPE = 32
QK_DIM = NOPE + ROPE
V_DIM = HEAD_DIM
ROPE_BASE = 10000.0
A_IN = Q_LORA + KV_LORA + ROPE
D_B = H_B * HEAD_DIM
W_LORA = 64
A_LORA = 64
G_LORA = 128
B_IN = 3 * D_B + W_LORA + A_LORA + G_LORA
DECAY_SCALE = math.exp(-0.5)
GN_EPS = 64e-5
D_C = H_C * HEAD_DIM
C_IN = 3 * D_C + H_C
D_IN = A_IN + B_IN + C_IN
D_MIX = H_A * V_DIM + D_B + D_C
N_GROUPS = 4
E_PER_GROUP = 4
N_EXPERTS = N_GROUPS * E_PER_GROUP
TOP_K = 2
D_FF_E = 256
NEG_INF = -1e30
RMS_EPS = 1e-6

kernel_name = 'hymba_mla_rwkv7_fox_hiermoe_stream_step'


def _rms(x, g):
    xf = x.astype(jnp.float32)
    y = xf * lax.rsqrt(jnp.mean(xf * xf, axis=-1, keepdims=True) + RMS_EPS)
    return (y * g.astype(jnp.float32)).astype(x.dtype)


def _rope_tail(x, pos):
    half = ROPE // 2
    inv = ROPE_BASE ** (-jnp.arange(half, dtype=jnp.float32) / half)
    ang = pos.astype(jnp.float32)[:, None] * inv[None, :]
    cos = jnp.cos(ang)[None, :, None, :]
    sin = jnp.sin(ang)[None, :, None, :]
    x1 = x[..., NOPE:NOPE + half].astype(jnp.float32)
    x2 = x[..., NOPE + half:].astype(jnp.float32)
    rot = jnp.concatenate([x1 * cos - x2 * sin, x1 * sin + x2 * cos], axis=-1)
    return jnp.concatenate([x[..., :NOPE], rot.astype(x.dtype)], axis=-1)


def _chunk_causal(qp, kp):
    return (kp // CHUNK) <= (qp // CHUNK)


def _frame_causal(qp, kp):
    return kp <= qp


def _attend(q, k, v, q_pos, k_pos, mask_fn, cq, ck):
    s = jnp.einsum('bqhd,bkhd->bhqk', q, k).astype(jnp.float32) * (q.shape[-1] ** -0.5)
    if cq is not None:
        s = s + (jnp.swapaxes(cq, 1, 2)[:, :, :, None] - jnp.swapaxes(ck, 1, 2)[:, :, None, :])
    s = jnp.where(mask_fn(q_pos[:, None], k_pos[None, :])[None, None], s, NEG_INF)
    p = jax.nn.softmax(s, axis=-1).astype(v.dtype)
    return jnp.einsum('bhqk,bkhd->bqhd', p, v)


def _attention(q, k, v, q_pos, k_pos, mask_fn, cq=None, ck=None):
    B, T, H, Dk = q.shape
    if T % QBLOCK != 0:
        return _attend(q, k, v, q_pos, k_pos, mask_fn, cq, ck)
    nb = T // QBLOCK
    qb = jnp.moveaxis(q.reshape(B, nb, QBLOCK, H, Dk), 1, 0)
    pb = q_pos.reshape(nb, QBLOCK)
    if cq is None:
        out = lax.map(lambda a: _attend(a[0], k, v, a[1], k_pos, mask_fn, None, None), (qb, pb))
    else:
        cb = jnp.moveaxis(cq.reshape(B, nb, QBLOCK, H), 1, 0)
        out = lax.map(lambda a: _attend(a[0], k, v, a[1], k_pos, mask_fn, a[2], ck), (qb, pb, cb))
    return jnp.moveaxis(out, 0, 1).reshape(B, T, H, v.shape[-1])


def _rwkv_scan(s0, r, w, k, v, kk, a):
    def step(S, inp):
        r_t, w_t, k_t, v_t, kk_t, a_t = inp
        sa = jnp.einsum('bhij,bhj->bhi', S, -kk_t)
        S = (S * w_t[:, :, None, :] + sa[..., None] * (kk_t * a_t)[:, :, None, :]
             + v_t[..., None] * k_t[:, :, None, :])
        return S, jnp.einsum('bhij,bhj->bhi', S, r_t)
    xs = tuple(jnp.moveaxis(t, 1, 0) for t in (r, w, k, v, kk, a))
    s_fin, ys = lax.scan(step, s0, xs)
    return jnp.moveaxis(ys, 0, 1), s_fin


def _moe(h, w_rg, b_rg, w_re, b_re, w_gate, w_up, w_down):
    B, T, D = h.shape
    f32 = jnp.float32
    ht = h.reshape(B * T, D)
    pg = jax.nn.softmax((ht @ w_rg).astype(f32) + b_rg, axis=-1)
    pg_top, g_idx = lax.top_k(pg, 1)
    le = ((ht @ w_re).astype(f32) + b_re).reshape(-1, N_GROUPS, E_PER_GROUP)
    le_g = jnp.sum(le * jax.nn.one_hot(g_idx[:, 0], N_GROUPS, dtype=f32)[:, :, None], axis=1)
    e_top, e_idx = lax.top_k(le_g, TOP_K)
    gate = jax.nn.softmax(e_top, axis=-1) * pg_top
    eid = g_idx * E_PER_GROUP + e_idx
    combine = jnp.sum(jax.nn.one_hot(eid, N_EXPERTS, dtype=f32) * gate[..., None], axis=1)
    out = jnp.zeros((B * T, D), f32)
    for e in range(N_EXPERTS):
        ye = (jax.nn.silu(ht @ w_gate[e]) * (ht @ w_up[e])) @ w_down[e]
        out = out + combine[:, e:e + 1] * ye.astype(f32)
    return out.astype(h.dtype).reshape(B, T, D)


def _layer(x, lp, hist):
    B, T, _ = x.shape
    f32 = jnp.float32
    past = 0 if hist is None else hist['ckv'].shape[1]
    q_pos = past + jnp.arange(T)
    k_pos = jnp.arange(past + T)
    h = _rms(x, lp['g_mix'])
    proj = h @ lp['w_in']
    pa = proj[..., :A_IN]
    pb = proj[..., A_IN:A_IN + B_IN]
    pc = proj[..., A_IN + B_IN:]

    c_kv = _rms(pa[..., Q_LORA:Q_LORA + KV_LORA], lp['mla_g_kva'])
    k_rope = pa[..., Q_LORA + KV_LORA:]
    q_a = (_rms(pa[..., :Q_LORA], lp['mla_g_qa']) @ lp['mla_w_uq']).reshape(B, T, H_A, QK_DIM)
    q_a = _rope_tail(_rms(q_a, lp['mla_g_qn']), q_pos)
    ckv_all = c_kv if hist is None else jnp.concatenate([hist['ckv'], c_kv], axis=1)
    kr_all = k_rope if hist is None else jnp.concatenate([hist['krope'], k_rope], axis=1)
    L = ckv_all.shape[1]
    kv_a = (ckv_all @ lp['mla_w_ukv']).reshape(B, L, H_A, NOPE + V_DIM)
    key_a = jnp.concatenate(
        [kv_a[..., :NOPE], jnp.broadcast_to(kr_all[:, :, None, :], (B, L, H_A, ROPE))], axis=-1)
    key_a = _rope_tail(_rms(key_a, lp['mla_g_kn']), k_pos)
    o_a = _attention(q_a, key_a, kv_a[..., NOPE:], q_pos, k_pos, _chunk_causal).reshape(B, T, H_A * V_DIM)

    prev = jnp.zeros((B, 1, B_IN), pb.dtype) if hist is None else hist['shift']
    xs = (pb + (jnp.concatenate([prev, pb[:, :-1]], axis=1) - pb) * lp['rw_mu']).astype(f32)
    o1, o2, o3 = D_B, 2 * D_B, 3 * D_B
    o4, o5 = o3 + W_LORA, o3 + W_LORA + A_LORA
    hs = (B, T, H_B, HEAD_DIM)
    r = xs[..., :o1].reshape(hs)
    kb = xs[..., o1:o2]
    vb = xs[..., o2:o3].reshape(hs)
    w = jnp.exp(-DECAY_SCALE * jax.nn.sigmoid(
        lp['rw_w0'] + jnp.tanh(xs[..., o3:o4]) @ lp['rw_w_up'])).reshape(hs)
    a = jax.nn.sigmoid(lp['rw_a0'] + xs[..., o4:o5] @ lp['rw_a_up'])
    g = jax.nn.sigmoid(xs[..., o5:]) @ lp['rw_g_up']
    kk = (kb * lp['rw_k_k']).reshape(hs)
    kk = kk * lax.rsqrt(jnp.sum(kk * kk, axis=-1, keepdims=True) + 1e-12)
    kb = (kb * (1.0 + (a - 1.0) * lp['rw_k_a'])).reshape(hs)
    a = a.reshape(hs)
    s0 = jnp.zeros((B, H_B, HEAD_DIM, HEAD_DIM), f32) if hist is None else hist['wkv'].astype(f32)
    y, s_fin = _rwkv_scan(s0, r, w, kb, vb, kk, a)
    mu = jnp.mean(y, axis=-1, keepdims=True)
    var = jnp.mean(jnp.square(y - mu), axis=-1, keepdims=True)
    y = ((y - mu) * lax.rsqrt(var + GN_EPS)).reshape(B, T, D_B) * lp['rw_ln_w'] + lp['rw_ln_b']
    y = y + (jnp.sum(r * kb * lp['rw_r_k'], axis=-1, keepdims=True) * vb).reshape(B, T, D_B)
    o_b = (y * g).astype(x.dtype)

    hc = (B, T, H_C, HEAD_DIM)
    q_c = _rms(pc[..., :D_C].reshape(hc), lp['fox_g_qn'])
    k_c = _rms(pc[..., D_C:2 * D_C].reshape(hc), lp['fox_g_kn'])
    v_c = pc[..., 2 * D_C:3 * D_C].reshape(hc)
    logf = jax.nn.log_sigmoid(pc[..., 3 * D_C:].astype(f32) + lp['fox_b_f'])
    if hist is None:
        kc_all, vc_all, lf_all = k_c, v_c, logf
    else:
        kc_all = jnp.concatenate([hist['fk'], k_c], axis=1)
        vc_all = jnp.concatenate([hist['fv'], v_c], axis=1)
        lf_all = jnp.concatenate([hist['flogf'].astype(f32), logf], axis=1)
    c_all = jnp.cumsum(lf_all, axis=1)
    o_c = _attention(q_c, kc_all, vc_all, q_pos, k_pos, _frame_causal,
                     c_all[:, past:], c_all).reshape(B, T, D_C)

    mix = jnp.concatenate([o_a, o_b, o_c.astype(x.dtype)], axis=-1)
    x = x + mix @ lp['w_out']
    x = x + _moe(_rms(x, lp['g_ffn']), lp['moe_w_rg'], lp['moe_b_rg'], lp['moe_w_re'],
                 lp['moe_b_re'], lp['moe_w_gate'], lp['moe_w_up'], lp['moe_w_down'])
    dt = x.dtype
    new = (c_kv, k_rope, k_c, v_c, logf.astype(dt), s_fin.astype(dt), pb[:, -1:])
    return x, new


def setup_inputs(seed: int = 0) -> dict:
    key = jax.random.key(seed)
    keys = jax.random.split(key, 48)
    cnt = [0]
    f32 = jnp.float32

    def nxt():
        k = keys[cnt[0]]
        cnt[0] += 1
        return k

    def nrm(shape, scale=1.0, shift=0.0):
        return shift + scale * jax.random.normal(nxt(), shape, f32)

    L = DEPTH
    return {
        'x_prompt': nrm((BATCH, SEQ, D_MODEL)),
        'x_sample': nrm((DEC_BATCH, DEC_SEQ, D_MODEL)),
        'cache_mla_latent': nrm((L, DEC_BATCH, PAST_LEN, KV_LORA)),
        'cache_mla_krope': nrm((L, DEC_BATCH, PAST_LEN, ROPE)),
        'cache_fox_k': nrm((L, DEC_BATCH, PAST_LEN, H_C, HEAD_DIM)),
        'cache_fox_v': nrm((L, DEC_BATCH, PAST_LEN, H_C, HEAD_DIM)),
        'cache_fox_logf': jax.nn.log_sigmoid(nrm((L, DEC_BATCH, PAST_LEN, H_C), 0.5, 3.0)),
        'state_rwkv_wkv': nrm((L, DEC_BATCH, H_B, HEAD_DIM, HEAD_DIM), 0.3),
        'state_rwkv_shift': nrm((L, DEC_BATCH, 1, B_IN)),
        'g_mix': nrm((L, D_MODEL), 0.02, 1.0),
        'w_in': nrm((L, D_MODEL, D_IN), D_MODEL ** -0.5),
        'mla_g_qa': nrm((L, Q_LORA), 0.02, 1.0),
        'mla_w_uq': nrm((L, Q_LORA, H_A * QK_DIM), Q_LORA ** -0.5),
        'mla_g_kva': nrm((L, KV_LORA), 0.02, 1.0),
        'mla_w_ukv': nrm((L, KV_LORA, H_A * (NOPE + V_DIM)), KV_LORA ** -0.5),
        'mla_g_qn': nrm((L, QK_DIM), 0.02, 1.0),
        'mla_g_kn': nrm((L, QK_DIM), 0.02, 1.0),
        'rw_mu': jax.random.uniform(nxt(), (L, B_IN), f32),
        'rw_w0': nrm((L, D_B), 0.5),
        'rw_w_up': nrm((L, W_LORA, D_B), W_LORA ** -0.5),
        'rw_a0': nrm((L, D_B), 0.5),
        'rw_a_up': nrm((L, A_LORA, D_B), A_LORA ** -0.5),
        'rw_g_up': nrm((L, G_LORA, D_B), G_LORA ** -0.5),
        'rw_k_k': nrm((L, D_B), 0.1, 1.0),
        'rw_k_a': nrm((L, D_B), 0.1, 1.0),
        'rw_r_k': nrm((L, H_B, HEAD_DIM), 0.1),
        'rw_ln_w': nrm((L, D_B), 0.02, 1.0),
        'rw_ln_b': nrm((L, D_B), 0.02),
        'fox_g_qn': nrm((L, HEAD_DIM), 0.02, 1.0),
        'fox_g_kn': nrm((L, HEAD_DIM), 0.02, 1.0),
        'fox_b_f': nrm((L, H_C), 0.5, 3.0),
        'w_out': nrm((L, D_MIX, D_MODEL), D_MIX ** -0.5),
        'g_ffn': nrm((L, D_MODEL), 0.02, 1.0),
        'moe_w_rg': nrm((L, D_MODEL, N_GROUPS), D_MODEL ** -0.5),
        'moe_b_rg': nrm((L, N_GROUPS), 0.01),
        'moe_w_re': nrm((L, D_MODEL, N_EXPERTS), D_MODEL ** -0.5),
        'moe_b_re': nrm((L, N_EXPERTS), 0.01),
        'moe_w_gate': nrm((L, N_EXPERTS, D_MODEL, D_FF_E), D_MODEL ** -0.5),
        'moe_w_up': nrm((L, N_EXPERTS, D_MODEL, D_FF_E), D_MODEL ** -0.5),
        'moe_w_down': nrm((L, N_EXPERTS, D_FF_E, D_MODEL), D_FF_E ** -0.5),
    }


def reference(x_prompt, x_sample, cache_mla_latent, cache_mla_krope, cache_fox_k, cache_fox_v,
              cache_fox_logf, state_rwkv_wkv, state_rwkv_shift, g_mix, w_in, mla_g_qa, mla_w_uq,
              mla_g_kva, mla_w_ukv, mla_g_qn, mla_g_kn, rw_mu, rw_w0, rw_w_up, rw_a0, rw_a_up,
              rw_g_up, rw_k_k, rw_k_a, rw_r_k, rw_ln_w, rw_ln_b, fox_g_qn, fox_g_kn, fox_b_f,
              w_out, g_ffn, moe_w_rg, moe_b_rg, moe_w_re, moe_b_re, moe_w_gate, moe_w_up,
              moe_w_down):
    params = dict(g_mix=g_mix, w_in=w_in, mla_g_qa=mla_g_qa, mla_w_uq=mla_w_uq,
                  mla_g_kva=mla_g_kva, mla_w_ukv=mla_w_ukv, mla_g_qn=mla_g_qn, mla_g_kn=mla_g_kn,
                  rw_mu=rw_mu, rw_w0=rw_w0, rw_w_up=rw_w_up, rw_a0=rw_a0, rw_a_up=rw_a_up,
                  rw_g_up=rw_g_up, rw_k_k=rw_k_k, rw_k_a=rw_k_a, rw_r_k=rw_r_k, rw_ln_w=rw_ln_w,
                  rw_ln_b=rw_ln_b, fox_g_qn=fox_g_qn, fox_g_kn=fox_g_kn, fox_b_f=fox_b_f,
                  w_out=w_out, g_ffn=g_ffn, moe_w_rg=moe_w_rg, moe_b_rg=moe_b_rg,
                  moe_w_re=moe_w_re, moe_b_re=moe_b_re, moe_w_gate=moe_w_gate,
                  moe_w_up=moe_w_up, moe_w_down=moe_w_down)
    yp, ys = x_prompt, x_sample
    p_new, s_new = [], []
    for l in range(DEPTH):
        lp = {name: arr[l] for name, arr in params.items()}
        hist = dict(ckv=cache_mla_latent[l], krope=cache_mla_krope[l], fk=cache_fox_k[l],
                    fv=cache_fox_v[l], flogf=cache_fox_logf[l], wkv=state_rwkv_wkv[l],
                    shift=state_rwkv_shift[l])
        yp, np_l = _layer(yp, lp, None)
        ys, ns_l = _layer(ys, lp, hist)
        p_new.append(np_l)
        s_new.append(ns_l)
    p_lat, p_kr, p_fk, p_fv, p_lf, p_wkv, p_sh = (jnp.stack(t) for t in zip(*p_new))
    s_lat, s_kr, s_fk, s_fv, s_lf, s_wkv, s_sh = (jnp.stack(t) for t in zip(*s_new))
    return (yp, ys, p_lat, p_kr, p_fk, p_fv, p_lf, p_wkv, p_sh,
            s_lat, s_kr, s_fk, s_fv, s_lf, s_wkv, s_sh)
```

```python
import functools
import math

import jax
import jax.numpy as jnp
from jax import lax
from jax.experimental import pallas as pl
from jax.experimental.pallas import tpu as pltpu

F32 = jnp.float32
BF16 = jnp.bfloat16

D_MODEL = 1024
HEAD_DIM = 64
H_A, H_B, H_C = 6, 6, 4
Q_LORA, KV_LORA, NOPE, ROPE = 192, 128, 64, 32
QK_DIM = NOPE + ROPE
ROPE_BASE = 10000.0
A_IN = Q_LORA + KV_LORA + ROPE
D_B = H_B * HEAD_DIM
W_LORA, A_LORA, G_LORA = 64, 64, 128
B_IN = 3 * D_B + W_LORA + A_LORA + G_LORA
DECAY_SCALE = math.exp(-0.5)
GN_EPS = 64e-5
D_C = H_C * HEAD_DIM
C_IN = 3 * D_C + H_C
N_GROUPS, E_PER_GROUP = 4, 4
N_EXPERTS = N_GROUPS * E_PER_GROUP
D_FF_E = 256
NEG_INF = -1e30
RMS_EPS = 1e-6
CHUNK = 64

LANES = 128
VMEM_LIMIT = 56 * 1024 * 1024

_C_QL = 0
_C_KV = 256
_C_KR = 384
_C_PB = 512
_C_QC = _C_PB + B_IN
_C_KC = _C_QC + D_C
_C_VC = _C_KC + D_C
_C_F = _C_VC + D_C
_C_END = _C_F + LANES


def _cparams(sem):
    return pltpu.CompilerParams(dimension_semantics=sem, vmem_limit_bytes=VMEM_LIMIT)


def _dot(a, b):
    return jnp.dot(a, b, preferred_element_type=F32)


def _dot_nt(a, b):
    return lax.dot_general(a, b, (((1,), (1,)), ((), ())), preferred_element_type=F32)


def _dot_tn(a, b):
    return lax.dot_general(a, b, (((0,), (0,)), ((), ())), preferred_element_type=F32)


def _split2(x):
    hi = x.astype(BF16)
    lo = (x - hi.astype(F32)).astype(BF16)
    return hi, lo


def _split3(x):
    hi = x.astype(BF16)
    r = x - hi.astype(F32)
    mid = r.astype(BF16)
    lo = (r - mid.astype(F32)).astype(BF16)
    return hi, mid, lo


def _dot3(a, b, dot=_dot):
    ah, al = _split2(a)
    bh, bl = _split2(b)
    return dot(ah, bh) + (dot(ah, bl) + dot(al, bh))


def _dot_sel(a, sel):
    ah, al = _split2(a)
    return _dot(ah, sel) + _dot(al, sel)


def _iota(shape, dim):
    return lax.broadcasted_iota(jnp.int32, shape, dim)


def _rope(x, c, s_left, s_right):
    return x * c + pltpu.roll(x, 112, 1) * s_left + pltpu.roll(x, 16, 1) * s_right


def _sigmoid(x):
    return 1.0 / (1.0 + jnp.exp(-x))


def _log_sigmoid(x):
    return jnp.minimum(x, 0.0) - jnp.log(1.0 + jnp.exp(-jnp.abs(x)))


def _inproj_kernel(x_ref, gmix_ref, w_ref, gqa_ref, wuq_ref, gkva_ref, gqn_ref, gfq_ref, gfk_ref, bf_ref,
                   ones_ref, bd_ref, c_ref, sl_ref, sr_ref,
                   ckv_ref, kr128_ref, kr_ref, qm_ref, pb_ref, qc_ref, kc_ref, vc_ref, lf_ref):
    x = x_ref[...]
    h = x * lax.rsqrt(jnp.mean(x * x, axis=-1, keepdims=True) + RMS_EPS) * gmix_ref[...]
    h = h.astype(BF16)

    ql = _dot(h, w_ref[:, _C_QL:_C_QL + 256])
    ql = ql * lax.rsqrt(jnp.sum(ql * ql, axis=-1, keepdims=True) * (1.0 / Q_LORA) + RMS_EPS) * gqa_ref[...]
    qh = _dot(ql.astype(BF16), wuq_ref[...])
    c, s_l, s_r = c_ref[...], sl_ref[...], sr_ref[...]
    ones = ones_ref[...]
    for hh in range(H_A):
        qv = qh[:, hh * LANES:(hh + 1) * LANES]
        ss = _dot_sel(qv * qv, ones)
        qn = qv * lax.rsqrt(ss * (1.0 / QK_DIM) + RMS_EPS) * gqn_ref[...]
        qm_ref[:, hh * LANES:(hh + 1) * LANES] = (_rope(qn, c, s_l, s_r) * (QK_DIM ** -0.5)).astype(BF16)

    kv = _dot(h, w_ref[:, _C_KV:_C_KV + KV_LORA])
    ckv_ref[...] = kv * lax.rsqrt(jnp.mean(kv * kv, axis=-1, keepdims=True) + RMS_EPS) * gkva_ref[...]
    kr = _dot(h, w_ref[:, _C_KR:_C_KR + LANES])
    kr128_ref[...] = kr
    kr_ref[...] = kr[:, NOPE:NOPE + ROPE]

    pb_ref[...] = _dot(h, w_ref[:, _C_PB:_C_PB + B_IN])

    bd = bd_ref[...]
    qc = _dot(h, w_ref[:, _C_QC:_C_QC + D_C])
    qc = qc * lax.rsqrt(_dot_sel(qc * qc, bd) * (1.0 / HEAD_DIM) + RMS_EPS) * gfq_ref[...]
    qc_ref[...] = (qc * (HEAD_DIM ** -0.5)).astype(BF16)
    kc = _dot(h, w_ref[:, _C_KC:_C_KC + D_C])
    kc_ref[...] = kc * lax.rsqrt(_dot_sel(kc * kc, bd) * (1.0 / HEAD_DIM) + RMS_EPS) * gfk_ref[...]
    vc_ref[...] = _dot(h, w_ref[:, _C_VC:_C_VC + D_C])
    f = _dot(h, w_ref[:, _C_F:_C_F + LANES]) + bf_ref[...]
    lf_ref[...] = _log_sigmoid(f)[:, :H_C]


def _inproj(x2d, lw, tabs, tab_rows, tm):
    n = x2d.shape[0]
    nt = tab_rows // tm
    row = lambda i: (i, 0)
    fixed = lambda i: (0, 0)
    tab = lambda i: (i % nt, 0)
    full = lambda a: pl.BlockSpec(a.shape, fixed)
    params = [lw['g_mix'], lw['w_in'], lw['g_qa'], lw['w_uq'], lw['g_kva'], lw['g_qn'], lw['g_fq'], lw['g_fk'],
              lw['b_f'], lw['ones128'], lw['bd256']]
    outs = [(KV_LORA, F32), (LANES, F32), (ROPE, F32), (H_A * LANES, BF16), (B_IN, F32),
            (D_C, BF16), (D_C, F32), (D_C, F32), (H_C, F32)]
    return pl.pallas_call(
        _inproj_kernel,
        grid=(n // tm,),
        in_specs=[pl.BlockSpec((tm, D_MODEL), row)] + [full(a) for a in params]
                 + [pl.BlockSpec((tm, LANES), tab)] * 3,
        out_specs=[pl.BlockSpec((tm, w), row) for w, _ in outs],
        out_shape=[jax.ShapeDtypeStruct((n, w), dt) for w, dt in outs],
        compiler_params=_cparams(("parallel",)),
        name="inproj",
    )(x2d, *params, *tabs)


def _kvprep_kernel(ckv_ref, kr_ref, wuk_ref, wuv_ref, gkn_ref, gkr_ref, ones_ref, c_ref, sl_ref, sr_ref,
                   k_ref, v_ref):
    cb = ckv_ref[...].astype(BF16)
    kn = _dot(cb, wuk_ref[...])
    v_ref[...] = _dot(cb, wuv_ref[...]).astype(BF16)
    ones = ones_ref[...]
    kr = kr_ref[...]
    ssr = _dot_sel(kr * kr, ones)
    krg = _rope(kr * gkr_ref[...], c_ref[...], sl_ref[...], sr_ref[...])
    for hh in range(H_A):
        knh = kn[:, hh * LANES:(hh + 1) * LANES]
        ssn = _dot_sel(knh * knh, ones)
        r = lax.rsqrt((ssn + ssr) * (1.0 / QK_DIM) + RMS_EPS)
        k_ref[:, hh * LANES:(hh + 1) * LANES] = ((knh * gkn_ref[...] + krg) * r).astype(BF16)


def _kvprep(ckv2d, kr128_2d, lw, tabs, tab_rows, tm):
    n = ckv2d.shape[0]
    nt = tab_rows // tm
    row = lambda i: (i, 0)
    fixed = lambda i: (0, 0)
    tab = lambda i: (i % nt, 0)
    full = lambda a: pl.BlockSpec(a.shape, fixed)
    params = [lw['w_uk'], lw['w_uv'], lw['g_kn'], lw['g_kr'], lw['ones128']]
    return pl.pallas_call(
        _kvprep_kernel,
        grid=(n // tm,),
        in_specs=[pl.BlockSpec((tm, KV_LORA), row), pl.BlockSpec((tm, LANES), row)] + [full(a) for a in params]
                 + [pl.BlockSpec((tm, LANES), tab)] * 3,
        out_specs=[pl.BlockSpec((tm, H_A * LANES), row), pl.BlockSpec((tm, H_A * HEAD_DIM), row)],
        out_shape=[jax.ShapeDtypeStruct((n, H_A * LANES), BF16), jax.ShapeDtypeStruct((n, H_A * HEAD_DIM), BF16)],
        compiler_params=_cparams(("parallel",)),
        name="kvprep",
    )(ckv2d, kr128_2d, *params, *tabs)


def _attn_kernel(*refs, tq, tk, tkp, nq, n_past, shared, frame_causal, has_bias):
    it = iter(refs)
    qa_ref = next(it)
    qb_ref = qa_ref if shared else next(it)
    ka_ref = next(it)
    kb_ref = ka_ref if shared else next(it)
    v_ref = next(it)
    if n_past:
        kpa_ref = next(it)
        kpb_ref = kpa_ref if shared else next(it)
        vp_ref = next(it)
    if has_bias:
        cq_ref = next(it)
        ck_ref = next(it)
    o_ref = next(it)
    m_scr, l_scr, acc_scr = next(it), next(it), next(it)

    qi = 0 if nq == 1 else pl.program_id(2)
    lane_q = _iota((tq, LANES), 1)
    low = lane_q < HEAD_DIM
    if shared:
        q2 = qa_ref[0]
        qs = (jnp.where(low, q2, jnp.zeros_like(q2)), jnp.where(low, jnp.zeros_like(q2), q2))
    else:
        qs = (qa_ref[0], qb_ref[0])

    m_scr[...] = jnp.full(m_scr.shape, -jnp.inf, F32)
    l_scr[...] = jnp.zeros(l_scr.shape, F32)
    acc_scr[...] = jnp.zeros(acc_scr.shape, F32)

    def step(k_tiles, v_tile, key_start, width, mask):
        vt = v_tile.astype(BF16)
        for j in range(2):
            s = _dot_nt(qs[j], k_tiles[j].astype(BF16))
            if has_bias:
                s = s + (cq_ref[0, 0, :, j:j + 1] - ck_ref[0, 0, j:j + 1, pl.ds(key_start, width)])
            if mask is not None:
                s = jnp.where(mask, s, NEG_INF)
            m_prev = m_scr[j]
            m_next = jnp.maximum(m_prev, jnp.max(s, axis=1, keepdims=True))
            p = jnp.exp(s - m_next[:, :1])
            alpha = jnp.exp(m_prev - m_next)
            l_scr[j] = alpha * l_scr[j] + jnp.sum(p, axis=1, keepdims=True)
            acc_scr[j] = acc_scr[j] * alpha + _dot(p.astype(BF16), vt)
            m_scr[j] = m_next

    if n_past:
        def past_body(t, carry):
            st = pl.multiple_of(t * tkp, tkp)
            step((kpa_ref[0, pl.ds(st, tkp), :], kpb_ref[0, pl.ds(st, tkp), :]), vp_ref[0, pl.ds(st, tkp), :],
                 st, tkp, None)
            return carry
        lax.fori_loop(0, n_past // tkp, past_body, 0)

    def new_body(t, carry):
        st = pl.multiple_of(t * tk, tk)
        step((ka_ref[0, pl.ds(st, tk), :], kb_ref[0, pl.ds(st, tk), :]), v_ref[0, pl.ds(st, tk), :],
             n_past + st, tk, None)
        return carry
    if nq > 1:
        lax.fori_loop(0, qi * (tq // tk), new_body, 0)

    r = _iota((tq, tk), 0)
    cidx = _iota((tq, tk), 1)
    for d in range(tq // tk):
        cc = cidx + d * tk
        mask = (cc <= r) if frame_causal else (jnp.right_shift(cc, 6) <= jnp.right_shift(r, 6))
        st = qi * tq + d * tk
        st = st if nq == 1 else pl.multiple_of(st, tk)
        step((ka_ref[0, pl.ds(st, tk), :], kb_ref[0, pl.ds(st, tk), :]), v_ref[0, pl.ds(st, tk), :],
             n_past + st, tk, mask)

    oa = acc_scr[0] / l_scr[0]
    ob = acc_scr[1] / l_scr[1]
    o_ref[0] = jnp.where(low, oa, ob).astype(o_ref.dtype)


def _attention(q, k_new, v_new, k_past, v_past, bias, *, n_pairs, shared, frame_causal, tq, tk, tkp):
    b, t, _ = q.shape
    n_past = 0 if k_past is None else k_past.shape[1]
    nq = t // tq
    qblk = lambda off: pl.BlockSpec((1, tq, LANES), lambda bi, p, i, off=off: (bi, i, 2 * p + off))
    kblk = lambda rows, off: pl.BlockSpec((1, rows, LANES), lambda bi, p, i, off=off: (bi, 0, 2 * p + off))
    pair_q = pl.BlockSpec((1, tq, LANES), lambda bi, p, i: (bi, i, p))
    pair_k = lambda rows: pl.BlockSpec((1, rows, LANES), lambda bi, p, i: (bi, 0, p))
    args, specs = [], []
    if shared:
        args += [q, k_new]
        specs += [pair_q, pair_k(t)]
    else:
        args += [q, q, k_new, k_new]
        specs += [qblk(0), qblk(1), kblk(t, 0), kblk(t, 1)]
    args.append(v_new)
    specs.append(pair_k(t))
    if n_past:
        if shared:
            args.append(k_past)
            specs.append(pair_k(n_past))
        else:
            args += [k_past, k_past]
            specs += [kblk(n_past, 0), kblk(n_past, 1)]
        args.append(v_past)
        specs.append(pair_k(n_past))
    if bias is not None:
        cq, ck = bias
        args += [cq, ck]
        specs += [pl.BlockSpec((1, 1, tq, 2), lambda bi, p, i: (bi, p, i, 0)),
                  pl.BlockSpec((1, 1, 2, ck.shape[-1]), lambda bi, p, i: (bi, p, 0, 0))]
    kern = functools.partial(_attn_kernel, tq=tq, tk=tk, tkp=tkp, nq=nq, n_past=n_past, shared=shared,
                             frame_causal=frame_causal, has_bias=bias is not None)
    return pl.pallas_call(
        kern,
        grid=(b, n_pairs, nq),
        in_specs=specs,
        out_specs=pair_q,
        out_shape=jax.ShapeDtypeStruct((b, t, n_pairs * LANES), BF16),
        scratch_shapes=[pltpu.VMEM((2, tq, LANES), F32)] * 3,
        compiler_params=_cparams(("parallel", "parallel", "arbitrary")),
        name="attn_fox" if shared else "attn_mla",
    )(*args)


def _cumsum_kernel(x_ref, tri_ref, o_ref):
    rows, n = x_ref.shape
    tri = tri_ref[...]
    carry = jnp.zeros((rows, 1), F32)
    for t in range(n // LANES):
        xt = x_ref[:, t * LANES:(t + 1) * LANES]
        hi, mid, lo = _split3(xt)
        o_ref[:, t * LANES:(t + 1) * LANES] = _dot(hi, tri) + _dot(mid, tri) + _dot(lo, tri) + carry
        carry = carry + jnp.sum(xt, axis=1, keepdims=True)


def _cumsum_lanes(x):
    rows, n = x.shape
    tri = (jnp.arange(LANES)[:, None] <= jnp.arange(LANES)[None, :]).astype(BF16)
    return pl.pallas_call(
        _cumsum_kernel,
        out_shape=jax.ShapeDtypeStruct((rows, n), F32),
        compiler_params=pltpu.CompilerParams(vmem_limit_bytes=VMEM_LIMIT),
        name="cumsum",
    )(x, tri)


def _rwkv_prep_kernel(pb_ref, prev_ref, mu_ref, w0_ref, ww_ref, a0_ref, wa_ref, wg_ref, kk_ref, ka_ref, bd_ref,
                      r_ref, kkn_ref, k_ref, b_ref, v_ref, lw_ref, g_ref, carry_ref):
    t = pl.program_id(1)
    tm = pb_ref.shape[1]

    @pl.when(t == 0)
    def _():
        carry_ref[0:1, :] = prev_ref[0]

    pbv = pb_ref[0]
    rolled = pltpu.roll(pbv, 1, 0)
    shifted = jnp.where(_iota(pbv.shape, 0) == 0, carry_ref[0:1, :], rolled)
    carry_ref[0:1, :] = pbv[tm - 1:tm, :]
    xs = pbv + (shifted - pbv) * mu_ref[...]

    r = xs[:, 0:D_B]
    kb = xs[:, D_B:2 * D_B]
    o3 = 3 * D_B
    wa_in = xs[:, o3:o3 + LANES]
    wa_in = jnp.where(_iota(wa_in.shape, 1) < W_LORA, jnp.tanh(wa_in), wa_in).astype(BF16)
    lw_ref[0] = -DECAY_SCALE * _sigmoid(w0_ref[...] + _dot(wa_in, ww_ref[...]))
    a = _sigmoid(a0_ref[...] + _dot(wa_in, wa_ref[...]))
    g_ref[0] = _dot(_sigmoid(xs[:, o3 + LANES:o3 + 2 * LANES]).astype(BF16), wg_ref[...])
    kk = kb * kk_ref[...]
    kk = kk * lax.rsqrt(_dot_sel(kk * kk, bd_ref[...]) + 1e-12)
    r_ref[0] = r
    kkn_ref[0] = kk
    k_ref[0] = kb * (1.0 + (a - 1.0) * ka_ref[...])
    b_ref[0] = kk * a
    v_ref[0] = xs[:, 2 * D_B:3 * D_B]


def _rwkv_prep(pb3, prev, lw, tm):
    b, t, _ = pb3.shape
    fixed = lambda bi, i: (0, 0)
    full = lambda a: pl.BlockSpec(a.shape, fixed)
    params = [lw['rw_mu'], lw['rw_w0'], lw['rw_ww'], lw['rw_a0'], lw['rw_wa'], lw['rw_wg'], lw['rw_k_k'],
              lw['rw_k_a'], lw['bd384']]
    out_spec = pl.BlockSpec((1, tm, D_B), lambda bi, i: (bi, i, 0))
    return pl.pallas_call(
        _rwkv_prep_kernel,
        grid=(b, t // tm),
        in_specs=[pl.BlockSpec((1, tm, B_IN), lambda bi, i: (bi, i, 0)),
                  pl.BlockSpec((1, 1, B_IN), lambda bi, i: (bi, 0, 0))] + [full(a) for a in params],
        out_specs=[out_spec] * 7,
        out_shape=[jax.ShapeDtypeStruct((b, t, D_B), F32)] * 7,
        scratch_shapes=[pltpu.VMEM((8, B_IN), F32)],
        compiler_params=_cparams(("parallel", "arbitrary")),
        name="rwkv_prep",
    )(pb3, prev, *params)


def _rwkv_chunk_kernel(r_ref, kk_ref, k_ref, b_ref, v_ref, lw_ref, rt_ref, yl_ref, g_ref, h_ref):
    s = r_ref.shape[1]
    nc = s // CHUNK
    r, kk, k, b, v, lw = r_ref[0], kk_ref[0], k_ref[0], b_ref[0], v_ref[0], lw_ref[0]

    row = _iota((s, LANES), 0)
    rin = jnp.bitwise_and(row, CHUNK - 1)
    cl = lw
    for sh in (1, 2, 4, 8, 16, 32):
        cl = cl + jnp.where(rin >= sh, pltpu.roll(cl, sh, 0), 0.0)
    tot = jnp.concatenate(
        [jnp.broadcast_to(cl[c * CHUNK + CHUNK - 1:(c + 1) * CHUNK, :], (CHUNK, LANES)) for c in range(nc)], axis=0)
    e_cl = jnp.exp(cl)
    e_ncl = jnp.exp(-cl)
    e_rem = jnp.exp(tot - cl)
    kkt = kk * jnp.exp(cl - lw)
    rt = r * e_cl
    kh = k * e_ncl
    bh = b * e_ncl
    kw = k * e_rem
    bw = b * e_rem

    ti = _iota((s, s), 0)
    si = _iota((s, s), 1)
    same = jnp.right_shift(ti, 6) == jnp.right_shift(si, 6)
    strict = jnp.logical_and(same, si < ti)
    incl = jnp.logical_and(same, si <= ti)
    eye = (ti == si).astype(F32)
    low = _iota((s, LANES), 1) < HEAD_DIM

    uloc2 = kkt2 = rt2 = yl2 = None
    for x in range(2):
        mx = low if x == 0 else jnp.logical_not(low)
        kkt_x = jnp.where(mx, kkt, 0.0)
        rt_x = jnp.where(mx, rt, 0.0)
        akk = jnp.where(strict, _dot3(kkt_x, kh, _dot_nt), 0.0)
        abk = jnp.where(strict, _dot3(kkt_x, bh, _dot_nt), 0.0)
        ark = jnp.where(incl, _dot3(rt_x, kh, _dot_nt), 0.0)
        arb = jnp.where(incl, _dot3(rt_x, bh, _dot_nt), 0.0)
        pw = -abk
        tinv = eye + pw
        for _ in range(5):
            pw = _dot3(pw, pw)
            tinv = tinv + _dot3(tinv, pw)
        uloc = _dot3(tinv, _dot3(akk, v))
        kkt_s = _dot3(tinv, kkt_x)
        rt_s = rt_x - _dot3(arb, kkt_s)
        yl = _dot3(ark, v) - _dot3(arb, uloc)
        if x == 0:
            uloc2, kkt2, rt2, yl2 = uloc, kkt_s, rt_s, yl
        else:
            uloc2 = jnp.where(low, uloc2, uloc)
            kkt2 = jnp.where(low, kkt2, kkt_s)
            rt2 = jnp.where(low, rt2, rt_s)
            yl2 = jnp.where(low, yl2, yl)
    rt_ref[0] = rt2
    yl_ref[0] = yl2

    ji = _iota((LANES, LANES), 0)
    jj = _iota((LANES, LANES), 1)
    blockdiag = (ji < HEAD_DIM) == (jj < HEAD_DIM)
    for c in range(nc):
        inc = jnp.right_shift(row, 6) == c
        bw_c = jnp.where(inc, bw, 0.0)
        kw_c = jnp.where(inc, kw, 0.0)
        e_tot = jnp.exp(tot[c * CHUNK:c * CHUNK + 1, :])
        gm = jnp.where(ji == jj, e_tot, 0.0) - _dot3(bw_c, kkt2, _dot_tn)
        hm = _dot3(kw_c, v, _dot_tn) - _dot3(bw_c, uloc2, _dot_tn)
        g_ref[0, 0, c] = jnp.where(blockdiag, gm, 0.0)
        h_ref[0, 0, c] = jnp.where(blockdiag, hm, 0.0)


def _rwkv_chunk(r, kk, k, b, v, lw, s):
    bsz, t, _ = r.shape
    npair = H_B // 2
    nc = t // CHUNK
    tok = pl.BlockSpec((1, s, LANES), lambda bi, p, i: (bi, i, p))
    mat = pl.BlockSpec((1, 1, s // CHUNK, LANES, LANES), lambda bi, p, i: (bi, p, i, 0, 0))
    return pl.pallas_call(
        _rwkv_chunk_kernel,
        grid=(bsz, npair, t // s),
        in_specs=[tok] * 6,
        out_specs=[tok, tok, mat, mat],
        out_shape=[jax.ShapeDtypeStruct((bsz, t, D_B), F32)] * 2
                  + [jax.ShapeDtypeStruct((bsz, npair, nc, LANES, LANES), F32)] * 2,
        compiler_params=_cparams(("parallel", "parallel", "parallel")),
        name="rwkv_chunk",
    )(r, kk, k, b, v, lw)


def _rwkv_scan_kernel(s0_ref, g_ref, h_ref, rt_ref, yl_ref, y_ref, sfin_ref):
    nc = g_ref.shape[2]

    def body(c, st):
        off = pl.multiple_of(c * CHUNK, CHUNK)
        y_ref[0, pl.ds(off, CHUNK), :] = _dot3(rt_ref[0, pl.ds(off, CHUNK), :], st) + yl_ref[0, pl.ds(off, CHUNK), :]
        return _dot3(g_ref[0, 0, c], st) + h_ref[0, 0, c]

    sfin_ref[0, 0] = lax.fori_loop(0, nc, body, s0_ref[0, 0])


def _rwkv_scan(s0, g, h, rt, yl):
    bsz, t, _ = rt.shape
    npair = H_B // 2
    nc = t // CHUNK
    st_spec = pl.BlockSpec((1, 1, LANES, LANES), lambda bi, p: (bi, p, 0, 0))
    mat = pl.BlockSpec((1, 1, nc, LANES, LANES), lambda bi, p: (bi, p, 0, 0, 0))
    tok = pl.BlockSpec((1, t, LANES), lambda bi, p: (bi, 0, p))
    return pl.pallas_call(
        _rwkv_scan_kernel,
        grid=(bsz, npair),
        in_specs=[st_spec, mat, mat, tok, tok],
        out_specs=[tok, st_spec],
        out_shape=[jax.ShapeDtypeStruct((bsz, t, D_B), F32), jax.ShapeDtypeStruct((bsz, npair, LANES, LANES), F32)],
        compiler_params=_cparams(("parallel", "parallel")),
        name="rwkv_scan",
    )(s0, g, h, rt, yl)


def _rwkv_post_kernel(y_ref, r_ref, k_ref, v_ref, g_ref, lnw_ref, lnb_ref, rk_ref, bd_ref, o_ref):
    bd = bd_ref[...]
    y = y_ref[...]
    mu = _dot_sel(y, bd) * (1.0 / HEAD_DIM)
    d = y - mu
    var = _dot_sel(d * d, bd) * (1.0 / HEAD_DIM)
    yn = d * lax.rsqrt(var + GN_EPS) * lnw_ref[...] + lnb_ref[...]
    v = v_ref[...]
    bonus = _dot_sel(r_ref[...] * k_ref[...] * rk_ref[...], bd) * v
    o_ref[...] = ((yn + bonus) * g_ref[...]).astype(BF16)


def _rwkv_post(y, r, k, v, g, lw, tm):
    n = y.shape[0]
    row = pl.BlockSpec((tm, D_B), lambda i: (i, 0))
    full = lambda a: pl.BlockSpec(a.shape, lambda i: (0, 0))
    params = [lw['rw_ln_w'], lw['rw_ln_b'], lw['rw_r_k'], lw['bd384']]
    return pl.pallas_call(
        _rwkv_post_kernel,
        grid=(n // tm,),
        in_specs=[row] * 5 + [full(a) for a in params],
        out_specs=row,
        out_shape=jax.ShapeDtypeStruct((n, D_B), BF16),
        compiler_params=_cparams(("parallel",)),
        name="rwkv_post",
    )(y, r, k, v, g, *params)


_L_EXP = 16


def _moe_kernel(x_ref, oa_ref, ob_ref, oc_ref, wo_ref, gffn_ref, wr_ref, br_ref, wgu_ref, wd_ref,
                o_ref, acc_ref, h_ref, comb_ref):
    e = pl.program_id(1)

    @pl.when(e == 0)
    def _():
        da = H_A * HEAD_DIM
        x1 = (x_ref[...] + _dot(oa_ref[...], wo_ref[0:da, :]) + _dot(ob_ref[...], wo_ref[da:da + D_B, :])
              + _dot(oc_ref[...], wo_ref[da + D_B:, :]))
        acc_ref[...] = x1
        hf = x1 * lax.rsqrt(jnp.mean(x1 * x1, axis=-1, keepdims=True) + RMS_EPS) * gffn_ref[...]
        h_ref[...] = hf.astype(BF16)

        logit = _dot3(hf, wr_ref[...]) + br_ref[...]
        lane_i = _iota(logit.shape, 1)
        lane = lane_i.astype(F32)
        big = jnp.float32(3e38)
        is_g = lane_i < N_GROUPS
        gl = jnp.where(is_g, logit, -big)
        gmax = jnp.max(gl, axis=1, keepdims=True)
        pg_top = 1.0 / jnp.sum(jnp.where(is_g, jnp.exp(gl - gmax), 0.0), axis=1, keepdims=True)
        g_idx = jnp.min(jnp.where(jnp.logical_and(is_g, gl == gmax), lane, big), axis=1, keepdims=True)
        el = lane_i - _L_EXP
        in_e = jnp.logical_and(el >= 0, el < N_EXPERTS)
        sel = jnp.logical_and(in_e, jnp.right_shift(el, 2).astype(F32) == g_idx)
        l1 = jnp.where(sel, logit, -big)
        v1 = jnp.max(l1, axis=1, keepdims=True)
        i1 = jnp.min(jnp.where(jnp.logical_and(sel, l1 == v1), lane, big), axis=1, keepdims=True)
        sel2 = jnp.logical_and(sel, lane != i1)
        l2 = jnp.where(sel2, logit, -big)
        v2 = jnp.max(l2, axis=1, keepdims=True)
        i2 = jnp.min(jnp.where(jnp.logical_and(sel2, l2 == v2), lane, big), axis=1, keepdims=True)
        e2 = jnp.exp(v2 - v1)
        den = 1.0 / (1.0 + e2)
        comb_ref[...] = (jnp.where(lane == i1, den * pg_top, 0.0) + jnp.where(lane == i2, e2 * den * pg_top, 0.0))

    gu = _dot(h_ref[...], wgu_ref[0])
    gate = gu[:, :D_FF_E]
    act = gate * _sigmoid(gate) * gu[:, D_FF_E:]
    y = _dot(act.astype(BF16), wd_ref[0])
    comb = comb_ref[...]
    ce = jnp.sum(jnp.where(_iota(comb.shape, 1) == e + _L_EXP, comb, 0.0), axis=1, keepdims=True)
    acc_ref[...] += ce * y

    @pl.when(e == N_EXPERTS - 1)
    def _():
        o_ref[...] = acc_ref[...]


def _outproj_moe(x2d, oa, ob, oc, lw, tm):
    n = x2d.shape[0]
    row = lambda w: pl.BlockSpec((tm, w), lambda i, e: (i, 0))
    full = lambda a: pl.BlockSpec(a.shape, lambda i, e: (0, 0))
    return pl.pallas_call(
        _moe_kernel,
        grid=(n // tm, N_EXPERTS),
        in_specs=[row(D_MODEL), row(H_A * HEAD_DIM), row(D_B), row(D_C), full(lw['w_out']), full(lw['g_ffn']),
                  full(lw['w_r']), full(lw['b_r']),
                  pl.BlockSpec((1, D_MODEL, 2 * D_FF_E), lambda i, e: (e, 0, 0)),
                  pl.BlockSpec((1, D_FF_E, D_MODEL), lambda i, e: (e, 0, 0))],
        out_specs=row(D_MODEL),
        out_shape=jax.ShapeDtypeStruct((n, D_MODEL), F32),
        scratch_shapes=[pltpu.VMEM((tm, D_MODEL), F32), pltpu.VMEM((tm, D_MODEL), BF16), pltpu.VMEM((tm, LANES), F32)],
        compiler_params=_cparams(("parallel", "arbitrary")),
        name="outproj_moe",
    )(x2d, oa, ob, oc, lw['w_out'], lw['g_ffn'], lw['w_r'], lw['b_r'], lw['w_gu'], lw['w_d'])


def _place(pieces, width):
    rows = pieces[0][1].shape[0]
    out = jnp.zeros((rows, width), F32)
    for off, a in pieces:
        out = out.at[:, off:off + a.shape[1]].set(a.astype(F32))
    return out


def _row(v, width=None, off=0):
    v = v.reshape(1, -1).astype(F32)
    return v if width is None else _place([(off, v)], width)


def _block_diag_ones(n, blk):
    i = jnp.arange(n) // blk
    return (i[:, None] == i[None, :]).astype(BF16)


def _layer_weights(p, l):
    g = lambda name: p[name][l]
    w_in = g('w_in')
    o_b, o_c = A_IN, A_IN + B_IN
    w_in_p = _place([(_C_QL, w_in[:, :Q_LORA]), (_C_KV, w_in[:, Q_LORA:Q_LORA + KV_LORA]),
                     (_C_KR + NOPE, w_in[:, Q_LORA + KV_LORA:A_IN]), (_C_PB, w_in[:, o_b:o_c]),
                     (_C_QC, w_in[:, o_c:o_c + 3 * D_C]), (_C_F, w_in[:, o_c + 3 * D_C:])], _C_END).astype(BF16)
    w_uq = g('mla_w_uq').reshape(Q_LORA, H_A, QK_DIM)
    w_uq = jnp.pad(w_uq, ((0, 256 - Q_LORA), (0, 0), (0, LANES - QK_DIM))).reshape(256, H_A * LANES).astype(BF16)
    w_ukv = g('mla_w_ukv').reshape(KV_LORA, H_A, NOPE + HEAD_DIM)
    w_uk = jnp.pad(w_ukv[:, :, :NOPE], ((0, 0), (0, 0), (0, LANES - NOPE))).reshape(KV_LORA, H_A * LANES).astype(BF16)
    w_uv = w_ukv[:, :, NOPE:].reshape(KV_LORA, H_A * HEAD_DIM).astype(BF16)
    zeros_w = jnp.zeros((W_LORA, D_B), F32)
    w_r = _place([(0, g('moe_w_rg')), (_L_EXP, g('moe_w_re'))], LANES)
    b_r = _place([(0, g('moe_b_rg').reshape(1, -1)), (_L_EXP, g('moe_b_re').reshape(1, -1))], LANES)
    return dict(
        g_mix=_row(g('g_mix')), w_in=w_in_p, g_qa=_row(g('mla_g_qa'), 256), w_uq=w_uq, g_kva=_row(g('mla_g_kva')),
        g_qn=_row(g('mla_g_qn'), LANES), g_fq=_row(jnp.tile(g('fox_g_qn'), H_C)),
        g_fk=_row(jnp.tile(g('fox_g_kn'), H_C)), b_f=_row(g('fox_b_f'), LANES),
        ones128=jnp.ones((LANES, LANES), BF16), bd256=_block_diag_ones(D_C, HEAD_DIM),
        bd384=_block_diag_ones(D_B, HEAD_DIM),
        w_uk=w_uk, w_uv=w_uv, g_kn=_row(g('mla_g_kn')[:NOPE], LANES), g_kr=_row(g('mla_g_kn')[NOPE:], LANES, NOPE),
        rw_mu=_row(g('rw_mu')), rw_w0=_row(g('rw_w0')), rw_a0=_row(g('rw_a0')),
        rw_ww=jnp.concatenate([g('rw_w_up'), zeros_w], axis=0).astype(BF16),
        rw_wa=jnp.concatenate([zeros_w, g('rw_a_up')], axis=0).astype(BF16),
        rw_wg=g('rw_g_up').astype(BF16), rw_k_k=_row(g('rw_k_k')), rw_k_a=_row(g('rw_k_a')),
        rw_r_k=_row(g('rw_r_k')), rw_ln_w=_row(g('rw_ln_w')), rw_ln_b=_row(g('rw_ln_b')),
        w_out=g('w_out').astype(BF16), g_ffn=_row(g('g_ffn')), w_r=w_r, b_r=b_r,
        w_gu=jnp.concatenate([g('moe_w_gate'), g('moe_w_up')], axis=-1).astype(BF16),
        w_d=g('moe_w_down').astype(BF16),
    )


def _rope_tables(pos):
    half = ROPE // 2
    inv = ROPE_BASE ** (-jnp.arange(half, dtype=F32) / half)
    ang = pos.astype(F32)[:, None] * inv[None, :]
    cos, sin = jnp.cos(ang), jnp.sin(ang)
    t = pos.shape[0]
    z = lambda w: jnp.zeros((t, w), F32)
    c = jnp.concatenate([jnp.ones((t, NOPE), F32), cos, cos, z(LANES - QK_DIM)], axis=1)
    s_left = jnp.concatenate([z(NOPE), -sin, z(half), z(LANES - QK_DIM)], axis=1)
    s_right = jnp.concatenate([z(NOPE), z(half), sin, z(LANES - QK_DIM)], axis=1)
    return c, s_left, s_right


def _pick(n, prefs):
    for t in prefs:
        if n % t == 0:
            return t
    return n


def _layer(x, lw, hist):
    b, t, _ = x.shape
    n = b * t
    past = 0 if hist is None else hist['ckv'].shape[1]
    x2d = x.reshape(n, D_MODEL)

    tm = _pick(t, (512, 256, 128, 64))
    q_tabs = _rope_tables(past + jnp.arange(t))
    if t < 256:
        tm = _pick(n, (512, 256, 128, 64))
        q_tabs = tuple(jnp.tile(a, (b, 1)) for a in q_tabs)
    tab_rows = q_tabs[0].shape[0]

    ckv, kr128, kr, qm, pb, qc, kc, vc, lf = _inproj(x2d, lw, q_tabs, tab_rows, tm)

    k_new, v_new = _kvprep(ckv, kr128, lw, q_tabs, tab_rows, tm)
    tq = _pick(t, (256, 128, 64))
    shp = lambda a: a.reshape(b, t, a.shape[-1])
    if hist is None:
        kp = vp = None
    else:
        ckv_past = hist['ckv'].reshape(b * past, KV_LORA)
        kr_past = jnp.pad(hist['krope'].reshape(b * past, ROPE), ((0, 0), (NOPE, LANES - QK_DIM)))
        tmp = _pick(past, (512, 256, 128, 64))
        kp, vp = _kvprep(ckv_past, kr_past, lw, _rope_tables(jnp.arange(past)), past, tmp)
        kp, vp = kp.reshape(b, past, -1), vp.reshape(b, past, -1)
    tkp = _pick(past, (512, 256, 128, 64)) if past else 0
    o_a = _attention(shp(qm), shp(k_new), shp(v_new), kp, vp, None, n_pairs=H_A // 2, shared=False,
                     frame_causal=False, tq=tq, tk=tq, tkp=tkp)

    lf3 = lf.reshape(b, t, H_C)
    lf_all = lf3 if hist is None else jnp.concatenate([hist['flogf'].astype(F32), lf3], axis=1)
    ltot = past + t
    lpad = -(-ltot // LANES) * LANES
    lf_t = jnp.pad(jnp.swapaxes(lf_all, 1, 2), ((0, 0), (0, 0), (0, lpad - ltot))).reshape(b * H_C, lpad)
    c_all = _cumsum_lanes(lf_t).reshape(b, H_C // 2, 2, lpad)
    cq = jnp.swapaxes(c_all[..., past:past + t], 2, 3)
    if hist is None:
        kcp = vcp = None
    else:
        kcp = hist['fk'].reshape(b, past, D_C)
        vcp = hist['fv'].reshape(b, past, D_C)
    o_c = _attention(shp(qc), shp(kc), shp(vc), kcp, vcp, (cq, c_all), n_pairs=H_C // 2, shared=True,
                     frame_causal=True, tq=tq, tk=tq, tkp=tkp)

    prev = jnp.zeros((b, 1, B_IN), F32) if hist is None else hist['shift'].astype(F32)
    pb3 = pb.reshape(b, t, B_IN)
    r, kk, k, bb, v, lwd, g = _rwkv_prep(pb3, prev, lw, _pick(t, (256, 128, 64)))
    rt, yl, gm, hm = _rwkv_chunk(r, kk, k, bb, v, lwd, _pick(t, (128, 64)))
    npair = H_B // 2
    if hist is None:
        s0 = jnp.zeros((b, npair, LANES, LANES), F32)
    else:
        st = jnp.swapaxes(hist['wkv'].astype(F32), -1, -2).reshape(b, npair, 2, HEAD_DIM, HEAD_DIM)
        s0 = jnp.zeros((b, npair, 2, HEAD_DIM, 2, HEAD_DIM), F32)
        s0 = s0.at[:, :, 0, :, 0, :].set(st[:, :, 0]).at[:, :, 1, :, 1, :].set(st[:, :, 1])
        s0 = s0.reshape(b, npair, LANES, LANES)
    y, sfin = _rwkv_scan(s0, gm, hm, rt, yl)
    flat = lambda a: a.reshape(n, D_B)
    o_b = _rwkv_post(flat(y), flat(r), flat(k), flat(v), flat(g), lw, tm)
    sf = sfin.reshape(b, npair, 2, HEAD_DIM, 2, HEAD_DIM)
    s_fin = jnp.stack([sf[:, :, 0, :, 0, :], sf[:, :, 1, :, 1, :]], axis=2).reshape(b, H_B, HEAD_DIM, HEAD_DIM)
    s_fin = jnp.swapaxes(s_fin, -1, -2)

    tmm = _pick(n, (512, 256, 128, 64))
    x_out = _outproj_moe(x2d, o_a.reshape(n, -1), flat(o_b), o_c.reshape(n, -1), lw, tmm).reshape(b, t, D_MODEL)

    new = (ckv.reshape(b, t, KV_LORA), kr.reshape(b, t, ROPE), kc.reshape(b, t, H_C, HEAD_DIM),
           vc.reshape(b, t, H_C, HEAD_DIM), lf3, s_fin, pb3[:, -1:])
    return x_out, new


def kernel(x_prompt, x_sample, cache_mla_latent, cache_mla_krope, cache_fox_k, cache_fox_v, cache_fox_logf,
           state_rwkv_wkv, state_rwkv_shift, g_mix, w_in, mla_g_qa, mla_w_uq, mla_g_kva, mla_w_ukv, mla_g_qn,
           mla_g_kn, rw_mu, rw_w0, rw_w_up, rw_a0, rw_a_up, rw_g_up, rw_k_k, rw_k_a, rw_r_k, rw_ln_w, rw_ln_b,
           fox_g_qn, fox_g_kn, fox_b_f, w_out, g_ffn, moe_w_rg, moe_b_rg, moe_w_re, moe_b_re, moe_w_gate,
           moe_w_up, moe_w_down):
    params = dict(g_mix=g_mix, w_in=w_in, mla_g_qa=mla_g_qa, mla_w_uq=mla_w_uq, mla_g_kva=mla_g_kva,
                  mla_w_ukv=mla_w_ukv, mla_g_qn=mla_g_qn, mla_g_kn=mla_g_kn, rw_mu=rw_mu, rw_w0=rw_w0,
                  rw_w_up=rw_w_up, rw_a0=rw_a0, rw_a_up=rw_a_up, rw_g_up=rw_g_up, rw_k_k=rw_k_k, rw_k_a=rw_k_a,
                  rw_r_k=rw_r_k, rw_ln_w=rw_ln_w, rw_ln_b=rw_ln_b, fox_g_qn=fox_g_qn, fox_g_kn=fox_g_kn,
                  fox_b_f=fox_b_f, w_out=w_out, g_ffn=g_ffn, moe_w_rg=moe_w_rg, moe_b_rg=moe_b_rg,
                  moe_w_re=moe_w_re, moe_b_re=moe_b_re, moe_w_gate=moe_w_gate, moe_w_up=moe_w_up,
                  moe_w_down=moe_w_down)
    depth = g_mix.shape[0]
    yp, ys = x_prompt, x_sample
    p_new, s_new = [], []
    for l in range(depth):
        lw = _layer_weights(params, l)
        hist = dict(ckv=cache_mla_latent[l], krope=cache_mla_krope[l], fk=cache_fox_k[l], fv=cache_fox_v[l],
                    flogf=cache_fox_logf[l], wkv=state_rwkv_wkv[l], shift=state_rwkv_shift[l])
        yp, np_l = _layer(yp, lw, None)
        ys, ns_l = _layer(ys, lw, hist)
        p_new.append(np_l)
        s_new.append(ns_l)
    p_out = tuple(jnp.stack(t) for t in zip(*p_new))
    s_out = tuple(jnp.stack(t) for t in zip(*s_new))
    return (yp, ys) + p_out + s_out
```

```python
import functools
import math

import jax
import jax.numpy as jnp
from jax import lax
from jax.experimental import pallas as pl
from jax.experimental.pallas import tpu as pltpu

F32 = jnp.float32
BF16 = jnp.bfloat16

D_MODEL = 1024
HEAD_DIM = 64
H_A, H_B, H_C = 6, 6, 4
Q_LORA, KV_LORA, NOPE, ROPE = 192, 128, 64, 32
QK_DIM = NOPE + ROPE
ROPE_BASE = 10000.0
A_IN = Q_LORA + KV_LORA + ROPE
D_B = H_B * HEAD_DIM
W_LORA, A_LORA, G_LORA = 64, 64, 128
B_IN = 3 * D_B + W_LORA + A_LORA + G_LORA
DECAY_SCALE = math.exp(-0.5)
GN_EPS = 64e-5
D_C = H_C * HEAD_DIM
C_IN = 3 * D_C + H_C
N_GROUPS, E_PER_GROUP = 4, 4
N_EXPERTS = N_GROUPS * E_PER_GROUP
D_FF_E = 256
NEG_INF = -1e30
RMS_EPS = 1e-6
CHUNK = 64
LOG2E = math.log2(math.e)

LANES = 128
VMEM_LIMIT = 56 * 1024 * 1024

_C_QL = 0
_C_KV = 256
_C_KR = 384
_C_PB = 512
_C_QC = _C_PB + B_IN
_C_KC = _C_QC + D_C
_C_VC = _C_KC + D_C
_C_F = _C_VC + D_C
_C_END = _C_F + LANES


def _cparams(sem):
    return pltpu.CompilerParams(dimension_semantics=sem, vmem_limit_bytes=VMEM_LIMIT)


def _dot(a, b):
    return jnp.dot(a, b, preferred_element_type=F32)


def _dot_nt(a, b):
    return lax.dot_general(a, b, (((1,), (1,)), ((), ())), preferred_element_type=F32)


def _dot_tn(a, b):
    return lax.dot_general(a, b, (((0,), (0,)), ((), ())), preferred_element_type=F32)


def _split2(x):
    hi = x.astype(BF16)
    lo = (x - hi.astype(F32)).astype(BF16)
    return hi, lo


def _split3(x):
    hi = x.astype(BF16)
    r = x - hi.astype(F32)
    mid = r.astype(BF16)
    lo = (r - mid.astype(F32)).astype(BF16)
    return hi, mid, lo


def _dot3(a, b, dot=_dot):
    ah, al = _split2(a)
    bh, bl = _split2(b)
    return dot(ah, bh) + (dot(ah, bl) + dot(al, bh))


def _mm(a, b, passes, dot=_dot):
    if passes == 1:
        return dot(a.astype(BF16), b.astype(BF16))
    return _dot3(a, b, dot)


def _dot_sel(a, sel):
    ah, al = _split2(a)
    return _dot(ah, sel) + _dot(al, sel)


def _iota(shape, dim):
    return lax.broadcasted_iota(jnp.int32, shape, dim)


def _rope(x, c, s_left, s_right):
    return x * c + pltpu.roll(x, 112, 1) * s_left + pltpu.roll(x, 16, 1) * s_right


def _sigmoid(x):
    return 1.0 / (1.0 + jnp.exp(-x))


def _log_sigmoid(x):
    return jnp.minimum(x, 0.0) - jnp.log(1.0 + jnp.exp(-jnp.abs(x)))


def _transpose_bf16(x, eye):
    return _dot_nt(eye, x.astype(BF16)).astype(BF16)


def _inproj_kernel(x_ref, gmix_ref, w_ref, gqa_ref, wuq_ref, gkva_ref, gqn_ref, gfq_ref, gfk_ref, bf_ref,
                   ones_ref, bd_ref, eye_ref, c_ref, sl_ref, sr_ref,
                   ckv_ref, kr128_ref, kr_ref, qt_ref, pb_ref, qct_ref, kc_ref, vc_ref, vct_ref, lf_ref):
    x = x_ref[...]
    h = x * lax.rsqrt(jnp.mean(x * x, axis=-1, keepdims=True) + RMS_EPS) * gmix_ref[...]
    h = h.astype(BF16)
    eye = eye_ref[...]

    ql = _dot(h, w_ref[:, _C_QL:_C_QL + 256])
    ql = ql * lax.rsqrt(jnp.sum(ql * ql, axis=-1, keepdims=True) * (1.0 / Q_LORA) + RMS_EPS) * gqa_ref[...]
    qh = _dot(ql.astype(BF16), wuq_ref[...])
    c, s_l, s_r = c_ref[...], sl_ref[...], sr_ref[...]
    ones = ones_ref[...]
    for hh in range(H_A):
        qv = qh[:, hh * LANES:(hh + 1) * LANES]
        ss = _dot_sel(qv * qv, ones)
        qn = qv * lax.rsqrt(ss * (1.0 / QK_DIM) + RMS_EPS) * gqn_ref[...]
        qt_ref[0, hh * LANES:(hh + 1) * LANES, :] = _transpose_bf16(
            _rope(qn, c, s_l, s_r) * (LOG2E * QK_DIM ** -0.5), eye)

    kv = _dot(h, w_ref[:, _C_KV:_C_KV + KV_LORA])
    ckv_ref[...] = kv * lax.rsqrt(jnp.mean(kv * kv, axis=-1, keepdims=True) + RMS_EPS) * gkva_ref[...]
    kr = _dot(h, w_ref[:, _C_KR:_C_KR + LANES])
    kr128_ref[...] = kr
    kr_ref[...] = kr[:, NOPE:NOPE + ROPE]

    pb_ref[...] = _dot(h, w_ref[:, _C_PB:_C_PB + B_IN])

    bd = bd_ref[...]
    qc = _dot(h, w_ref[:, _C_QC:_C_QC + D_C])
    qc = qc * lax.rsqrt(_dot_sel(qc * qc, bd) * (1.0 / HEAD_DIM) + RMS_EPS) * gfq_ref[...]
    qc = qc * (LOG2E * HEAD_DIM ** -0.5)
    kc = _dot(h, w_ref[:, _C_KC:_C_KC + D_C])
    kc_ref[...] = kc * lax.rsqrt(_dot_sel(kc * kc, bd) * (1.0 / HEAD_DIM) + RMS_EPS) * gfk_ref[...]
    vc = _dot(h, w_ref[:, _C_VC:_C_VC + D_C])
    vc_ref[...] = vc
    for pp in range(D_C // LANES):
        qct_ref[0, pp * LANES:(pp + 1) * LANES, :] = _transpose_bf16(qc[:, pp * LANES:(pp + 1) * LANES], eye)
        vct_ref[0, pp * LANES:(pp + 1) * LANES, :] = _transpose_bf16(vc[:, pp * LANES:(pp + 1) * LANES], eye)
    f = _dot(h, w_ref[:, _C_F:_C_F + LANES]) + bf_ref[...]
    lf_ref[...] = _log_sigmoid(f)[:, :H_C]


def _inproj(x2d, lw, tabs, bsz, t, tm):
    n = x2d.shape[0]
    nt = t // tm
    row = lambda i: (i, 0)
    fixed = lambda i: (0, 0)
    tab = lambda i: (i % nt, 0)
    colmajor = lambda i: (i // nt, 0, i % nt)
    full = lambda a: pl.BlockSpec(a.shape, fixed)
    params = [lw['g_mix'], lw['w_in'], lw['g_qa'], lw['w_uq'], lw['g_kva'], lw['g_qn'], lw['g_fq'], lw['g_fk'],
              lw['b_f'], lw['ones128'], lw['bd256'], lw['eye128']]
    rowout = lambda w, dt: (pl.BlockSpec((tm, w), row), jax.ShapeDtypeStruct((n, w), dt))
    colout = lambda w: (pl.BlockSpec((1, w, tm), colmajor), jax.ShapeDtypeStruct((bsz, w, t), BF16))
    outs = [rowout(KV_LORA, F32), rowout(LANES, F32), rowout(ROPE, F32), colout(H_A * LANES), rowout(B_IN, F32),
            colout(D_C), rowout(D_C, F32), rowout(D_C, F32), colout(D_C), rowout(H_C, F32)]
    return pl.pallas_call(
        _inproj_kernel,
        grid=(n // tm,),
        in_specs=[pl.BlockSpec((tm, D_MODEL), row)] + [full(a) for a in params]
                 + [pl.BlockSpec((tm, LANES), tab)] * 3,
        out_specs=[o[0] for o in outs],
        out_shape=[o[1] for o in outs],
        compiler_params=_cparams(("parallel",)),
        name="inproj",
    )(x2d, *params, *tabs)


def _kvprep_kernel(ckv_ref, kr_ref, wuk_ref, wuv_ref, gkn_ref, gkr_ref, ones_ref, eye_ref, c_ref, sl_ref, sr_ref,
                   k_ref, vt_ref):
    cb = ckv_ref[...].astype(BF16)
    kn = _dot(cb, wuk_ref[...])
    v = _dot(cb, wuv_ref[...])
    eye = eye_ref[...]
    for pp in range(H_A // 2):
        vt_ref[0, pp * LANES:(pp + 1) * LANES, :] = _transpose_bf16(v[:, pp * LANES:(pp + 1) * LANES], eye)
    ones = ones_ref[...]
    kr = kr_ref[...]
    ssr = _dot_sel(kr * kr, ones)
    krg = _rope(kr * gkr_ref[...], c_ref[...], sl_ref[...], sr_ref[...])
    for hh in range(H_A):
        knh = kn[:, hh * LANES:(hh + 1) * LANES]
        ssn = _dot_sel(knh * knh, ones)
        r = lax.rsqrt((ssn + ssr) * (1.0 / QK_DIM) + RMS_EPS)
        k_ref[:, hh * LANES:(hh + 1) * LANES] = ((knh * gkn_ref[...] + krg) * r).astype(BF16)


def _kvprep(ckv2d, kr128_2d, lw, tabs, bsz, t, tm):
    n = ckv2d.shape[0]
    nt = t // tm
    row = lambda i: (i, 0)
    fixed = lambda i: (0, 0)
    tab = lambda i: (i % nt, 0)
    full = lambda a: pl.BlockSpec(a.shape, fixed)
    params = [lw['w_uk'], lw['w_uv'], lw['g_kn'], lw['g_kr'], lw['ones128'], lw['eye128']]
    return pl.pallas_call(
        _kvprep_kernel,
        grid=(n // tm,),
        in_specs=[pl.BlockSpec((tm, KV_LORA), row), pl.BlockSpec((tm, LANES), row)] + [full(a) for a in params]
                 + [pl.BlockSpec((tm, LANES), tab)] * 3,
        out_specs=[pl.BlockSpec((tm, H_A * LANES), row),
                   pl.BlockSpec((1, H_A * HEAD_DIM, tm), lambda i: (i // nt, 0, i % nt))],
        out_shape=[jax.ShapeDtypeStruct((n, H_A * LANES), BF16),
                   jax.ShapeDtypeStruct((bsz, H_A * HEAD_DIM, t), BF16)],
        compiler_params=_cparams(("parallel",)),
        name="kvprep",
    )(ckv2d, kr128_2d, *params, *tabs)


def _transpose_kernel(x_ref, eye_ref, o_ref):
    eye = eye_ref[...]
    for pp in range(x_ref.shape[1] // LANES):
        o_ref[0, pp * LANES:(pp + 1) * LANES, :] = _transpose_bf16(x_ref[:, pp * LANES:(pp + 1) * LANES], eye)


def _transpose_cast(x2d, lw, bsz, t, tm):
    n, w = x2d.shape
    nt = t // tm
    return pl.pallas_call(
        _transpose_kernel,
        grid=(n // tm,),
        in_specs=[pl.BlockSpec((tm, w), lambda i: (i, 0)), pl.BlockSpec((LANES, LANES), lambda i: (0, 0))],
        out_specs=pl.BlockSpec((1, w, tm), lambda i: (i // nt, 0, i % nt)),
        out_shape=jax.ShapeDtypeStruct((bsz, w, t), BF16),
        compiler_params=_cparams(("parallel",)),
        name="transpose_cast",
    )(x2d, lw['eye128'])


def _attn_kernel(*refs, tq, tk, tkp, nq, n_past, shared, frame_causal, has_bias):
    it = iter(refs)
    qa_ref = next(it)
    qb_ref = qa_ref if shared else next(it)
    ka_ref = next(it)
    kb_ref = ka_ref if shared else next(it)
    vt_ref = next(it)
    if n_past:
        kpa_ref = next(it)
        kpb_ref = kpa_ref if shared else next(it)
        vpt_ref = next(it)
    if has_bias:
        cq_ref = next(it)
        ck_ref = next(it)
    eye_ref = next(it)
    o_ref = next(it)
    m_scr, l_scr, acc_scr = next(it), next(it), next(it)

    qi = 0 if nq == 1 else pl.program_id(2)
    low = _iota((LANES, tq), 0) < HEAD_DIM
    if shared:
        q2 = qa_ref[0]
        qs = (jnp.where(low, q2, jnp.zeros_like(q2)), jnp.where(low, jnp.zeros_like(q2), q2))
    else:
        qs = (qa_ref[0], qb_ref[0])

    m_scr[...] = jnp.full(m_scr.shape, -jnp.inf, F32)
    l_scr[...] = jnp.zeros(l_scr.shape, F32)
    acc_scr[...] = jnp.zeros(acc_scr.shape, F32)

    def step(k_tiles, vt, key_start, width, mask):
        for j in range(2):
            s = _dot(k_tiles[j].astype(BF16), qs[j])
            if has_bias:
                s = s + (cq_ref[0, 0, j:j + 1, :] - ck_ref[0, 0, pl.ds(key_start, width), j:j + 1])
            if mask is not None:
                s = jnp.where(mask, s, NEG_INF)
            m_prev = m_scr[j]
            m_next = jnp.maximum(m_prev, jnp.max(s, axis=0, keepdims=True))
            p = jnp.exp2(s - m_next)
            alpha = jnp.exp2(m_prev - m_next)
            l_scr[j] = alpha * l_scr[j] + jnp.sum(p, axis=0, keepdims=True)
            acc_scr[j] = acc_scr[j] * alpha + _dot(vt, p.astype(BF16))
            m_scr[j] = m_next

    if n_past:
        def past_body(t, carry):
            st = pl.multiple_of(t * tkp, tkp)
            step((kpa_ref[0, pl.ds(st, tkp), :], kpb_ref[0, pl.ds(st, tkp), :]), vpt_ref[0, :, pl.ds(st, tkp)],
                 st, tkp, None)
            return carry
        lax.fori_loop(0, n_past // tkp, past_body, 0)

    def new_body(t, carry):
        st = pl.multiple_of(t * tk, tk)
        step((ka_ref[0, pl.ds(st, tk), :], kb_ref[0, pl.ds(st, tk), :]), vt_ref[0, :, pl.ds(st, tk)],
             n_past + st, tk, None)
        return carry
    if nq > 1:
        lax.fori_loop(0, qi * (tq // tk), new_body, 0)

    kidx = _iota((tk, tq), 0)
    qidx = _iota((tk, tq), 1)
    for d in range(tq // tk):
        kk = kidx + d * tk
        mask = (kk <= qidx) if frame_causal else (jnp.right_shift(kk, 6) <= jnp.right_shift(qidx, 6))
        st = qi * tq + d * tk
        st = st if nq == 1 else pl.multiple_of(st, tk)
        step((ka_ref[0, pl.ds(st, tk), :], kb_ref[0, pl.ds(st, tk), :]), vt_ref[0, :, pl.ds(st, tk)],
             n_past + st, tk, mask)

    ot = jnp.where(low, acc_scr[0] / l_scr[0], acc_scr[1] / l_scr[1])
    o_ref[0] = _dot_tn(ot.astype(BF16), eye_ref[...]).astype(o_ref.dtype)


def _attention(qt, k_new, vt_new, k_past, vt_past, bias, eye, *, n_pairs, shared, frame_causal, tq, tk, tkp):
    b, t = k_new.shape[0], k_new.shape[1]
    n_past = 0 if k_past is None else k_past.shape[1]
    nq = t // tq
    qblk = lambda off: pl.BlockSpec((1, LANES, tq), lambda bi, p, i, off=off: (bi, 2 * p + off, i))
    kblk = lambda rows, off: pl.BlockSpec((1, rows, LANES), lambda bi, p, i, off=off: (bi, 0, 2 * p + off))
    pair_q = pl.BlockSpec((1, LANES, tq), lambda bi, p, i: (bi, p, i))
    pair_k = lambda rows: pl.BlockSpec((1, rows, LANES), lambda bi, p, i: (bi, 0, p))
    pair_vt = lambda cols: pl.BlockSpec((1, LANES, cols), lambda bi, p, i: (bi, p, 0))
    args, specs = [], []
    if shared:
        args += [qt, k_new]
        specs += [pair_q, pair_k(t)]
    else:
        args += [qt, qt, k_new, k_new]
        specs += [qblk(0), qblk(1), kblk(t, 0), kblk(t, 1)]
    args.append(vt_new)
    specs.append(pair_vt(t))
    if n_past:
        if shared:
            args.append(k_past)
            specs.append(pair_k(n_past))
        else:
            args += [k_past, k_past]
            specs += [kblk(n_past, 0), kblk(n_past, 1)]
        args.append(vt_past)
        specs.append(pair_vt(n_past))
    if bias is not None:
        cq, ck = bias
        args += [cq, ck]
        specs += [pl.BlockSpec((1, 1, 2, tq), lambda bi, p, i: (bi, p, 0, i)),
                  pl.BlockSpec((1, 1, ck.shape[2], 2), lambda bi, p, i: (bi, p, 0, 0))]
    args.append(eye)
    specs.append(pl.BlockSpec((LANES, LANES), lambda bi, p, i: (0, 0)))
    kern = functools.partial(_attn_kernel, tq=tq, tk=tk, tkp=tkp, nq=nq, n_past=n_past, shared=shared,
                             frame_causal=frame_causal, has_bias=bias is not None)
    return pl.pallas_call(
        kern,
        grid=(b, n_pairs, nq),
        in_specs=specs,
        out_specs=pl.BlockSpec((1, tq, LANES), lambda bi, p, i: (bi, i, p)),
        out_shape=jax.ShapeDtypeStruct((b, t, n_pairs * LANES), BF16),
        scratch_shapes=[pltpu.VMEM((2, 1, tq), F32), pltpu.VMEM((2, 1, tq), F32), pltpu.VMEM((2, LANES, tq), F32)],
        compiler_params=_cparams(("parallel", "parallel", "arbitrary")),
        name="attn_fox" if shared else "attn_mla",
    )(*args)


def _cumsum_kernel(x_ref, tri_ref, o_ref):
    rows, n = x_ref.shape
    tri = tri_ref[...]
    carry = jnp.zeros((rows, 1), F32)
    for t in range(n // LANES):
        xt = x_ref[:, t * LANES:(t + 1) * LANES]
        hi, mid, lo = _split3(xt)
        o_ref[:, t * LANES:(t + 1) * LANES] = (_dot(hi, tri) + _dot(mid, tri) + _dot(lo, tri) + carry) * LOG2E
        carry = carry + jnp.sum(xt, axis=1, keepdims=True)


def _cumsum_lanes(x):
    rows, n = x.shape
    tri = (jnp.arange(LANES)[:, None] <= jnp.arange(LANES)[None, :]).astype(BF16)
    return pl.pallas_call(
        _cumsum_kernel,
        out_shape=jax.ShapeDtypeStruct((rows, n), F32),
        compiler_params=pltpu.CompilerParams(vmem_limit_bytes=VMEM_LIMIT),
        name="cumsum",
    )(x, tri)


def _rwkv_prep_kernel(pb_ref, prev_ref, mu_ref, w0_ref, ww_ref, a0_ref, wa_ref, wg_ref, kk_ref, ka_ref, bd_ref,
                      r_ref, kkn_ref, k_ref, b_ref, v_ref, lw_ref, g_ref, carry_ref):
    t = pl.program_id(1)
    tm = pb_ref.shape[1]

    @pl.when(t == 0)
    def _():
        carry_ref[0:1, :] = prev_ref[0]

    pbv = pb_ref[0]
    rolled = pltpu.roll(pbv, 1, 0)
    shifted = jnp.where(_iota(pbv.shape, 0) == 0, carry_ref[0:1, :], rolled)
    carry_ref[0:1, :] = pbv[tm - 1:tm, :]
    xs = pbv + (shifted - pbv) * mu_ref[...]

    r = xs[:, 0:D_B]
    kb = xs[:, D_B:2 * D_B]
    o3 = 3 * D_B
    wa_in = xs[:, o3:o3 + LANES]
    wa_in = jnp.where(_iota(wa_in.shape, 1) < W_LORA, jnp.tanh(wa_in), wa_in).astype(BF16)
    lw_ref[0] = -DECAY_SCALE * _sigmoid(w0_ref[...] + _dot(wa_in, ww_ref[...]))
    a = _sigmoid(a0_ref[...] + _dot(wa_in, wa_ref[...]))
    g_ref[0] = _dot(_sigmoid(xs[:, o3 + LANES:o3 + 2 * LANES]).astype(BF16), wg_ref[...])
    kk = kb * kk_ref[...]
    kk = kk * lax.rsqrt(_dot_sel(kk * kk, bd_ref[...]) + 1e-12)
    r_ref[0] = r
    kkn_ref[0] = kk
    k_ref[0] = kb * (1.0 + (a - 1.0) * ka_ref[...])
    b_ref[0] = kk * a
    v_ref[0] = xs[:, 2 * D_B:3 * D_B]


def _rwkv_prep(pb3, prev, lw, tm):
    b, t, _ = pb3.shape
    fixed = lambda bi, i: (0, 0)
    full = lambda a: pl.BlockSpec(a.shape, fixed)
    params = [lw['rw_mu'], lw['rw_w0'], lw['rw_ww'], lw['rw_a0'], lw['rw_wa'], lw['rw_wg'], lw['rw_k_k'],
              lw['rw_k_a'], lw['bd384']]
    out_spec = pl.BlockSpec((1, tm, D_B), lambda bi, i: (bi, i, 0))
    return pl.pallas_call(
        _rwkv_prep_kernel,
        grid=(b, t // tm),
        in_specs=[pl.BlockSpec((1, tm, B_IN), lambda bi, i: (bi, i, 0)),
                  pl.BlockSpec((1, 1, B_IN), lambda bi, i: (bi, 0, 0))] + [full(a) for a in params],
        out_specs=[out_spec] * 7,
        out_shape=[jax.ShapeDtypeStruct((b, t, D_B), F32)] * 7,
        scratch_shapes=[pltpu.VMEM((8, B_IN), F32)],
        compiler_params=_cparams(("parallel", "arbitrary")),
        name="rwkv_prep",
    )(pb3, prev, *params)


_P_A = 1
_P_INV = 1
_P_APPLY = 1
_P_GH = 1


def _rwkv_chunk_kernel(r_ref, kk_ref, k_ref, b_ref, v_ref, lw_ref, rt_ref, yl_ref, g_ref, h_ref):
    s = r_ref.shape[1]
    nc = s // CHUNK
    r, kk, k, b, v, lw = r_ref[0], kk_ref[0], k_ref[0], b_ref[0], v_ref[0], lw_ref[0]

    row = _iota((s, LANES), 0)
    rin = jnp.bitwise_and(row, CHUNK - 1)
    cl = lw
    for sh in (1, 2, 4, 8, 16, 32):
        cl = cl + jnp.where(rin >= sh, pltpu.roll(cl, sh, 0), 0.0)
    tot = jnp.concatenate(
        [jnp.broadcast_to(cl[c * CHUNK + CHUNK - 1:(c + 1) * CHUNK, :], (CHUNK, LANES)) for c in range(nc)], axis=0)
    e_ncl = jnp.exp(-cl)
    e_rem = jnp.exp(tot - cl)
    kkt = kk * jnp.exp(cl - lw)
    rt = r * jnp.exp(cl)
    khbh = jnp.concatenate([k * e_ncl, b * e_ncl], axis=0)
    kw = k * e_rem
    bw = b * e_rem

    ti = _iota((s, s), 0)
    si = _iota((s, s), 1)
    same = jnp.right_shift(ti, 6) == jnp.right_shift(si, 6)
    strict = jnp.logical_and(same, si < ti)
    incl = jnp.logical_and(same, si <= ti)
    eye = (ti == si).astype(F32)
    low = _iota((s, LANES), 1) < HEAD_DIM

    uloc2 = kkt2 = rt2 = yl2 = None
    for x in range(2):
        mx = low if x == 0 else jnp.logical_not(low)
        kkt_x = jnp.where(mx, kkt, 0.0)
        rt_x = jnp.where(mx, rt, 0.0)
        p4 = _mm(jnp.concatenate([kkt_x, rt_x], axis=0), khbh, _P_A, _dot_nt)
        akk = jnp.where(strict, p4[:s, :s], 0.0)
        abk = jnp.where(strict, p4[:s, s:], 0.0)
        ark = jnp.where(incl, p4[s:, :s], 0.0)
        arb = jnp.where(incl, p4[s:, s:], 0.0)
        pw = -abk
        tinv = eye + pw
        for _ in range(5):
            pw = _mm(pw, pw, _P_INV)
            tinv = tinv + _mm(tinv, pw, _P_INV)
        av = _mm(jnp.concatenate([akk, ark], axis=0), v, _P_APPLY)
        tx = _mm(tinv, jnp.concatenate([av[:s], kkt_x], axis=1), _P_APPLY)
        ax = _mm(arb, tx, _P_APPLY)
        uloc, kkt_s = tx[:, :LANES], tx[:, LANES:]
        rt_s = rt_x - ax[:, LANES:]
        yl = av[s:] - ax[:, :LANES]
        if x == 0:
            uloc2, kkt2, rt2, yl2 = uloc, kkt_s, rt_s, yl
        else:
            uloc2 = jnp.where(low, uloc2, uloc)
            kkt2 = jnp.where(low, kkt2, kkt_s)
            rt2 = jnp.where(low, rt2, rt_s)
            yl2 = jnp.where(low, yl2, yl)
    rt_ref[0] = rt2
    yl_ref[0] = yl2

    ji = _iota((LANES, LANES), 0)
    jj = _iota((LANES, LANES), 1)
    blockdiag = (ji < HEAD_DIM) == (jj < HEAD_DIM)
    for c in range(nc):
        inc = jnp.right_shift(row, 6) == c
        bw_c = jnp.where(inc, bw, 0.0)
        kw_c = jnp.where(inc, kw, 0.0)
        e_tot = jnp.exp(tot[c * CHUNK:c * CHUNK + 1, :])
        gm = jnp.where(ji == jj, e_tot, 0.0) - _mm(bw_c, kkt2, _P_GH, _dot_tn)
        hm = _mm(kw_c, v, _P_GH, _dot_tn) - _mm(bw_c, uloc2, _P_GH, _dot_tn)
        g_ref[0, 0, c] = jnp.where(blockdiag, gm, 0.0)
        h_ref[0, 0, c] = jnp.where(blockdiag, hm, 0.0)


def _rwkv_chunk(r, kk, k, b, v, lw, s):
    bsz, t, _ = r.shape
    npair = H_B // 2
    nc = t // CHUNK
    tok = pl.BlockSpec((1, s, LANES), lambda bi, p, i: (bi, i, p))
    mat = pl.BlockSpec((1, 1, s // CHUNK, LANES, LANES), lambda bi, p, i: (bi, p, i, 0, 0))
    return pl.pallas_call(
        _rwkv_chunk_kernel,
        grid=(bsz, npair, t // s),
        in_specs=[tok] * 6,
        out_specs=[tok, tok, mat, mat],
        out_shape=[jax.ShapeDtypeStruct((bsz, t, D_B), F32)] * 2
                  + [jax.ShapeDtypeStruct((bsz, npair, nc, LANES, LANES), F32)] * 2,
        compiler_params=_cparams(("parallel", "parallel", "parallel")),
        name="rwkv_chunk",
    )(r, kk, k, b, v, lw)


def _rwkv_scan_kernel(s0_ref, g_ref, h_ref, rt_ref, yl_ref, y_ref, sfin_ref):
    nc = g_ref.shape[2]

    def body(c, st):
        off = pl.multiple_of(c * CHUNK, CHUNK)
        y_ref[0, pl.ds(off, CHUNK), :] = _dot3(rt_ref[0, pl.ds(off, CHUNK), :], st) + yl_ref[0, pl.ds(off, CHUNK), :]
        return _dot3(g_ref[0, 0, c], st) + h_ref[0, 0, c]

    sfin_ref[0, 0] = lax.fori_loop(0, nc, body, s0_ref[0, 0])


def _rwkv_scan(s0, g, h, rt, yl):
    bsz, t, _ = rt.shape
    npair = H_B // 2
    nc = t // CHUNK
    st_spec = pl.BlockSpec((1, 1, LANES, LANES), lambda bi, p: (bi, p, 0, 0))
    mat = pl.BlockSpec((1, 1, nc, LANES, LANES), lambda bi, p: (bi, p, 0, 0, 0))
    tok = pl.BlockSpec((1, t, LANES), lambda bi, p: (bi, 0, p))
    return pl.pallas_call(
        _rwkv_scan_kernel,
        grid=(bsz, npair),
        in_specs=[st_spec, mat, mat, tok, tok],
        out_specs=[tok, st_spec],
        out_shape=[jax.ShapeDtypeStruct((bsz, t, D_B), F32), jax.ShapeDtypeStruct((bsz, npair, LANES, LANES), F32)],
        compiler_params=_cparams(("parallel", "parallel")),
        name="rwkv_scan",
    )(s0, g, h, rt, yl)


def _rwkv_post_kernel(y_ref, r_ref, k_ref, v_ref, g_ref, lnw_ref, lnb_ref, rk_ref, bd_ref, o_ref):
    bd = bd_ref[...]
    y = y_ref[...]
    mu = _dot_sel(y, bd) * (1.0 / HEAD_DIM)
    d = y - mu
    var = _dot_sel(d * d, bd) * (1.0 / HEAD_DIM)
    yn = d * lax.rsqrt(var + GN_EPS) * lnw_ref[...] + lnb_ref[...]
    v = v_ref[...]
    bonus = _dot_sel(r_ref[...] * k_ref[...] * rk_ref[...], bd) * v
    o_ref[...] = ((yn + bonus) * g_ref[...]).astype(BF16)


def _rwkv_post(y, r, k, v, g, lw, tm):
    n = y.shape[0]
    row = pl.BlockSpec((tm, D_B), lambda i: (i, 0))
    full = lambda a: pl.BlockSpec(a.shape, lambda i: (0, 0))
    params = [lw['rw_ln_w'], lw['rw_ln_b'], lw['rw_r_k'], lw['bd384']]
    return pl.pallas_call(
        _rwkv_post_kernel,
        grid=(n // tm,),
        in_specs=[row] * 5 + [full(a) for a in params],
        out_specs=row,
        out_shape=jax.ShapeDtypeStruct((n, D_B), BF16),
        compiler_params=_cparams(("parallel",)),
        name="rwkv_post",
    )(y, r, k, v, g, *params)


_L_EXP = 16


def _moe_kernel(x_ref, oa_ref, ob_ref, oc_ref, wo_ref, gffn_ref, wr_ref, br_ref, wgu_ref, wd_ref,
                o_ref, acc_ref, h_ref, comb_ref):
    e = pl.program_id(1)

    @pl.when(e == 0)
    def _():
        da = H_A * HEAD_DIM
        x1 = (x_ref[...] + _dot(oa_ref[...], wo_ref[0:da, :]) + _dot(ob_ref[...], wo_ref[da:da + D_B, :])
              + _dot(oc_ref[...], wo_ref[da + D_B:, :]))
        acc_ref[...] = x1
        hf = x1 * lax.rsqrt(jnp.mean(x1 * x1, axis=-1, keepdims=True) + RMS_EPS) * gffn_ref[...]
        h_ref[...] = hf.astype(BF16)

        logit = _dot3(hf, wr_ref[...]) + br_ref[...]
        lane_i = _iota(logit.shape, 1)
        lane = lane_i.astype(F32)
        big = jnp.float32(3e38)
        is_g = lane_i < N_GROUPS
        gl = jnp.where(is_g, logit, -big)
        gmax = jnp.max(gl, axis=1, keepdims=True)
        pg_top = 1.0 / jnp.sum(jnp.where(is_g, jnp.exp(gl - gmax), 0.0), axis=1, keepdims=True)
        g_idx = jnp.min(jnp.where(jnp.logical_and(is_g, gl == gmax), lane, big), axis=1, keepdims=True)
        el = lane_i - _L_EXP
        in_e = jnp.logical_and(el >= 0, el < N_EXPERTS)
        sel = jnp.logical_and(in_e, jnp.right_shift(el, 2).astype(F32) == g_idx)
        l1 = jnp.where(sel, logit, -big)
        v1 = jnp.max(l1, axis=1, keepdims=True)
        i1 = jnp.min(jnp.where(jnp.logical_and(sel, l1 == v1), lane, big), axis=1, keepdims=True)
        sel2 = jnp.logical_and(sel, lane != i1)
        l2 = jnp.where(sel2, logit, -big)
        v2 = jnp.max(l2, axis=1, keepdims=True)
        i2 = jnp.min(jnp.where(jnp.logical_and(sel2, l2 == v2), lane, big), axis=1, keepdims=True)
        e2 = jnp.exp(v2 - v1)
        den = 1.0 / (1.0 + e2)
        comb_ref[...] = (jnp.where(lane == i1, den * pg_top, 0.0) + jnp.where(lane == i2, e2 * den * pg_top, 0.0))

    gu = _dot(h_ref[...], wgu_ref[0])
    gate = gu[:, :D_FF_E]
    act = gate * _sigmoid(gate) * gu[:, D_FF_E:]
    y = _dot(act.astype(BF16), wd_ref[0])
    comb = comb_ref[...]
    ce = jnp.sum(jnp.where(_iota(comb.shape, 1) == e + _L_EXP, comb, 0.0), axis=1, keepdims=True)
    acc_ref[...] += ce * y

    @pl.when(e == N_EXPERTS - 1)
    def _():
        o_ref[...] = acc_ref[...]


def _outproj_moe(x2d, oa, ob, oc, lw, tm):
    n = x2d.shape[0]
    row = lambda w: pl.BlockSpec((tm, w), lambda i, e: (i, 0))
    full = lambda a: pl.BlockSpec(a.shape, lambda i, e: (0, 0))
    return pl.pallas_call(
        _moe_kernel,
        grid=(n // tm, N_EXPERTS),
        in_specs=[row(D_MODEL), row(H_A * HEAD_DIM), row(D_B), row(D_C), full(lw['w_out']), full(lw['g_ffn']),
                  full(lw['w_r']), full(lw['b_r']),
                  pl.BlockSpec((1, D_MODEL, 2 * D_FF_E), lambda i, e: (e, 0, 0)),
                  pl.BlockSpec((1, D_FF_E, D_MODEL), lambda i, e: (e, 0, 0))],
        out_specs=row(D_MODEL),
        out_shape=jax.ShapeDtypeStruct((n, D_MODEL), F32),
        scratch_shapes=[pltpu.VMEM((tm, D_MODEL), F32), pltpu.VMEM((tm, D_MODEL), BF16), pltpu.VMEM((tm, LANES), F32)],
        compiler_params=_cparams(("parallel", "arbitrary")),
        name="outproj_moe",
    )(x2d, oa, ob, oc, lw['w_out'], lw['g_ffn'], lw['w_r'], lw['b_r'], lw['w_gu'], lw['w_d'])


def _place(pieces, width):
    rows = pieces[0][1].shape[0]
    out = jnp.zeros((rows, width), F32)
    for off, a in pieces:
        out = out.at[:, off:off + a.shape[1]].set(a.astype(F32))
    return out


def _row(v, width=None, off=0):
    v = v.reshape(1, -1).astype(F32)
    return v if width is None else _place([(off, v)], width)


def _block_diag_ones(n, blk):
    i = jnp.arange(n) // blk
    return (i[:, None] == i[None, :]).astype(BF16)


def _layer_weights(p, l):
    g = lambda name: p[name][l]
    w_in = g('w_in')
    o_b, o_c = A_IN, A_IN + B_IN
    w_in_p = _place([(_C_QL, w_in[:, :Q_LORA]), (_C_KV, w_in[:, Q_LORA:Q_LORA + KV_LORA]),
                     (_C_KR + NOPE, w_in[:, Q_LORA + KV_LORA:A_IN]), (_C_PB, w_in[:, o_b:o_c]),
                     (_C_QC, w_in[:, o_c:o_c + 3 * D_C]), (_C_F, w_in[:, o_c + 3 * D_C:])], _C_END).astype(BF16)
    w_uq = g('mla_w_uq').reshape(Q_LORA, H_A, QK_DIM)
    w_uq = jnp.pad(w_uq, ((0, 256 - Q_LORA), (0, 0), (0, LANES - QK_DIM))).reshape(256, H_A * LANES).astype(BF16)
    w_ukv = g('mla_w_ukv').reshape(KV_LORA, H_A, NOPE + HEAD_DIM)
    w_uk = jnp.pad(w_ukv[:, :, :NOPE], ((0, 0), (0, 0), (0, LANES - NOPE))).reshape(KV_LORA, H_A * LANES).astype(BF16)
    w_uv = w_ukv[:, :, NOPE:].reshape(KV_LORA, H_A * HEAD_DIM).astype(BF16)
    zeros_w = jnp.zeros((W_LORA, D_B), F32)
    w_r = _place([(0, g('moe_w_rg')), (_L_EXP, g('moe_w_re'))], LANES)
    b_r = _place([(0, g('moe_b_rg').reshape(1, -1)), (_L_EXP, g('moe_b_re').reshape(1, -1))], LANES)
    return dict(
        g_mix=_row(g('g_mix')), w_in=w_in_p, g_qa=_row(g('mla_g_qa'), 256), w_uq=w_uq, g_kva=_row(g('mla_g_kva')),
        g_qn=_row(g('mla_g_qn'), LANES), g_fq=_row(jnp.tile(g('fox_g_qn'), H_C)),
        g_fk=_row(jnp.tile(g('fox_g_kn'), H_C)), b_f=_row(g('fox_b_f'), LANES),
        ones128=jnp.ones((LANES, LANES), BF16), eye128=jnp.eye(LANES, dtype=BF16),
        bd256=_block_diag_ones(D_C, HEAD_DIM), bd384=_block_diag_ones(D_B, HEAD_DIM),
        w_uk=w_uk, w_uv=w_uv, g_kn=_row(g('mla_g_kn')[:NOPE], LANES), g_kr=_row(g('mla_g_kn')[NOPE:], LANES, NOPE),
        rw_mu=_row(g('rw_mu')), rw_w0=_row(g('rw_w0')), rw_a0=_row(g('rw_a0')),
        rw_ww=jnp.concatenate([g('rw_w_up'), zeros_w], axis=0).astype(BF16),
        rw_wa=jnp.concatenate([zeros_w, g('rw_a_up')], axis=0).astype(BF16),
        rw_wg=g('rw_g_up').astype(BF16), rw_k_k=_row(g('rw_k_k')), rw_k_a=_row(g('rw_k_a')),
        rw_r_k=_row(g('rw_r_k')), rw_ln_w=_row(g('rw_ln_w')), rw_ln_b=_row(g('rw_ln_b')),
        w_out=g('w_out').astype(BF16), g_ffn=_row(g('g_ffn')), w_r=w_r, b_r=b_r,
        w_gu=jnp.concatenate([g('moe_w_gate'), g('moe_w_up')], axis=-1).astype(BF16),
        w_d=g('moe_w_down').astype(BF16),
    )


def _rope_tables(pos):
    half = ROPE // 2
    inv = ROPE_BASE ** (-jnp.arange(half, dtype=F32) / half)
    ang = pos.astype(F32)[:, None] * inv[None, :]
    cos, sin = jnp.cos(ang), jnp.sin(ang)
    t = pos.shape[0]
    z = lambda w: jnp.zeros((t, w), F32)
    c = jnp.concatenate([jnp.ones((t, NOPE), F32), cos, cos, z(LANES - QK_DIM)], axis=1)
    s_left = jnp.concatenate([z(NOPE), -sin, z(half), z(LANES - QK_DIM)], axis=1)
    s_right = jnp.concatenate([z(NOPE), z(half), sin, z(LANES - QK_DIM)], axis=1)
    return c, s_left, s_right


def _pick(n, prefs):
    for t in prefs:
        if n % t == 0:
            return t
    return n


def _layer(x, lw, hist):
    b, t, _ = x.shape
    n = b * t
    past = 0 if hist is None else hist['ckv'].shape[1]
    x2d = x.reshape(n, D_MODEL)
    tm = _pick(t, (512, 256, 128, 64))
    tmp = _pick(past, (512, 256, 128, 64)) if past else 0
    eye = lw['eye128']

    q_tabs = _rope_tables(past + jnp.arange(t))
    ckv, kr128, kr, qt, pb, qct, kc, vc, vct, lf = _inproj(x2d, lw, q_tabs, b, t, tm)

    k_new, vt_new = _kvprep(ckv, kr128, lw, q_tabs, b, t, tm)
    if hist is None:
        kp = vpt = None
    else:
        ckv_past = hist['ckv'].reshape(b * past, KV_LORA)
        kr_past = jnp.pad(hist['krope'].reshape(b * past, ROPE), ((0, 0), (NOPE, LANES - QK_DIM)))
        kp, vpt = _kvprep(ckv_past, kr_past, lw, _rope_tables(jnp.arange(past)), b, past, tmp)
        kp = kp.reshape(b, past, -1)
    o_a = _attention(qt, k_new.reshape(b, t, -1), vt_new, kp, vpt, None, eye, n_pairs=H_A // 2, shared=False,
                     frame_causal=False, tq=tm, tk=tm, tkp=tmp)

    lf3 = lf.reshape(b, t, H_C)
    lf_all = lf3 if hist is None else jnp.concatenate([hist['flogf'].astype(F32), lf3], axis=1)
    ltot = past + t
    lpad = -(-ltot // LANES) * LANES
    lf_t = jnp.pad(jnp.swapaxes(lf_all, 1, 2), ((0, 0), (0, 0), (0, lpad - ltot))).reshape(b * H_C, lpad)
    c_all = _cumsum_lanes(lf_t).reshape(b, H_C // 2, 2, lpad)
    cq = c_all[..., past:past + t]
    ck = jnp.swapaxes(c_all, 2, 3)
    if hist is None:
        kcp = vcpt = None
    else:
        kcp = hist['fk'].reshape(b, past, D_C)
        vcpt = _transpose_cast(hist['fv'].reshape(b * past, D_C), lw, b, past, tmp)
    o_c = _attention(qct, kc.reshape(b, t, D_C), vct, kcp, vcpt, (cq, ck), eye, n_pairs=H_C // 2, shared=True,
                     frame_causal=True, tq=tm, tk=tm, tkp=tmp)

    prev = jnp.zeros((b, 1, B_IN), F32) if hist is None else hist['shift'].astype(F32)
    pb3 = pb.reshape(b, t, B_IN)
    r, kk, k, bb, v, lwd, g = _rwkv_prep(pb3, prev, lw, _pick(t, (256, 128, 64)))
    rt, yl, gm, hm = _rwkv_chunk(r, kk, k, bb, v, lwd, _pick(t, (256, 128, 64)))
    npair = H_B // 2
    if hist is None:
        s0 = jnp.zeros((b, npair, LANES, LANES), F32)
    else:
        st = jnp.swapaxes(hist['wkv'].astype(F32), -1, -2).reshape(b, npair, 2, HEAD_DIM, HEAD_DIM)
        s0 = jnp.zeros((b, npair, 2, HEAD_DIM, 2, HEAD_DIM), F32)
        s0 = s0.at[:, :, 0, :, 0, :].set(st[:, :, 0]).at[:, :, 1, :, 1, :].set(st[:, :, 1])
        s0 = s0.reshape(b, npair, LANES, LANES)
    y, sfin = _rwkv_scan(s0, gm, hm, rt, yl)
    flat = lambda a: a.reshape(n, D_B)
    o_b = _rwkv_post(flat(y), flat(r), flat(k), flat(v), flat(g), lw, tm)
    sf = sfin.reshape(b, npair, 2, HEAD_DIM, 2, HEAD_DIM)
    s_fin = jnp.stack([sf[:, :, 0, :, 0, :], sf[:, :, 1, :, 1, :]], axis=2).reshape(b, H_B, HEAD_DIM, HEAD_DIM)
    s_fin = jnp.swapaxes(s_fin, -1, -2)

    tmm = _pick(n, (512, 256, 128, 64))
    x_out = _outproj_moe(x2d, o_a.reshape(n, -1), flat(o_b), o_c.reshape(n, -1), lw, tmm).reshape(b, t, D_MODEL)

    new = (ckv.reshape(b, t, KV_LORA), kr.reshape(b, t, ROPE), kc.reshape(b, t, H_C, HEAD_DIM),
           vc.reshape(b, t, H_C, HEAD_DIM), lf3, s_fin, pb3[:, -1:])
    return x_out, new


def kernel(x_prompt, x_sample, cache_mla_latent, cache_mla_krope, cache_fox_k, cache_fox_v, cache_fox_logf,
           state_rwkv_wkv, state_rwkv_shift, g_mix, w_in, mla_g_qa, mla_w_uq, mla_g_kva, mla_w_ukv, mla_g_qn,
           mla_g_kn, rw_mu, rw_w0, rw_w_up, rw_a0, rw_a_up, rw_g_up, rw_k_k, rw_k_a, rw_r_k, rw_ln_w, rw_ln_b,
           fox_g_qn, fox_g_kn, fox_b_f, w_out, g_ffn, moe_w_rg, moe_b_rg, moe_w_re, moe_b_re, moe_w_gate,
           moe_w_up, moe_w_down):
    params = dict(g_mix=g_mix, w_in=w_in, mla_g_qa=mla_g_qa, mla_w_uq=mla_w_uq, mla_g_kva=mla_g_kva,
                  mla_w_ukv=mla_w_ukv, mla_g_qn=mla_g_qn, mla_g_kn=mla_g_kn, rw_mu=rw_mu, rw_w0=rw_w0,
                  rw_w_up=rw_w_up, rw_a0=rw_a0, rw_a_up=rw_a_up, rw_g_up=rw_g_up, rw_k_k=rw_k_k, rw_k_a=rw_k_a,
                  rw_r_k=rw_r_k, rw_ln_w=rw_ln_w, rw_ln_b=rw_ln_b, fox_g_qn=fox_g_qn, fox_g_kn=fox_g_kn,
                  fox_b_f=fox_b_f, w_out=w_out, g_ffn=g_ffn, moe_w_rg=moe_w_rg, moe_b_rg=moe_b_rg,
                  moe_w_re=moe_w_re, moe_b_re=moe_b_re, moe_w_gate=moe_w_gate, moe_w_up=moe_w_up,
                  moe_w_down=moe_w_down)
    depth = g_mix.shape[0]
    yp, ys = x_prompt, x_sample
    p_new, s_new = [], []
    for l in range(depth):
        lw = _layer_weights(params, l)
        hist = dict(ckv=cache_mla_latent[l], krope=cache_mla_krope[l], fk=cache_fox_k[l], fv=cache_fox_v[l],
                    flogf=cache_fox_logf[l], wkv=state_rwkv_wkv[l], shift=state_rwkv_shift[l])
        yp, np_l = _layer(yp, lw, None)
        ys, ns_l = _layer(ys, lw, hist)
        p_new.append(np_l)
        s_new.append(ns_l)
    p_out = tuple(jnp.stack(t) for t in zip(*p_new))
    s_out = tuple(jnp.stack(t) for t in zip(*s_new))
    return (yp, ys) + p_out + s_out
```

```python
import functools
import math

import jax
import jax.numpy as jnp
from jax import lax
from jax.experimental import pallas as pl
from jax.experimental.pallas import tpu as pltpu

F32 = jnp.float32
BF16 = jnp.bfloat16

D_MODEL = 1024
HEAD_DIM = 64
H_A, H_B, H_C = 6, 6, 4
Q_LORA, KV_LORA, NOPE, ROPE = 192, 128, 64, 32
QK_DIM = NOPE + ROPE
ROPE_BASE = 10000.0
A_IN = Q_LORA + KV_LORA + ROPE
D_B = H_B * HEAD_DIM
W_LORA, A_LORA, G_LORA = 64, 64, 128
B_IN = 3 * D_B + W_LORA + A_LORA + G_LORA
DECAY_SCALE = math.exp(-0.5)
GN_EPS = 64e-5
D_C = H_C * HEAD_DIM
C_IN = 3 * D_C + H_C
N_GROUPS, E_PER_GROUP = 4, 4
N_EXPERTS = N_GROUPS * E_PER_GROUP
D_FF_E = 256
NEG_INF = -1e30
RMS_EPS = 1e-6
CHUNK = 64
LOG2E = math.log2(math.e)

LANES = 128
VMEM_LIMIT = 56 * 1024 * 1024
ATTN_TK = 512

_C_QL = 0
_C_KV = 256
_C_KR = 384
_C_PB = 512
_C_QC = _C_PB + B_IN
_C_KC = _C_QC + D_C
_C_VC = _C_KC + D_C
_C_F = _C_VC + D_C
_C_END = _C_F + LANES


def _cparams(sem):
    return pltpu.CompilerParams(dimension_semantics=sem, vmem_limit_bytes=VMEM_LIMIT)


def _dot(a, b):
    return jnp.dot(a, b, preferred_element_type=F32)


def _dot_nt(a, b):
    return lax.dot_general(a, b, (((1,), (1,)), ((), ())), preferred_element_type=F32)


def _dot_tn(a, b):
    return lax.dot_general(a, b, (((0,), (0,)), ((), ())), preferred_element_type=F32)


def _split2(x):
    hi = x.astype(BF16)
    lo = (x - hi.astype(F32)).astype(BF16)
    return hi, lo


def _split3(x):
    hi = x.astype(BF16)
    r = x - hi.astype(F32)
    mid = r.astype(BF16)
    lo = (r - mid.astype(F32)).astype(BF16)
    return hi, mid, lo


def _dot3(a, b, dot=_dot):
    ah, al = _split2(a)
    bh, bl = _split2(b)
    return dot(ah, bh) + (dot(ah, bl) + dot(al, bh))


def _mm(a, b, passes, dot=_dot):
    if passes == 1:
        return dot(a.astype(BF16), b.astype(BF16))
    return _dot3(a, b, dot)


def _dot_sel(a, sel):
    return _dot(a.astype(BF16), sel)


def _iota(shape, dim):
    return lax.broadcasted_iota(jnp.int32, shape, dim)


def _rope(x, c, s_left, s_right):
    return x * c + pltpu.roll(x, 112, 1) * s_left + pltpu.roll(x, 16, 1) * s_right


def _sigmoid(x):
    return 1.0 / (1.0 + jnp.exp(-x))


def _log_sigmoid(x):
    return jnp.minimum(x, 0.0) - jnp.log(1.0 + jnp.exp(-jnp.abs(x)))


def _transpose_bf16(x, eye):
    return _dot_nt(eye, x.astype(BF16)).astype(BF16)


def _inproj_kernel(x_ref, gmix_ref, w_ref, gqa_ref, wuq_ref, gkva_ref, gqn_ref, gfq_ref, gfk_ref, bf_ref,
                   ones_ref, bd_ref, eye_ref, c_ref, sl_ref, sr_ref,
                   ckv_ref, kr128_ref, kr_ref, qt_ref, pb_ref, qct_ref, kc_ref, kcb_ref, vc_ref, vct_ref, lf_ref):
    x = x_ref[...]
    h = x * lax.rsqrt(jnp.mean(x * x, axis=-1, keepdims=True) + RMS_EPS) * gmix_ref[...]
    h = h.astype(BF16)
    eye = eye_ref[...]

    ql = _dot(h, w_ref[:, _C_QL:_C_QL + 256])
    ql = ql * lax.rsqrt(jnp.sum(ql * ql, axis=-1, keepdims=True) * (1.0 / Q_LORA) + RMS_EPS) * gqa_ref[...]
    qh = _dot(ql.astype(BF16), wuq_ref[...])
    c, s_l, s_r = c_ref[...], sl_ref[...], sr_ref[...]
    ones = ones_ref[...]
    for hh in range(H_A):
        qv = qh[:, hh * LANES:(hh + 1) * LANES]
        ss = _dot_sel(qv * qv, ones)
        qn = qv * lax.rsqrt(ss * (1.0 / QK_DIM) + RMS_EPS) * gqn_ref[...]
        qt_ref[0, hh * LANES:(hh + 1) * LANES, :] = _transpose_bf16(
            _rope(qn, c, s_l, s_r) * (LOG2E * QK_DIM ** -0.5), eye)

    kv = _dot(h, w_ref[:, _C_KV:_C_KV + KV_LORA])
    ckv_ref[...] = kv * lax.rsqrt(jnp.mean(kv * kv, axis=-1, keepdims=True) + RMS_EPS) * gkva_ref[...]
    kr = _dot(h, w_ref[:, _C_KR:_C_KR + LANES])
    kr128_ref[...] = kr
    kr_ref[...] = kr[:, NOPE:NOPE + ROPE]

    pb_ref[...] = _dot(h, w_ref[:, _C_PB:_C_PB + B_IN])

    bd = bd_ref[...]
    qc = _dot(h, w_ref[:, _C_QC:_C_QC + D_C])
    qc = qc * lax.rsqrt(_dot_sel(qc * qc, bd) * (1.0 / HEAD_DIM) + RMS_EPS) * gfq_ref[...]
    qc = qc * (LOG2E * HEAD_DIM ** -0.5)
    kc = _dot(h, w_ref[:, _C_KC:_C_KC + D_C])
    kc = kc * lax.rsqrt(_dot_sel(kc * kc, bd) * (1.0 / HEAD_DIM) + RMS_EPS) * gfk_ref[...]
    kc_ref[...] = kc
    kcb_ref[...] = kc.astype(BF16)
    vc = _dot(h, w_ref[:, _C_VC:_C_VC + D_C])
    vc_ref[...] = vc
    for pp in range(D_C // LANES):
        qct_ref[0, pp * LANES:(pp + 1) * LANES, :] = _transpose_bf16(qc[:, pp * LANES:(pp + 1) * LANES], eye)
        vct_ref[0, pp * LANES:(pp + 1) * LANES, :] = _transpose_bf16(vc[:, pp * LANES:(pp + 1) * LANES], eye)
    f = _dot(h, w_ref[:, _C_F:_C_F + LANES]) + bf_ref[...]
    lf_ref[...] = _log_sigmoid(f)[:, :H_C]


def _inproj(x2d, lw, tabs, bsz, t, tm):
    n = x2d.shape[0]
    nt = t // tm
    row = lambda i: (i, 0)
    fixed = lambda i: (0, 0)
    tab = lambda i: (i % nt, 0)
    colmajor = lambda i: (i // nt, 0, i % nt)
    full = lambda a: pl.BlockSpec(a.shape, fixed)
    params = [lw['g_mix'], lw['w_in'], lw['g_qa'], lw['w_uq'], lw['g_kva'], lw['g_qn'], lw['g_fq'], lw['g_fk'],
              lw['b_f'], lw['ones128'], lw['bd256'], lw['eye128']]
    rowout = lambda w, dt: (pl.BlockSpec((tm, w), row), jax.ShapeDtypeStruct((n, w), dt))
    colout = lambda w: (pl.BlockSpec((1, w, tm), colmajor), jax.ShapeDtypeStruct((bsz, w, t), BF16))
    outs = [rowout(KV_LORA, F32), rowout(LANES, F32), rowout(ROPE, F32), colout(H_A * LANES), rowout(B_IN, F32),
            colout(D_C), rowout(D_C, F32), rowout(D_C, BF16), rowout(D_C, F32), colout(D_C), rowout(H_C, F32)]
    return pl.pallas_call(
        _inproj_kernel,
        grid=(n // tm,),
        in_specs=[pl.BlockSpec((tm, D_MODEL), row)] + [full(a) for a in params]
                 + [pl.BlockSpec((tm, LANES), tab)] * 3,
        out_specs=[o[0] for o in outs],
        out_shape=[o[1] for o in outs],
        compiler_params=_cparams(("parallel",)),
        name="inproj",
    )(x2d, *params, *tabs)


def _kvprep_kernel(ckv_ref, kr_ref, wuk_ref, wuv_ref, gkn_ref, gkr_ref, ones_ref, eye_ref, c_ref, sl_ref, sr_ref,
                   k_ref, vt_ref):
    cb = ckv_ref[...].astype(BF16)
    kn = _dot(cb, wuk_ref[...])
    v = _dot(cb, wuv_ref[...])
    eye = eye_ref[...]
    for pp in range(H_A // 2):
        vt_ref[0, pp * LANES:(pp + 1) * LANES, :] = _transpose_bf16(v[:, pp * LANES:(pp + 1) * LANES], eye)
    ones = ones_ref[...]
    kr = kr_ref[...]
    ssr = _dot_sel(kr * kr, ones)
    krg = _rope(kr * gkr_ref[...], c_ref[...], sl_ref[...], sr_ref[...])
    for hh in range(H_A):
        knh = kn[:, hh * LANES:(hh + 1) * LANES]
        ssn = _dot_sel(knh * knh, ones)
        r = lax.rsqrt((ssn + ssr) * (1.0 / QK_DIM) + RMS_EPS)
        k_ref[:, hh * LANES:(hh + 1) * LANES] = ((knh * gkn_ref[...] + krg) * r).astype(BF16)


def _kvprep(ckv2d, kr128_2d, lw, tabs, bsz, t, tm):
    n = ckv2d.shape[0]
    nt = t // tm
    row = lambda i: (i, 0)
    fixed = lambda i: (0, 0)
    tab = lambda i: (i % nt, 0)
    full = lambda a: pl.BlockSpec(a.shape, fixed)
    params = [lw['w_uk'], lw['w_uv'], lw['g_kn'], lw['g_kr'], lw['ones128'], lw['eye128']]
    return pl.pallas_call(
        _kvprep_kernel,
        grid=(n // tm,),
        in_specs=[pl.BlockSpec((tm, KV_LORA), row), pl.BlockSpec((tm, LANES), row)] + [full(a) for a in params]
                 + [pl.BlockSpec((tm, LANES), tab)] * 3,
        out_specs=[pl.BlockSpec((tm, H_A * LANES), row),
                   pl.BlockSpec((1, H_A * HEAD_DIM, tm), lambda i: (i // nt, 0, i % nt))],
        out_shape=[jax.ShapeDtypeStruct((n, H_A * LANES), BF16),
                   jax.ShapeDtypeStruct((bsz, H_A * HEAD_DIM, t), BF16)],
        compiler_params=_cparams(("parallel",)),
        name="kvprep",
    )(ckv2d, kr128_2d, *params, *tabs)


def _transpose_kernel(x_ref, eye_ref, o_ref):
    eye = eye_ref[...]
    for pp in range(x_ref.shape[1] // LANES):
        o_ref[0, pp * LANES:(pp + 1) * LANES, :] = _transpose_bf16(x_ref[:, pp * LANES:(pp + 1) * LANES], eye)


def _transpose_cast(x2d, lw, bsz, t, tm):
    n, w = x2d.shape
    nt = t // tm
    return pl.pallas_call(
        _transpose_kernel,
        grid=(n // tm,),
        in_specs=[pl.BlockSpec((tm, w), lambda i: (i, 0)), pl.BlockSpec((LANES, LANES), lambda i: (0, 0))],
        out_specs=pl.BlockSpec((1, w, tm), lambda i: (i // nt, 0, i % nt)),
        out_shape=jax.ShapeDtypeStruct((bsz, w, t), BF16),
        compiler_params=_cparams(("parallel",)),
        name="transpose_cast",
    )(x2d, lw['eye128'])


def _attn_kernel(*refs, tq, tk, tkp, nq, n_past, n_pairs, shared, frame_causal, has_bias, pipelined):
    it = iter(refs)
    q_ref, k_ref, vt_ref = next(it), next(it), next(it)
    if n_past:
        kp_ref, vpt_ref = next(it), next(it)
    if has_bias:
        cq_ref, ck_ref = next(it), next(it)
    eye_ref, o_ref = next(it), next(it)
    q_scr, m_scr, l_scr, acc_scr = next(it), next(it), next(it), next(it)

    qi = 0 if nq == 1 else pl.program_id(1)
    low = _iota((LANES, tq), 0) < HEAD_DIM
    blk = lambda p, j: p if shared else 2 * p + j
    for p in range(n_pairs):
        for j in range(2):
            q = q_ref[0, blk(p, j) * LANES:(blk(p, j) + 1) * LANES, :]
            if shared:
                keep = low if j == 0 else jnp.logical_not(low)
                q = jnp.where(keep, q, jnp.zeros_like(q))
            q_scr[2 * p + j] = q
    m_scr[...] = jnp.full(m_scr.shape, -jnp.inf, F32)
    l_scr[...] = jnp.zeros(l_scr.shape, F32)
    acc_scr[...] = jnp.zeros(acc_scr.shape, F32)

    def score(kr, st, width, p, j):
        kt = kr[0, pl.ds(st, width), blk(p, j) * LANES:(blk(p, j) + 1) * LANES].astype(BF16)
        return _dot(kt, q_scr[2 * p + j])

    def scores_to(buf, kr, st, width):
        for p in range(n_pairs):
            for j in range(2):
                buf[2 * p + j] = score(kr, st, width, p, j)

    def consume(get_s, vr, st, width, key_start, mask):
        for p in range(n_pairs):
            vt = vr[0, p * LANES:(p + 1) * LANES, pl.ds(st, width)]
            for j in range(2):
                h = 2 * p + j
                s = get_s(p, j)
                if has_bias:
                    s = s + (cq_ref[0, p, j:j + 1, :] - ck_ref[0, p, pl.ds(key_start, width), j:j + 1])
                if mask is not None:
                    s = jnp.where(mask, s, NEG_INF)
                m_prev = m_scr[h]
                m_next = jnp.maximum(m_prev, jnp.max(s, axis=0, keepdims=True))
                pr = jnp.exp2(s - m_next)
                alpha = jnp.exp2(m_prev - m_next)
                l_scr[h] = alpha * l_scr[h] + jnp.sum(pr, axis=0, keepdims=True)
                acc_scr[h] = acc_scr[h] * alpha + _dot(vt, pr.astype(BF16))
                m_scr[h] = m_next

    def step(kr, vr, st, width, key_start, mask):
        consume(lambda p, j: score(kr, st, width, p, j), vr, st, width, key_start, mask)

    kidx = _iota((tk, tq), 0)
    qidx = _iota((tk, tq), 1)
    diag_mask = lambda d: ((kidx + d * tk <= qidx) if frame_causal else
                           (jnp.right_shift(kidx + d * tk, 6) <= jnp.right_shift(qidx, 6)))

    if pipelined:
        sa, sb = next(it), next(it)
        tile = lambda t: pl.multiple_of(t * tk, tk)
        from_a = lambda p, j: sa[2 * p + j]
        from_b = lambda p, j: sb[2 * p + j]
        scores_to(sa, k_ref, tile(0), tk)

        def pair_body(u, carry):
            t0 = 2 * u
            scores_to(sb, k_ref, tile(t0 + 1), tk)
            consume(from_a, vt_ref, tile(t0), tk, tile(t0), None)
            scores_to(sa, k_ref, tile(t0 + 2), tk)
            consume(from_b, vt_ref, tile(t0 + 1), tk, tile(t0 + 1), None)
            return carry
        lax.fori_loop(0, qi // 2, pair_body, 0)
        odd = lax.rem(qi, 2) == 1

        @pl.when(jnp.logical_not(odd))
        def _():
            consume(from_a, vt_ref, tile(qi), tk, tile(qi), diag_mask(0))

        @pl.when(odd)
        def _():
            scores_to(sb, k_ref, tile(qi), tk)
            consume(from_a, vt_ref, tile(qi - 1), tk, tile(qi - 1), None)
            consume(from_b, vt_ref, tile(qi), tk, tile(qi), diag_mask(0))
    else:
        if n_past:
            def past_body(t, carry):
                st = pl.multiple_of(t * tkp, tkp)
                step(kp_ref, vpt_ref, st, tkp, st, None)
                return carry
            lax.fori_loop(0, n_past // tkp, past_body, 0)

        def new_body(t, carry):
            st = pl.multiple_of(t * tk, tk)
            step(k_ref, vt_ref, st, tk, n_past + st, None)
            return carry
        if nq > 1:
            lax.fori_loop(0, qi * (tq // tk), new_body, 0)

        for d in range(tq // tk):
            st = qi * tq + d * tk
            st = st if nq == 1 else pl.multiple_of(st, tk)
            step(k_ref, vt_ref, st, tk, n_past + st, diag_mask(d))

    for p in range(n_pairs):
        ot = jnp.where(low, acc_scr[2 * p] / l_scr[2 * p], acc_scr[2 * p + 1] / l_scr[2 * p + 1])
        o_ref[0, :, p * LANES:(p + 1) * LANES] = _dot_tn(ot.astype(BF16), eye_ref[...]).astype(o_ref.dtype)


def _attention(qt, k_new, vt_new, k_past, vt_past, bias, eye, *, n_pairs, shared, frame_causal, tq, tk, tkp):
    b, t = k_new.shape[0], k_new.shape[1]
    n_past = 0 if k_past is None else k_past.shape[1]
    nq = t // tq
    wq = qt.shape[1]
    args = [qt, k_new, vt_new]
    specs = [pl.BlockSpec((1, wq, tq), lambda bi, i: (bi, 0, i)),
             pl.BlockSpec((1, t, wq), lambda bi, i: (bi, 0, 0)),
             pl.BlockSpec((1, n_pairs * LANES, t), lambda bi, i: (bi, 0, 0))]
    if n_past:
        args += [k_past, vt_past]
        specs += [pl.BlockSpec((1, n_past, wq), lambda bi, i: (bi, 0, 0)),
                  pl.BlockSpec((1, n_pairs * LANES, n_past), lambda bi, i: (bi, 0, 0))]
    if bias is not None:
        cq, ck = bias
        args += [cq, ck]
        specs += [pl.BlockSpec((1, n_pairs, 2, tq), lambda bi, i: (bi, 0, 0, i)),
                  pl.BlockSpec((1, n_pairs, ck.shape[2], 2), lambda bi, i: (bi, 0, 0, 0))]
    args.append(eye)
    specs.append(pl.BlockSpec((LANES, LANES), lambda bi, i: (0, 0)))
    pipelined = n_past == 0 and nq > 1 and tq == tk
    kern = functools.partial(_attn_kernel, tq=tq, tk=tk, tkp=tkp, nq=nq, n_past=n_past, n_pairs=n_pairs,
                             shared=shared, frame_causal=frame_causal, has_bias=bias is not None,
                             pipelined=pipelined)
    nh = 2 * n_pairs
    score_bufs = [pltpu.VMEM((nh, tk, tq), F32)] * 2 if pipelined else []
    return pl.pallas_call(
        kern,
        grid=(b, nq),
        in_specs=specs,
        out_specs=pl.BlockSpec((1, tq, n_pairs * LANES), lambda bi, i: (bi, i, 0)),
        out_shape=jax.ShapeDtypeStruct((b, t, n_pairs * LANES), BF16),
        scratch_shapes=[pltpu.VMEM((nh, LANES, tq), BF16), pltpu.VMEM((nh, 1, tq), F32),
                        pltpu.VMEM((nh, 1, tq), F32), pltpu.VMEM((nh, LANES, tq), F32)] + score_bufs,
        compiler_params=_cparams(("parallel", "arbitrary")),
        name="attn_fox" if shared else "attn_mla",
    )(*args)


def _attn_kernel_pair(*refs, tq, tk, tkp, nq, n_past, shared, frame_causal, has_bias):
    it = iter(refs)
    qa_ref = next(it)
    qb_ref = qa_ref if shared else next(it)
    ka_ref = next(it)
    kb_ref = ka_ref if shared else next(it)
    vt_ref = next(it)
    if n_past:
        kpa_ref = next(it)
        kpb_ref = kpa_ref if shared else next(it)
        vpt_ref = next(it)
    if has_bias:
        cq_ref = next(it)
        ck_ref = next(it)
    eye_ref = next(it)
    o_ref = next(it)
    m_scr, l_scr, acc_scr = next(it), next(it), next(it)

    qi = 0 if nq == 1 else pl.program_id(2)
    low = _iota((LANES, tq), 0) < HEAD_DIM
    if shared:
        q2 = qa_ref[0]
        qs = (jnp.where(low, q2, jnp.zeros_like(q2)), jnp.where(low, jnp.zeros_like(q2), q2))
    else:
        qs = (qa_ref[0], qb_ref[0])

    m_scr[...] = jnp.full(m_scr.shape, -jnp.inf, F32)
    l_scr[...] = jnp.zeros(l_scr.shape, F32)
    acc_scr[...] = jnp.zeros(acc_scr.shape, F32)

    def step(k_tiles, vt, key_start, width, mask):
        for j in range(2):
            s = _dot(k_tiles[j].astype(BF16), qs[j])
            if has_bias:
                s = s + (cq_ref[0, 0, j:j + 1, :] - ck_ref[0, 0, pl.ds(key_start, width), j:j + 1])
            if mask is not None:
                s = jnp.where(mask, s, NEG_INF)
            m_prev = m_scr[j]
            m_next = jnp.maximum(m_prev, jnp.max(s, axis=0, keepdims=True))
            p = jnp.exp2(s - m_next)
            alpha = jnp.exp2(m_prev - m_next)
            l_scr[j] = alpha * l_scr[j] + jnp.sum(p, axis=0, keepdims=True)
            acc_scr[j] = acc_scr[j] * alpha + _dot(vt, p.astype(BF16))
            m_scr[j] = m_next

    if n_past:
        def past_body(t, carry):
            st = pl.multiple_of(t * tkp, tkp)
            step((kpa_ref[0, pl.ds(st, tkp), :], kpb_ref[0, pl.ds(st, tkp), :]), vpt_ref[0, :, pl.ds(st, tkp)],
                 st, tkp, None)
            return carry
        lax.fori_loop(0, n_past // tkp, past_body, 0)

    def new_body(t, carry):
        st = pl.multiple_of(t * tk, tk)
        step((ka_ref[0, pl.ds(st, tk), :], kb_ref[0, pl.ds(st, tk), :]), vt_ref[0, :, pl.ds(st, tk)],
             n_past + st, tk, None)
        return carry
    if nq > 1:
        lax.fori_loop(0, qi * (tq // tk), new_body, 0)

    kidx = _iota((tk, tq), 0)
    qidx = _iota((tk, tq), 1)
    for d in range(tq // tk):
        kk = kidx + d * tk
        mask = (kk <= qidx) if frame_causal else (jnp.right_shift(kk, 6) <= jnp.right_shift(qidx, 6))
        st = qi * tq + d * tk
        st = st if nq == 1 else pl.multiple_of(st, tk)
        step((ka_ref[0, pl.ds(st, tk), :], kb_ref[0, pl.ds(st, tk), :]), vt_ref[0, :, pl.ds(st, tk)],
             n_past + st, tk, mask)

    ot = jnp.where(low, acc_scr[0] / l_scr[0], acc_scr[1] / l_scr[1])
    o_ref[0] = _dot_tn(ot.astype(BF16), eye_ref[...]).astype(o_ref.dtype)


def _attention_pair(qt, k_new, vt_new, k_past, vt_past, bias, eye, *, n_pairs, shared, frame_causal, tq, tk, tkp):
    b, t = k_new.shape[0], k_new.shape[1]
    n_past = 0 if k_past is None else k_past.shape[1]
    nq = t // tq
    qblk = lambda off: pl.BlockSpec((1, LANES, tq), lambda bi, p, i, off=off: (bi, 2 * p + off, i))
    kblk = lambda rows, off: pl.BlockSpec((1, rows, LANES), lambda bi, p, i, off=off: (bi, 0, 2 * p + off))
    pair_q = pl.BlockSpec((1, LANES, tq), lambda bi, p, i: (bi, p, i))
    pair_k = lambda rows: pl.BlockSpec((1, rows, LANES), lambda bi, p, i: (bi, 0, p))
    pair_vt = lambda cols: pl.BlockSpec((1, LANES, cols), lambda bi, p, i: (bi, p, 0))
    args, specs = [], []
    if shared:
        args += [qt, k_new]
        specs += [pair_q, pair_k(t)]
    else:
        args += [qt, qt, k_new, k_new]
        specs += [qblk(0), qblk(1), kblk(t, 0), kblk(t, 1)]
    args.append(vt_new)
    specs.append(pair_vt(t))
    if n_past:
        if shared:
            args.append(k_past)
            specs.append(pair_k(n_past))
        else:
            args += [k_past, k_past]
            specs += [kblk(n_past, 0), kblk(n_past, 1)]
        args.append(vt_past)
        specs.append(pair_vt(n_past))
    if bias is not None:
        cq, ck = bias
        args += [cq, ck]
        specs += [pl.BlockSpec((1, 1, 2, tq), lambda bi, p, i: (bi, p, 0, i)),
                  pl.BlockSpec((1, 1, ck.shape[2], 2), lambda bi, p, i: (bi, p, 0, 0))]
    args.append(eye)
    specs.append(pl.BlockSpec((LANES, LANES), lambda bi, p, i: (0, 0)))
    kern = functools.partial(_attn_kernel, tq=tq, tk=tk, tkp=tkp, nq=nq, n_past=n_past, shared=shared,
                             frame_causal=frame_causal, has_bias=bias is not None)
    return pl.pallas_call(
        kern,
        grid=(b, n_pairs, nq),
        in_specs=specs,
        out_specs=pl.BlockSpec((1, tq, LANES), lambda bi, p, i: (bi, i, p)),
        out_shape=jax.ShapeDtypeStruct((b, t, n_pairs * LANES), BF16),
        scratch_shapes=[pltpu.VMEM((2, 1, tq), F32), pltpu.VMEM((2, 1, tq), F32), pltpu.VMEM((2, LANES, tq), F32)],
        compiler_params=_cparams(("parallel", "parallel", "arbitrary")),
        name="attn_fox" if shared else "attn_mla",
    )(*args)


def _cumsum_kernel(x_ref, tri_ref, o_ref):
    rows, n = x_ref.shape
    tri = tri_ref[...]
    carry = jnp.zeros((rows, 1), F32)
    for t in range(n // LANES):
        xt = x_ref[:, t * LANES:(t + 1) * LANES]
        hi, mid, lo = _split3(xt)
        o_ref[:, t * LANES:(t + 1) * LANES] = (_dot(hi, tri) + _dot(mid, tri) + _dot(lo, tri) + carry) * LOG2E
        carry = carry + jnp.sum(xt, axis=1, keepdims=True)


def _cumsum_lanes(x):
    rows, n = x.shape
    tri = (jnp.arange(LANES)[:, None] <= jnp.arange(LANES)[None, :]).astype(BF16)
    return pl.pallas_call(
        _cumsum_kernel,
        out_shape=jax.ShapeDtypeStruct((rows, n), F32),
        compiler_params=pltpu.CompilerParams(vmem_limit_bytes=VMEM_LIMIT),
        name="cumsum",
    )(x, tri)


def _rwkv_prep_kernel(pb_ref, prev_ref, mu_ref, w0_ref, ww_ref, a0_ref, wa_ref, wg_ref, kk_ref, ka_ref, bd_ref,
                      r_ref, kkn_ref, k_ref, b_ref, v_ref, lw_ref, g_ref, carry_ref):
    t = pl.program_id(1)
    tm = pb_ref.shape[1]

    @pl.when(t == 0)
    def _():
        carry_ref[0:1, :] = prev_ref[0]

    pbv = pb_ref[0]
    rolled = pltpu.roll(pbv, 1, 0)
    shifted = jnp.where(_iota(pbv.shape, 0) == 0, carry_ref[0:1, :], rolled)
    carry_ref[0:1, :] = pbv[tm - 1:tm, :]
    xs = pbv + (shifted - pbv) * mu_ref[...]

    r = xs[:, 0:D_B]
    kb = xs[:, D_B:2 * D_B]
    o3 = 3 * D_B
    wa_in = xs[:, o3:o3 + LANES]
    wa_in = jnp.where(_iota(wa_in.shape, 1) < W_LORA, jnp.tanh(wa_in), wa_in).astype(BF16)
    lw_ref[0] = -DECAY_SCALE * _sigmoid(w0_ref[...] + _dot(wa_in, ww_ref[...]))
    a = _sigmoid(a0_ref[...] + _dot(wa_in, wa_ref[...]))
    g_ref[0] = _dot(_sigmoid(xs[:, o3 + LANES:o3 + 2 * LANES]).astype(BF16), wg_ref[...])
    kk = kb * kk_ref[...]
    kk = kk * lax.rsqrt(_dot_sel(kk * kk, bd_ref[...]) + 1e-12)
    r_ref[0] = r
    kkn_ref[0] = kk
    k_ref[0] = kb * (1.0 + (a - 1.0) * ka_ref[...])
    b_ref[0] = kk * a
    v_ref[0] = xs[:, 2 * D_B:3 * D_B]


def _rwkv_prep(pb3, prev, lw, tm):
    b, t, _ = pb3.shape
    fixed = lambda bi, i: (0, 0)
    full = lambda a: pl.BlockSpec(a.shape, fixed)
    params = [lw['rw_mu'], lw['rw_w0'], lw['rw_ww'], lw['rw_a0'], lw['rw_wa'], lw['rw_wg'], lw['rw_k_k'],
              lw['rw_k_a'], lw['bd384']]
    out_spec = pl.BlockSpec((1, tm, D_B), lambda bi, i: (bi, i, 0))
    return pl.pallas_call(
        _rwkv_prep_kernel,
        grid=(b, t // tm),
        in_specs=[pl.BlockSpec((1, tm, B_IN), lambda bi, i: (bi, i, 0)),
                  pl.BlockSpec((1, 1, B_IN), lambda bi, i: (bi, 0, 0))] + [full(a) for a in params],
        out_specs=[out_spec] * 7,
        out_shape=[jax.ShapeDtypeStruct((b, t, D_B), F32)] * 7,
        scratch_shapes=[pltpu.VMEM((8, B_IN), F32)],
        compiler_params=_cparams(("parallel", "arbitrary")),
        name="rwkv_prep",
    )(pb3, prev, *params)


_P_A = 1
_P_INV = 1
_P_APPLY = 1
_P_GH = 1


def _rwkv_chunk_kernel(r_ref, kk_ref, k_ref, b_ref, v_ref, lw_ref, rt_ref, yl_ref, g_ref, h_ref):
    s = r_ref.shape[1]
    nc = s // CHUNK
    row = _iota((s, LANES), 0)
    rin = jnp.bitwise_and(row, CHUNK - 1)
    ti = _iota((s, s), 0)
    si = _iota((s, s), 1)
    same = jnp.right_shift(ti, 6) == jnp.right_shift(si, 6)
    strict = jnp.logical_and(same, si < ti)
    incl = jnp.logical_and(same, si <= ti)
    eye = (ti == si).astype(F32)
    low = _iota((s, LANES), 1) < HEAD_DIM
    ji = _iota((LANES, LANES), 0)
    jj = _iota((LANES, LANES), 1)
    blockdiag = (ji < HEAD_DIM) == (jj < HEAD_DIM)
    masks = (row, rin, strict, incl, eye, low, ji, jj, blockdiag)
    for p in range(r_ref.shape[2] // LANES):
        sl = slice(p * LANES, (p + 1) * LANES)
        rt2, yl2, gms, hms = _rwkv_chunk_pair(r_ref[0, :, sl], kk_ref[0, :, sl], k_ref[0, :, sl], b_ref[0, :, sl],
                                              v_ref[0, :, sl], lw_ref[0, :, sl], nc, masks)
        rt_ref[0, :, sl] = rt2
        yl_ref[0, :, sl] = yl2
        for c in range(nc):
            g_ref[0, p, c] = gms[c]
            h_ref[0, p, c] = hms[c]


def _rwkv_chunk_pair(r, kk, k, b, v, lw, nc, masks):
    row, rin, strict, incl, eye, low, ji, jj, blockdiag = masks
    s = r.shape[0]
    cl = lw
    for sh in (1, 2, 4, 8, 16, 32):
        cl = cl + jnp.where(rin >= sh, pltpu.roll(cl, sh, 0), 0.0)
    tot = jnp.concatenate(
        [jnp.broadcast_to(cl[c * CHUNK + CHUNK - 1:(c + 1) * CHUNK, :], (CHUNK, LANES)) for c in range(nc)], axis=0)
    e_ncl = jnp.exp(-cl)
    e_rem = jnp.exp(tot - cl)
    kkt = kk * jnp.exp(cl - lw)
    rt = r * jnp.exp(cl)
    khbh = jnp.concatenate([k * e_ncl, b * e_ncl], axis=0)
    kw = k * e_rem
    bw = b * e_rem

    uloc2 = kkt2 = rt2 = yl2 = None
    for x in range(2):
        mx = low if x == 0 else jnp.logical_not(low)
        kkt_x = jnp.where(mx, kkt, 0.0)
        rt_x = jnp.where(mx, rt, 0.0)
        p4 = _mm(jnp.concatenate([kkt_x, rt_x], axis=0), khbh, _P_A, _dot_nt)
        akk = jnp.where(strict, p4[:s, :s], 0.0)
        abk = jnp.where(strict, p4[:s, s:], 0.0)
        ark = jnp.where(incl, p4[s:, :s], 0.0)
        arb = jnp.where(incl, p4[s:, s:], 0.0)
        pw = -abk
        tinv = eye + pw
        for _ in range(5):
            pw = _mm(pw, pw, _P_INV)
            tinv = tinv + _mm(tinv, pw, _P_INV)
        av = _mm(jnp.concatenate([akk, ark], axis=0), v, _P_APPLY)
        tx = _mm(tinv, jnp.concatenate([av[:s], kkt_x], axis=1), _P_APPLY)
        ax = _mm(arb, tx, _P_APPLY)
        uloc, kkt_s = tx[:, :LANES], tx[:, LANES:]
        rt_s = rt_x - ax[:, LANES:]
        yl = av[s:] - ax[:, :LANES]
        if x == 0:
            uloc2, kkt2, rt2, yl2 = uloc, kkt_s, rt_s, yl
        else:
            uloc2 = jnp.where(low, uloc2, uloc)
            kkt2 = jnp.where(low, kkt2, kkt_s)
            rt2 = jnp.where(low, rt2, rt_s)
            yl2 = jnp.where(low, yl2, yl)
    gms, hms = [], []
    for c in range(nc):
        inc = jnp.right_shift(row, 6) == c
        bw_c = jnp.where(inc, bw, 0.0)
        kw_c = jnp.where(inc, kw, 0.0)
        e_tot = jnp.exp(tot[c * CHUNK:c * CHUNK + 1, :])
        gm = jnp.where(ji == jj, e_tot, 0.0) - _mm(bw_c, kkt2, _P_GH, _dot_tn)
        hm = _mm(kw_c, v, _P_GH, _dot_tn) - _mm(bw_c, uloc2, _P_GH, _dot_tn)
        gms.append(jnp.where(blockdiag, gm, 0.0))
        hms.append(jnp.where(blockdiag, hm, 0.0))
    return rt2, yl2, gms, hms


def _rwkv_chunk(r, kk, k, b, v, lw, s):
    bsz, t, _ = r.shape
    npair = H_B // 2
    nc = t // CHUNK
    tok = pl.BlockSpec((1, s, D_B), lambda bi, i: (bi, i, 0))
    mat = pl.BlockSpec((1, npair, s // CHUNK, LANES, LANES), lambda bi, i: (bi, 0, i, 0, 0))
    return pl.pallas_call(
        _rwkv_chunk_kernel,
        grid=(bsz, t // s),
        in_specs=[tok] * 6,
        out_specs=[tok, tok, mat, mat],
        out_shape=[jax.ShapeDtypeStruct((bsz, t, D_B), F32)] * 2
                  + [jax.ShapeDtypeStruct((bsz, npair, nc, LANES, LANES), F32)] * 2,
        compiler_params=_cparams(("parallel", "parallel")),
        name="rwkv_chunk",
    )(r, kk, k, b, v, lw)


def _rwkv_scan_kernel(s0_ref, g_ref, h_ref, rt_ref, yl_ref, y_ref, sfin_ref, st_scr):
    npair, ncb = g_ref.shape[1], g_ref.shape[2]

    @pl.when(pl.program_id(1) == 0)
    def _():
        st_scr[...] = s0_ref[0]

    sts = [st_scr[p] for p in range(npair)]
    for c in range(ncb):
        rows = slice(c * CHUNK, (c + 1) * CHUNK)
        for p in range(npair):
            sl = slice(p * LANES, (p + 1) * LANES)
            y_ref[0, rows, sl] = _dot3(rt_ref[0, rows, sl], sts[p]) + yl_ref[0, rows, sl]
            sts[p] = _dot3(g_ref[0, p, c], sts[p]) + h_ref[0, p, c]
    for p in range(npair):
        st_scr[p] = sts[p]
        sfin_ref[0, p] = sts[p]


def _rwkv_scan(s0, g, h, rt, yl, ncb):
    bsz, t, _ = rt.shape
    npair = H_B // 2
    nc = t // CHUNK
    st_spec = pl.BlockSpec((1, npair, LANES, LANES), lambda bi, i: (bi, 0, 0, 0))
    mat = pl.BlockSpec((1, npair, ncb, LANES, LANES), lambda bi, i: (bi, 0, i, 0, 0))
    tok = pl.BlockSpec((1, ncb * CHUNK, D_B), lambda bi, i: (bi, i, 0))
    return pl.pallas_call(
        _rwkv_scan_kernel,
        grid=(bsz, nc // ncb),
        in_specs=[st_spec, mat, mat, tok, tok],
        out_specs=[tok, st_spec],
        out_shape=[jax.ShapeDtypeStruct((bsz, t, D_B), F32), jax.ShapeDtypeStruct((bsz, npair, LANES, LANES), F32)],
        scratch_shapes=[pltpu.VMEM((npair, LANES, LANES), F32)],
        compiler_params=_cparams(("parallel", "arbitrary")),
        name="rwkv_scan",
    )(s0, g, h, rt, yl)


def _rwkv_post_kernel(y_ref, r_ref, k_ref, v_ref, g_ref, lnw_ref, lnb_ref, rk_ref, bd_ref, o_ref):
    bd = bd_ref[...]
    y = y_ref[...]
    mu = _dot_sel(y, bd) * (1.0 / HEAD_DIM)
    d = y - mu
    var = _dot_sel(d * d, bd) * (1.0 / HEAD_DIM)
    yn = d * lax.rsqrt(var + GN_EPS) * lnw_ref[...] + lnb_ref[...]
    v = v_ref[...]
    bonus = _dot_sel(r_ref[...] * k_ref[...] * rk_ref[...], bd) * v
    o_ref[...] = ((yn + bonus) * g_ref[...]).astype(BF16)


def _rwkv_post(y, r, k, v, g, lw, tm):
    n = y.shape[0]
    row = pl.BlockSpec((tm, D_B), lambda i: (i, 0))
    full = lambda a: pl.BlockSpec(a.shape, lambda i: (0, 0))
    params = [lw['rw_ln_w'], lw['rw_ln_b'], lw['rw_r_k'], lw['bd384']]
    return pl.pallas_call(
        _rwkv_post_kernel,
        grid=(n // tm,),
        in_specs=[row] * 5 + [full(a) for a in params],
        out_specs=row,
        out_shape=jax.ShapeDtypeStruct((n, D_B), BF16),
        compiler_params=_cparams(("parallel",)),
        name="rwkv_post",
    )(y, r, k, v, g, *params)


_L_EXP = 16


def _moe_kernel(x_ref, oa_ref, ob_ref, oc_ref, wo_ref, gffn_ref, wr_ref, br_ref, wgu_ref, wd_ref,
                o_ref, acc_ref, h_ref, comb_ref):
    e = pl.program_id(1)

    @pl.when(e == 0)
    def _():
        da = H_A * HEAD_DIM
        x1 = (x_ref[...] + _dot(oa_ref[...], wo_ref[0:da, :]) + _dot(ob_ref[...], wo_ref[da:da + D_B, :])
              + _dot(oc_ref[...], wo_ref[da + D_B:, :]))
        acc_ref[...] = x1
        hf = x1 * lax.rsqrt(jnp.mean(x1 * x1, axis=-1, keepdims=True) + RMS_EPS) * gffn_ref[...]
        h_ref[...] = hf.astype(BF16)

        logit = _dot3(hf, wr_ref[...]) + br_ref[...]
        lane_i = _iota(logit.shape, 1)
        lane = lane_i.astype(F32)
        big = jnp.float32(3e38)
        is_g = lane_i < N_GROUPS
        gl = jnp.where(is_g, logit, -big)
        gmax = jnp.max(gl, axis=1, keepdims=True)
        pg_top = 1.0 / jnp.sum(jnp.where(is_g, jnp.exp(gl - gmax), 0.0), axis=1, keepdims=True)
        g_idx = jnp.min(jnp.where(jnp.logical_and(is_g, gl == gmax), lane, big), axis=1, keepdims=True)
        el = lane_i - _L_EXP
        in_e = jnp.logical_and(el >= 0, el < N_EXPERTS)
        sel = jnp.logical_and(in_e, jnp.right_shift(el, 2).astype(F32) == g_idx)
        l1 = jnp.where(sel, logit, -big)
        v1 = jnp.max(l1, axis=1, keepdims=True)
        i1 = jnp.min(jnp.where(jnp.logical_and(sel, l1 == v1), lane, big), axis=1, keepdims=True)
        sel2 = jnp.logical_and(sel, lane != i1)
        l2 = jnp.where(sel2, logit, -big)
        v2 = jnp.max(l2, axis=1, keepdims=True)
        i2 = jnp.min(jnp.where(jnp.logical_and(sel2, l2 == v2), lane, big), axis=1, keepdims=True)
        e2 = jnp.exp(v2 - v1)
        den = 1.0 / (1.0 + e2)
        comb_ref[...] = (jnp.where(lane == i1, den * pg_top, 0.0) + jnp.where(lane == i2, e2 * den * pg_top, 0.0))

    gu = _dot(h_ref[...], wgu_ref[0])
    gate = gu[:, :D_FF_E]
    act = gate * _sigmoid(gate) * gu[:, D_FF_E:]
    y = _dot(act.astype(BF16), wd_ref[0])
    comb = comb_ref[...]
    ce = jnp.sum(jnp.where(_iota(comb.shape, 1) == e + _L_EXP, comb, 0.0), axis=1, keepdims=True)
    acc_ref[...] += ce * y

    @pl.when(e == N_EXPERTS - 1)
    def _():
        o_ref[...] = acc_ref[...]


def _outproj_moe(x2d, oa, ob, oc, lw, tm):
    n = x2d.shape[0]
    row = lambda w: pl.BlockSpec((tm, w), lambda i, e: (i, 0))
    full = lambda a: pl.BlockSpec(a.shape, lambda i, e: (0, 0))
    return pl.pallas_call(
        _moe_kernel,
        grid=(n // tm, N_EXPERTS),
        in_specs=[row(D_MODEL), row(H_A * HEAD_DIM), row(D_B), row(D_C), full(lw['w_out']), full(lw['g_ffn']),
                  full(lw['w_r']), full(lw['b_r']),
                  pl.BlockSpec((1, D_MODEL, 2 * D_FF_E), lambda i, e: (e, 0, 0)),
                  pl.BlockSpec((1, D_FF_E, D_MODEL), lambda i, e: (e, 0, 0))],
        out_specs=row(D_MODEL),
        out_shape=jax.ShapeDtypeStruct((n, D_MODEL), F32),
        scratch_shapes=[pltpu.VMEM((tm, D_MODEL), F32), pltpu.VMEM((tm, D_MODEL), BF16), pltpu.VMEM((tm, LANES), F32)],
        compiler_params=_cparams(("parallel", "arbitrary")),
        name="outproj_moe",
    )(x2d, oa, ob, oc, lw['w_out'], lw['g_ffn'], lw['w_r'], lw['b_r'], lw['w_gu'], lw['w_d'])


def _place(pieces, width):
    rows = pieces[0][1].shape[0]
    out = jnp.zeros((rows, width), F32)
    for off, a in pieces:
        out = out.at[:, off:off + a.shape[1]].set(a.astype(F32))
    return out


def _row(v, width=None, off=0):
    v = v.reshape(1, -1).astype(F32)
    return v if width is None else _place([(off, v)], width)


def _block_diag_ones(n, blk):
    i = jnp.arange(n) // blk
    return (i[:, None] == i[None, :]).astype(BF16)


def _layer_weights(p, l):
    g = lambda name: p[name][l]
    w_in = g('w_in')
    o_b, o_c = A_IN, A_IN + B_IN
    w_in_p = _place([(_C_QL, w_in[:, :Q_LORA]), (_C_KV, w_in[:, Q_LORA:Q_LORA + KV_LORA]),
                     (_C_KR + NOPE, w_in[:, Q_LORA + KV_LORA:A_IN]), (_C_PB, w_in[:, o_b:o_c]),
                     (_C_QC, w_in[:, o_c:o_c + 3 * D_C]), (_C_F, w_in[:, o_c + 3 * D_C:])], _C_END).astype(BF16)
    w_uq = g('mla_w_uq').reshape(Q_LORA, H_A, QK_DIM)
    w_uq = jnp.pad(w_uq, ((0, 256 - Q_LORA), (0, 0), (0, LANES - QK_DIM))).reshape(256, H_A * LANES).astype(BF16)
    w_ukv = g('mla_w_ukv').reshape(KV_LORA, H_A, NOPE + HEAD_DIM)
    w_uk = jnp.pad(w_ukv[:, :, :NOPE], ((0, 0), (0, 0), (0, LANES - NOPE))).reshape(KV_LORA, H_A * LANES).astype(BF16)
    w_uv = w_ukv[:, :, NOPE:].reshape(KV_LORA, H_A * HEAD_DIM).astype(BF16)
    zeros_w = jnp.zeros((W_LORA, D_B), F32)
    w_r = _place([(0, g('moe_w_rg')), (_L_EXP, g('moe_w_re'))], LANES)
    b_r = _place([(0, g('moe_b_rg').reshape(1, -1)), (_L_EXP, g('moe_b_re').reshape(1, -1))], LANES)
    return dict(
        g_mix=_row(g('g_mix')), w_in=w_in_p, g_qa=_row(g('mla_g_qa'), 256), w_uq=w_uq, g_kva=_row(g('mla_g_kva')),
        g_qn=_row(g('mla_g_qn'), LANES), g_fq=_row(jnp.tile(g('fox_g_qn'), H_C)),
        g_fk=_row(jnp.tile(g('fox_g_kn'), H_C)), b_f=_row(g('fox_b_f'), LANES),
        ones128=jnp.ones((LANES, LANES), BF16), eye128=jnp.eye(LANES, dtype=BF16),
        bd256=_block_diag_ones(D_C, HEAD_DIM), bd384=_block_diag_ones(D_B, HEAD_DIM),
        w_uk=w_uk, w_uv=w_uv, g_kn=_row(g('mla_g_kn')[:NOPE], LANES), g_kr=_row(g('mla_g_kn')[NOPE:], LANES, NOPE),
        rw_mu=_row(g('rw_mu')), rw_w0=_row(g('rw_w0')), rw_a0=_row(g('rw_a0')),
        rw_ww=jnp.concatenate([g('rw_w_up'), zeros_w], axis=0).astype(BF16),
        rw_wa=jnp.concatenate([zeros_w, g('rw_a_up')], axis=0).astype(BF16),
        rw_wg=g('rw_g_up').astype(BF16), rw_k_k=_row(g('rw_k_k')), rw_k_a=_row(g('rw_k_a')),
        rw_r_k=_row(g('rw_r_k')), rw_ln_w=_row(g('rw_ln_w')), rw_ln_b=_row(g('rw_ln_b')),
        w_out=g('w_out').astype(BF16), g_ffn=_row(g('g_ffn')), w_r=w_r, b_r=b_r,
        w_gu=jnp.concatenate([g('moe_w_gate'), g('moe_w_up')], axis=-1).astype(BF16),
        w_d=g('moe_w_down').astype(BF16),
    )


def _rope_tables(pos):
    half = ROPE // 2
    inv = ROPE_BASE ** (-jnp.arange(half, dtype=F32) / half)
    ang = pos.astype(F32)[:, None] * inv[None, :]
    cos, sin = jnp.cos(ang), jnp.sin(ang)
    t = pos.shape[0]
    z = lambda w: jnp.zeros((t, w), F32)
    c = jnp.concatenate([jnp.ones((t, NOPE), F32), cos, cos, z(LANES - QK_DIM)], axis=1)
    s_left = jnp.concatenate([z(NOPE), -sin, z(half), z(LANES - QK_DIM)], axis=1)
    s_right = jnp.concatenate([z(NOPE), z(half), sin, z(LANES - QK_DIM)], axis=1)
    return c, s_left, s_right


def _pick(n, prefs):
    for t in prefs:
        if n % t == 0:
            return t
    return n


def _layer(x, lw, hist):
    b, t, _ = x.shape
    n = b * t
    past = 0 if hist is None else hist['ckv'].shape[1]
    x2d = x.reshape(n, D_MODEL)
    tm = _pick(t, (512, 256, 128, 64))
    tmp = _pick(past, (512, 256, 128, 64)) if past else 0
    eye = lw['eye128']

    q_tabs = _rope_tables(past + jnp.arange(t))
    ckv, kr128, kr, qt, pb, qct, kc, kcb, vc, vct, lf = _inproj(x2d, lw, q_tabs, b, t, tm)

    k_new, vt_new = _kvprep(ckv, kr128, lw, q_tabs, b, t, tm)
    if hist is None:
        kp = vpt = None
    else:
        ckv_past = hist['ckv'].reshape(b * past, KV_LORA)
        kr_past = jnp.pad(hist['krope'].reshape(b * past, ROPE), ((0, 0), (NOPE, LANES - QK_DIM)))
        kp, vpt = _kvprep(ckv_past, kr_past, lw, _rope_tables(jnp.arange(past)), b, past, tmp)
        kp = kp.reshape(b, past, -1)
    o_a = _attention(qt, k_new.reshape(b, t, -1), vt_new, kp, vpt, None, eye, n_pairs=H_A // 2, shared=False,
                     frame_causal=False, tq=tm, tk=min(tm, ATTN_TK), tkp=min(tmp, ATTN_TK))

    lf3 = lf.reshape(b, t, H_C)
    lf_all = lf3 if hist is None else jnp.concatenate([hist['flogf'].astype(F32), lf3], axis=1)
    ltot = past + t
    lpad = -(-ltot // LANES) * LANES
    lf_t = jnp.pad(jnp.swapaxes(lf_all, 1, 2), ((0, 0), (0, 0), (0, lpad - ltot))).reshape(b * H_C, lpad)
    c_all = _cumsum_lanes(lf_t).reshape(b, H_C // 2, 2, lpad)
    cq = c_all[..., past:past + t]
    ck = jnp.swapaxes(c_all, 2, 3)
    if hist is None:
        kcp = vcpt = None
    else:
        kcp = hist['fk'].reshape(b, past, D_C)
        vcpt = _transpose_cast(hist['fv'].reshape(b * past, D_C), lw, b, past, tmp)
    o_c = _attention(qct, kcb.reshape(b, t, D_C), vct, kcp, vcpt, (cq, ck), eye, n_pairs=H_C // 2, shared=True,
                     frame_causal=True, tq=tm, tk=min(tm, ATTN_TK), tkp=min(tmp, ATTN_TK))

    prev = jnp.zeros((b, 1, B_IN), F32) if hist is None else hist['shift'].astype(F32)
    pb3 = pb.reshape(b, t, B_IN)
    r, kk, k, bb, v, lwd, g = _rwkv_prep(pb3, prev, lw, _pick(t, (256, 128, 64)))
    rt, yl, gm, hm = _rwkv_chunk(r, kk, k, bb, v, lwd, _pick(t, (256, 128, 64)))
    npair = H_B // 2
    if hist is None:
        s0 = jnp.zeros((b, npair, LANES, LANES), F32)
    else:
        st = jnp.swapaxes(hist['wkv'].astype(F32), -1, -2).reshape(b, npair, 2, HEAD_DIM, HEAD_DIM)
        s0 = jnp.zeros((b, npair, 2, HEAD_DIM, 2, HEAD_DIM), F32)
        s0 = s0.at[:, :, 0, :, 0, :].set(st[:, :, 0]).at[:, :, 1, :, 1, :].set(st[:, :, 1])
        s0 = s0.reshape(b, npair, LANES, LANES)
    y, sfin = _rwkv_scan(s0, gm, hm, rt, yl, _pick(t // CHUNK, (8, 4, 2, 1)))
    flat = lambda a: a.reshape(n, D_B)
    o_b = _rwkv_post(flat(y), flat(r), flat(k), flat(v), flat(g), lw, tm)
    sf = sfin.reshape(b, npair, 2, HEAD_DIM, 2, HEAD_DIM)
    s_fin = jnp.stack([sf[:, :, 0, :, 0, :], sf[:, :, 1, :, 1, :]], axis=2).reshape(b, H_B, HEAD_DIM, HEAD_DIM)
    s_fin = jnp.swapaxes(s_fin, -1, -2)

    tmm = _pick(n, (1024, 512, 256, 128, 64))
    x_out = _outproj_moe(x2d, o_a.reshape(n, -1), flat(o_b), o_c.reshape(n, -1), lw, tmm).reshape(b, t, D_MODEL)

    new = (ckv.reshape(b, t, KV_LORA), kr.reshape(b, t, ROPE), kc.reshape(b, t, H_C, HEAD_DIM),
           vc.reshape(b, t, H_C, HEAD_DIM), lf3, s_fin, pb3[:, -1:])
    return x_out, new


def kernel(x_prompt, x_sample, cache_mla_latent, cache_mla_krope, cache_fox_k, cache_fox_v, cache_fox_logf,
           state_rwkv_wkv, state_rwkv_shift, g_mix, w_in, mla_g_qa, mla_w_uq, mla_g_kva, mla_w_ukv, mla_g_qn,
           mla_g_kn, rw_mu, rw_w0, rw_w_up, rw_a0, rw_a_up, rw_g_up, rw_k_k, rw_k_a, rw_r_k, rw_ln_w, rw_ln_b,
           fox_g_qn, fox_g_kn, fox_b_f, w_out, g_ffn, moe_w_rg, moe_b_rg, moe_w_re, moe_b_re, moe_w_gate,
           moe_w_up, moe_w_down):
    params = dict(g_mix=g_mix, w_in=w_in, mla_g_qa=mla_g_qa, mla_w_uq=mla_w_uq, mla_g_kva=mla_g_kva,
                  mla_w_ukv=mla_w_ukv, mla_g_qn=mla_g_qn, mla_g_kn=mla_g_kn, rw_mu=rw_mu, rw_w0=rw_w0,
                  rw_w_up=rw_w_up, rw_a0=rw_a0, rw_a_up=rw_a_up, rw_g_up=rw_g_up, rw_k_k=rw_k_k, rw_k_a=rw_k_a,
                  rw_r_k=rw_r_k, rw_ln_w=rw_ln_w, rw_ln_b=rw_ln_b, fox_g_qn=fox_g_qn, fox_g_kn=fox_g_kn,
                  fox_b_f=fox_b_f, w_out=w_out, g_ffn=g_ffn, moe_w_rg=moe_w_rg, moe_b_rg=moe_b_rg,
                  moe_w_re=moe_w_re, moe_b_re=moe_b_re, moe_w_gate=moe_w_gate, moe_w_up=moe_w_up,
                  moe_w_down=moe_w_down)
    depth = g_mix.shape[0]
    yp, ys = x_prompt, x_sample
    p_new, s_new = [], []
    for l in range(depth):
        lw = _layer_weights(params, l)
        hist = dict(ckv=cache_mla_latent[l], krope=cache_mla_krope[l], fk=cache_fox_k[l], fv=cache_fox_v[l],
                    flogf=cache_fox_logf[l], wkv=state_rwkv_wkv[l], shift=state_rwkv_shift[l])
        yp, np_l = _layer(yp, lw, None)
        ys, ns_l = _layer(ys, lw, hist)
        p_new.append(np_l)
        s_new.append(ns_l)
    p_out = tuple(jnp.stack(t) for t in zip(*p_new))
    s_out = tuple(jnp.stack(t) for t in zip(*s_new))
    return (yp, ys) + p_out + s_out
```

```python
import functools
import math

import jax
import jax.numpy as jnp
from jax import lax
from jax.experimental import pallas as pl
from jax.experimental.pallas import tpu as pltpu

F32 = jnp.float32
BF16 = jnp.bfloat16

D_MODEL = 1024
HEAD_DIM = 64
H_A, H_B, H_C = 6, 6, 4
Q_LORA, KV_LORA, NOPE, ROPE = 192, 128, 64, 32
QK_DIM = NOPE + ROPE
ROPE_BASE = 10000.0
A_IN = Q_LORA + KV_LORA + ROPE
D_B = H_B * HEAD_DIM
W_LORA, A_LORA, G_LORA = 64, 64, 128
B_IN = 3 * D_B + W_LORA + A_LORA + G_LORA
DECAY_SCALE = math.exp(-0.5)
GN_EPS = 64e-5
D_C = H_C * HEAD_DIM
C_IN = 3 * D_C + H_C
N_GROUPS, E_PER_GROUP = 4, 4
N_EXPERTS = N_GROUPS * E_PER_GROUP
D_FF_E = 256
NEG_INF = -1e30
RMS_EPS = 1e-6
CHUNK = 64
LOG2E = math.log2(math.e)

LANES = 128
VMEM_LIMIT = 56 * 1024 * 1024
ATTN_TK = 512

_C_QL = 0
_C_KV = 256
_C_KR = 384
_C_PB = 512
_C_QC = _C_PB + B_IN
_C_KC = _C_QC + D_C
_C_VC = _C_KC + D_C
_C_F = _C_VC + D_C
_C_END = _C_F + LANES


def _cparams(sem):
    return pltpu.CompilerParams(dimension_semantics=sem, vmem_limit_bytes=VMEM_LIMIT)


def _dot(a, b):
    return jnp.dot(a, b, preferred_element_type=F32)


def _dot_nt(a, b):
    return lax.dot_general(a, b, (((1,), (1,)), ((), ())), preferred_element_type=F32)


def _dot_tn(a, b):
    return lax.dot_general(a, b, (((0,), (0,)), ((), ())), preferred_element_type=F32)


def _split2(x):
    hi = x.astype(BF16)
    lo = (x - hi.astype(F32)).astype(BF16)
    return hi, lo


def _split3(x):
    hi = x.astype(BF16)
    r = x - hi.astype(F32)
    mid = r.astype(BF16)
    lo = (r - mid.astype(F32)).astype(BF16)
    return hi, mid, lo


def _dot3(a, b, dot=_dot):
    ah, al = _split2(a)
    bh, bl = _split2(b)
    return dot(ah, bh) + (dot(ah, bl) + dot(al, bh))


def _mm(a, b, passes, dot=_dot):
    if passes == 1:
        return dot(a.astype(BF16), b.astype(BF16))
    return _dot3(a, b, dot)


def _dot_sel(a, sel):
    return _dot(a.astype(BF16), sel)


def _iota(shape, dim):
    return lax.broadcasted_iota(jnp.int32, shape, dim)


def _rope(x, c, s_left, s_right):
    return x * c + pltpu.roll(x, 112, 1) * s_left + pltpu.roll(x, 16, 1) * s_right


def _sigmoid(x):
    return 1.0 / (1.0 + jnp.exp(-x))


def _log_sigmoid(x):
    return jnp.minimum(x, 0.0) - jnp.log(1.0 + jnp.exp(-jnp.abs(x)))


def _transpose_bf16(x, eye):
    return _dot_nt(eye, x.astype(BF16)).astype(BF16)


def _inproj_kernel(x_ref, gmix_ref, w_ref, gqa_ref, wuq_ref, gkva_ref, gqn_ref, gfq_ref, gfk_ref, bf_ref,
                   ones_ref, bd_ref, eye_ref, c_ref, sl_ref, sr_ref,
                   ckv_ref, kr128_ref, kr_ref, qt_ref, pb_ref, qct_ref, kc_ref, kcb_ref, vc_ref, vct_ref, lf_ref):
    x = x_ref[...]
    h = x * lax.rsqrt(jnp.mean(x * x, axis=-1, keepdims=True) + RMS_EPS) * gmix_ref[...]
    h = h.astype(BF16)
    eye = eye_ref[...]

    ql = _dot(h, w_ref[:, _C_QL:_C_QL + 256])
    ql = ql * lax.rsqrt(jnp.sum(ql * ql, axis=-1, keepdims=True) * (1.0 / Q_LORA) + RMS_EPS) * gqa_ref[...]
    qh = _dot(ql.astype(BF16), wuq_ref[...])
    c, s_l, s_r = c_ref[...], sl_ref[...], sr_ref[...]
    ones = ones_ref[...]
    for hh in range(H_A):
        qv = qh[:, hh * LANES:(hh + 1) * LANES]
        ss = _dot_sel(qv * qv, ones)
        qn = qv * lax.rsqrt(ss * (1.0 / QK_DIM) + RMS_EPS) * gqn_ref[...]
        qt_ref[0, hh * LANES:(hh + 1) * LANES, :] = _transpose_bf16(
            _rope(qn, c, s_l, s_r) * (LOG2E * QK_DIM ** -0.5), eye)

    kv = _dot(h, w_ref[:, _C_KV:_C_KV + KV_LORA])
    ckv_ref[...] = kv * lax.rsqrt(jnp.mean(kv * kv, axis=-1, keepdims=True) + RMS_EPS) * gkva_ref[...]
    kr = _dot(h, w_ref[:, _C_KR:_C_KR + LANES])
    kr128_ref[...] = kr
    kr_ref[...] = kr[:, NOPE:NOPE + ROPE]

    pb_ref[...] = _dot(h, w_ref[:, _C_PB:_C_PB + B_IN])

    bd = bd_ref[...]
    qc = _dot(h, w_ref[:, _C_QC:_C_QC + D_C])
    qc = qc * lax.rsqrt(_dot_sel(qc * qc, bd) * (1.0 / HEAD_DIM) + RMS_EPS) * gfq_ref[...]
    qc = qc * (LOG2E * HEAD_DIM ** -0.5)
    kc = _dot(h, w_ref[:, _C_KC:_C_KC + D_C])
    kc = kc * lax.rsqrt(_dot_sel(kc * kc, bd) * (1.0 / HEAD_DIM) + RMS_EPS) * gfk_ref[...]
    kc_ref[...] = kc
    kcb_ref[...] = kc.astype(BF16)
    vc = _dot(h, w_ref[:, _C_VC:_C_VC + D_C])
    vc_ref[...] = vc
    for pp in range(D_C // LANES):
        qct_ref[0, pp * LANES:(pp + 1) * LANES, :] = _transpose_bf16(qc[:, pp * LANES:(pp + 1) * LANES], eye)
        vct_ref[0, pp * LANES:(pp + 1) * LANES, :] = _transpose_bf16(vc[:, pp * LANES:(pp + 1) * LANES], eye)
    f = _dot(h, w_ref[:, _C_F:_C_F + LANES]) + bf_ref[...]
    lf_ref[...] = _log_sigmoid(f)[:, :H_C]


def _inproj(x2d, lw, tabs, bsz, t, tm):
    n = x2d.shape[0]
    nt = t // tm
    row = lambda i: (i, 0)
    fixed = lambda i: (0, 0)
    tab = lambda i: (i % nt, 0)
    colmajor = lambda i: (i // nt, 0, i % nt)
    full = lambda a: pl.BlockSpec(a.shape, fixed)
    params = [lw['g_mix'], lw['w_in'], lw['g_qa'], lw['w_uq'], lw['g_kva'], lw['g_qn'], lw['g_fq'], lw['g_fk'],
              lw['b_f'], lw['ones128'], lw['bd256'], lw['eye128']]
    rowout = lambda w, dt: (pl.BlockSpec((tm, w), row), jax.ShapeDtypeStruct((n, w), dt))
    colout = lambda w: (pl.BlockSpec((1, w, tm), colmajor), jax.ShapeDtypeStruct((bsz, w, t), BF16))
    outs = [rowout(KV_LORA, F32), rowout(LANES, F32), rowout(ROPE, F32), colout(H_A * LANES), rowout(B_IN, F32),
            colout(D_C), rowout(D_C, F32), rowout(D_C, BF16), rowout(D_C, F32), colout(D_C), rowout(H_C, F32)]
    return pl.pallas_call(
        _inproj_kernel,
        grid=(n // tm,),
        in_specs=[pl.BlockSpec((tm, D_MODEL), row)] + [full(a) for a in params]
                 + [pl.BlockSpec((tm, LANES), tab)] * 3,
        out_specs=[o[0] for o in outs],
        out_shape=[o[1] for o in outs],
        compiler_params=_cparams(("parallel",)),
        name="inproj",
    )(x2d, *params, *tabs)


def _kvprep_kernel(ckv_ref, kr_ref, wuk_ref, wuv_ref, gkn_ref, gkr_ref, ones_ref, eye_ref, c_ref, sl_ref, sr_ref,
                   k_ref, vt_ref):
    cb = ckv_ref[...].astype(BF16)
    kn = _dot(cb, wuk_ref[...])
    v = _dot(cb, wuv_ref[...])
    eye = eye_ref[...]
    for pp in range(H_A // 2):
        vt_ref[0, pp * LANES:(pp + 1) * LANES, :] = _transpose_bf16(v[:, pp * LANES:(pp + 1) * LANES], eye)
    ones = ones_ref[...]
    kr = kr_ref[...]
    ssr = _dot_sel(kr * kr, ones)
    krg = _rope(kr * gkr_ref[...], c_ref[...], sl_ref[...], sr_ref[...])
    for hh in range(H_A):
        knh = kn[:, hh * LANES:(hh + 1) * LANES]
        ssn = _dot_sel(knh * knh, ones)
        r = lax.rsqrt((ssn + ssr) * (1.0 / QK_DIM) + RMS_EPS)
        k_ref[:, hh * LANES:(hh + 1) * LANES] = ((knh * gkn_ref[...] + krg) * r).astype(BF16)


def _kvprep(ckv2d, kr128_2d, lw, tabs, bsz, t, tm):
    n = ckv2d.shape[0]
    nt = t // tm
    row = lambda i: (i, 0)
    fixed = lambda i: (0, 0)
    tab = lambda i: (i % nt, 0)
    full = lambda a: pl.BlockSpec(a.shape, fixed)
    params = [lw['w_uk'], lw['w_uv'], lw['g_kn'], lw['g_kr'], lw['ones128'], lw['eye128']]
    return pl.pallas_call(
        _kvprep_kernel,
        grid=(n // tm,),
        in_specs=[pl.BlockSpec((tm, KV_LORA), row), pl.BlockSpec((tm, LANES), row)] + [full(a) for a in params]
                 + [pl.BlockSpec((tm, LANES), tab)] * 3,
        out_specs=[pl.BlockSpec((tm, H_A * LANES), row),
                   pl.BlockSpec((1, H_A * HEAD_DIM, tm), lambda i: (i // nt, 0, i % nt))],
        out_shape=[jax.ShapeDtypeStruct((n, H_A * LANES), BF16),
                   jax.ShapeDtypeStruct((bsz, H_A * HEAD_DIM, t), BF16)],
        compiler_params=_cparams(("parallel",)),
        name="kvprep",
    )(ckv2d, kr128_2d, *params, *tabs)


def _transpose_kernel(x_ref, eye_ref, o_ref):
    eye = eye_ref[...]
    for pp in range(x_ref.shape[1] // LANES):
        o_ref[0, pp * LANES:(pp + 1) * LANES, :] = _transpose_bf16(x_ref[:, pp * LANES:(pp + 1) * LANES], eye)


def _transpose_cast(x2d, lw, bsz, t, tm):
    n, w = x2d.shape
    nt = t // tm
    return pl.pallas_call(
        _transpose_kernel,
        grid=(n // tm,),
        in_specs=[pl.BlockSpec((tm, w), lambda i: (i, 0)), pl.BlockSpec((LANES, LANES), lambda i: (0, 0))],
        out_specs=pl.BlockSpec((1, w, tm), lambda i: (i // nt, 0, i % nt)),
        out_shape=jax.ShapeDtypeStruct((bsz, w, t), BF16),
        compiler_params=_cparams(("parallel",)),
        name="transpose_cast",
    )(x2d, lw['eye128'])


def _attn_kernel(*refs, tq, tk, tkp, nq, n_past, n_pairs, shared, frame_causal, has_bias, pipelined):
    it = iter(refs)
    q_ref, k_ref, vt_ref = next(it), next(it), next(it)
    if n_past:
        kp_ref, vpt_ref = next(it), next(it)
    if has_bias:
        cq_ref, ck_ref = next(it), next(it)
    eye_ref, o_ref = next(it), next(it)
    q_scr, m_scr, l_scr, acc_scr = next(it), next(it), next(it), next(it)

    qi = 0 if nq == 1 else pl.program_id(1)
    low = _iota((LANES, tq), 0) < HEAD_DIM
    blk = lambda p, j: p if shared else 2 * p + j
    for p in range(n_pairs):
        for j in range(2):
            q = q_ref[0, blk(p, j) * LANES:(blk(p, j) + 1) * LANES, :]
            if shared:
                keep = low if j == 0 else jnp.logical_not(low)
                q = jnp.where(keep, q, jnp.zeros_like(q))
            q_scr[2 * p + j] = q
    m_scr[...] = jnp.full(m_scr.shape, -jnp.inf, F32)
    l_scr[...] = jnp.zeros(l_scr.shape, F32)
    acc_scr[...] = jnp.zeros(acc_scr.shape, F32)

    def score(kr, st, width, p, j):
        kt = kr[0, pl.ds(st, width), blk(p, j) * LANES:(blk(p, j) + 1) * LANES].astype(BF16)
        return _dot(kt, q_scr[2 * p + j])

    def scores_to(buf, kr, st, width):
        for p in range(n_pairs):
            for j in range(2):
                buf[2 * p + j] = score(kr, st, width, p, j)

    def consume(get_s, vr, st, width, key_start, mask):
        for p in range(n_pairs):
            vt = vr[0, p * LANES:(p + 1) * LANES, pl.ds(st, width)]
            for j in range(2):
                h = 2 * p + j
                s = get_s(p, j)
                if has_bias:
                    s = s + (cq_ref[0, p, j:j + 1, :] - ck_ref[0, p, pl.ds(key_start, width), j:j + 1])
                if mask is not None:
                    s = jnp.where(mask, s, NEG_INF)
                m_prev = m_scr[h]
                m_next = jnp.maximum(m_prev, jnp.max(s, axis=0, keepdims=True))
                pr = jnp.exp2(s - m_next)
                alpha = jnp.exp2(m_prev - m_next)
                l_scr[h] = alpha * l_scr[h] + jnp.sum(pr, axis=0, keepdims=True)
                acc_scr[h] = acc_scr[h] * alpha + _dot(vt, pr.astype(BF16))
                m_scr[h] = m_next

    def step(kr, vr, st, width, key_start, mask):
        consume(lambda p, j: score(kr, st, width, p, j), vr, st, width, key_start, mask)

    kidx = _iota((tk, tq), 0)
    qidx = _iota((tk, tq), 1)
    diag_mask = lambda d: ((kidx + d * tk <= qidx) if frame_causal else
                           (jnp.right_shift(kidx + d * tk, 6) <= jnp.right_shift(qidx, 6)))

    if pipelined:
        sa, sb = next(it), next(it)
        tile = lambda t: pl.multiple_of(t * tk, tk)
        from_a = lambda p, j: sa[2 * p + j]
        from_b = lambda p, j: sb[2 * p + j]
        scores_to(sa, k_ref, tile(0), tk)

        def pair_body(u, carry):
            t0 = 2 * u
            scores_to(sb, k_ref, tile(t0 + 1), tk)
            consume(from_a, vt_ref, tile(t0), tk, tile(t0), None)
            scores_to(sa, k_ref, tile(t0 + 2), tk)
            consume(from_b, vt_ref, tile(t0 + 1), tk, tile(t0 + 1), None)
            return carry
        lax.fori_loop(0, qi // 2, pair_body, 0)
        odd = lax.rem(qi, 2) == 1

        @pl.when(jnp.logical_not(odd))
        def _():
            consume(from_a, vt_ref, tile(qi), tk, tile(qi), diag_mask(0))

        @pl.when(odd)
        def _():
            scores_to(sb, k_ref, tile(qi), tk)
            consume(from_a, vt_ref, tile(qi - 1), tk, tile(qi - 1), None)
            consume(from_b, vt_ref, tile(qi), tk, tile(qi), diag_mask(0))
    else:
        if n_past:
            def past_body(t, carry):
                st = pl.multiple_of(t * tkp, tkp)
                step(kp_ref, vpt_ref, st, tkp, st, None)
                return carry
            lax.fori_loop(0, n_past // tkp, past_body, 0)

        def new_body(t, carry):
            st = pl.multiple_of(t * tk, tk)
            step(k_ref, vt_ref, st, tk, n_past + st, None)
            return carry
        if nq > 1:
            lax.fori_loop(0, qi * (tq // tk), new_body, 0)

        for d in range(tq // tk):
            st = qi * tq + d * tk
            st = st if nq == 1 else pl.multiple_of(st, tk)
            step(k_ref, vt_ref, st, tk, n_past + st, diag_mask(d))

    for p in range(n_pairs):
        ot = jnp.where(low, acc_scr[2 * p] / l_scr[2 * p], acc_scr[2 * p + 1] / l_scr[2 * p + 1])
        o_ref[0, :, p * LANES:(p + 1) * LANES] = _dot_tn(ot.astype(BF16), eye_ref[...]).astype(o_ref.dtype)


def _attention(qt, k_new, vt_new, k_past, vt_past, bias, eye, *, n_pairs, shared, frame_causal, tq, tk, tkp):
    b, t = k_new.shape[0], k_new.shape[1]
    n_past = 0 if k_past is None else k_past.shape[1]
    nq = t // tq
    wq = qt.shape[1]
    args = [qt, k_new, vt_new]
    specs = [pl.BlockSpec((1, wq, tq), lambda bi, i: (bi, 0, i)),
             pl.BlockSpec((1, t, wq), lambda bi, i: (bi, 0, 0)),
             pl.BlockSpec((1, n_pairs * LANES, t), lambda bi, i: (bi, 0, 0))]
    if n_past:
        args += [k_past, vt_past]
        specs += [pl.BlockSpec((1, n_past, wq), lambda bi, i: (bi, 0, 0)),
                  pl.BlockSpec((1, n_pairs * LANES, n_past), lambda bi, i: (bi, 0, 0))]
    if bias is not None:
        cq, ck = bias
        args += [cq, ck]
        specs += [pl.BlockSpec((1, n_pairs, 2, tq), lambda bi, i: (bi, 0, 0, i)),
                  pl.BlockSpec((1, n_pairs, ck.shape[2], 2), lambda bi, i: (bi, 0, 0, 0))]
    args.append(eye)
    specs.append(pl.BlockSpec((LANES, LANES), lambda bi, i: (0, 0)))
    pipelined = n_past == 0 and nq > 1 and tq == tk
    kern = functools.partial(_attn_kernel, tq=tq, tk=tk, tkp=tkp, nq=nq, n_past=n_past, n_pairs=n_pairs,
                             shared=shared, frame_causal=frame_causal, has_bias=bias is not None,
                             pipelined=pipelined)
    nh = 2 * n_pairs
    score_bufs = [pltpu.VMEM((nh, tk, tq), F32)] * 2 if pipelined else []
    return pl.pallas_call(
        kern,
        grid=(b, nq),
        in_specs=specs,
        out_specs=pl.BlockSpec((1, tq, n_pairs * LANES), lambda bi, i: (bi, i, 0)),
        out_shape=jax.ShapeDtypeStruct((b, t, n_pairs * LANES), BF16),
        scratch_shapes=[pltpu.VMEM((nh, LANES, tq), BF16), pltpu.VMEM((nh, 1, tq), F32),
                        pltpu.VMEM((nh, 1, tq), F32), pltpu.VMEM((nh, LANES, tq), F32)] + score_bufs,
        compiler_params=_cparams(("parallel", "arbitrary")),
        name="attn_fox" if shared else "attn_mla",
    )(*args)


def _attn_kernel_pair(*refs, tq, tk, tkp, nq, n_past, shared, frame_causal, has_bias):
    it = iter(refs)
    qa_ref = next(it)
    qb_ref = qa_ref if shared else next(it)
    ka_ref = next(it)
    kb_ref = ka_ref if shared else next(it)
    vt_ref = next(it)
    if n_past:
        kpa_ref = next(it)
        kpb_ref = kpa_ref if shared else next(it)
        vpt_ref = next(it)
    if has_bias:
        cq_ref = next(it)
        ck_ref = next(it)
    eye_ref = next(it)
    o_ref = next(it)
    m_scr, l_scr, acc_scr = next(it), next(it), next(it)

    qi = 0 if nq == 1 else pl.program_id(2)
    low = _iota((LANES, tq), 0) < HEAD_DIM
    if shared:
        q2 = qa_ref[0]
        qs = (jnp.where(low, q2, jnp.zeros_like(q2)), jnp.where(low, jnp.zeros_like(q2), q2))
    else:
        qs = (qa_ref[0], qb_ref[0])

    m_scr[...] = jnp.full(m_scr.shape, -jnp.inf, F32)
    l_scr[...] = jnp.zeros(l_scr.shape, F32)
    acc_scr[...] = jnp.zeros(acc_scr.shape, F32)

    def step(k_tiles, vt, key_start, width, mask):
        for j in range(2):
            s = _dot(k_tiles[j].astype(BF16), qs[j])
            if has_bias:
                s = s + (cq_ref[0, 0, j:j + 1, :] - ck_ref[0, 0, pl.ds(key_start, width), j:j + 1])
            if mask is not None:
                s = jnp.where(mask, s, NEG_INF)
            m_prev = m_scr[j]
            m_next = jnp.maximum(m_prev, jnp.max(s, axis=0, keepdims=True))
            p = jnp.exp2(s - m_next)
            alpha = jnp.exp2(m_prev - m_next)
            l_scr[j] = alpha * l_scr[j] + jnp.sum(p, axis=0, keepdims=True)
            acc_scr[j] = acc_scr[j] * alpha + _dot(vt, p.astype(BF16))
            m_scr[j] = m_next

    if n_past:
        def past_body(t, carry):
            st = pl.multiple_of(t * tkp, tkp)
            step((kpa_ref[0, pl.ds(st, tkp), :], kpb_ref[0, pl.ds(st, tkp), :]), vpt_ref[0, :, pl.ds(st, tkp)],
                 st, tkp, None)
            return carry
        lax.fori_loop(0, n_past // tkp, past_body, 0)

    def new_body(t, carry):
        st = pl.multiple_of(t * tk, tk)
        step((ka_ref[0, pl.ds(st, tk), :], kb_ref[0, pl.ds(st, tk), :]), vt_ref[0, :, pl.ds(st, tk)],
             n_past + st, tk, None)
        return carry
    if nq > 1:
        lax.fori_loop(0, qi * (tq // tk), new_body, 0)

    kidx = _iota((tk, tq), 0)
    qidx = _iota((tk, tq), 1)
    for d in range(tq // tk):
        kk = kidx + d * tk
        mask = (kk <= qidx) if frame_causal else (jnp.right_shift(kk, 6) <= jnp.right_shift(qidx, 6))
        st = qi * tq + d * tk
        st = st if nq == 1 else pl.multiple_of(st, tk)
        step((ka_ref[0, pl.ds(st, tk), :], kb_ref[0, pl.ds(st, tk), :]), vt_ref[0, :, pl.ds(st, tk)],
             n_past + st, tk, mask)

    ot = jnp.where(low, acc_scr[0] / l_scr[0], acc_scr[1] / l_scr[1])
    o_ref[0] = _dot_tn(ot.astype(BF16), eye_ref[...]).astype(o_ref.dtype)


def _attention_pair(qt, k_new, vt_new, k_past, vt_past, bias, eye, *, n_pairs, shared, frame_causal, tq, tk, tkp):
    b, t = k_new.shape[0], k_new.shape[1]
    n_past = 0 if k_past is None else k_past.shape[1]
    nq = t // tq
    qblk = lambda off: pl.BlockSpec((1, LANES, tq), lambda bi, p, i, off=off: (bi, 2 * p + off, i))
    kblk = lambda rows, off: pl.BlockSpec((1, rows, LANES), lambda bi, p, i, off=off: (bi, 0, 2 * p + off))
    pair_q = pl.BlockSpec((1, LANES, tq), lambda bi, p, i: (bi, p, i))
    pair_k = lambda rows: pl.BlockSpec((1, rows, LANES), lambda bi, p, i: (bi, 0, p))
    pair_vt = lambda cols: pl.BlockSpec((1, LANES, cols), lambda bi, p, i: (bi, p, 0))
    args, specs = [], []
    if shared:
        args += [qt, k_new]
        specs += [pair_q, pair_k(t)]
    else:
        args += [qt, qt, k_new, k_new]
        specs += [qblk(0), qblk(1), kblk(t, 0), kblk(t, 1)]
    args.append(vt_new)
    specs.append(pair_vt(t))
    if n_past:
        if shared:
            args.append(k_past)
            specs.append(pair_k(n_past))
        else:
            args += [k_past, k_past]
            specs += [kblk(n_past, 0), kblk(n_past, 1)]
        args.append(vt_past)
        specs.append(pair_vt(n_past))
    if bias is not None:
        cq, ck = bias
        args += [cq, ck]
        specs += [pl.BlockSpec((1, 1, 2, tq), lambda bi, p, i: (bi, p, 0, i)),
                  pl.BlockSpec((1, 1, ck.shape[2], 2), lambda bi, p, i: (bi, p, 0, 0))]
    args.append(eye)
    specs.append(pl.BlockSpec((LANES, LANES), lambda bi, p, i: (0, 0)))
    kern = functools.partial(_attn_kernel, tq=tq, tk=tk, tkp=tkp, nq=nq, n_past=n_past, shared=shared,
                             frame_causal=frame_causal, has_bias=bias is not None)
    return pl.pallas_call(
        kern,
        grid=(b, n_pairs, nq),
        in_specs=specs,
        out_specs=pl.BlockSpec((1, tq, LANES), lambda bi, p, i: (bi, i, p)),
        out_shape=jax.ShapeDtypeStruct((b, t, n_pairs * LANES), BF16),
        scratch_shapes=[pltpu.VMEM((2, 1, tq), F32), pltpu.VMEM((2, 1, tq), F32), pltpu.VMEM((2, LANES, tq), F32)],
        compiler_params=_cparams(("parallel", "parallel", "arbitrary")),
        name="attn_fox" if shared else "attn_mla",
    )(*args)


def _cumsum_kernel(x_ref, tri_ref, o_ref):
    rows, n = x_ref.shape
    tri = tri_ref[...]
    carry = jnp.zeros((rows, 1), F32)
    for t in range(n // LANES):
        xt = x_ref[:, t * LANES:(t + 1) * LANES]
        hi, mid, lo = _split3(xt)
        o_ref[:, t * LANES:(t + 1) * LANES] = (_dot(hi, tri) + _dot(mid, tri) + _dot(lo, tri) + carry) * LOG2E
        carry = carry + jnp.sum(xt, axis=1, keepdims=True)


def _cumsum_lanes(x):
    rows, n = x.shape
    tri = (jnp.arange(LANES)[:, None] <= jnp.arange(LANES)[None, :]).astype(BF16)
    return pl.pallas_call(
        _cumsum_kernel,
        out_shape=jax.ShapeDtypeStruct((rows, n), F32),
        compiler_params=pltpu.CompilerParams(vmem_limit_bytes=VMEM_LIMIT),
        name="cumsum",
    )(x, tri)


def _rwkv_prep_kernel(pb_ref, prev_ref, mu_ref, w0_ref, ww_ref, a0_ref, wa_ref, wg_ref, kk_ref, ka_ref, bd_ref,
                      r_ref, kkn_ref, k_ref, b_ref, v_ref, lw_ref, g_ref, carry_ref):
    t = pl.program_id(1)
    tm = pb_ref.shape[1]

    @pl.when(t == 0)
    def _():
        carry_ref[0:1, :] = prev_ref[0]

    pbv = pb_ref[0]
    rolled = pltpu.roll(pbv, 1, 0)
    shifted = jnp.where(_iota(pbv.shape, 0) == 0, carry_ref[0:1, :], rolled)
    carry_ref[0:1, :] = pbv[tm - 1:tm, :]
    xs = pbv + (shifted - pbv) * mu_ref[...]

    r = xs[:, 0:D_B]
    kb = xs[:, D_B:2 * D_B]
    o3 = 3 * D_B
    wa_in = xs[:, o3:o3 + LANES]
    wa_in = jnp.where(_iota(wa_in.shape, 1) < W_LORA, jnp.tanh(wa_in), wa_in).astype(BF16)
    lw_ref[0] = -DECAY_SCALE * _sigmoid(w0_ref[...] + _dot(wa_in, ww_ref[...]))
    a = _sigmoid(a0_ref[...] + _dot(wa_in, wa_ref[...]))
    g_ref[0] = _dot(_sigmoid(xs[:, o3 + LANES:o3 + 2 * LANES]).astype(BF16), wg_ref[...])
    kk = kb * kk_ref[...]
    kk = kk * lax.rsqrt(_dot_sel(kk * kk, bd_ref[...]) + 1e-12)
    r_ref[0] = r
    kkn_ref[0] = kk
    k_ref[0] = kb * (1.0 + (a - 1.0) * ka_ref[...])
    b_ref[0] = kk * a
    v_ref[0] = xs[:, 2 * D_B:3 * D_B]


def _rwkv_prep(pb3, prev, lw, tm):
    b, t, _ = pb3.shape
    fixed = lambda bi, i: (0, 0)
    full = lambda a: pl.BlockSpec(a.shape, fixed)
    params = [lw['rw_mu'], lw['rw_w0'], lw['rw_ww'], lw['rw_a0'], lw['rw_wa'], lw['rw_wg'], lw['rw_k_k'],
              lw['rw_k_a'], lw['bd384']]
    out_spec = pl.BlockSpec((1, tm, D_B), lambda bi, i: (bi, i, 0))
    return pl.pallas_call(
        _rwkv_prep_kernel,
        grid=(b, t // tm),
        in_specs=[pl.BlockSpec((1, tm, B_IN), lambda bi, i: (bi, i, 0)),
                  pl.BlockSpec((1, 1, B_IN), lambda bi, i: (bi, 0, 0))] + [full(a) for a in params],
        out_specs=[out_spec] * 7,
        out_shape=[jax.ShapeDtypeStruct((b, t, D_B), F32)] * 7,
        scratch_shapes=[pltpu.VMEM((8, B_IN), F32)],
        compiler_params=_cparams(("parallel", "arbitrary")),
        name="rwkv_prep",
    )(pb3, prev, *params)


_P_A = 1
_P_INV = 1
_P_APPLY = 1
_P_GH = 1


def _rwkv_chunk_kernel(r_ref, kk_ref, k_ref, b_ref, v_ref, lw_ref, rt_ref, yl_ref, g_ref, h_ref):
    s = r_ref.shape[1]
    nc = s // CHUNK
    row = _iota((s, LANES), 0)
    rin = jnp.bitwise_and(row, CHUNK - 1)
    ti = _iota((s, s), 0)
    si = _iota((s, s), 1)
    same = jnp.right_shift(ti, 6) == jnp.right_shift(si, 6)
    strict = jnp.logical_and(same, si < ti)
    incl = jnp.logical_and(same, si <= ti)
    eye = (ti == si).astype(F32)
    low = _iota((s, LANES), 1) < HEAD_DIM
    ji = _iota((LANES, LANES), 0)
    jj = _iota((LANES, LANES), 1)
    blockdiag = (ji < HEAD_DIM) == (jj < HEAD_DIM)
    masks = (row, rin, strict, incl, eye, low, ji, jj, blockdiag)
    for p in range(r_ref.shape[2] // LANES):
        sl = slice(p * LANES, (p + 1) * LANES)
        rt2, yl2, gms, hms = _rwkv_chunk_pair(r_ref[0, :, sl], kk_ref[0, :, sl], k_ref[0, :, sl], b_ref[0, :, sl],
                                              v_ref[0, :, sl], lw_ref[0, :, sl], nc, masks)
        rt_ref[0, :, sl] = rt2
        yl_ref[0, :, sl] = yl2
        for c in range(nc):
            g_ref[0, p, c] = gms[c]
            h_ref[0, p, c] = hms[c]


def _rwkv_chunk_pair(r, kk, k, b, v, lw, nc, masks):
    row, rin, strict, incl, eye, low, ji, jj, blockdiag = masks
    s = r.shape[0]
    cl = lw
    for sh in (1, 2, 4, 8, 16, 32):
        cl = cl + jnp.where(rin >= sh, pltpu.roll(cl, sh, 0), 0.0)
    tot = jnp.concatenate(
        [jnp.broadcast_to(cl[c * CHUNK + CHUNK - 1:(c + 1) * CHUNK, :], (CHUNK, LANES)) for c in range(nc)], axis=0)
    e_ncl = jnp.exp(-cl)
    e_rem = jnp.exp(tot - cl)
    kkt = kk * jnp.exp(cl - lw)
    rt = r * jnp.exp(cl)
    khbh = jnp.concatenate([k * e_ncl, b * e_ncl], axis=0)
    kw = k * e_rem
    bw = b * e_rem

    uloc2 = kkt2 = rt2 = yl2 = None
    for x in range(2):
        mx = low if x == 0 else jnp.logical_not(low)
        kkt_x = jnp.where(mx, kkt, 0.0)
        rt_x = jnp.where(mx, rt, 0.0)
        p4 = _mm(jnp.concatenate([kkt_x, rt_x], axis=0), khbh, _P_A, _dot_nt)
        akk = jnp.where(strict, p4[:s, :s], 0.0)
        abk = jnp.where(strict, p4[:s, s:], 0.0)
        ark = jnp.where(incl, p4[s:, :s], 0.0)
        arb = jnp.where(incl, p4[s:, s:], 0.0)
        pw = -abk
        tinv = eye + pw
        for _ in range(5):
            pw = _mm(pw, pw, _P_INV)
            tinv = tinv + _mm(tinv, pw, _P_INV)
        av = _mm(jnp.concatenate([akk, ark], axis=0), v, _P_APPLY)
        tx = _mm(tinv, jnp.concatenate([av[:s], kkt_x], axis=1), _P_APPLY)
        ax = _mm(arb, tx, _P_APPLY)
        uloc, kkt_s = tx[:, :LANES], tx[:, LANES:]
        rt_s = rt_x - ax[:, LANES:]
        yl = av[s:] - ax[:, :LANES]
        if x == 0:
            uloc2, kkt2, rt2, yl2 = uloc, kkt_s, rt_s, yl
        else:
            uloc2 = jnp.where(low, uloc2, uloc)
            kkt2 = jnp.where(low, kkt2, kkt_s)
            rt2 = jnp.where(low, rt2, rt_s)
            yl2 = jnp.where(low, yl2, yl)
    gms, hms = [], []
    for c in range(nc):
        inc = jnp.right_shift(row, 6) == c
        bw_c = jnp.where(inc, bw, 0.0)
        kw_c = jnp.where(inc, kw, 0.0)
        e_tot = jnp.exp(tot[c * CHUNK:c * CHUNK + 1, :])
        gm = jnp.where(ji == jj, e_tot, 0.0) - _mm(bw_c, kkt2, _P_GH, _dot_tn)
        hm = _mm(kw_c, v, _P_GH, _dot_tn) - _mm(bw_c, uloc2, _P_GH, _dot_tn)
        gms.append(jnp.where(blockdiag, gm, 0.0))
        hms.append(jnp.where(blockdiag, hm, 0.0))
    return rt2, yl2, gms, hms


def _rwkv_chunk(r, kk, k, b, v, lw, s):
    bsz, t, _ = r.shape
    npair = H_B // 2
    nc = t // CHUNK
    tok = pl.BlockSpec((1, s, D_B), lambda bi, i: (bi, i, 0))
    mat = pl.BlockSpec((1, npair, s // CHUNK, LANES, LANES), lambda bi, i: (bi, 0, i, 0, 0))
    return pl.pallas_call(
        _rwkv_chunk_kernel,
        grid=(bsz, t // s),
        in_specs=[tok] * 6,
        out_specs=[tok, tok, mat, mat],
        out_shape=[jax.ShapeDtypeStruct((bsz, t, D_B), F32)] * 2
                  + [jax.ShapeDtypeStruct((bsz, npair, nc, LANES, LANES), F32)] * 2,
        compiler_params=_cparams(("parallel", "parallel")),
        name="rwkv_chunk",
    )(r, kk, k, b, v, lw)


def _rwkv_scan_kernel(s0_ref, g_ref, h_ref, rt_ref, yl_ref, y_ref, sfin_ref, st_scr):
    npair, ncb = g_ref.shape[1], g_ref.shape[2]

    @pl.when(pl.program_id(1) == 0)
    def _():
        st_scr[...] = s0_ref[0]

    sts = [st_scr[p] for p in range(npair)]
    for c in range(ncb):
        rows = slice(c * CHUNK, (c + 1) * CHUNK)
        for p in range(npair):
            sl = slice(p * LANES, (p + 1) * LANES)
            y_ref[0, rows, sl] = _dot3(rt_ref[0, rows, sl], sts[p]) + yl_ref[0, rows, sl]
            sts[p] = _dot3(g_ref[0, p, c], sts[p]) + h_ref[0, p, c]
    for p in range(npair):
        st_scr[p] = sts[p]
        sfin_ref[0, p] = sts[p]


def _rwkv_scan(s0, g, h, rt, yl, ncb):
    bsz, t, _ = rt.shape
    npair = H_B // 2
    nc = t // CHUNK
    st_spec = pl.BlockSpec((1, npair, LANES, LANES), lambda bi, i: (bi, 0, 0, 0))
    mat = pl.BlockSpec((1, npair, ncb, LANES, LANES), lambda bi, i: (bi, 0, i, 0, 0))
    tok = pl.BlockSpec((1, ncb * CHUNK, D_B), lambda bi, i: (bi, i, 0))
    return pl.pallas_call(
        _rwkv_scan_kernel,
        grid=(bsz, nc // ncb),
        in_specs=[st_spec, mat, mat, tok, tok],
        out_specs=[tok, st_spec],
        out_shape=[jax.ShapeDtypeStruct((bsz, t, D_B), F32), jax.ShapeDtypeStruct((bsz, npair, LANES, LANES), F32)],
        scratch_shapes=[pltpu.VMEM((npair, LANES, LANES), F32)],
        compiler_params=_cparams(("parallel", "arbitrary")),
        name="rwkv_scan",
    )(s0, g, h, rt, yl)


def _rwkv_post_kernel(y_ref, r_ref, k_ref, v_ref, g_ref, lnw_ref, lnb_ref, rk_ref, bd_ref, o_ref):
    bd = bd_ref[...]
    y = y_ref[...]
    mu = _dot_sel(y, bd) * (1.0 / HEAD_DIM)
    d = y - mu
    var = _dot_sel(d * d, bd) * (1.0 / HEAD_DIM)
    yn = d * lax.rsqrt(var + GN_EPS) * lnw_ref[...] + lnb_ref[...]
    v = v_ref[...]
    bonus = _dot_sel(r_ref[...] * k_ref[...] * rk_ref[...], bd) * v
    o_ref[...] = ((yn + bonus) * g_ref[...]).astype(BF16)


def _rwkv_post(y, r, k, v, g, lw, tm):
    n = y.shape[0]
    row = pl.BlockSpec((tm, D_B), lambda i: (i, 0))
    full = lambda a: pl.BlockSpec(a.shape, lambda i: (0, 0))
    params = [lw['rw_ln_w'], lw['rw_ln_b'], lw['rw_r_k'], lw['bd384']]
    return pl.pallas_call(
        _rwkv_post_kernel,
        grid=(n // tm,),
        in_specs=[row] * 5 + [full(a) for a in params],
        out_specs=row,
        out_shape=jax.ShapeDtypeStruct((n, D_B), BF16),
        compiler_params=_cparams(("parallel",)),
        name="rwkv_post",
    )(y, r, k, v, g, *params)


_L_EXP = 16


def _moe_kernel(x_ref, oa_ref, ob_ref, oc_ref, wo_ref, gffn_ref, wr_ref, br_ref, wgu_ref, wd_ref,
                o_ref, acc_ref, h_ref, comb_ref):
    e = pl.program_id(1)

    @pl.when(e == 0)
    def _():
        da = H_A * HEAD_DIM
        x1 = (x_ref[...] + _dot(oa_ref[...], wo_ref[0:da, :]) + _dot(ob_ref[...], wo_ref[da:da + D_B, :])
              + _dot(oc_ref[...], wo_ref[da + D_B:, :]))
        acc_ref[...] = x1
        hf = x1 * lax.rsqrt(jnp.mean(x1 * x1, axis=-1, keepdims=True) + RMS_EPS) * gffn_ref[...]
        h_ref[...] = hf.astype(BF16)

        logit = _dot3(hf, wr_ref[...]) + br_ref[...]
        lane_i = _iota(logit.shape, 1)
        lane = lane_i.astype(F32)
        big = jnp.float32(3e38)
        is_g = lane_i < N_GROUPS
        gl = jnp.where(is_g, logit, -big)
        gmax = jnp.max(gl, axis=1, keepdims=True)
        pg_top = 1.0 / jnp.sum(jnp.where(is_g, jnp.exp(gl - gmax), 0.0), axis=1, keepdims=True)
        g_idx = jnp.min(jnp.where(jnp.logical_and(is_g, gl == gmax), lane, big), axis=1, keepdims=True)
        el = lane_i - _L_EXP
        in_e = jnp.logical_and(el >= 0, el < N_EXPERTS)
        sel = jnp.logical_and(in_e, jnp.right_shift(el, 2).astype(F32) == g_idx)
        l1 = jnp.where(sel, logit, -big)
        v1 = jnp.max(l1, axis=1, keepdims=True)
        i1 = jnp.min(jnp.where(jnp.logical_and(sel, l1 == v1), lane, big), axis=1, keepdims=True)
        sel2 = jnp.logical_and(sel, lane != i1)
        l2 = jnp.where(sel2, logit, -big)
        v2 = jnp.max(l2, axis=1, keepdims=True)
        i2 = jnp.min(jnp.where(jnp.logical_and(sel2, l2 == v2), lane, big), axis=1, keepdims=True)
        e2 = jnp.exp(v2 - v1)
        den = 1.0 / (1.0 + e2)
        comb_ref[...] = (jnp.where(lane == i1, den * pg_top, 0.0) + jnp.where(lane == i2, e2 * den * pg_top, 0.0))

    h = h_ref[...]
    comb = comb_ref[...]
    lane_c = _iota(comb.shape, 1)
    acts = []
    for j in range(E_PER_GROUP):
        gu = _dot(h, wgu_ref[j])
        gate = gu[:, :D_FF_E]
        ce = jnp.sum(jnp.where(lane_c == e * E_PER_GROUP + j + _L_EXP, comb, 0.0), axis=1, keepdims=True)
        acts.append((gate * _sigmoid(gate) * gu[:, D_FF_E:] * ce).astype(BF16))
    acc_ref[...] += _dot(jnp.concatenate(acts, axis=1), wd_ref[...])

    @pl.when(e == N_GROUPS - 1)
    def _():
        o_ref[...] = acc_ref[...]


def _outproj_moe(x2d, oa, ob, oc, lw, tm):
    n = x2d.shape[0]
    row = lambda w: pl.BlockSpec((tm, w), lambda i, e: (i, 0))
    full = lambda a: pl.BlockSpec(a.shape, lambda i, e: (0, 0))
    return pl.pallas_call(
        _moe_kernel,
        grid=(n // tm, N_GROUPS),
        in_specs=[row(D_MODEL), row(H_A * HEAD_DIM), row(D_B), row(D_C), full(lw['w_out']), full(lw['g_ffn']),
                  full(lw['w_r']), full(lw['b_r']),
                  pl.BlockSpec((E_PER_GROUP, D_MODEL, 2 * D_FF_E), lambda i, e: (e, 0, 0)),
                  pl.BlockSpec((E_PER_GROUP * D_FF_E, D_MODEL), lambda i, e: (e, 0))],
        out_specs=row(D_MODEL),
        out_shape=jax.ShapeDtypeStruct((n, D_MODEL), F32),
        scratch_shapes=[pltpu.VMEM((tm, D_MODEL), F32), pltpu.VMEM((tm, D_MODEL), BF16), pltpu.VMEM((tm, LANES), F32)],
        compiler_params=_cparams(("parallel", "arbitrary")),
        name="outproj_moe",
    )(x2d, oa, ob, oc, lw['w_out'], lw['g_ffn'], lw['w_r'], lw['b_r'], lw['w_gu'], lw['w_d'])


def _place(pieces, width):
    rows = pieces[0][1].shape[0]
    cols, at = [], 0
    for off, a in pieces:
        if off > at:
            cols.append(jnp.zeros((rows, off - at), F32))
        cols.append(a.astype(F32))
        at = off + a.shape[1]
    if width > at:
        cols.append(jnp.zeros((rows, width - at), F32))
    return jnp.concatenate(cols, axis=1)


def _row(v, width=None, off=0):
    v = v.reshape(1, -1).astype(F32)
    return v if width is None else _place([(off, v)], width)


def _block_diag_ones(n, blk):
    i = jnp.arange(n) // blk
    return (i[:, None] == i[None, :]).astype(BF16)


def _layer_weights(p, l):
    g = lambda name: p[name][l]
    w_in = g('w_in')
    o_b, o_c = A_IN, A_IN + B_IN
    w_in_p = _place([(_C_QL, w_in[:, :Q_LORA]), (_C_KV, w_in[:, Q_LORA:Q_LORA + KV_LORA]),
                     (_C_KR + NOPE, w_in[:, Q_LORA + KV_LORA:A_IN]), (_C_PB, w_in[:, o_b:o_c]),
                     (_C_QC, w_in[:, o_c:o_c + 3 * D_C]), (_C_F, w_in[:, o_c + 3 * D_C:])], _C_END).astype(BF16)
    w_uq = g('mla_w_uq').reshape(Q_LORA, H_A, QK_DIM)
    w_uq = jnp.pad(w_uq, ((0, 256 - Q_LORA), (0, 0), (0, LANES - QK_DIM))).reshape(256, H_A * LANES).astype(BF16)
    w_ukv = g('mla_w_ukv').reshape(KV_LORA, H_A, NOPE + HEAD_DIM)
    w_uk = jnp.pad(w_ukv[:, :, :NOPE], ((0, 0), (0, 0), (0, LANES - NOPE))).reshape(KV_LORA, H_A * LANES).astype(BF16)
    w_uv = w_ukv[:, :, NOPE:].reshape(KV_LORA, H_A * HEAD_DIM).astype(BF16)
    zeros_w = jnp.zeros((W_LORA, D_B), F32)
    w_r = _place([(0, g('moe_w_rg')), (_L_EXP, g('moe_w_re'))], LANES)
    b_r = _place([(0, g('moe_b_rg').reshape(1, -1)), (_L_EXP, g('moe_b_re').reshape(1, -1))], LANES)
    return dict(
        g_mix=_row(g('g_mix')), w_in=w_in_p, g_qa=_row(g('mla_g_qa'), 256), w_uq=w_uq, g_kva=_row(g('mla_g_kva')),
        g_qn=_row(g('mla_g_qn'), LANES), g_fq=_row(jnp.tile(g('fox_g_qn'), H_C)),
        g_fk=_row(jnp.tile(g('fox_g_kn'), H_C)), b_f=_row(g('fox_b_f'), LANES),
        ones128=jnp.ones((LANES, LANES), BF16), eye128=jnp.eye(LANES, dtype=BF16),
        bd256=_block_diag_ones(D_C, HEAD_DIM), bd384=_block_diag_ones(D_B, HEAD_DIM),
        w_uk=w_uk, w_uv=w_uv, g_kn=_row(g('mla_g_kn')[:NOPE], LANES), g_kr=_row(g('mla_g_kn')[NOPE:], LANES, NOPE),
        rw_mu=_row(g('rw_mu')), rw_w0=_row(g('rw_w0')), rw_a0=_row(g('rw_a0')),
        rw_ww=jnp.concatenate([g('rw_w_up'), zeros_w], axis=0).astype(BF16),
        rw_wa=jnp.concatenate([zeros_w, g('rw_a_up')], axis=0).astype(BF16),
        rw_wg=g('rw_g_up').astype(BF16), rw_k_k=_row(g('rw_k_k')), rw_k_a=_row(g('rw_k_a')),
        rw_r_k=_row(g('rw_r_k')), rw_ln_w=_row(g('rw_ln_w')), rw_ln_b=_row(g('rw_ln_b')),
        w_out=g('w_out').astype(BF16), g_ffn=_row(g('g_ffn')), w_r=w_r, b_r=b_r,
        w_gu=jnp.concatenate([g('moe_w_gate'), g('moe_w_up')], axis=-1).astype(BF16),
        w_d=g('moe_w_down').astype(BF16).reshape(N_EXPERTS * D_FF_E, D_MODEL),
    )


def _rope_tables(pos):
    half = ROPE // 2
    inv = ROPE_BASE ** (-jnp.arange(half, dtype=F32) / half)
    ang = pos.astype(F32)[:, None] * inv[None, :]
    cos, sin = jnp.cos(ang), jnp.sin(ang)
    t = pos.shape[0]
    z = lambda w: jnp.zeros((t, w), F32)
    c = jnp.concatenate([jnp.ones((t, NOPE), F32), cos, cos, z(LANES - QK_DIM)], axis=1)
    s_left = jnp.concatenate([z(NOPE), -sin, z(half), z(LANES - QK_DIM)], axis=1)
    s_right = jnp.concatenate([z(NOPE), z(half), sin, z(LANES - QK_DIM)], axis=1)
    return c, s_left, s_right


def _pick(n, prefs):
    for t in prefs:
        if n % t == 0:
            return t
    return n


def _layer(x, lw, hist):
    b, t, _ = x.shape
    n = b * t
    past = 0 if hist is None else hist['ckv'].shape[1]
    x2d = x.reshape(n, D_MODEL)
    tm = _pick(t, (512, 256, 128, 64))
    tmp = _pick(past, (512, 256, 128, 64)) if past else 0
    eye = lw['eye128']

    q_tabs = _rope_tables(past + jnp.arange(t))
    ckv, kr128, kr, qt, pb, qct, kc, kcb, vc, vct, lf = _inproj(x2d, lw, q_tabs, b, t, tm)

    k_new, vt_new = _kvprep(ckv, kr128, lw, q_tabs, b, t, tm)
    if hist is None:
        kp = vpt = None
    else:
        ckv_past = hist['ckv'].reshape(b * past, KV_LORA)
        kr_past = jnp.pad(hist['krope'].reshape(b * past, ROPE), ((0, 0), (NOPE, LANES - QK_DIM)))
        kp, vpt = _kvprep(ckv_past, kr_past, lw, _rope_tables(jnp.arange(past)), b, past, tmp)
        kp = kp.reshape(b, past, -1)
    o_a = _attention(qt, k_new.reshape(b, t, -1), vt_new, kp, vpt, None, eye, n_pairs=H_A // 2, shared=False,
                     frame_causal=False, tq=tm, tk=min(tm, ATTN_TK), tkp=min(tmp, ATTN_TK))

    lf3 = lf.reshape(b, t, H_C)
    lf_all = lf3 if hist is None else jnp.concatenate([hist['flogf'].astype(F32), lf3], axis=1)
    ltot = past + t
    lpad = -(-ltot // LANES) * LANES
    lf_t = jnp.pad(jnp.swapaxes(lf_all, 1, 2), ((0, 0), (0, 0), (0, lpad - ltot))).reshape(b * H_C, lpad)
    c_all = _cumsum_lanes(lf_t).reshape(b, H_C // 2, 2, lpad)
    cq = c_all[..., past:past + t]
    ck = jnp.swapaxes(c_all, 2, 3)
    if hist is None:
        kcp = vcpt = None
    else:
        kcp = hist['fk'].reshape(b, past, D_C)
        vcpt = _transpose_cast(hist['fv'].reshape(b * past, D_C), lw, b, past, tmp)
    o_c = _attention(qct, kcb.reshape(b, t, D_C), vct, kcp, vcpt, (cq, ck), eye, n_pairs=H_C // 2, shared=True,
                     frame_causal=True, tq=tm, tk=min(tm, ATTN_TK), tkp=min(tmp, ATTN_TK))

    prev = jnp.zeros((b, 1, B_IN), F32) if hist is None else hist['shift'].astype(F32)
    pb3 = pb.reshape(b, t, B_IN)
    r, kk, k, bb, v, lwd, g = _rwkv_prep(pb3, prev, lw, _pick(t, (256, 128, 64)))
    rt, yl, gm, hm = _rwkv_chunk(r, kk, k, bb, v, lwd, _pick(t, (256, 128, 64)))
    npair = H_B // 2
    if hist is None:
        s0 = jnp.zeros((b, npair, LANES, LANES), F32)
    else:
        st = jnp.swapaxes(hist['wkv'].astype(F32), -1, -2).reshape(b, npair, 2, HEAD_DIM, HEAD_DIM)
        s0 = jnp.zeros((b, npair, 2, HEAD_DIM, 2, HEAD_DIM), F32)
        s0 = s0.at[:, :, 0, :, 0, :].set(st[:, :, 0]).at[:, :, 1, :, 1, :].set(st[:, :, 1])
        s0 = s0.reshape(b, npair, LANES, LANES)
    y, sfin = _rwkv_scan(s0, gm, hm, rt, yl, _pick(t // CHUNK, (8, 4, 2, 1)))
    flat = lambda a: a.reshape(n, D_B)
    o_b = _rwkv_post(flat(y), flat(r), flat(k), flat(v), flat(g), lw, tm)
    sf = sfin.reshape(b, npair, 2, HEAD_DIM, 2, HEAD_DIM)
    s_fin = jnp.stack([sf[:, :, 0, :, 0, :], sf[:, :, 1, :, 1, :]], axis=2).reshape(b, H_B, HEAD_DIM, HEAD_DIM)
    s_fin = jnp.swapaxes(s_fin, -1, -2)

    tmm = _pick(n, (1024, 512, 256, 128, 64))
    x_out = _outproj_moe(x2d, o_a.reshape(n, -1), flat(o_b), o_c.reshape(n, -1), lw, tmm).reshape(b, t, D_MODEL)

    new = (ckv.reshape(b, t, KV_LORA), kr.reshape(b, t, ROPE), kc.reshape(b, t, H_C, HEAD_DIM),
           vc.reshape(b, t, H_C, HEAD_DIM), lf3, s_fin, pb3[:, -1:])
    return x_out, new


def kernel(x_prompt, x_sample, cache_mla_latent, cache_mla_krope, cache_fox_k, cache_fox_v, cache_fox_logf,
           state_rwkv_wkv, state_rwkv_shift, g_mix, w_in, mla_g_qa, mla_w_uq, mla_g_kva, mla_w_ukv, mla_g_qn,
           mla_g_kn, rw_mu, rw_w0, rw_w_up, rw_a0, rw_a_up, rw_g_up, rw_k_k, rw_k_a, rw_r_k, rw_ln_w, rw_ln_b,
           fox_g_qn, fox_g_kn, fox_b_f, w_out, g_ffn, moe_w_rg, moe_b_rg, moe_w_re, moe_b_re, moe_w_gate,
           moe_w_up, moe_w_down):
    params = dict(g_mix=g_mix, w_in=w_in, mla_g_qa=mla_g_qa, mla_w_uq=mla_w_uq, mla_g_kva=mla_g_kva,
                  mla_w_ukv=mla_w_ukv, mla_g_qn=mla_g_qn, mla_g_kn=mla_g_kn, rw_mu=rw_mu, rw_w0=rw_w0,
                  rw_w_up=rw_w_up, rw_a0=rw_a0, rw_a_up=rw_a_up, rw_g_up=rw_g_up, rw_k_k=rw_k_k, rw_k_a=rw_k_a,
                  rw_r_k=rw_r_k, rw_ln_w=rw_ln_w, rw_ln_b=rw_ln_b, fox_g_qn=fox_g_qn, fox_g_kn=fox_g_kn,
                  fox_b_f=fox_b_f, w_out=w_out, g_ffn=g_ffn, moe_w_rg=moe_w_rg, moe_b_rg=moe_b_rg,
                  moe_w_re=moe_w_re, moe_b_re=moe_b_re, moe_w_gate=moe_w_gate, moe_w_up=moe_w_up,
                  moe_w_down=moe_w_down)
    depth = g_mix.shape[0]
    yp, ys = x_prompt, x_sample
    p_new, s_new = [], []
    for l in range(depth):
        lw = _layer_weights(params, l)
        hist = dict(ckv=cache_mla_latent[l], krope=cache_mla_krope[l], fk=cache_fox_k[l], fv=cache_fox_v[l],
                    flogf=cache_fox_logf[l], wkv=state_rwkv_wkv[l], shift=state_rwkv_shift[l])
        yp, np_l = _layer(yp, lw, None)
        ys, ns_l = _layer(ys, lw, hist)
        p_new.append(np_l)
        s_new.append(ns_l)
    p_out = tuple(jnp.stack(t) for t in zip(*p_new))
    s_out = tuple(jnp.stack(t) for t in zip(*s_new))
    return (yp, ys) + p_out + s_out
```

```python
import functools
import math

import jax
import jax.numpy as jnp
from jax import lax
from jax.experimental import pallas as pl
from jax.experimental.pallas import tpu as pltpu

F32 = jnp.float32
BF16 = jnp.bfloat16

D_MODEL = 1024
HEAD_DIM = 64
H_A, H_B, H_C = 6, 6, 4
Q_LORA, KV_LORA, NOPE, ROPE = 192, 128, 64, 32
QK_DIM = NOPE + ROPE
ROPE_BASE = 10000.0
A_IN = Q_LORA + KV_LORA + ROPE
D_B = H_B * HEAD_DIM
W_LORA, A_LORA, G_LORA = 64, 64, 128
B_IN = 3 * D_B + W_LORA + A_LORA + G_LORA
DECAY_SCALE = math.exp(-0.5)
GN_EPS = 64e-5
D_C = H_C * HEAD_DIM
C_IN = 3 * D_C + H_C
N_GROUPS, E_PER_GROUP = 4, 4
N_EXPERTS = N_GROUPS * E_PER_GROUP
D_FF_E = 256
NEG_INF = -1e30
RMS_EPS = 1e-6
CHUNK = 64
LOG2E = math.log2(math.e)

LANES = 128
VMEM_LIMIT = 56 * 1024 * 1024
ATTN_TK = 512

_C_QL = 0
_C_KV = 256
_C_KR = 384
_C_PB = 512
_C_QC = _C_PB + B_IN
_C_KC = _C_QC + D_C
_C_VC = _C_KC + D_C
_C_F = _C_VC + D_C
_C_END = _C_F + LANES


def _cparams(sem):
    return pltpu.CompilerParams(dimension_semantics=sem, vmem_limit_bytes=VMEM_LIMIT)


def _dot(a, b):
    return jnp.dot(a, b, preferred_element_type=F32)


def _dot_nt(a, b):
    return lax.dot_general(a, b, (((1,), (1,)), ((), ())), preferred_element_type=F32)


def _dot_tn(a, b):
    return lax.dot_general(a, b, (((0,), (0,)), ((), ())), preferred_element_type=F32)


def _split2(x):
    hi = x.astype(BF16)
    lo = (x - hi.astype(F32)).astype(BF16)
    return hi, lo


def _split3(x):
    hi = x.astype(BF16)
    r = x - hi.astype(F32)
    mid = r.astype(BF16)
    lo = (r - mid.astype(F32)).astype(BF16)
    return hi, mid, lo


def _dot3(a, b, dot=_dot):
    ah, al = _split2(a)
    bh, bl = _split2(b)
    return dot(ah, bh) + (dot(ah, bl) + dot(al, bh))


def _mm(a, b, passes, dot=_dot):
    if passes == 1:
        return dot(a.astype(BF16), b.astype(BF16))
    return _dot3(a, b, dot)


def _dot_sel(a, sel):
    return _dot(a.astype(BF16), sel)


def _iota(shape, dim):
    return lax.broadcasted_iota(jnp.int32, shape, dim)


def _rope(x, c, s_left, s_right):
    return x * c + pltpu.roll(x, 112, 1) * s_left + pltpu.roll(x, 16, 1) * s_right


def _sigmoid(x):
    return 1.0 / (1.0 + jnp.exp(-x))


def _log_sigmoid(x):
    return jnp.minimum(x, 0.0) - jnp.log(1.0 + jnp.exp(-jnp.abs(x)))


def _transpose_bf16(x, eye):
    return _dot_nt(eye, x.astype(BF16)).astype(BF16)


def _inproj_kernel(x_ref, gmix_ref, w_ref, gqa_ref, wuq_ref, gkva_ref, gqn_ref, gfq_ref, gfk_ref, bf_ref,
                   ones_ref, bd_ref, eye_ref, c_ref, sl_ref, sr_ref,
                   ckv_ref, kr128_ref, kr_ref, qt_ref, pb_ref, qct_ref, kc_ref, kcb_ref, vc_ref, vct_ref, lf_ref):
    x = x_ref[...]
    h = x * lax.rsqrt(jnp.mean(x * x, axis=-1, keepdims=True) + RMS_EPS) * gmix_ref[...]
    h = h.astype(BF16)
    eye = eye_ref[...]

    ql = _dot(h, w_ref[:, _C_QL:_C_QL + 256])
    ql = ql * lax.rsqrt(jnp.sum(ql * ql, axis=-1, keepdims=True) * (1.0 / Q_LORA) + RMS_EPS) * gqa_ref[...]
    qh = _dot(ql.astype(BF16), wuq_ref[...])
    c, s_l, s_r = c_ref[...], sl_ref[...], sr_ref[...]
    ones = ones_ref[...]
    for hh in range(H_A):
        qv = qh[:, hh * LANES:(hh + 1) * LANES]
        ss = _dot_sel(qv * qv, ones)
        qn = qv * lax.rsqrt(ss * (1.0 / QK_DIM) + RMS_EPS) * gqn_ref[...]
        qt_ref[0, hh * LANES:(hh + 1) * LANES, :] = _transpose_bf16(
            _rope(qn, c, s_l, s_r) * (LOG2E * QK_DIM ** -0.5), eye)

    kv = _dot(h, w_ref[:, _C_KV:_C_KV + KV_LORA])
    ckv_ref[...] = kv * lax.rsqrt(jnp.mean(kv * kv, axis=-1, keepdims=True) + RMS_EPS) * gkva_ref[...]
    kr = _dot(h, w_ref[:, _C_KR:_C_KR + LANES])
    kr128_ref[...] = kr
    kr_ref[...] = kr[:, NOPE:NOPE + ROPE]

    pb_ref[...] = _dot(h, w_ref[:, _C_PB:_C_PB + B_IN])

    bd = bd_ref[...]
    qc = _dot(h, w_ref[:, _C_QC:_C_QC + D_C])
    qc = qc * lax.rsqrt(_dot_sel(qc * qc, bd) * (1.0 / HEAD_DIM) + RMS_EPS) * gfq_ref[...]
    qc = qc * (LOG2E * HEAD_DIM ** -0.5)
    kc = _dot(h, w_ref[:, _C_KC:_C_KC + D_C])
    kc = kc * lax.rsqrt(_dot_sel(kc * kc, bd) * (1.0 / HEAD_DIM) + RMS_EPS) * gfk_ref[...]
    kc_ref[...] = kc
    kcb_ref[...] = kc.astype(BF16)
    vc = _dot(h, w_ref[:, _C_VC:_C_VC + D_C])
    vc_ref[...] = vc
    for pp in range(D_C // LANES):
        qct_ref[0, pp * LANES:(pp + 1) * LANES, :] = _transpose_bf16(qc[:, pp * LANES:(pp + 1) * LANES], eye)
        vct_ref[0, pp * LANES:(pp + 1) * LANES, :] = _transpose_bf16(vc[:, pp * LANES:(pp + 1) * LANES], eye)
    f = _dot(h, w_ref[:, _C_F:_C_F + LANES]) + bf_ref[...]
    lf_ref[...] = _log_sigmoid(f)[:, :H_C]


def _inproj(x2d, lw, tabs, bsz, t, tm):
    n = x2d.shape[0]
    nt = t // tm
    row = lambda i: (i, 0)
    fixed = lambda i: (0, 0)
    tab = lambda i: (i % nt, 0)
    colmajor = lambda i: (i // nt, 0, i % nt)
    full = lambda a: pl.BlockSpec(a.shape, fixed)
    params = [lw['g_mix'], lw['w_in'], lw['g_qa'], lw['w_uq'], lw['g_kva'], lw['g_qn'], lw['g_fq'], lw['g_fk'],
              lw['b_f'], lw['ones128'], lw['bd256'], lw['eye128']]
    rowout = lambda w, dt: (pl.BlockSpec((tm, w), row), jax.ShapeDtypeStruct((n, w), dt))
    colout = lambda w: (pl.BlockSpec((1, w, tm), colmajor), jax.ShapeDtypeStruct((bsz, w, t), BF16))
    outs = [rowout(KV_LORA, F32), rowout(LANES, F32), rowout(ROPE, F32), colout(H_A * LANES), rowout(B_IN, F32),
            colout(D_C), rowout(D_C, F32), rowout(D_C, BF16), rowout(D_C, F32), colout(D_C), rowout(H_C, F32)]
    return pl.pallas_call(
        _inproj_kernel,
        grid=(n // tm,),
        in_specs=[pl.BlockSpec((tm, D_MODEL), row)] + [full(a) for a in params]
                 + [pl.BlockSpec((tm, LANES), tab)] * 3,
        out_specs=[o[0] for o in outs],
        out_shape=[o[1] for o in outs],
        compiler_params=_cparams(("parallel",)),
        name="inproj",
    )(x2d, *params, *tabs)


def _kvprep_kernel(ckv_ref, kr_ref, wuk_ref, wuv_ref, gkn_ref, gkr_ref, ones_ref, eye_ref, c_ref, sl_ref, sr_ref,
                   k_ref, vt_ref):
    cb = ckv_ref[...].astype(BF16)
    kn = _dot(cb, wuk_ref[...])
    v = _dot(cb, wuv_ref[...])
    eye = eye_ref[...]
    for pp in range(H_A // 2):
        vt_ref[0, pp * LANES:(pp + 1) * LANES, :] = _transpose_bf16(v[:, pp * LANES:(pp + 1) * LANES], eye)
    ones = ones_ref[...]
    kr = kr_ref[...]
    ssr = _dot_sel(kr * kr, ones)
    krg = _rope(kr * gkr_ref[...], c_ref[...], sl_ref[...], sr_ref[...])
    for hh in range(H_A):
        knh = kn[:, hh * LANES:(hh + 1) * LANES]
        ssn = _dot_sel(knh * knh, ones)
        r = lax.rsqrt((ssn + ssr) * (1.0 / QK_DIM) + RMS_EPS)
        k_ref[:, hh * LANES:(hh + 1) * LANES] = ((knh * gkn_ref[...] + krg) * r).astype(BF16)


def _kvprep(ckv2d, kr128_2d, lw, tabs, bsz, t, tm):
    n = ckv2d.shape[0]
    nt = t // tm
    row = lambda i: (i, 0)
    fixed = lambda i: (0, 0)
    tab = lambda i: (i % nt, 0)
    full = lambda a: pl.BlockSpec(a.shape, fixed)
    params = [lw['w_uk'], lw['w_uv'], lw['g_kn'], lw['g_kr'], lw['ones128'], lw['eye128']]
    return pl.pallas_call(
        _kvprep_kernel,
        grid=(n // tm,),
        in_specs=[pl.BlockSpec((tm, KV_LORA), row), pl.BlockSpec((tm, LANES), row)] + [full(a) for a in params]
                 + [pl.BlockSpec((tm, LANES), tab)] * 3,
        out_specs=[pl.BlockSpec((tm, H_A * LANES), row),
                   pl.BlockSpec((1, H_A * HEAD_DIM, tm), lambda i: (i // nt, 0, i % nt))],
        out_shape=[jax.ShapeDtypeStruct((n, H_A * LANES), BF16),
                   jax.ShapeDtypeStruct((bsz, H_A * HEAD_DIM, t), BF16)],
        compiler_params=_cparams(("parallel",)),
        name="kvprep",
    )(ckv2d, kr128_2d, *params, *tabs)


def _transpose_kernel(x_ref, eye_ref, o_ref):
    eye = eye_ref[...]
    for pp in range(x_ref.shape[1] // LANES):
        o_ref[0, pp * LANES:(pp + 1) * LANES, :] = _transpose_bf16(x_ref[:, pp * LANES:(pp + 1) * LANES], eye)


def _transpose_cast(x2d, lw, bsz, t, tm):
    n, w = x2d.shape
    nt = t // tm
    return pl.pallas_call(
        _transpose_kernel,
        grid=(n // tm,),
        in_specs=[pl.BlockSpec((tm, w), lambda i: (i, 0)), pl.BlockSpec((LANES, LANES), lambda i: (0, 0))],
        out_specs=pl.BlockSpec((1, w, tm), lambda i: (i // nt, 0, i % nt)),
        out_shape=jax.ShapeDtypeStruct((bsz, w, t), BF16),
        compiler_params=_cparams(("parallel",)),
        name="transpose_cast",
    )(x2d, lw['eye128'])


def _attn_kernel(*refs, tq, tk, tkp, nq, n_past, n_pairs, shared, frame_causal, has_bias, pipelined):
    it = iter(refs)
    q_ref, k_ref, vt_ref = next(it), next(it), next(it)
    if n_past:
        kp_ref, vpt_ref = next(it), next(it)
    if has_bias:
        cq_ref, ck_ref = next(it), next(it)
    eye_ref, o_ref = next(it), next(it)
    q_scr, m_scr, l_scr, acc_scr = next(it), next(it), next(it), next(it)

    qi = 0 if nq == 1 else pl.program_id(1)
    low = _iota((LANES, tq), 0) < HEAD_DIM
    blk = lambda p, j: p if shared else 2 * p + j
    for p in range(n_pairs):
        for j in range(2):
            q = q_ref[0, blk(p, j) * LANES:(blk(p, j) + 1) * LANES, :]
            if shared:
                keep = low if j == 0 else jnp.logical_not(low)
                q = jnp.where(keep, q, jnp.zeros_like(q))
            q_scr[2 * p + j] = q
    m_scr[...] = jnp.full(m_scr.shape, -jnp.inf, F32)
    l_scr[...] = jnp.zeros(l_scr.shape, F32)
    acc_scr[...] = jnp.zeros(acc_scr.shape, F32)

    def score(kr, st, width, p, j):
        kt = kr[0, pl.ds(st, width), blk(p, j) * LANES:(blk(p, j) + 1) * LANES].astype(BF16)
        return _dot(kt, q_scr[2 * p + j])

    def scores_to(buf, kr, st, width):
        for p in range(n_pairs):
            for j in range(2):
                buf[2 * p + j] = score(kr, st, width, p, j)

    heads = [(p, j) for p in range(n_pairs) for j in range(2)]

    def consume(get_s, vr, st, width, key_start, mask):
        ss = []
        for p, j in heads:
            s = get_s(p, j)
            if has_bias:
                s = s - ck_ref[0, p, pl.ds(key_start, width), j:j + 1]
            if mask is not None:
                s = jnp.where(mask, s, NEG_INF)
            ss.append(s)
        cqs = [cq_ref[0, p, j:j + 1, :] if has_bias else 0.0 for p, j in heads]
        m_prev = [m_scr[h] for h in range(len(heads))]
        m_next = [jnp.maximum(m_prev[h], jnp.max(ss[h], axis=0, keepdims=True) + cqs[h]) for h in range(len(heads))]
        prs = [jnp.exp2(ss[h] - (m_next[h] - cqs[h])) for h in range(len(heads))]
        alphas = [jnp.exp2(m_prev[h] - m_next[h]) for h in range(len(heads))]
        pvs = [_dot(vr[0, p * LANES:(p + 1) * LANES, pl.ds(st, width)], prs[2 * p + j].astype(BF16))
               for p, j in heads]
        for h in range(len(heads)):
            l_scr[h] = alphas[h] * l_scr[h] + jnp.sum(prs[h], axis=0, keepdims=True)
            acc_scr[h] = acc_scr[h] * alphas[h] + pvs[h]
            m_scr[h] = m_next[h]

    def step(kr, vr, st, width, key_start, mask):
        consume(lambda p, j: score(kr, st, width, p, j), vr, st, width, key_start, mask)

    kidx = _iota((tk, tq), 0)
    qidx = _iota((tk, tq), 1)
    diag_mask = lambda d: ((kidx + d * tk <= qidx) if frame_causal else
                           (jnp.right_shift(kidx + d * tk, 6) <= jnp.right_shift(qidx, 6)))

    if pipelined:
        sa, sb = next(it), next(it)
        tile = lambda t: pl.multiple_of(t * tk, tk)
        from_a = lambda p, j: sa[2 * p + j]
        from_b = lambda p, j: sb[2 * p + j]
        scores_to(sa, k_ref, tile(0), tk)

        def pair_body(u, carry):
            t0 = 2 * u
            scores_to(sb, k_ref, tile(t0 + 1), tk)
            consume(from_a, vt_ref, tile(t0), tk, tile(t0), None)
            scores_to(sa, k_ref, tile(t0 + 2), tk)
            consume(from_b, vt_ref, tile(t0 + 1), tk, tile(t0 + 1), None)
            return carry
        lax.fori_loop(0, qi // 2, pair_body, 0)
        odd = lax.rem(qi, 2) == 1

        @pl.when(jnp.logical_not(odd))
        def _():
            consume(from_a, vt_ref, tile(qi), tk, tile(qi), diag_mask(0))

        @pl.when(odd)
        def _():
            scores_to(sb, k_ref, tile(qi), tk)
            consume(from_a, vt_ref, tile(qi - 1), tk, tile(qi - 1), None)
            consume(from_b, vt_ref, tile(qi), tk, tile(qi), diag_mask(0))
    else:
        if n_past:
            def past_body(t, carry):
                st = pl.multiple_of(t * tkp, tkp)
                step(kp_ref, vpt_ref, st, tkp, st, None)
                return carry
            lax.fori_loop(0, n_past // tkp, past_body, 0)

        def new_body(t, carry):
            st = pl.multiple_of(t * tk, tk)
            step(k_ref, vt_ref, st, tk, n_past + st, None)
            return carry
        if nq > 1:
            lax.fori_loop(0, qi * (tq // tk), new_body, 0)

        for d in range(tq // tk):
            st = qi * tq + d * tk
            st = st if nq == 1 else pl.multiple_of(st, tk)
            step(k_ref, vt_ref, st, tk, n_past + st, diag_mask(d))

    for p in range(n_pairs):
        ot = jnp.where(low, acc_scr[2 * p] / l_scr[2 * p], acc_scr[2 * p + 1] / l_scr[2 * p + 1])
        o_ref[0, :, p * LANES:(p + 1) * LANES] = _dot_tn(ot.astype(BF16), eye_ref[...]).astype(o_ref.dtype)


def _attention(qt, k_new, vt_new, k_past, vt_past, bias, eye, *, n_pairs, shared, frame_causal, tq, tk, tkp):
    b, t = k_new.shape[0], k_new.shape[1]
    n_past = 0 if k_past is None else k_past.shape[1]
    nq = t // tq
    wq = qt.shape[1]
    args = [qt, k_new, vt_new]
    specs = [pl.BlockSpec((1, wq, tq), lambda bi, i: (bi, 0, i)),
             pl.BlockSpec((1, t, wq), lambda bi, i: (bi, 0, 0)),
             pl.BlockSpec((1, n_pairs * LANES, t), lambda bi, i: (bi, 0, 0))]
    if n_past:
        args += [k_past, vt_past]
        specs += [pl.BlockSpec((1, n_past, wq), lambda bi, i: (bi, 0, 0)),
                  pl.BlockSpec((1, n_pairs * LANES, n_past), lambda bi, i: (bi, 0, 0))]
    if bias is not None:
        cq, ck = bias
        args += [cq, ck]
        specs += [pl.BlockSpec((1, n_pairs, 2, tq), lambda bi, i: (bi, 0, 0, i)),
                  pl.BlockSpec((1, n_pairs, ck.shape[2], 2), lambda bi, i: (bi, 0, 0, 0))]
    args.append(eye)
    specs.append(pl.BlockSpec((LANES, LANES), lambda bi, i: (0, 0)))
    pipelined = n_past == 0 and nq > 1 and tq == tk
    kern = functools.partial(_attn_kernel, tq=tq, tk=tk, tkp=tkp, nq=nq, n_past=n_past, n_pairs=n_pairs,
                             shared=shared, frame_causal=frame_causal, has_bias=bias is not None,
                             pipelined=pipelined)
    nh = 2 * n_pairs
    score_bufs = [pltpu.VMEM((nh, tk, tq), F32)] * 2 if pipelined else []
    return pl.pallas_call(
        kern,
        grid=(b, nq),
        in_specs=specs,
        out_specs=pl.BlockSpec((1, tq, n_pairs * LANES), lambda bi, i: (bi, i, 0)),
        out_shape=jax.ShapeDtypeStruct((b, t, n_pairs * LANES), BF16),
        scratch_shapes=[pltpu.VMEM((nh, LANES, tq), BF16), pltpu.VMEM((nh, 1, tq), F32),
                        pltpu.VMEM((nh, 1, tq), F32), pltpu.VMEM((nh, LANES, tq), F32)] + score_bufs,
        compiler_params=_cparams(("parallel", "arbitrary")),
        name="attn_fox" if shared else "attn_mla",
    )(*args)


def _attn_kernel_pair(*refs, tq, tk, tkp, nq, n_past, shared, frame_causal, has_bias):
    it = iter(refs)
    qa_ref = next(it)
    qb_ref = qa_ref if shared else next(it)
    ka_ref = next(it)
    kb_ref = ka_ref if shared else next(it)
    vt_ref = next(it)
    if n_past:
        kpa_ref = next(it)
        kpb_ref = kpa_ref if shared else next(it)
        vpt_ref = next(it)
    if has_bias:
        cq_ref = next(it)
        ck_ref = next(it)
    eye_ref = next(it)
    o_ref = next(it)
    m_scr, l_scr, acc_scr = next(it), next(it), next(it)

    qi = 0 if nq == 1 else pl.program_id(2)
    low = _iota((LANES, tq), 0) < HEAD_DIM
    if shared:
        q2 = qa_ref[0]
        qs = (jnp.where(low, q2, jnp.zeros_like(q2)), jnp.where(low, jnp.zeros_like(q2), q2))
    else:
        qs = (qa_ref[0], qb_ref[0])

    m_scr[...] = jnp.full(m_scr.shape, -jnp.inf, F32)
    l_scr[...] = jnp.zeros(l_scr.shape, F32)
    acc_scr[...] = jnp.zeros(acc_scr.shape, F32)

    def step(k_tiles, vt, key_start, width, mask):
        for j in range(2):
            s = _dot(k_tiles[j].astype(BF16), qs[j])
            if has_bias:
                s = s + (cq_ref[0, 0, j:j + 1, :] - ck_ref[0, 0, pl.ds(key_start, width), j:j + 1])
            if mask is not None:
                s = jnp.where(mask, s, NEG_INF)
            m_prev = m_scr[j]
            m_next = jnp.maximum(m_prev, jnp.max(s, axis=0, keepdims=True))
            p = jnp.exp2(s - m_next)
            alpha = jnp.exp2(m_prev - m_next)
            l_scr[j] = alpha * l_scr[j] + jnp.sum(p, axis=0, keepdims=True)
            acc_scr[j] = acc_scr[j] * alpha + _dot(vt, p.astype(BF16))
            m_scr[j] = m_next

    if n_past:
        def past_body(t, carry):
            st = pl.multiple_of(t * tkp, tkp)
            step((kpa_ref[0, pl.ds(st, tkp), :], kpb_ref[0, pl.ds(st, tkp), :]), vpt_ref[0, :, pl.ds(st, tkp)],
                 st, tkp, None)
            return carry
        lax.fori_loop(0, n_past // tkp, past_body, 0)

    def new_body(t, carry):
        st = pl.multiple_of(t * tk, tk)
        step((ka_ref[0, pl.ds(st, tk), :], kb_ref[0, pl.ds(st, tk), :]), vt_ref[0, :, pl.ds(st, tk)],
             n_past + st, tk, None)
        return carry
    if nq > 1:
        lax.fori_loop(0, qi * (tq // tk), new_body, 0)

    kidx = _iota((tk, tq), 0)
    qidx = _iota((tk, tq), 1)
    for d in range(tq // tk):
        kk = kidx + d * tk
        mask = (kk <= qidx) if frame_causal else (jnp.right_shift(kk, 6) <= jnp.right_shift(qidx, 6))
        st = qi * tq + d * tk
        st = st if nq == 1 else pl.multiple_of(st, tk)
        step((ka_ref[0, pl.ds(st, tk), :], kb_ref[0, pl.ds(st, tk), :]), vt_ref[0, :, pl.ds(st, tk)],
             n_past + st, tk, mask)

    ot = jnp.where(low, acc_scr[0] / l_scr[0], acc_scr[1] / l_scr[1])
    o_ref[0] = _dot_tn(ot.astype(BF16), eye_ref[...]).astype(o_ref.dtype)


def _attention_pair(qt, k_new, vt_new, k_past, vt_past, bias, eye, *, n_pairs, shared, frame_causal, tq, tk, tkp):
    b, t = k_new.shape[0], k_new.shape[1]
    n_past = 0 if k_past is None else k_past.shape[1]
    nq = t // tq
    qblk = lambda off: pl.BlockSpec((1, LANES, tq), lambda bi, p, i, off=off: (bi, 2 * p + off, i))
    kblk = lambda rows, off: pl.BlockSpec((1, rows, LANES), lambda bi, p, i, off=off: (bi, 0, 2 * p + off))
    pair_q = pl.BlockSpec((1, LANES, tq), lambda bi, p, i: (bi, p, i))
    pair_k = lambda rows: pl.BlockSpec((1, rows, LANES), lambda bi, p, i: (bi, 0, p))
    pair_vt = lambda cols: pl.BlockSpec((1, LANES, cols), lambda bi, p, i: (bi, p, 0))
    args, specs = [], []
    if shared:
        args += [qt, k_new]
        specs += [pair_q, pair_k(t)]
    else:
        args += [qt, qt, k_new, k_new]
        specs += [qblk(0), qblk(1), kblk(t, 0), kblk(t, 1)]
    args.append(vt_new)
    specs.append(pair_vt(t))
    if n_past:
        if shared:
            args.append(k_past)
            specs.append(pair_k(n_past))
        else:
            args += [k_past, k_past]
            specs += [kblk(n_past, 0), kblk(n_past, 1)]
        args.append(vt_past)
        specs.append(pair_vt(n_past))
    if bias is not None:
        cq, ck = bias
        args += [cq, ck]
        specs += [pl.BlockSpec((1, 1, 2, tq), lambda bi, p, i: (bi, p, 0, i)),
                  pl.BlockSpec((1, 1, ck.shape[2], 2), lambda bi, p, i: (bi, p, 0, 0))]
    args.append(eye)
    specs.append(pl.BlockSpec((LANES, LANES), lambda bi, p, i: (0, 0)))
    kern = functools.partial(_attn_kernel, tq=tq, tk=tk, tkp=tkp, nq=nq, n_past=n_past, shared=shared,
                             frame_causal=frame_causal, has_bias=bias is not None)
    return pl.pallas_call(
        kern,
        grid=(b, n_pairs, nq),
        in_specs=specs,
        out_specs=pl.BlockSpec((1, tq, LANES), lambda bi, p, i: (bi, i, p)),
        out_shape=jax.ShapeDtypeStruct((b, t, n_pairs * LANES), BF16),
        scratch_shapes=[pltpu.VMEM((2, 1, tq), F32), pltpu.VMEM((2, 1, tq), F32), pltpu.VMEM((2, LANES, tq), F32)],
        compiler_params=_cparams(("parallel", "parallel", "arbitrary")),
        name="attn_fox" if shared else "attn_mla",
    )(*args)


def _cumsum_kernel(x_ref, tri_ref, o_ref):
    rows, n = x_ref.shape
    tri = tri_ref[...]
    carry = jnp.zeros((rows, 1), F32)
    for t in range(n // LANES):
        xt = x_ref[:, t * LANES:(t + 1) * LANES]
        hi, mid, lo = _split3(xt)
        o_ref[:, t * LANES:(t + 1) * LANES] = (_dot(hi, tri) + _dot(mid, tri) + _dot(lo, tri) + carry) * LOG2E
        carry = carry + jnp.sum(xt, axis=1, keepdims=True)


def _cumsum_lanes(x):
    rows, n = x.shape
    tri = (jnp.arange(LANES)[:, None] <= jnp.arange(LANES)[None, :]).astype(BF16)
    return pl.pallas_call(
        _cumsum_kernel,
        out_shape=jax.ShapeDtypeStruct((rows, n), F32),
        compiler_params=pltpu.CompilerParams(vmem_limit_bytes=VMEM_LIMIT),
        name="cumsum",
    )(x, tri)


def _rwkv_prep_kernel(pb_ref, prev_ref, mu_ref, w0_ref, ww_ref, a0_ref, wa_ref, wg_ref, kk_ref, ka_ref, bd_ref,
                      r_ref, kkn_ref, k_ref, b_ref, v_ref, lw_ref, g_ref, carry_ref):
    t = pl.program_id(1)
    tm = pb_ref.shape[1]

    @pl.when(t == 0)
    def _():
        carry_ref[0:1, :] = prev_ref[0]

    pbv = pb_ref[0]
    rolled = pltpu.roll(pbv, 1, 0)
    shifted = jnp.where(_iota(pbv.shape, 0) == 0, carry_ref[0:1, :], rolled)
    carry_ref[0:1, :] = pbv[tm - 1:tm, :]
    xs = pbv + (shifted - pbv) * mu_ref[...]

    r = xs[:, 0:D_B]
    kb = xs[:, D_B:2 * D_B]
    o3 = 3 * D_B
    wa_in = xs[:, o3:o3 + LANES]
    wa_in = jnp.where(_iota(wa_in.shape, 1) < W_LORA, jnp.tanh(wa_in), wa_in).astype(BF16)
    lw_ref[0] = -DECAY_SCALE * _sigmoid(w0_ref[...] + _dot(wa_in, ww_ref[...]))
    a = _sigmoid(a0_ref[...] + _dot(wa_in, wa_ref[...]))
    g_ref[0] = _dot(_sigmoid(xs[:, o3 + LANES:o3 + 2 * LANES]).astype(BF16), wg_ref[...])
    kk = kb * kk_ref[...]
    kk = kk * lax.rsqrt(_dot_sel(kk * kk, bd_ref[...]) + 1e-12)
    r_ref[0] = r
    kkn_ref[0] = kk
    k_ref[0] = kb * (1.0 + (a - 1.0) * ka_ref[...])
    b_ref[0] = kk * a
    v_ref[0] = xs[:, 2 * D_B:3 * D_B]


def _rwkv_prep(pb3, prev, lw, tm):
    b, t, _ = pb3.shape
    fixed = lambda bi, i: (0, 0)
    full = lambda a: pl.BlockSpec(a.shape, fixed)
    params = [lw['rw_mu'], lw['rw_w0'], lw['rw_ww'], lw['rw_a0'], lw['rw_wa'], lw['rw_wg'], lw['rw_k_k'],
              lw['rw_k_a'], lw['bd384']]
    out_spec = pl.BlockSpec((1, tm, D_B), lambda bi, i: (bi, i, 0))
    return pl.pallas_call(
        _rwkv_prep_kernel,
        grid=(b, t // tm),
        in_specs=[pl.BlockSpec((1, tm, B_IN), lambda bi, i: (bi, i, 0)),
                  pl.BlockSpec((1, 1, B_IN), lambda bi, i: (bi, 0, 0))] + [full(a) for a in params],
        out_specs=[out_spec] * 7,
        out_shape=[jax.ShapeDtypeStruct((b, t, D_B), F32)] * 7,
        scratch_shapes=[pltpu.VMEM((8, B_IN), F32)],
        compiler_params=_cparams(("parallel", "arbitrary")),
        name="rwkv_prep",
    )(pb3, prev, *params)


_P_A = 1
_P_INV = 1
_P_APPLY = 1
_P_GH = 1


def _rwkv_chunk_kernel(r_ref, kk_ref, k_ref, b_ref, v_ref, lw_ref, rt_ref, yl_ref, g_ref, h_ref):
    s = r_ref.shape[1]
    nc = s // CHUNK
    row = _iota((s, LANES), 0)
    rin = jnp.bitwise_and(row, CHUNK - 1)
    ti = _iota((s, s), 0)
    si = _iota((s, s), 1)
    same = jnp.right_shift(ti, 6) == jnp.right_shift(si, 6)
    strict = jnp.logical_and(same, si < ti)
    incl = jnp.logical_and(same, si <= ti)
    eye = (ti == si).astype(F32)
    low = _iota((s, LANES), 1) < HEAD_DIM
    ji = _iota((LANES, LANES), 0)
    jj = _iota((LANES, LANES), 1)
    blockdiag = (ji < HEAD_DIM) == (jj < HEAD_DIM)
    npair = r_ref.shape[2] // LANES
    pairs = range(npair)
    heads = [(p, x) for p in pairs for x in range(2)]

    tot, kkt, rt, khbh, kw, bw, v = [], [], [], [], [], [], []
    for p in pairs:
        sl = slice(p * LANES, (p + 1) * LANES)
        lw, kp, bp = lw_ref[0, :, sl], k_ref[0, :, sl], b_ref[0, :, sl]
        cl = lw
        for sh in (1, 2, 4, 8, 16, 32):
            cl = cl + jnp.where(rin >= sh, pltpu.roll(cl, sh, 0), 0.0)
        tp = jnp.concatenate(
            [jnp.broadcast_to(cl[c * CHUNK + CHUNK - 1:(c + 1) * CHUNK, :], (CHUNK, LANES)) for c in range(nc)],
            axis=0)
        e_ncl = jnp.exp(-cl)
        e_rem = jnp.exp(tp - cl)
        tot.append(tp)
        kkt.append(kk_ref[0, :, sl] * jnp.exp(cl - lw))
        rt.append(r_ref[0, :, sl] * jnp.exp(cl))
        khbh.append(jnp.concatenate([kp * e_ncl, bp * e_ncl], axis=0).astype(BF16))
        kw.append(kp * e_rem)
        bw.append(bp * e_rem)
        v.append(v_ref[0, :, sl])

    keep = lambda x: low if x == 0 else jnp.logical_not(low)
    kkt_x = [jnp.where(keep(x), kkt[p], 0.0) for p, x in heads]
    rt_x = [jnp.where(keep(x), rt[p], 0.0) for p, x in heads]
    akk, ark, arb, pw, tinv = [], [], [], [], []
    for i, (p, x) in enumerate(heads):
        p4 = _dot_nt(jnp.concatenate([kkt_x[i], rt_x[i]], axis=0).astype(BF16), khbh[p])
        akk.append(jnp.where(strict, p4[:s, :s], 0.0))
        ark.append(jnp.where(incl, p4[s:, :s], 0.0))
        arb.append(jnp.where(incl, p4[s:, s:], 0.0))
        pw.append(jnp.where(strict, -p4[:s, s:], 0.0))
        tinv.append(eye + pw[i])
    for _ in range(5):
        pw = [_mm(m, m, 1) for m in pw]
        tinv = [t + _mm(t, m, 1) for t, m in zip(tinv, pw)]
    av = [_mm(jnp.concatenate([akk[i], ark[i]], axis=0), v[p], 1) for i, (p, x) in enumerate(heads)]
    tx = [_mm(tinv[i], jnp.concatenate([av[i][:s], kkt_x[i]], axis=1), 1) for i in range(len(heads))]
    ax = [_mm(arb[i], tx[i], 1) for i in range(len(heads))]

    for p in pairs:
        sl = slice(p * LANES, (p + 1) * LANES)
        a, b2 = 2 * p, 2 * p + 1
        uloc2 = jnp.where(low, tx[a][:, :LANES], tx[b2][:, :LANES])
        kkt2 = jnp.where(low, tx[a][:, LANES:], tx[b2][:, LANES:])
        rt_ref[0, :, sl] = jnp.where(low, rt_x[a] - ax[a][:, LANES:], rt_x[b2] - ax[b2][:, LANES:])
        yl_ref[0, :, sl] = jnp.where(low, av[a][s:] - ax[a][:, :LANES], av[b2][s:] - ax[b2][:, :LANES])
        for c in range(nc):
            inc = jnp.right_shift(row, 6) == c
            bw_c = jnp.where(inc, bw[p], 0.0)
            kw_c = jnp.where(inc, kw[p], 0.0)
            e_tot = jnp.exp(tot[p][c * CHUNK:c * CHUNK + 1, :])
            gm = jnp.where(ji == jj, e_tot, 0.0) - _mm(bw_c, kkt2, 1, _dot_tn)
            hm = _mm(kw_c, v[p], 1, _dot_tn) - _mm(bw_c, uloc2, 1, _dot_tn)
            g_ref[0, p, c] = jnp.where(blockdiag, gm, 0.0)
            h_ref[0, p, c] = jnp.where(blockdiag, hm, 0.0)


def _rwkv_chunk(r, kk, k, b, v, lw, s):
    bsz, t, _ = r.shape
    npair = H_B // 2
    nc = t // CHUNK
    tok = pl.BlockSpec((1, s, D_B), lambda bi, i: (bi, i, 0))
    mat = pl.BlockSpec((1, npair, s // CHUNK, LANES, LANES), lambda bi, i: (bi, 0, i, 0, 0))
    return pl.pallas_call(
        _rwkv_chunk_kernel,
        grid=(bsz, t // s),
        in_specs=[tok] * 6,
        out_specs=[tok, tok, mat, mat],
        out_shape=[jax.ShapeDtypeStruct((bsz, t, D_B), F32)] * 2
                  + [jax.ShapeDtypeStruct((bsz, npair, nc, LANES, LANES), F32)] * 2,
        compiler_params=_cparams(("parallel", "parallel")),
        name="rwkv_chunk",
    )(r, kk, k, b, v, lw)


def _rwkv_scan_kernel(s0_ref, g_ref, h_ref, rt_ref, yl_ref, y_ref, sfin_ref, st_scr):
    npair, ncb = g_ref.shape[1], g_ref.shape[2]

    @pl.when(pl.program_id(1) == 0)
    def _():
        st_scr[...] = s0_ref[0]

    sts = [st_scr[p] for p in range(npair)]
    for c in range(ncb):
        rows = slice(c * CHUNK, (c + 1) * CHUNK)
        for p in range(npair):
            sl = slice(p * LANES, (p + 1) * LANES)
            y_ref[0, rows, sl] = _dot3(rt_ref[0, rows, sl], sts[p]) + yl_ref[0, rows, sl]
            sts[p] = _dot3(g_ref[0, p, c], sts[p]) + h_ref[0, p, c]
    for p in range(npair):
        st_scr[p] = sts[p]
        sfin_ref[0, p] = sts[p]


def _rwkv_scan(s0, g, h, rt, yl, ncb):
    bsz, t, _ = rt.shape
    npair = H_B // 2
    nc = t // CHUNK
    st_spec = pl.BlockSpec((1, npair, LANES, LANES), lambda bi, i: (bi, 0, 0, 0))
    mat = pl.BlockSpec((1, npair, ncb, LANES, LANES), lambda bi, i: (bi, 0, i, 0, 0))
    tok = pl.BlockSpec((1, ncb * CHUNK, D_B), lambda bi, i: (bi, i, 0))
    return pl.pallas_call(
        _rwkv_scan_kernel,
        grid=(bsz, nc // ncb),
        in_specs=[st_spec, mat, mat, tok, tok],
        out_specs=[tok, st_spec],
        out_shape=[jax.ShapeDtypeStruct((bsz, t, D_B), F32), jax.ShapeDtypeStruct((bsz, npair, LANES, LANES), F32)],
        scratch_shapes=[pltpu.VMEM((npair, LANES, LANES), F32)],
        compiler_params=_cparams(("parallel", "arbitrary")),
        name="rwkv_scan",
    )(s0, g, h, rt, yl)


def _rwkv_post_kernel(y_ref, r_ref, k_ref, v_ref, g_ref, lnw_ref, lnb_ref, rk_ref, bd_ref, o_ref):
    bd = bd_ref[...]
    y = y_ref[...]
    mu = _dot_sel(y, bd) * (1.0 / HEAD_DIM)
    d = y - mu
    var = _dot_sel(d * d, bd) * (1.0 / HEAD_DIM)
    yn = d * lax.rsqrt(var + GN_EPS) * lnw_ref[...] + lnb_ref[...]
    v = v_ref[...]
    bonus = _dot_sel(r_ref[...] * k_ref[...] * rk_ref[...], bd) * v
    o_ref[...] = ((yn + bonus) * g_ref[...]).astype(BF16)


def _rwkv_post(y, r, k, v, g, lw, tm):
    n = y.shape[0]
    row = pl.BlockSpec((tm, D_B), lambda i: (i, 0))
    full = lambda a: pl.BlockSpec(a.shape, lambda i: (0, 0))
    params = [lw['rw_ln_w'], lw['rw_ln_b'], lw['rw_r_k'], lw['bd384']]
    return pl.pallas_call(
        _rwkv_post_kernel,
        grid=(n // tm,),
        in_specs=[row] * 5 + [full(a) for a in params],
        out_specs=row,
        out_shape=jax.ShapeDtypeStruct((n, D_B), BF16),
        compiler_params=_cparams(("parallel",)),
        name="rwkv_post",
    )(y, r, k, v, g, *params)


_L_EXP = 16


def _moe_kernel(x_ref, oa_ref, ob_ref, oc_ref, wo_ref, gffn_ref, wr_ref, br_ref, wgu_ref, wd_ref,
                o_ref, acc_ref, h_ref, comb_ref):
    e = pl.program_id(1)

    @pl.when(e == 0)
    def _():
        da = H_A * HEAD_DIM
        x1 = (x_ref[...] + _dot(oa_ref[...], wo_ref[0:da, :]) + _dot(ob_ref[...], wo_ref[da:da + D_B, :])
              + _dot(oc_ref[...], wo_ref[da + D_B:, :]))
        acc_ref[...] = x1
        hf = x1 * lax.rsqrt(jnp.mean(x1 * x1, axis=-1, keepdims=True) + RMS_EPS) * gffn_ref[...]
        h_ref[...] = hf.astype(BF16)

        logit = _dot3(hf, wr_ref[...]) + br_ref[...]
        lane_i = _iota(logit.shape, 1)
        lane = lane_i.astype(F32)
        big = jnp.float32(3e38)
        is_g = lane_i < N_GROUPS
        gl = jnp.where(is_g, logit, -big)
        gmax = jnp.max(gl, axis=1, keepdims=True)
        pg_top = 1.0 / jnp.sum(jnp.where(is_g, jnp.exp(gl - gmax), 0.0), axis=1, keepdims=True)
        g_idx = jnp.min(jnp.where(jnp.logical_and(is_g, gl == gmax), lane, big), axis=1, keepdims=True)
        el = lane_i - _L_EXP
        in_e = jnp.logical_and(el >= 0, el < N_EXPERTS)
        sel = jnp.logical_and(in_e, jnp.right_shift(el, 2).astype(F32) == g_idx)
        l1 = jnp.where(sel, logit, -big)
        v1 = jnp.max(l1, axis=1, keepdims=True)
        i1 = jnp.min(jnp.where(jnp.logical_and(sel, l1 == v1), lane, big), axis=1, keepdims=True)
        sel2 = jnp.logical_and(sel, lane != i1)
        l2 = jnp.where(sel2, logit, -big)
        v2 = jnp.max(l2, axis=1, keepdims=True)
        i2 = jnp.min(jnp.where(jnp.logical_and(sel2, l2 == v2), lane, big), axis=1, keepdims=True)
        e2 = jnp.exp(v2 - v1)
        den = 1.0 / (1.0 + e2)
        comb_ref[...] = (jnp.where(lane == i1, den * pg_top, 0.0) + jnp.where(lane == i2, e2 * den * pg_top, 0.0))

    h = h_ref[...]
    comb = comb_ref[...]
    lane_c = _iota(comb.shape, 1)
    acts = []
    for j in range(E_PER_GROUP):
        gu = _dot(h, wgu_ref[j])
        gate = gu[:, :D_FF_E]
        ce = jnp.sum(jnp.where(lane_c == e * E_PER_GROUP + j + _L_EXP, comb, 0.0), axis=1, keepdims=True)
        acts.append((gate * _sigmoid(gate) * gu[:, D_FF_E:] * ce).astype(BF16))
    acc_ref[...] += _dot(jnp.concatenate(acts, axis=1), wd_ref[...])

    @pl.when(e == N_GROUPS - 1)
    def _():
        o_ref[...] = acc_ref[...]


def _outproj_moe(x2d, oa, ob, oc, lw, tm):
    n = x2d.shape[0]
    row = lambda w: pl.BlockSpec((tm, w), lambda i, e: (i, 0))
    full = lambda a: pl.BlockSpec(a.shape, lambda i, e: (0, 0))
    return pl.pallas_call(
        _moe_kernel,
        grid=(n // tm, N_GROUPS),
        in_specs=[row(D_MODEL), row(H_A * HEAD_DIM), row(D_B), row(D_C), full(lw['w_out']), full(lw['g_ffn']),
                  full(lw['w_r']), full(lw['b_r']),
                  pl.BlockSpec((E_PER_GROUP, D_MODEL, 2 * D_FF_E), lambda i, e: (e, 0, 0)),
                  pl.BlockSpec((E_PER_GROUP * D_FF_E, D_MODEL), lambda i, e: (e, 0))],
        out_specs=row(D_MODEL),
        out_shape=jax.ShapeDtypeStruct((n, D_MODEL), F32),
        scratch_shapes=[pltpu.VMEM((tm, D_MODEL), F32), pltpu.VMEM((tm, D_MODEL), BF16), pltpu.VMEM((tm, LANES), F32)],
        compiler_params=_cparams(("parallel", "arbitrary")),
        name="outproj_moe",
    )(x2d, oa, ob, oc, lw['w_out'], lw['g_ffn'], lw['w_r'], lw['b_r'], lw['w_gu'], lw['w_d'])


def _place(pieces, width):
    rows = pieces[0][1].shape[0]
    cols, at = [], 0
    for off, a in pieces:
        if off > at:
            cols.append(jnp.zeros((rows, off - at), F32))
        cols.append(a.astype(F32))
        at = off + a.shape[1]
    if width > at:
        cols.append(jnp.zeros((rows, width - at), F32))
    return jnp.concatenate(cols, axis=1)


def _row(v, width=None, off=0):
    v = v.reshape(1, -1).astype(F32)
    return v if width is None else _place([(off, v)], width)


def _block_diag_ones(n, blk):
    i = jnp.arange(n) // blk
    return (i[:, None] == i[None, :]).astype(BF16)


def _layer_weights(p, l):
    g = lambda name: p[name][l]
    w_in = g('w_in')
    o_b, o_c = A_IN, A_IN + B_IN
    w_in_p = _place([(_C_QL, w_in[:, :Q_LORA]), (_C_KV, w_in[:, Q_LORA:Q_LORA + KV_LORA]),
                     (_C_KR + NOPE, w_in[:, Q_LORA + KV_LORA:A_IN]), (_C_PB, w_in[:, o_b:o_c]),
                     (_C_QC, w_in[:, o_c:o_c + 3 * D_C]), (_C_F, w_in[:, o_c + 3 * D_C:])], _C_END).astype(BF16)
    w_uq = g('mla_w_uq').reshape(Q_LORA, H_A, QK_DIM)
    w_uq = jnp.pad(w_uq, ((0, 256 - Q_LORA), (0, 0), (0, LANES - QK_DIM))).reshape(256, H_A * LANES).astype(BF16)
    w_ukv = g('mla_w_ukv').reshape(KV_LORA, H_A, NOPE + HEAD_DIM)
    w_uk = jnp.pad(w_ukv[:, :, :NOPE], ((0, 0), (0, 0), (0, LANES - NOPE))).reshape(KV_LORA, H_A * LANES).astype(BF16)
    w_uv = w_ukv[:, :, NOPE:].reshape(KV_LORA, H_A * HEAD_DIM).astype(BF16)
    zeros_w = jnp.zeros((W_LORA, D_B), F32)
    w_r = _place([(0, g('moe_w_rg')), (_L_EXP, g('moe_w_re'))], LANES)
    b_r = _place([(0, g('moe_b_rg').reshape(1, -1)), (_L_EXP, g('moe_b_re').reshape(1, -1))], LANES)
    return dict(
        g_mix=_row(g('g_mix')), w_in=w_in_p, g_qa=_row(g('mla_g_qa'), 256), w_uq=w_uq, g_kva=_row(g('mla_g_kva')),
        g_qn=_row(g('mla_g_qn'), LANES), g_fq=_row(jnp.tile(g('fox_g_qn'), H_C)),
        g_fk=_row(jnp.tile(g('fox_g_kn'), H_C)), b_f=_row(g('fox_b_f'), LANES),
        ones128=jnp.ones((LANES, LANES), BF16), eye128=jnp.eye(LANES, dtype=BF16),
        bd256=_block_diag_ones(D_C, HEAD_DIM), bd384=_block_diag_ones(D_B, HEAD_DIM),
        w_uk=w_uk, w_uv=w_uv, g_kn=_row(g('mla_g_kn')[:NOPE], LANES), g_kr=_row(g('mla_g_kn')[NOPE:], LANES, NOPE),
        rw_mu=_row(g('rw_mu')), rw_w0=_row(g('rw_w0')), rw_a0=_row(g('rw_a0')),
        rw_ww=jnp.concatenate([g('rw_w_up'), zeros_w], axis=0).astype(BF16),
        rw_wa=jnp.concatenate([zeros_w, g('rw_a_up')], axis=0).astype(BF16),
        rw_wg=g('rw_g_up').astype(BF16), rw_k_k=_row(g('rw_k_k')), rw_k_a=_row(g('rw_k_a')),
        rw_r_k=_row(g('rw_r_k')), rw_ln_w=_row(g('rw_ln_w')), rw_ln_b=_row(g('rw_ln_b')),
        w_out=g('w_out').astype(BF16), g_ffn=_row(g('g_ffn')), w_r=w_r, b_r=b_r,
        w_gu=jnp.concatenate([g('moe_w_gate'), g('moe_w_up')], axis=-1).astype(BF16),
        w_d=g('moe_w_down').astype(BF16).reshape(N_EXPERTS * D_FF_E, D_MODEL),
    )


def _rope_tables(pos):
    half = ROPE // 2
    inv = ROPE_BASE ** (-jnp.arange(half, dtype=F32) / half)
    ang = pos.astype(F32)[:, None] * inv[None, :]
    cos, sin = jnp.cos(ang), jnp.sin(ang)
    t = pos.shape[0]
    z = lambda w: jnp.zeros((t, w), F32)
    c = jnp.concatenate([jnp.ones((t, NOPE), F32), cos, cos, z(LANES - QK_DIM)], axis=1)
    s_left = jnp.concatenate([z(NOPE), -sin, z(half), z(LANES - QK_DIM)], axis=1)
    s_right = jnp.concatenate([z(NOPE), z(half), sin, z(LANES - QK_DIM)], axis=1)
    return c, s_left, s_right


def _pick(n, prefs):
    for t in prefs:
        if n % t == 0:
            return t
    return n


def _layer(x, lw, hist):
    b, t, _ = x.shape
    n = b * t
    past = 0 if hist is None else hist['ckv'].shape[1]
    x2d = x.reshape(n, D_MODEL)
    tm = _pick(t, (512, 256, 128, 64))
    tmp = _pick(past, (512, 256, 128, 64)) if past else 0
    eye = lw['eye128']

    q_tabs = _rope_tables(past + jnp.arange(t))
    ckv, kr128, kr, qt, pb, qct, kc, kcb, vc, vct, lf = _inproj(x2d, lw, q_tabs, b, t, tm)

    k_new, vt_new = _kvprep(ckv, kr128, lw, q_tabs, b, t, tm)
    if hist is None:
        kp = vpt = None
    else:
        ckv_past = hist['ckv'].reshape(b * past, KV_LORA)
        kr_past = jnp.pad(hist['krope'].reshape(b * past, ROPE), ((0, 0), (NOPE, LANES - QK_DIM)))
        kp, vpt = _kvprep(ckv_past, kr_past, lw, _rope_tables(jnp.arange(past)), b, past, tmp)
        kp = kp.reshape(b, past, -1)
    o_a = _attention(qt, k_new.reshape(b, t, -1), vt_new, kp, vpt, None, eye, n_pairs=H_A // 2, shared=False,
                     frame_causal=False, tq=tm, tk=min(tm, ATTN_TK), tkp=min(tmp, ATTN_TK))

    lf3 = lf.reshape(b, t, H_C)
    lf_all = lf3 if hist is None else jnp.concatenate([hist['flogf'].astype(F32), lf3], axis=1)
    ltot = past + t
    lpad = -(-ltot // LANES) * LANES
    lf_t = jnp.pad(jnp.swapaxes(lf_all, 1, 2), ((0, 0), (0, 0), (0, lpad - ltot))).reshape(b * H_C, lpad)
    c_all = _cumsum_lanes(lf_t).reshape(b, H_C // 2, 2, lpad)
    cq = c_all[..., past:past + t]
    ck = jnp.swapaxes(c_all, 2, 3)
    if hist is None:
        kcp = vcpt = None
    else:
        kcp = hist['fk'].reshape(b, past, D_C)
        vcpt = _transpose_cast(hist['fv'].reshape(b * past, D_C), lw, b, past, tmp)
    o_c = _attention(qct, kcb.reshape(b, t, D_C), vct, kcp, vcpt, (cq, ck), eye, n_pairs=H_C // 2, shared=True,
                     frame_causal=True, tq=tm, tk=min(tm, ATTN_TK), tkp=min(tmp, ATTN_TK))

    prev = jnp.zeros((b, 1, B_IN), F32) if hist is None else hist['shift'].astype(F32)
    pb3 = pb.reshape(b, t, B_IN)
    r, kk, k, bb, v, lwd, g = _rwkv_prep(pb3, prev, lw, _pick(t, (256, 128, 64)))
    rt, yl, gm, hm = _rwkv_chunk(r, kk, k, bb, v, lwd, _pick(t, (256, 128, 64)))
    npair = H_B // 2
    if hist is None:
        s0 = jnp.zeros((b, npair, LANES, LANES), F32)
    else:
        st = jnp.swapaxes(hist['wkv'].astype(F32), -1, -2).reshape(b, npair, 2, HEAD_DIM, HEAD_DIM)
        s0 = jnp.zeros((b, npair, 2, HEAD_DIM, 2, HEAD_DIM), F32)
        s0 = s0.at[:, :, 0, :, 0, :].set(st[:, :, 0]).at[:, :, 1, :, 1, :].set(st[:, :, 1])
        s0 = s0.reshape(b, npair, LANES, LANES)
    y, sfin = _rwkv_scan(s0, gm, hm, rt, yl, _pick(t // CHUNK, (8, 4, 2, 1)))
    flat = lambda a: a.reshape(n, D_B)
    o_b = _rwkv_post(flat(y), flat(r), flat(k), flat(v), flat(g), lw, tm)
    sf = sfin.reshape(b, npair, 2, HEAD_DIM, 2, HEAD_DIM)
    s_fin = jnp.stack([sf[:, :, 0, :, 0, :], sf[:, :, 1, :, 1, :]], axis=2).reshape(b, H_B, HEAD_DIM, HEAD_DIM)
    s_fin = jnp.swapaxes(s_fin, -1, -2)

    tmm = _pick(n, (1024, 512, 256, 128, 64))
    x_out = _outproj_moe(x2d, o_a.reshape(n, -1), flat(o_b), o_c.reshape(n, -1), lw, tmm).reshape(b, t, D_MODEL)

    new = (ckv.reshape(b, t, KV_LORA), kr.reshape(b, t, ROPE), kc.reshape(b, t, H_C, HEAD_DIM),
           vc.reshape(b, t, H_C, HEAD_DIM), lf3, s_fin, pb3[:, -1:])
    return x_out, new


def kernel(x_prompt, x_sample, cache_mla_latent, cache_mla_krope, cache_fox_k, cache_fox_v, cache_fox_logf,
           state_rwkv_wkv, state_rwkv_shift, g_mix, w_in, mla_g_qa, mla_w_uq, mla_g_kva, mla_w_ukv, mla_g_qn,
           mla_g_kn, rw_mu, rw_w0, rw_w_up, rw_a0, rw_a_up, rw_g_up, rw_k_k, rw_k_a, rw_r_k, rw_ln_w, rw_ln_b,
           fox_g_qn, fox_g_kn, fox_b_f, w_out, g_ffn, moe_w_rg, moe_b_rg, moe_w_re, moe_b_re, moe_w_gate,
           moe_w_up, moe_w_down):
    params = dict(g_mix=g_mix, w_in=w_in, mla_g_qa=mla_g_qa, mla_w_uq=mla_w_uq, mla_g_kva=mla_g_kva,
                  mla_w_ukv=mla_w_ukv, mla_g_qn=mla_g_qn, mla_g_kn=mla_g_kn, rw_mu=rw_mu, rw_w0=rw_w0,
                  rw_w_up=rw_w_up, rw_a0=rw_a0, rw_a_up=rw_a_up, rw_g_up=rw_g_up, rw_k_k=rw_k_k, rw_k_a=rw_k_a,
                  rw_r_k=rw_r_k, rw_ln_w=rw_ln_w, rw_ln_b=rw_ln_b, fox_g_qn=fox_g_qn, fox_g_kn=fox_g_kn,
                  fox_b_f=fox_b_f, w_out=w_out, g_ffn=g_ffn, moe_w_rg=moe_w_rg, moe_b_rg=moe_b_rg,
                  moe_w_re=moe_w_re, moe_b_re=moe_b_re, moe_w_gate=moe_w_gate, moe_w_up=moe_w_up,
                  moe_w_down=moe_w_down)
    depth = g_mix.shape[0]
    yp, ys = x_prompt, x_sample
    p_new, s_new = [], []
    for l in range(depth):
        lw = _layer_weights(params, l)
        hist = dict(ckv=cache_mla_latent[l], krope=cache_mla_krope[l], fk=cache_fox_k[l], fv=cache_fox_v[l],
                    flogf=cache_fox_logf[l], wkv=state_rwkv_wkv[l], shift=state_rwkv_shift[l])
        yp, np_l = _layer(yp, lw, None)
        ys, ns_l = _layer(ys, lw, hist)
        p_new.append(np_l)
        s_new.append(ns_l)
    p_out = tuple(jnp.stack(t) for t in zip(*p_new))
    s_out = tuple(jnp.stack(t) for t in zip(*s_new))
    return (yp, ys) + p_out + s_out
```

```python
import functools
import math

import jax
import jax.numpy as jnp
from jax import lax
from jax.experimental import pallas as pl
from jax.experimental.pallas import tpu as pltpu

F32 = jnp.float32
BF16 = jnp.bfloat16

D_MODEL = 1024
HEAD_DIM = 64
H_A, H_B, H_C = 6, 6, 4
Q_LORA, KV_LORA, NOPE, ROPE = 192, 128, 64, 32
QK_DIM = NOPE + ROPE
ROPE_BASE = 10000.0
A_IN = Q_LORA + KV_LORA + ROPE
D_B = H_B * HEAD_DIM
W_LORA, A_LORA, G_LORA = 64, 64, 128
B_IN = 3 * D_B + W_LORA + A_LORA + G_LORA
DECAY_SCALE = math.exp(-0.5)
GN_EPS = 64e-5
D_C = H_C * HEAD_DIM
C_IN = 3 * D_C + H_C
N_GROUPS, E_PER_GROUP = 4, 4
N_EXPERTS = N_GROUPS * E_PER_GROUP
D_FF_E = 256
NEG_INF = -1e30
RMS_EPS = 1e-6
CHUNK = 64
LOG2E = math.log2(math.e)

LANES = 128
BF16_ROWS = 16
VMEM_LIMIT = 56 * 1024 * 1024
ATTN_TK = 512
VT_ROWS = LANES + BF16_ROWS

_C_QL = 0
_C_KV = 256
_C_KR = 384
_C_PB = 512
_C_QC = _C_PB + B_IN
_C_KC = _C_QC + D_C
_C_VC = _C_KC + D_C
_C_F = _C_VC + D_C
_C_END = _C_F + LANES


def _cparams(sem):
    return pltpu.CompilerParams(dimension_semantics=sem, vmem_limit_bytes=VMEM_LIMIT)


def _dot(a, b):
    return jnp.dot(a, b, preferred_element_type=F32)


def _dot_nt(a, b):
    return lax.dot_general(a, b, (((1,), (1,)), ((), ())), preferred_element_type=F32)


def _dot_tn(a, b):
    return lax.dot_general(a, b, (((0,), (0,)), ((), ())), preferred_element_type=F32)


def _split2(x):
    hi = x.astype(BF16)
    lo = (x - hi.astype(F32)).astype(BF16)
    return hi, lo


def _split3(x):
    hi = x.astype(BF16)
    r = x - hi.astype(F32)
    mid = r.astype(BF16)
    lo = (r - mid.astype(F32)).astype(BF16)
    return hi, mid, lo


def _dot3(a, b, dot=_dot):
    ah, al = _split2(a)
    bh, bl = _split2(b)
    return dot(ah, bh) + (dot(ah, bl) + dot(al, bh))


def _mm(a, b, dot=_dot):
    return dot(a.astype(BF16), b.astype(BF16))


def _dot_sel(a, sel):
    return _dot(a.astype(BF16), sel)


def _iota(shape, dim):
    return lax.broadcasted_iota(jnp.int32, shape, dim)


def _rope(x, c, s_left, s_right):
    return x * c + pltpu.roll(x, 112, 1) * s_left + pltpu.roll(x, 16, 1) * s_right


def _sigmoid(x):
    return 1.0 / (1.0 + jnp.exp(-x))


def _log_sigmoid(x):
    return jnp.minimum(x, 0.0) - jnp.log(1.0 + jnp.exp(-jnp.abs(x)))


def _transpose_bf16(x, eye):
    return _dot_nt(eye, x.astype(BF16)).astype(BF16)


def _store_vt(vt_ref, pp, v_block, eye):
    base = pp * VT_ROWS
    vt_ref[0, base:base + LANES, :] = _transpose_bf16(v_block, eye)
    cols = vt_ref.shape[2]
    vt_ref[0, base + LANES:base + VT_ROWS, :] = jnp.where(_iota((BF16_ROWS, cols), 0) == 0, 1.0, 0.0).astype(BF16)


def _inproj_kernel(x_ref, gmix_ref, w_ref, gqa_ref, wuq_ref, gkva_ref, gqn_ref, gfq_ref, gfk_ref, bf_ref,
                   ones_ref, bd_ref, eye_ref, c_ref, sl_ref, sr_ref,
                   ckv_ref, kr128_ref, kr_ref, qt_ref, pb_ref, qct_ref, kc_ref, kcb_ref, vc_ref, vct_ref, lf_ref):
    x = x_ref[...]
    h = x * lax.rsqrt(jnp.mean(x * x, axis=-1, keepdims=True) + RMS_EPS) * gmix_ref[...]
    h = h.astype(BF16)
    eye = eye_ref[...]
    ones = ones_ref[...]
    bd = bd_ref[...]

    ql = _dot(h, w_ref[:, _C_QL:_C_QL + 256])
    kv = _dot(h, w_ref[:, _C_KV:_C_KV + KV_LORA])
    kr = _dot(h, w_ref[:, _C_KR:_C_KR + LANES])
    pb_ref[...] = _dot(h, w_ref[:, _C_PB:_C_PB + B_IN])
    qc = _dot(h, w_ref[:, _C_QC:_C_QC + D_C])
    kc = _dot(h, w_ref[:, _C_KC:_C_KC + D_C])
    vc = _dot(h, w_ref[:, _C_VC:_C_VC + D_C])
    f = _dot(h, w_ref[:, _C_F:_C_F + LANES]) + bf_ref[...]

    ql = ql * lax.rsqrt(jnp.sum(ql * ql, axis=-1, keepdims=True) * (1.0 / Q_LORA) + RMS_EPS) * gqa_ref[...]
    qh = _dot(ql.astype(BF16), wuq_ref[...])
    c, s_l, s_r = c_ref[...], sl_ref[...], sr_ref[...]
    qv = [qh[:, hh * LANES:(hh + 1) * LANES] for hh in range(H_A)]
    ss = [_dot_sel(q * q, ones) for q in qv]
    qn = [q * lax.rsqrt(s * (1.0 / QK_DIM) + RMS_EPS) * gqn_ref[...] for q, s in zip(qv, ss)]
    qr = [_rope(q, c, s_l, s_r) * (LOG2E * QK_DIM ** -0.5) for q in qn]
    for hh in range(H_A):
        qt_ref[0, hh * LANES:(hh + 1) * LANES, :] = _transpose_bf16(qr[hh], eye)

    ckv_ref[...] = kv * lax.rsqrt(jnp.mean(kv * kv, axis=-1, keepdims=True) + RMS_EPS) * gkva_ref[...]
    kr128_ref[...] = kr
    kr_ref[...] = kr[:, NOPE:NOPE + ROPE]

    qc = qc * lax.rsqrt(_dot_sel(qc * qc, bd) * (1.0 / HEAD_DIM) + RMS_EPS) * gfq_ref[...]
    qc = qc * (LOG2E * HEAD_DIM ** -0.5)
    kc = kc * lax.rsqrt(_dot_sel(kc * kc, bd) * (1.0 / HEAD_DIM) + RMS_EPS) * gfk_ref[...]
    kc_ref[...] = kc
    kcb_ref[...] = kc.astype(BF16)
    vc_ref[...] = vc
    for pp in range(D_C // LANES):
        qct_ref[0, pp * LANES:(pp + 1) * LANES, :] = _transpose_bf16(qc[:, pp * LANES:(pp + 1) * LANES], eye)
        _store_vt(vct_ref, pp, vc[:, pp * LANES:(pp + 1) * LANES], eye)
    lf_ref[...] = _log_sigmoid(f)[:, :H_C]


def _inproj(x2d, lw, tabs, bsz, t, tm):
    n = x2d.shape[0]
    nt = t // tm
    row = lambda i: (i, 0)
    fixed = lambda i: (0, 0)
    tab = lambda i: (i % nt, 0)
    colmajor = lambda i: (i // nt, 0, i % nt)
    full = lambda a: pl.BlockSpec(a.shape, fixed)
    params = [lw['g_mix'], lw['w_in'], lw['g_qa'], lw['w_uq'], lw['g_kva'], lw['g_qn'], lw['g_fq'], lw['g_fk'],
              lw['b_f'], lw['ones128'], lw['bd256'], lw['eye128']]
    rowout = lambda w, dt: (pl.BlockSpec((tm, w), row), jax.ShapeDtypeStruct((n, w), dt))
    colout = lambda w: (pl.BlockSpec((1, w, tm), colmajor), jax.ShapeDtypeStruct((bsz, w, t), BF16))
    outs = [rowout(KV_LORA, F32), rowout(LANES, F32), rowout(ROPE, F32), colout(H_A * LANES), rowout(B_IN, F32),
            colout(D_C), rowout(D_C, F32), rowout(D_C, BF16), rowout(D_C, F32), colout(H_C // 2 * VT_ROWS),
            rowout(H_C, F32)]
    return pl.pallas_call(
        _inproj_kernel,
        grid=(n // tm,),
        in_specs=[pl.BlockSpec((tm, D_MODEL), row)] + [full(a) for a in params]
                 + [pl.BlockSpec((tm, LANES), tab)] * 3,
        out_specs=[o[0] for o in outs],
        out_shape=[o[1] for o in outs],
        compiler_params=_cparams(("parallel",)),
        name="inproj",
    )(x2d, *params, *tabs)


def _kvprep_kernel(ckv_ref, kr_ref, wuk_ref, wuv_ref, gkn_ref, gkr_ref, ones_ref, eye_ref, c_ref, sl_ref, sr_ref,
                   k_ref, vt_ref):
    cb = ckv_ref[...].astype(BF16)
    kn = _dot(cb, wuk_ref[...])
    v = _dot(cb, wuv_ref[...])
    eye = eye_ref[...]
    for pp in range(H_A // 2):
        _store_vt(vt_ref, pp, v[:, pp * LANES:(pp + 1) * LANES], eye)
    ones = ones_ref[...]
    kr = kr_ref[...]
    ssr = _dot_sel(kr * kr, ones)
    krg = _rope(kr * gkr_ref[...], c_ref[...], sl_ref[...], sr_ref[...])
    knh = [kn[:, hh * LANES:(hh + 1) * LANES] for hh in range(H_A)]
    ssn = [_dot_sel(k * k, ones) for k in knh]
    for hh in range(H_A):
        r = lax.rsqrt((ssn[hh] + ssr) * (1.0 / QK_DIM) + RMS_EPS)
        k_ref[:, hh * LANES:(hh + 1) * LANES] = ((knh[hh] * gkn_ref[...] + krg) * r).astype(BF16)


def _kvprep(ckv2d, kr128_2d, lw, tabs, bsz, t, tm):
    n = ckv2d.shape[0]
    nt = t // tm
    row = lambda i: (i, 0)
    fixed = lambda i: (0, 0)
    tab = lambda i: (i % nt, 0)
    full = lambda a: pl.BlockSpec(a.shape, fixed)
    params = [lw['w_uk'], lw['w_uv'], lw['g_kn'], lw['g_kr'], lw['ones128'], lw['eye128']]
    vt_rows = H_A // 2 * VT_ROWS
    return pl.pallas_call(
        _kvprep_kernel,
        grid=(n // tm,),
        in_specs=[pl.BlockSpec((tm, KV_LORA), row), pl.BlockSpec((tm, LANES), row)] + [full(a) for a in params]
                 + [pl.BlockSpec((tm, LANES), tab)] * 3,
        out_specs=[pl.BlockSpec((tm, H_A * LANES), row),
                   pl.BlockSpec((1, vt_rows, tm), lambda i: (i // nt, 0, i % nt))],
        out_shape=[jax.ShapeDtypeStruct((n, H_A * LANES), BF16),
                   jax.ShapeDtypeStruct((bsz, vt_rows, t), BF16)],
        compiler_params=_cparams(("parallel",)),
        name="kvprep",
    )(ckv2d, kr128_2d, *params, *tabs)


def _transpose_kernel(x_ref, eye_ref, o_ref):
    eye = eye_ref[...]
    for pp in range(x_ref.shape[1] // LANES):
        _store_vt(o_ref, pp, x_ref[:, pp * LANES:(pp + 1) * LANES], eye)


def _transpose_cast(x2d, lw, bsz, t, tm):
    n, w = x2d.shape
    nt = t // tm
    rows = w // LANES * VT_ROWS
    return pl.pallas_call(
        _transpose_kernel,
        grid=(n // tm,),
        in_specs=[pl.BlockSpec((tm, w), lambda i: (i, 0)), pl.BlockSpec((LANES, LANES), lambda i: (0, 0))],
        out_specs=pl.BlockSpec((1, rows, tm), lambda i: (i // nt, 0, i % nt)),
        out_shape=jax.ShapeDtypeStruct((bsz, rows, t), BF16),
        compiler_params=_cparams(("parallel",)),
        name="transpose_cast",
    )(x2d, lw['eye128'])


def _attn_kernel(*refs, tq, tk, tkp, nq, n_past, n_pairs, shared, frame_causal, has_bias, pipelined):
    it = iter(refs)
    q_ref, k_ref, vt_ref = next(it), next(it), next(it)
    if n_past:
        kp_ref, vpt_ref = next(it), next(it)
    if has_bias:
        cq_ref, ck_ref = next(it), next(it)
    eye_ref, o_ref = next(it), next(it)
    q_scr, m_scr, acc_scr = next(it), next(it), next(it)

    qi = 0 if nq == 1 else pl.program_id(1)
    low = _iota((LANES, tq), 0) < HEAD_DIM
    blk = lambda p, j: p if shared else 2 * p + j
    heads = [(p, j) for p in range(n_pairs) for j in range(2)]
    nh = len(heads)
    for p, j in heads:
        q = q_ref[0, blk(p, j) * LANES:(blk(p, j) + 1) * LANES, :]
        if shared:
            keep = low if j == 0 else jnp.logical_not(low)
            q = jnp.where(keep, q, jnp.zeros_like(q))
        q_scr[2 * p + j] = q
    m_scr[...] = jnp.full(m_scr.shape, -jnp.inf, F32)
    acc_scr[...] = jnp.zeros(acc_scr.shape, F32)

    def score(kr, st, width, p, j):
        kt = kr[0, pl.ds(st, width), blk(p, j) * LANES:(blk(p, j) + 1) * LANES].astype(BF16)
        return _dot(kt, q_scr[2 * p + j])

    def scores_to(buf, kr, st, width):
        for p, j in heads:
            buf[2 * p + j] = score(kr, st, width, p, j)

    def consume(get_s, vr, st, width, key_start, mask):
        ss = []
        for p, j in heads:
            s = get_s(p, j)
            if has_bias:
                s = s - ck_ref[0, p, pl.ds(key_start, width), j:j + 1]
            if mask is not None:
                s = jnp.where(mask, s, NEG_INF)
            ss.append(s)
        cqs = [cq_ref[0, p, j:j + 1, :] if has_bias else 0.0 for p, j in heads]
        m_prev = [m_scr[h] for h in range(nh)]
        m_next = [jnp.maximum(m_prev[h], jnp.max(ss[h], axis=0, keepdims=True) + cqs[h]) for h in range(nh)]
        prs = [jnp.exp2(ss[h] - (m_next[h] - cqs[h])).astype(BF16) for h in range(nh)]
        alphas = [jnp.exp2(m_prev[h] - m_next[h]) for h in range(nh)]
        pvs = [_dot(vr[0, p * VT_ROWS:(p + 1) * VT_ROWS, pl.ds(st, width)], prs[2 * p + j]) for p, j in heads]
        for h in range(nh):
            acc_scr[h] = acc_scr[h] * alphas[h] + pvs[h]
            m_scr[h] = m_next[h]

    def step(kr, vr, st, width, key_start, mask):
        consume(lambda p, j: score(kr, st, width, p, j), vr, st, width, key_start, mask)

    kidx = _iota((tk, tq), 0)
    qidx = _iota((tk, tq), 1)
    diag_mask = lambda d: ((kidx + d * tk <= qidx) if frame_causal else
                           (jnp.right_shift(kidx + d * tk, 6) <= jnp.right_shift(qidx, 6)))

    if pipelined:
        sa, sb = next(it), next(it)
        tile = lambda t: pl.multiple_of(t * tk, tk)
        from_a = lambda p, j: sa[2 * p + j]
        from_b = lambda p, j: sb[2 * p + j]
        scores_to(sa, k_ref, tile(0), tk)

        def pair_body(u, carry):
            t0 = 2 * u
            scores_to(sb, k_ref, tile(t0 + 1), tk)
            consume(from_a, vt_ref, tile(t0), tk, tile(t0), None)
            scores_to(sa, k_ref, tile(t0 + 2), tk)
            consume(from_b, vt_ref, tile(t0 + 1), tk, tile(t0 + 1), None)
            return carry
        lax.fori_loop(0, qi // 2, pair_body, 0)
        odd = lax.rem(qi, 2) == 1

        @pl.when(jnp.logical_not(odd))
        def _():
            consume(from_a, vt_ref, tile(qi), tk, tile(qi), diag_mask(0))

        @pl.when(odd)
        def _():
            scores_to(sb, k_ref, tile(qi), tk)
            consume(from_a, vt_ref, tile(qi - 1), tk, tile(qi - 1), None)
            consume(from_b, vt_ref, tile(qi), tk, tile(qi), diag_mask(0))
    else:
        if n_past:
            def past_body(t, carry):
                st = pl.multiple_of(t * tkp, tkp)
                step(kp_ref, vpt_ref, st, tkp, st, None)
                return carry
            lax.fori_loop(0, n_past // tkp, past_body, 0)

        def new_body(t, carry):
            st = pl.multiple_of(t * tk, tk)
            step(k_ref, vt_ref, st, tk, n_past + st, None)
            return carry
        if nq > 1:
            lax.fori_loop(0, qi * (tq // tk), new_body, 0)

        for d in range(tq // tk):
            st = qi * tq + d * tk
            st = st if nq == 1 else pl.multiple_of(st, tk)
            step(k_ref, vt_ref, st, tk, n_past + st, diag_mask(d))

    for p in range(n_pairs):
        a, b = acc_scr[2 * p], acc_scr[2 * p + 1]
        ot = jnp.where(low, a[:LANES] / a[LANES:LANES + 1], b[:LANES] / b[LANES:LANES + 1])
        o_ref[0, :, p * LANES:(p + 1) * LANES] = _dot_tn(ot.astype(BF16), eye_ref[...]).astype(o_ref.dtype)


def _attention(qt, k_new, vt_new, k_past, vt_past, bias, eye, *, n_pairs, shared, frame_causal, tq, tk, tkp):
    b, t = k_new.shape[0], k_new.shape[1]
    n_past = 0 if k_past is None else k_past.shape[1]
    nq = t // tq
    wq = qt.shape[1]
    args = [qt, k_new, vt_new]
    specs = [pl.BlockSpec((1, wq, tq), lambda bi, i: (bi, 0, i)),
             pl.BlockSpec((1, t, wq), lambda bi, i: (bi, 0, 0)),
             pl.BlockSpec((1, n_pairs * VT_ROWS, t), lambda bi, i: (bi, 0, 0))]
    if n_past:
        args += [k_past, vt_past]
        specs += [pl.BlockSpec((1, n_past, wq), lambda bi, i: (bi, 0, 0)),
                  pl.BlockSpec((1, n_pairs * VT_ROWS, n_past), lambda bi, i: (bi, 0, 0))]
    if bias is not None:
        cq, ck = bias
        args += [cq, ck]
        specs += [pl.BlockSpec((1, n_pairs, 2, tq), lambda bi, i: (bi, 0, 0, i)),
                  pl.BlockSpec((1, n_pairs, ck.shape[2], 2), lambda bi, i: (bi, 0, 0, 0))]
    args.append(eye)
    specs.append(pl.BlockSpec((LANES, LANES), lambda bi, i: (0, 0)))
    pipelined = n_past == 0 and nq > 1 and tq == tk
    kern = functools.partial(_attn_kernel, tq=tq, tk=tk, tkp=tkp, nq=nq, n_past=n_past, n_pairs=n_pairs,
                             shared=shared, frame_causal=frame_causal, has_bias=bias is not None,
                             pipelined=pipelined)
    nh = 2 * n_pairs
    score_bufs = [pltpu.VMEM((nh, tk, tq), F32)] * 2 if pipelined else []
    return pl.pallas_call(
        kern,
        grid=(b, nq),
        in_specs=specs,
        out_specs=pl.BlockSpec((1, tq, n_pairs * LANES), lambda bi, i: (bi, i, 0)),
        out_shape=jax.ShapeDtypeStruct((b, t, n_pairs * LANES), BF16),
        scratch_shapes=[pltpu.VMEM((nh, LANES, tq), BF16), pltpu.VMEM((nh, 1, tq), F32),
                        pltpu.VMEM((nh, VT_ROWS, tq), F32)] + score_bufs,
        compiler_params=_cparams(("parallel", "arbitrary")),
        name="attn_fox" if shared else "attn_mla",
    )(*args)


def _cumsum_kernel(x_ref, tri_ref, o_ref):
    rows, n = x_ref.shape
    tri = tri_ref[...]
    carry = jnp.zeros((rows, 1), F32)
    for t in range(n // LANES):
        xt = x_ref[:, t * LANES:(t + 1) * LANES]
        hi, mid, lo = _split3(xt)
        o_ref[:, t * LANES:(t + 1) * LANES] = (_dot(hi, tri) + _dot(mid, tri) + _dot(lo, tri) + carry) * LOG2E
        carry = carry + jnp.sum(xt, axis=1, keepdims=True)


def _cumsum_lanes(x):
    rows, n = x.shape
    tri = (jnp.arange(LANES)[:, None] <= jnp.arange(LANES)[None, :]).astype(BF16)
    return pl.pallas_call(
        _cumsum_kernel,
        out_shape=jax.ShapeDtypeStruct((rows, n), F32),
        compiler_params=pltpu.CompilerParams(vmem_limit_bytes=VMEM_LIMIT),
        name="cumsum",
    )(x, tri)


def _rwkv_prep_kernel(pb_ref, prev_ref, mu_ref, w0_ref, ww_ref, a0_ref, wa_ref, wg_ref, kk_ref, ka_ref, bd_ref,
                      r_ref, kkn_ref, k_ref, b_ref, v_ref, lw_ref, g_ref, carry_ref):
    t = pl.program_id(1)
    tm = pb_ref.shape[1]

    @pl.when(t == 0)
    def _():
        carry_ref[0:1, :] = prev_ref[0]

    pbv = pb_ref[0]
    rolled = pltpu.roll(pbv, 1, 0)
    shifted = jnp.where(_iota(pbv.shape, 0) == 0, carry_ref[0:1, :], rolled)
    carry_ref[0:1, :] = pbv[tm - 1:tm, :]
    xs = pbv + (shifted - pbv) * mu_ref[...]

    r = xs[:, 0:D_B]
    kb = xs[:, D_B:2 * D_B]
    o3 = 3 * D_B
    wa_in = xs[:, o3:o3 + LANES]
    wa_in = jnp.where(_iota(wa_in.shape, 1) < W_LORA, jnp.tanh(wa_in), wa_in).astype(BF16)
    lw_ref[0] = -DECAY_SCALE * _sigmoid(w0_ref[...] + _dot(wa_in, ww_ref[...]))
    a = _sigmoid(a0_ref[...] + _dot(wa_in, wa_ref[...]))
    g_ref[0] = _dot(_sigmoid(xs[:, o3 + LANES:o3 + 2 * LANES]).astype(BF16), wg_ref[...])
    kk = kb * kk_ref[...]
    kk = kk * lax.rsqrt(_dot_sel(kk * kk, bd_ref[...]) + 1e-12)
    r_ref[0] = r
    kkn_ref[0] = kk
    k_ref[0] = kb * (1.0 + (a - 1.0) * ka_ref[...])
    b_ref[0] = kk * a
    v_ref[0] = xs[:, 2 * D_B:3 * D_B]


def _rwkv_prep(pb3, prev, lw, tm):
    b, t, _ = pb3.shape
    fixed = lambda bi, i: (0, 0)
    full = lambda a: pl.BlockSpec(a.shape, fixed)
    params = [lw['rw_mu'], lw['rw_w0'], lw['rw_ww'], lw['rw_a0'], lw['rw_wa'], lw['rw_wg'], lw['rw_k_k'],
              lw['rw_k_a'], lw['bd384']]
    out_spec = pl.BlockSpec((1, tm, D_B), lambda bi, i: (bi, i, 0))
    return pl.pallas_call(
        _rwkv_prep_kernel,
        grid=(b, t // tm),
        in_specs=[pl.BlockSpec((1, tm, B_IN), lambda bi, i: (bi, i, 0)),
                  pl.BlockSpec((1, 1, B_IN), lambda bi, i: (bi, 0, 0))] + [full(a) for a in params],
        out_specs=[out_spec] * 7,
        out_shape=[jax.ShapeDtypeStruct((b, t, D_B), F32)] * 7,
        scratch_shapes=[pltpu.VMEM((8, B_IN), F32)],
        compiler_params=_cparams(("parallel", "arbitrary")),
        name="rwkv_prep",
    )(pb3, prev, *params)


def _rwkv_chunk_kernel(r_ref, kk_ref, k_ref, b_ref, v_ref, lw_ref, rt_ref, yl_ref, g_ref, h_ref):
    s = r_ref.shape[1]
    nc = s // CHUNK
    row = _iota((s, LANES), 0)
    rin = jnp.bitwise_and(row, CHUNK - 1)
    ti = _iota((s, s), 0)
    si = _iota((s, s), 1)
    same = jnp.right_shift(ti, 6) == jnp.right_shift(si, 6)
    strict = jnp.logical_and(same, si < ti)
    incl = jnp.logical_and(same, si <= ti)
    eye = (ti == si).astype(F32)
    low = _iota((s, LANES), 1) < HEAD_DIM
    ji = _iota((LANES, LANES), 0)
    jj = _iota((LANES, LANES), 1)
    blockdiag = (ji < HEAD_DIM) == (jj < HEAD_DIM)
    npair = r_ref.shape[2] // LANES
    pairs = range(npair)
    heads = [(p, x) for p in pairs for x in range(2)]

    tot, kkt, rt, khbh, kw, bw, v = [], [], [], [], [], [], []
    for p in pairs:
        sl = slice(p * LANES, (p + 1) * LANES)
        lw, kp, bp = lw_ref[0, :, sl], k_ref[0, :, sl], b_ref[0, :, sl]
        cl = lw
        for sh in (1, 2, 4, 8, 16, 32):
            cl = cl + jnp.where(rin >= sh, pltpu.roll(cl, sh, 0), 0.0)
        tp = jnp.concatenate(
            [jnp.broadcast_to(cl[c * CHUNK + CHUNK - 1:(c + 1) * CHUNK, :], (CHUNK, LANES)) for c in range(nc)],
            axis=0)
        e_ncl = jnp.exp(-cl)
        e_rem = jnp.exp(tp - cl)
        tot.append(tp)
        kkt.append(kk_ref[0, :, sl] * jnp.exp(cl - lw))
        rt.append(r_ref[0, :, sl] * jnp.exp(cl))
        khbh.append(jnp.concatenate([kp * e_ncl, bp * e_ncl], axis=0).astype(BF16))
        kw.append(kp * e_rem)
        bw.append(bp * e_rem)
        v.append(v_ref[0, :, sl])

    keep = lambda x: low if x == 0 else jnp.logical_not(low)
    kkt_x = [jnp.where(keep(x), kkt[p], 0.0) for p, x in heads]
    rt_x = [jnp.where(keep(x), rt[p], 0.0) for p, x in heads]
    akk, ark, arb, pw, tinv = [], [], [], [], []
    for i, (p, x) in enumerate(heads):
        p4 = _dot_nt(jnp.concatenate([kkt_x[i], rt_x[i]], axis=0).astype(BF16), khbh[p])
        akk.append(jnp.where(strict, p4[:s, :s], 0.0))
        ark.append(jnp.where(incl, p4[s:, :s], 0.0))
        arb.append(jnp.where(incl, p4[s:, s:], 0.0))
        pw.append(jnp.where(strict, -p4[:s, s:], 0.0))
        tinv.append(eye + pw[i])
    for _ in range(5):
        pw = [_mm(m, m) for m in pw]
        tinv = [t + _mm(t, m) for t, m in zip(tinv, pw)]
    av = [_mm(jnp.concatenate([akk[i], ark[i]], axis=0), v[p]) for i, (p, x) in enumerate(heads)]
    tx = [_mm(tinv[i], jnp.concatenate([av[i][:s], kkt_x[i]], axis=1)) for i in range(len(heads))]
    ax = [_mm(arb[i], tx[i]) for i in range(len(heads))]

    for p in pairs:
        sl = slice(p * LANES, (p + 1) * LANES)
        a, b2 = 2 * p, 2 * p + 1
        uloc2 = jnp.where(low, tx[a][:, :LANES], tx[b2][:, :LANES])
        kkt2 = jnp.where(low, tx[a][:, LANES:], tx[b2][:, LANES:])
        rt_ref[0, :, sl] = jnp.where(low, rt_x[a] - ax[a][:, LANES:], rt_x[b2] - ax[b2][:, LANES:])
        yl_ref[0, :, sl] = jnp.where(low, av[a][s:] - ax[a][:, :LANES], av[b2][s:] - ax[b2][:, :LANES])
        for c in range(nc):
            inc = jnp.right_shift(row, 6) == c
            bw_c = jnp.where(inc, bw[p], 0.0)
            kw_c = jnp.where(inc, kw[p], 0.0)
            e_tot = jnp.exp(tot[p][c * CHUNK:c * CHUNK + 1, :])
            gm = jnp.where(ji == jj, e_tot, 0.0) - _mm(bw_c, kkt2, _dot_tn)
            hm = _mm(kw_c, v[p], _dot_tn) - _mm(bw_c, uloc2, _dot_tn)
            g_ref[0, p, c] = jnp.where(blockdiag, gm, 0.0)
            h_ref[0, p, c] = jnp.where(blockdiag, hm, 0.0)


def _rwkv_chunk(r, kk, k, b, v, lw, s):
    bsz, t, _ = r.shape
    npair = H_B // 2
    nc = t // CHUNK
    tok = pl.BlockSpec((1, s, D_B), lambda bi, i: (bi, i, 0))
    mat = pl.BlockSpec((1, npair, s // CHUNK, LANES, LANES), lambda bi, i: (bi, 0, i, 0, 0))
    return pl.pallas_call(
        _rwkv_chunk_kernel,
        grid=(bsz, t // s),
        in_specs=[tok] * 6,
        out_specs=[tok, tok, mat, mat],
        out_shape=[jax.ShapeDtypeStruct((bsz, t, D_B), F32)] * 2
                  + [jax.ShapeDtypeStruct((bsz, npair, nc, LANES, LANES), F32)] * 2,
        compiler_params=_cparams(("parallel", "parallel")),
        name="rwkv_chunk",
    )(r, kk, k, b, v, lw)


def _rwkv_scan_kernel(s0_ref, g_ref, h_ref, rt_ref, yl_ref, r_ref, k_ref, v_ref, gate_ref,
                      lnw_ref, lnb_ref, rk_ref, bd_ref, o_ref, sfin_ref, st_scr, y_scr):
    npair, ncb = g_ref.shape[1], g_ref.shape[2]

    @pl.when(pl.program_id(1) == 0)
    def _():
        st_scr[...] = s0_ref[0]

    sts = [st_scr[p] for p in range(npair)]
    for c in range(ncb):
        rows = slice(c * CHUNK, (c + 1) * CHUNK)
        for p in range(npair):
            sl = slice(p * LANES, (p + 1) * LANES)
            y_scr[rows, sl] = _dot3(rt_ref[0, rows, sl], sts[p]) + yl_ref[0, rows, sl]
            sts[p] = _dot3(g_ref[0, p, c], sts[p]) + h_ref[0, p, c]
    for p in range(npair):
        st_scr[p] = sts[p]
        sfin_ref[0, p] = sts[p]

    bd = bd_ref[...]
    y = y_scr[...]
    mu = _dot_sel(y, bd) * (1.0 / HEAD_DIM)
    d = y - mu
    var = _dot_sel(d * d, bd) * (1.0 / HEAD_DIM)
    yn = d * lax.rsqrt(var + GN_EPS) * lnw_ref[...] + lnb_ref[...]
    bonus = _dot_sel(r_ref[0] * k_ref[0] * rk_ref[...], bd) * v_ref[0]
    o_ref[0] = ((yn + bonus) * gate_ref[0]).astype(BF16)


def _rwkv_scan(s0, g, h, rt, yl, r, k, v, gate, lw, ncb):
    bsz, t, _ = rt.shape
    npair = H_B // 2
    nc = t // CHUNK
    st_spec = pl.BlockSpec((1, npair, LANES, LANES), lambda bi, i: (bi, 0, 0, 0))
    mat = pl.BlockSpec((1, npair, ncb, LANES, LANES), lambda bi, i: (bi, 0, i, 0, 0))
    tok = pl.BlockSpec((1, ncb * CHUNK, D_B), lambda bi, i: (bi, i, 0))
    params = [lw['rw_ln_w'], lw['rw_ln_b'], lw['rw_r_k'], lw['bd384']]
    full = lambda a: pl.BlockSpec(a.shape, lambda bi, i: (0, 0))
    return pl.pallas_call(
        _rwkv_scan_kernel,
        grid=(bsz, nc // ncb),
        in_specs=[st_spec, mat, mat] + [tok] * 6 + [full(a) for a in params],
        out_specs=[tok, st_spec],
        out_shape=[jax.ShapeDtypeStruct((bsz, t, D_B), BF16), jax.ShapeDtypeStruct((bsz, npair, LANES, LANES), F32)],
        scratch_shapes=[pltpu.VMEM((npair, LANES, LANES), F32), pltpu.VMEM((ncb * CHUNK, D_B), F32)],
        compiler_params=_cparams(("parallel", "arbitrary")),
        name="rwkv_scan",
    )(s0, g, h, rt, yl, r, k, v, gate, *params)


_L_EXP = 16


def _moe_kernel(x_ref, oa_ref, ob_ref, oc_ref, wo_ref, gffn_ref, wr_ref, br_ref, wg_ref, wu_ref, wd_ref,
                o_ref, acc_ref, h_ref, comb_ref):
    e = pl.program_id(1)

    @pl.when(e == 0)
    def _():
        da = H_A * HEAD_DIM
        x1 = (x_ref[...] + _dot(oa_ref[...], wo_ref[0:da, :]) + _dot(ob_ref[...], wo_ref[da:da + D_B, :])
              + _dot(oc_ref[...], wo_ref[da + D_B:, :]))
        acc_ref[...] = x1
        hf = x1 * lax.rsqrt(jnp.mean(x1 * x1, axis=-1, keepdims=True) + RMS_EPS) * gffn_ref[...]
        h_ref[...] = hf.astype(BF16)

        logit = _dot3(hf, wr_ref[...]) + br_ref[...]
        lane_i = _iota(logit.shape, 1)
        lane = lane_i.astype(F32)
        big = jnp.float32(3e38)
        is_g = lane_i < N_GROUPS
        gl = jnp.where(is_g, logit, -big)
        gmax = jnp.max(gl, axis=1, keepdims=True)
        pg_top = 1.0 / jnp.sum(jnp.where(is_g, jnp.exp(gl - gmax), 0.0), axis=1, keepdims=True)
        g_idx = jnp.min(jnp.where(jnp.logical_and(is_g, gl == gmax), lane, big), axis=1, keepdims=True)
        el = lane_i - _L_EXP
        in_e = jnp.logical_and(el >= 0, el < N_EXPERTS)
        sel = jnp.logical_and(in_e, jnp.right_shift(el, 2).astype(F32) == g_idx)
        l1 = jnp.where(sel, logit, -big)
        v1 = jnp.max(l1, axis=1, keepdims=True)
        i1 = jnp.min(jnp.where(jnp.logical_and(sel, l1 == v1), lane, big), axis=1, keepdims=True)
        sel2 = jnp.logical_and(sel, lane != i1)
        l2 = jnp.where(sel2, logit, -big)
        v2 = jnp.max(l2, axis=1, keepdims=True)
        i2 = jnp.min(jnp.where(jnp.logical_and(sel2, l2 == v2), lane, big), axis=1, keepdims=True)
        e2 = jnp.exp(v2 - v1)
        den = 1.0 / (1.0 + e2)
        comb_ref[...] = (jnp.where(lane == i1, den * pg_top, 0.0) + jnp.where(lane == i2, e2 * den * pg_top, 0.0))

    h = h_ref[...]
    comb = comb_ref[...]
    lane_c = _iota(comb.shape, 1)
    gates = [_dot(h, wg_ref[j]) for j in range(E_PER_GROUP)]
    ups = [_dot(h, wu_ref[j]) for j in range(E_PER_GROUP)]
    acts = []
    for j in range(E_PER_GROUP):
        ce = jnp.sum(jnp.where(lane_c == e * E_PER_GROUP + j + _L_EXP, comb, 0.0), axis=1, keepdims=True)
        acts.append((gates[j] * _sigmoid(gates[j]) * ups[j] * ce).astype(BF16))
    acc_ref[...] += _dot(jnp.concatenate(acts, axis=1), wd_ref[...])

    @pl.when(e == N_GROUPS - 1)
    def _():
        o_ref[...] = acc_ref[...]


def _outproj_moe(x2d, oa, ob, oc, lw, tm):
    n = x2d.shape[0]
    row = lambda w: pl.BlockSpec((tm, w), lambda i, e: (i, 0))
    full = lambda a: pl.BlockSpec(a.shape, lambda i, e: (0, 0))
    expert_in = pl.BlockSpec((E_PER_GROUP, D_MODEL, D_FF_E), lambda i, e: (e, 0, 0))
    return pl.pallas_call(
        _moe_kernel,
        grid=(n // tm, N_GROUPS),
        in_specs=[row(D_MODEL), row(H_A * HEAD_DIM), row(D_B), row(D_C), full(lw['w_out']), full(lw['g_ffn']),
                  full(lw['w_r']), full(lw['b_r']), expert_in, expert_in,
                  pl.BlockSpec((E_PER_GROUP * D_FF_E, D_MODEL), lambda i, e: (e, 0))],
        out_specs=row(D_MODEL),
        out_shape=jax.ShapeDtypeStruct((n, D_MODEL), F32),
        scratch_shapes=[pltpu.VMEM((tm, D_MODEL), F32), pltpu.VMEM((tm, D_MODEL), BF16), pltpu.VMEM((tm, LANES), F32)],
        compiler_params=_cparams(("parallel", "arbitrary")),
        name="outproj_moe",
    )(x2d, oa, ob, oc, lw['w_out'], lw['g_ffn'], lw['w_r'], lw['b_r'], lw['w_g'], lw['w_u'], lw['w_d'])


def _place(pieces, width):
    rows = pieces[0][1].shape[0]
    cols, at = [], 0
    for off, a in pieces:
        if off > at:
            cols.append(jnp.zeros((rows, off - at), F32))
        cols.append(a.astype(F32))
        at = off + a.shape[1]
    if width > at:
        cols.append(jnp.zeros((rows, width - at), F32))
    return jnp.concatenate(cols, axis=1)


def _row(v, width=None, off=0):
    v = v.reshape(1, -1).astype(F32)
    return v if width is None else _place([(off, v)], width)


def _block_diag_ones(n, blk):
    i = jnp.arange(n) // blk
    return (i[:, None] == i[None, :]).astype(BF16)


def _layer_weights(p, l):
    g = lambda name: p[name][l]
    w_in = g('w_in')
    o_b, o_c = A_IN, A_IN + B_IN
    w_in_p = _place([(_C_QL, w_in[:, :Q_LORA]), (_C_KV, w_in[:, Q_LORA:Q_LORA + KV_LORA]),
                     (_C_KR + NOPE, w_in[:, Q_LORA + KV_LORA:A_IN]), (_C_PB, w_in[:, o_b:o_c + 3 * D_C]),
                     (_C_F, w_in[:, o_c + 3 * D_C:])], _C_END).astype(BF16)
    w_uq = g('mla_w_uq').reshape(Q_LORA, H_A, QK_DIM)
    w_uq = jnp.pad(w_uq, ((0, 256 - Q_LORA), (0, 0), (0, LANES - QK_DIM))).reshape(256, H_A * LANES).astype(BF16)
    w_ukv = g('mla_w_ukv').reshape(KV_LORA, H_A, NOPE + HEAD_DIM)
    w_uk = jnp.pad(w_ukv[:, :, :NOPE], ((0, 0), (0, 0), (0, LANES - NOPE))).reshape(KV_LORA, H_A * LANES).astype(BF16)
    w_uv = w_ukv[:, :, NOPE:].reshape(KV_LORA, H_A * HEAD_DIM).astype(BF16)
    zeros_w = jnp.zeros((W_LORA, D_B), F32)
    w_r = _place([(0, g('moe_w_rg')), (_L_EXP, g('moe_w_re'))], LANES)
    b_r = _place([(0, g('moe_b_rg').reshape(1, -1)), (_L_EXP, g('moe_b_re').reshape(1, -1))], LANES)
    return dict(
        g_mix=_row(g('g_mix')), w_in=w_in_p, g_qa=_row(g('mla_g_qa'), 256), w_uq=w_uq, g_kva=_row(g('mla_g_kva')),
        g_qn=_row(g('mla_g_qn'), LANES), g_fq=_row(jnp.tile(g('fox_g_qn'), H_C)),
        g_fk=_row(jnp.tile(g('fox_g_kn'), H_C)), b_f=_row(g('fox_b_f'), LANES),
        ones128=jnp.ones((LANES, LANES), BF16), eye128=jnp.eye(LANES, dtype=BF16),
        bd256=_block_diag_ones(D_C, HEAD_DIM), bd384=_block_diag_ones(D_B, HEAD_DIM),
        w_uk=w_uk, w_uv=w_uv, g_kn=_row(g('mla_g_kn')[:NOPE], LANES), g_kr=_row(g('mla_g_kn')[NOPE:], LANES, NOPE),
        rw_mu=_row(g('rw_mu')), rw_w0=_row(g('rw_w0')), rw_a0=_row(g('rw_a0')),
        rw_ww=jnp.concatenate([g('rw_w_up'), zeros_w], axis=0).astype(BF16),
        rw_wa=jnp.concatenate([zeros_w, g('rw_a_up')], axis=0).astype(BF16),
        rw_wg=g('rw_g_up').astype(BF16), rw_k_k=_row(g('rw_k_k')), rw_k_a=_row(g('rw_k_a')),
        rw_r_k=_row(g('rw_r_k')), rw_ln_w=_row(g('rw_ln_w')), rw_ln_b=_row(g('rw_ln_b')),
        w_out=g('w_out').astype(BF16), g_ffn=_row(g('g_ffn')), w_r=w_r, b_r=b_r,
        w_g=g('moe_w_gate').astype(BF16), w_u=g('moe_w_up').astype(BF16),
        w_d=g('moe_w_down').astype(BF16).reshape(N_EXPERTS * D_FF_E, D_MODEL),
    )


def _rope_tables(pos):
    half = ROPE // 2
    inv = ROPE_BASE ** (-jnp.arange(half, dtype=F32) / half)
    ang = pos.astype(F32)[:, None] * inv[None, :]
    cos, sin = jnp.cos(ang), jnp.sin(ang)
    t = pos.shape[0]
    z = lambda w: jnp.zeros((t, w), F32)
    c = jnp.concatenate([jnp.ones((t, NOPE), F32), cos, cos, z(LANES - QK_DIM)], axis=1)
    s_left = jnp.concatenate([z(NOPE), -sin, z(half), z(LANES - QK_DIM)], axis=1)
    s_right = jnp.concatenate([z(NOPE), z(half), sin, z(LANES - QK_DIM)], axis=1)
    return c, s_left, s_right


def _pick(n, prefs):
    for t in prefs:
        if n % t == 0:
            return t
    return n


def _layer(x, lw, hist):
    b, t, _ = x.shape
    n = b * t
    past = 0 if hist is None else hist['ckv'].shape[1]
    x2d = x.reshape(n, D_MODEL)
    tm = _pick(t, (512, 256, 128, 64))
    tmp = _pick(past, (512, 256, 128, 64)) if past else 0
    eye = lw['eye128']

    q_tabs = _rope_tables(past + jnp.arange(t))
    ckv, kr128, kr, qt, pb, qct, kc, kcb, vc, vct, lf = _inproj(x2d, lw, q_tabs, b, t, tm)

    k_new, vt_new = _kvprep(ckv, kr128, lw, q_tabs, b, t, tm)
    if hist is None:
        kp = vpt = None
    else:
        ckv_past = hist['ckv'].reshape(b * past, KV_LORA)
        kr_past = jnp.pad(hist['krope'].reshape(b * past, ROPE), ((0, 0), (NOPE, LANES - QK_DIM)))
        kp, vpt = _kvprep(ckv_past, kr_past, lw, _rope_tables(jnp.arange(past)), b, past, tmp)
        kp = kp.reshape(b, past, -1)
    o_a = _attention(qt, k_new.reshape(b, t, -1), vt_new, kp, vpt, None, eye, n_pairs=H_A // 2, shared=False,
                     frame_causal=False, tq=tm, tk=min(tm, ATTN_TK), tkp=min(tmp, ATTN_TK))

    lf3 = lf.reshape(b, t, H_C)
    lf_all = lf3 if hist is None else jnp.concatenate([hist['flogf'].astype(F32), lf3], axis=1)
    ltot = past + t
    lpad = -(-ltot // LANES) * LANES
    lf_t = jnp.pad(jnp.swapaxes(lf_all, 1, 2), ((0, 0), (0, 0), (0, lpad - ltot))).reshape(b * H_C, lpad)
    c_all = _cumsum_lanes(lf_t).reshape(b, H_C // 2, 2, lpad)
    cq = c_all[..., past:past + t]
    ck = jnp.swapaxes(c_all, 2, 3)
    if hist is None:
        kcp = vcpt = None
    else:
        kcp = hist['fk'].reshape(b, past, D_C)
        vcpt = _transpose_cast(hist['fv'].reshape(b * past, D_C), lw, b, past, tmp)
    o_c = _attention(qct, kcb.reshape(b, t, D_C), vct, kcp, vcpt, (cq, ck), eye, n_pairs=H_C // 2, shared=True,
                     frame_causal=True, tq=tm, tk=min(tm, ATTN_TK), tkp=min(tmp, ATTN_TK))

    prev = jnp.zeros((b, 1, B_IN), F32) if hist is None else hist['shift'].astype(F32)
    pb3 = pb.reshape(b, t, B_IN)
    r, kk, k, bb, v, lwd, g = _rwkv_prep(pb3, prev, lw, _pick(t, (256, 128, 64)))
    rt, yl, gm, hm = _rwkv_chunk(r, kk, k, bb, v, lwd, _pick(t, (256, 128, 64)))
    npair = H_B // 2
    if hist is None:
        s0 = jnp.zeros((b, npair, LANES, LANES), F32)
    else:
        st = jnp.swapaxes(hist['wkv'].astype(F32), -1, -2).reshape(b, npair, 2, HEAD_DIM, HEAD_DIM)
        s0 = jnp.zeros((b, npair, 2, HEAD_DIM, 2, HEAD_DIM), F32)
        s0 = s0.at[:, :, 0, :, 0, :].set(st[:, :, 0]).at[:, :, 1, :, 1, :].set(st[:, :, 1])
        s0 = s0.reshape(b, npair, LANES, LANES)
    o_b, sfin = _rwkv_scan(s0, gm, hm, rt, yl, r, k, v, g, lw, _pick(t // CHUNK, (8, 4, 2, 1)))
    sf = sfin.reshape(b, npair, 2, HEAD_DIM, 2, HEAD_DIM)
    s_fin = jnp.stack([sf[:, :, 0, :, 0, :], sf[:, :, 1, :, 1, :]], axis=2).reshape(b, H_B, HEAD_DIM, HEAD_DIM)
    s_fin = jnp.swapaxes(s_fin, -1, -2)

    tmm = _pick(n, (1024, 512, 256, 128, 64))
    x_out = _outproj_moe(x2d, o_a.reshape(n, -1), o_b.reshape(n, D_B), o_c.reshape(n, -1), lw, tmm)

    new = (ckv.reshape(b, t, KV_LORA), kr.reshape(b, t, ROPE), kc.reshape(b, t, H_C, HEAD_DIM),
           vc.reshape(b, t, H_C, HEAD_DIM), lf3, s_fin, pb3[:, -1:])
    return x_out.reshape(b, t, D_MODEL), new


def kernel(x_prompt, x_sample, cache_mla_latent, cache_mla_krope, cache_fox_k, cache_fox_v, cache_fox_logf,
           state_rwkv_wkv, state_rwkv_shift, g_mix, w_in, mla_g_qa, mla_w_uq, mla_g_kva, mla_w_ukv, mla_g_qn,
           mla_g_kn, rw_mu, rw_w0, rw_w_up, rw_a0, rw_a_up, rw_g_up, rw_k_k, rw_k_a, rw_r_k, rw_ln_w, rw_ln_b,
           fox_g_qn, fox_g_kn, fox_b_f, w_out, g_ffn, moe_w_rg, moe_b_rg, moe_w_re, moe_b_re, moe_w_gate,
           moe_w_up, moe_w_down):
    params = dict(g_mix=g_mix, w_in=w_in, mla_g_qa=mla_g_qa, mla_w_uq=mla_w_uq, mla_g_kva=mla_g_kva,
                  mla_w_ukv=mla_w_ukv, mla_g_qn=mla_g_qn, mla_g_kn=mla_g_kn, rw_mu=rw_mu, rw_w0=rw_w0,
                  rw_w_up=rw_w_up, rw_a0=rw_a0, rw_a_up=rw_a_up, rw_g_up=rw_g_up, rw_k_k=rw_k_k, rw_k_a=rw_k_a,
                  rw_r_k=rw_r_k, rw_ln_w=rw_ln_w, rw_ln_b=rw_ln_b, fox_g_qn=fox_g_qn, fox_g_kn=fox_g_kn,
                  fox_b_f=fox_b_f, w_out=w_out, g_ffn=g_ffn, moe_w_rg=moe_w_rg, moe_b_rg=moe_b_rg,
                  moe_w_re=moe_w_re, moe_b_re=moe_b_re, moe_w_gate=moe_w_gate, moe_w_up=moe_w_up,
                  moe_w_down=moe_w_down)
    depth = g_mix.shape[0]
    yp, ys = x_prompt, x_sample
    p_new, s_new = [], []
    for l in range(depth):
        lw = _layer_weights(params, l)
        hist = dict(ckv=cache_mla_latent[l], krope=cache_mla_krope[l], fk=cache_fox_k[l], fv=cache_fox_v[l],
                    flogf=cache_fox_logf[l], wkv=state_rwkv_wkv[l], shift=state_rwkv_shift[l])
        yp, np_l = _layer(yp, lw, None)
        ys, ns_l = _layer(ys, lw, hist)
        p_new.append(np_l)
        s_new.append(ns_l)
    p_out = tuple(jnp.stack(t) for t in zip(*p_new))
    s_out = tuple(jnp.stack(t) for t in zip(*s_new))
    return (yp, ys) + p_out + s_out
```

```python
import functools
import math

import jax
import jax.numpy as jnp
from jax import lax
from jax.experimental import pallas as pl
from jax.experimental.pallas import tpu as pltpu

F32 = jnp.float32
BF16 = jnp.bfloat16

D_MODEL = 1024
HEAD_DIM = 64
H_A, H_B, H_C = 6, 6, 4
Q_LORA, KV_LORA, NOPE, ROPE = 192, 128, 64, 32
QK_DIM = NOPE + ROPE
ROPE_BASE = 10000.0
A_IN = Q_LORA + KV_LORA + ROPE
D_B = H_B * HEAD_DIM
W_LORA, A_LORA, G_LORA = 64, 64, 128
B_IN = 3 * D_B + W_LORA + A_LORA + G_LORA
DECAY_SCALE = math.exp(-0.5)
GN_EPS = 64e-5
D_C = H_C * HEAD_DIM
C_IN = 3 * D_C + H_C
N_GROUPS, E_PER_GROUP = 4, 4
N_EXPERTS = N_GROUPS * E_PER_GROUP
D_FF_E = 256
NEG_INF = -1e30
RMS_EPS = 1e-6
CHUNK = 64
LOG2E = math.log2(math.e)

LANES = 128
BF16_ROWS = 16
VMEM_LIMIT = 56 * 1024 * 1024
ATTN_TK = 512
VT_ROWS = LANES + BF16_ROWS

_C_QL = 0
_C_KV = 256
_C_KR = 384
_C_PB = 512
_C_QC = _C_PB + B_IN
_C_KC = _C_QC + D_C
_C_VC = _C_KC + D_C
_C_F = _C_VC + D_C
_C_END = _C_F + LANES


def _cparams(sem):
    return pltpu.CompilerParams(dimension_semantics=sem, vmem_limit_bytes=VMEM_LIMIT)


def _dot(a, b):
    return jnp.dot(a, b, preferred_element_type=F32)


def _dot_nt(a, b):
    return lax.dot_general(a, b, (((1,), (1,)), ((), ())), preferred_element_type=F32)


def _dot_tn(a, b):
    return lax.dot_general(a, b, (((0,), (0,)), ((), ())), preferred_element_type=F32)


def _split2(x):
    hi = x.astype(BF16)
    lo = (x - hi.astype(F32)).astype(BF16)
    return hi, lo


def _split3(x):
    hi = x.astype(BF16)
    r = x - hi.astype(F32)
    mid = r.astype(BF16)
    lo = (r - mid.astype(F32)).astype(BF16)
    return hi, mid, lo


def _dot3(a, b, dot=_dot):
    ah, al = _split2(a)
    bh, bl = _split2(b)
    return dot(ah, bh) + (dot(ah, bl) + dot(al, bh))


def _mm(a, b, dot=_dot):
    return dot(a.astype(BF16), b.astype(BF16))


def _dot_sel(a, sel):
    return _dot(a.astype(BF16), sel)


def _iota(shape, dim):
    return lax.broadcasted_iota(jnp.int32, shape, dim)


def _rope(x, c, s_left, s_right):
    return x * c + pltpu.roll(x, 112, 1) * s_left + pltpu.roll(x, 16, 1) * s_right


def _sigmoid(x):
    return 1.0 / (1.0 + jnp.exp(-x))


def _log_sigmoid(x):
    return jnp.minimum(x, 0.0) - jnp.log(1.0 + jnp.exp(-jnp.abs(x)))


def _transpose_bf16(x, eye):
    return _dot_nt(eye, x.astype(BF16)).astype(BF16)


def _store_vt(vt_ref, pp, v_block, eye):
    base = pp * VT_ROWS
    vt_ref[0, base:base + LANES, :] = _transpose_bf16(v_block, eye)
    cols = vt_ref.shape[2]
    vt_ref[0, base + LANES:base + VT_ROWS, :] = jnp.where(_iota((BF16_ROWS, cols), 0) == 0, 1.0, 0.0).astype(BF16)


def _inproj_kernel(x_ref, gmix_ref, w_ref, gqa_ref, wuq_ref, gkva_ref, gqn_ref, gfq_ref, gfk_ref, bf_ref,
                   ones_ref, bd_ref, eye_ref, c_ref, sl_ref, sr_ref,
                   ckv_ref, kr128_ref, kr_ref, qt_ref, pb_ref, qct_ref, kc_ref, kcb_ref, vc_ref, vct_ref, lf_ref):
    x = x_ref[...]
    h = x * lax.rsqrt(jnp.mean(x * x, axis=-1, keepdims=True) + RMS_EPS) * gmix_ref[...]
    h = h.astype(BF16)
    eye = eye_ref[...]
    ones = ones_ref[...]
    bd = bd_ref[...]

    ql = _dot(h, w_ref[:, _C_QL:_C_QL + 256])
    kv = _dot(h, w_ref[:, _C_KV:_C_KV + KV_LORA])
    kr = _dot(h, w_ref[:, _C_KR:_C_KR + LANES])
    pb_ref[...] = _dot(h, w_ref[:, _C_PB:_C_PB + B_IN])
    qc = _dot(h, w_ref[:, _C_QC:_C_QC + D_C])
    kc = _dot(h, w_ref[:, _C_KC:_C_KC + D_C])
    vc = _dot(h, w_ref[:, _C_VC:_C_VC + D_C])
    f = _dot(h, w_ref[:, _C_F:_C_F + LANES]) + bf_ref[...]

    ql = ql * lax.rsqrt(jnp.sum(ql * ql, axis=-1, keepdims=True) * (1.0 / Q_LORA) + RMS_EPS) * gqa_ref[...]
    qh = _dot(ql.astype(BF16), wuq_ref[...])
    c, s_l, s_r = c_ref[...], sl_ref[...], sr_ref[...]
    qv = [qh[:, hh * LANES:(hh + 1) * LANES] for hh in range(H_A)]
    ss = [_dot_sel(q * q, ones) for q in qv]
    qn = [q * lax.rsqrt(s * (1.0 / QK_DIM) + RMS_EPS) * gqn_ref[...] for q, s in zip(qv, ss)]
    qr = [_rope(q, c, s_l, s_r) * (LOG2E * QK_DIM ** -0.5) for q in qn]
    for hh in range(H_A):
        qt_ref[0, hh * LANES:(hh + 1) * LANES, :] = _transpose_bf16(qr[hh], eye)

    ckv_ref[...] = kv * lax.rsqrt(jnp.mean(kv * kv, axis=-1, keepdims=True) + RMS_EPS) * gkva_ref[...]
    kr128_ref[...] = kr
    kr_ref[...] = kr[:, NOPE:NOPE + ROPE]

    qc = qc * lax.rsqrt(_dot_sel(qc * qc, bd) * (1.0 / HEAD_DIM) + RMS_EPS) * gfq_ref[...]
    qc = qc * (LOG2E * HEAD_DIM ** -0.5)
    kc = kc * lax.rsqrt(_dot_sel(kc * kc, bd) * (1.0 / HEAD_DIM) + RMS_EPS) * gfk_ref[...]
    kc_ref[...] = kc
    kcb_ref[...] = kc.astype(BF16)
    vc_ref[...] = vc
    for pp in range(D_C // LANES):
        qct_ref[0, pp * LANES:(pp + 1) * LANES, :] = _transpose_bf16(qc[:, pp * LANES:(pp + 1) * LANES], eye)
        _store_vt(vct_ref, pp, vc[:, pp * LANES:(pp + 1) * LANES], eye)
    lf_ref[...] = _log_sigmoid(f)[:, :H_C]


def _inproj(x2d, lw, tabs, bsz, t, tm):
    n = x2d.shape[0]
    nt = t // tm
    row = lambda i: (i, 0)
    fixed = lambda i: (0, 0)
    tab = lambda i: (i % nt, 0)
    colmajor = lambda i: (i // nt, 0, i % nt)
    full = lambda a: pl.BlockSpec(a.shape, fixed)
    params = [lw['g_mix'], lw['w_in'], lw['g_qa'], lw['w_uq'], lw['g_kva'], lw['g_qn'], lw['g_fq'], lw['g_fk'],
              lw['b_f'], lw['ones128'], lw['bd256'], lw['eye128']]
    rowout = lambda w, dt: (pl.BlockSpec((tm, w), row), jax.ShapeDtypeStruct((n, w), dt))
    colout = lambda w: (pl.BlockSpec((1, w, tm), colmajor), jax.ShapeDtypeStruct((bsz, w, t), BF16))
    outs = [rowout(KV_LORA, F32), rowout(LANES, F32), rowout(ROPE, F32), colout(H_A * LANES), rowout(B_IN, F32),
            colout(D_C), rowout(D_C, F32), rowout(D_C, BF16), rowout(D_C, F32), colout(H_C // 2 * VT_ROWS),
            rowout(H_C, F32)]
    return pl.pallas_call(
        _inproj_kernel,
        grid=(n // tm,),
        in_specs=[pl.BlockSpec((tm, D_MODEL), row)] + [full(a) for a in params]
                 + [pl.BlockSpec((tm, LANES), tab)] * 3,
        out_specs=[o[0] for o in outs],
        out_shape=[o[1] for o in outs],
        compiler_params=_cparams(("parallel",)),
        name="inproj",
    )(x2d, *params, *tabs)


def _kvprep_kernel(ckv_ref, kr_ref, wuk_ref, wuv_ref, gkn_ref, gkr_ref, ones_ref, eye_ref, c_ref, sl_ref, sr_ref,
                   k_ref, vt_ref):
    cb = ckv_ref[...].astype(BF16)
    kn = _dot(cb, wuk_ref[...])
    v = _dot(cb, wuv_ref[...])
    eye = eye_ref[...]
    for pp in range(H_A // 2):
        _store_vt(vt_ref, pp, v[:, pp * LANES:(pp + 1) * LANES], eye)
    ones = ones_ref[...]
    kr = kr_ref[...]
    ssr = _dot_sel(kr * kr, ones)
    krg = _rope(kr * gkr_ref[...], c_ref[...], sl_ref[...], sr_ref[...])
    knh = [kn[:, hh * LANES:(hh + 1) * LANES] for hh in range(H_A)]
    ssn = [_dot_sel(k * k, ones) for k in knh]
    for hh in range(H_A):
        r = lax.rsqrt((ssn[hh] + ssr) * (1.0 / QK_DIM) + RMS_EPS)
        k_ref[:, hh * LANES:(hh + 1) * LANES] = ((knh[hh] * gkn_ref[...] + krg) * r).astype(BF16)


def _kvprep(ckv2d, kr128_2d, lw, tabs, bsz, t, tm):
    n = ckv2d.shape[0]
    nt = t // tm
    row = lambda i: (i, 0)
    fixed = lambda i: (0, 0)
    tab = lambda i: (i % nt, 0)
    full = lambda a: pl.BlockSpec(a.shape, fixed)
    params = [lw['w_uk'], lw['w_uv'], lw['g_kn'], lw['g_kr'], lw['ones128'], lw['eye128']]
    vt_rows = H_A // 2 * VT_ROWS
    return pl.pallas_call(
        _kvprep_kernel,
        grid=(n // tm,),
        in_specs=[pl.BlockSpec((tm, KV_LORA), row), pl.BlockSpec((tm, LANES), row)] + [full(a) for a in params]
                 + [pl.BlockSpec((tm, LANES), tab)] * 3,
        out_specs=[pl.BlockSpec((tm, H_A * LANES), row),
                   pl.BlockSpec((1, vt_rows, tm), lambda i: (i // nt, 0, i % nt))],
        out_shape=[jax.ShapeDtypeStruct((n, H_A * LANES), BF16),
                   jax.ShapeDtypeStruct((bsz, vt_rows, t), BF16)],
        compiler_params=_cparams(("parallel",)),
        name="kvprep",
    )(ckv2d, kr128_2d, *params, *tabs)


def _transpose_kernel(x_ref, eye_ref, o_ref):
    eye = eye_ref[...]
    for pp in range(x_ref.shape[1] // LANES):
        _store_vt(o_ref, pp, x_ref[:, pp * LANES:(pp + 1) * LANES], eye)


def _transpose_cast(x2d, lw, bsz, t, tm):
    n, w = x2d.shape
    nt = t // tm
    rows = w // LANES * VT_ROWS
    return pl.pallas_call(
        _transpose_kernel,
        grid=(n // tm,),
        in_specs=[pl.BlockSpec((tm, w), lambda i: (i, 0)), pl.BlockSpec((LANES, LANES), lambda i: (0, 0))],
        out_specs=pl.BlockSpec((1, rows, tm), lambda i: (i // nt, 0, i % nt)),
        out_shape=jax.ShapeDtypeStruct((bsz, rows, t), BF16),
        compiler_params=_cparams(("parallel",)),
        name="transpose_cast",
    )(x2d, lw['eye128'])


def _attn_kernel(*refs, tq, tk, tkp, nq, n_past, n_pairs, shared, frame_causal, has_bias, pipelined):
    it = iter(refs)
    q_ref, k_ref, vt_ref = next(it), next(it), next(it)
    if n_past:
        kp_ref, vpt_ref = next(it), next(it)
    if has_bias:
        cq_ref, ck_ref = next(it), next(it)
    eye_ref, o_ref = next(it), next(it)
    q_scr, m_scr, acc_scr = next(it), next(it), next(it)

    qi = 0 if nq == 1 else pl.program_id(1)
    low = _iota((LANES, tq), 0) < HEAD_DIM
    blk = lambda p, j: p if shared else 2 * p + j
    heads = [(p, j) for p in range(n_pairs) for j in range(2)]
    nh = len(heads)
    for p, j in heads:
        q = q_ref[0, blk(p, j) * LANES:(blk(p, j) + 1) * LANES, :]
        if shared:
            keep = low if j == 0 else jnp.logical_not(low)
            q = jnp.where(keep, q, jnp.zeros_like(q))
        q_scr[2 * p + j] = q
    m_scr[...] = jnp.full(m_scr.shape, -jnp.inf, F32)
    acc_scr[...] = jnp.zeros(acc_scr.shape, F32)

    def score(kr, st, width, p, j):
        kt = kr[0, pl.ds(st, width), blk(p, j) * LANES:(blk(p, j) + 1) * LANES].astype(BF16)
        return _dot(kt, q_scr[2 * p + j])

    def scores_to(buf, kr, st, width):
        for p, j in heads:
            buf[2 * p + j] = score(kr, st, width, p, j)

    def consume(get_s, vr, st, width, key_start, mask):
        ss = []
        for p, j in heads:
            s = get_s(p, j)
            if has_bias:
                s = s - ck_ref[0, p, pl.ds(key_start, width), j:j + 1]
            if mask is not None:
                s = jnp.where(mask, s, NEG_INF)
            ss.append(s)
        cqs = [cq_ref[0, p, j:j + 1, :] if has_bias else 0.0 for p, j in heads]
        m_prev = [m_scr[h] for h in range(nh)]
        m_next = [jnp.maximum(m_prev[h], jnp.max(ss[h], axis=0, keepdims=True) + cqs[h]) for h in range(nh)]
        prs = [jnp.exp2(ss[h] - (m_next[h] - cqs[h])).astype(BF16) for h in range(nh)]
        alphas = [jnp.exp2(m_prev[h] - m_next[h]) for h in range(nh)]
        pvs = [_dot(vr[0, p * VT_ROWS:(p + 1) * VT_ROWS, pl.ds(st, width)], prs[2 * p + j]) for p, j in heads]
        for h in range(nh):
            acc_scr[h] = acc_scr[h] * alphas[h] + pvs[h]
            m_scr[h] = m_next[h]

    def step(kr, vr, st, width, key_start, mask):
        consume(lambda p, j: score(kr, st, width, p, j), vr, st, width, key_start, mask)

    kidx = _iota((tk, tq), 0)
    qidx = _iota((tk, tq), 1)
    diag_mask = lambda d: ((kidx + d * tk <= qidx) if frame_causal else
                           (jnp.right_shift(kidx + d * tk, 6) <= jnp.right_shift(qidx, 6)))

    if pipelined:
        sa, sb = next(it), next(it)
        tile = lambda t: pl.multiple_of(t * tk, tk)
        from_a = lambda p, j: sa[2 * p + j]
        from_b = lambda p, j: sb[2 * p + j]
        scores_to(sa, k_ref, tile(0), tk)

        def pair_body(u, carry):
            t0 = 2 * u
            scores_to(sb, k_ref, tile(t0 + 1), tk)
            consume(from_a, vt_ref, tile(t0), tk, tile(t0), None)
            scores_to(sa, k_ref, tile(t0 + 2), tk)
            consume(from_b, vt_ref, tile(t0 + 1), tk, tile(t0 + 1), None)
            return carry
        lax.fori_loop(0, qi // 2, pair_body, 0)
        odd = lax.rem(qi, 2) == 1

        @pl.when(jnp.logical_not(odd))
        def _():
            consume(from_a, vt_ref, tile(qi), tk, tile(qi), diag_mask(0))

        @pl.when(odd)
        def _():
            scores_to(sb, k_ref, tile(qi), tk)
            consume(from_a, vt_ref, tile(qi - 1), tk, tile(qi - 1), None)
            consume(from_b, vt_ref, tile(qi), tk, tile(qi), diag_mask(0))
    else:
        if n_past:
            def past_body(t, carry):
                st = pl.multiple_of(t * tkp, tkp)
                step(kp_ref, vpt_ref, st, tkp, st, None)
                return carry
            lax.fori_loop(0, n_past // tkp, past_body, 0)

        def new_body(t, carry):
            st = pl.multiple_of(t * tk, tk)
            step(k_ref, vt_ref, st, tk, n_past + st, None)
            return carry
        if nq > 1:
            lax.fori_loop(0, qi * (tq // tk), new_body, 0)

        for d in range(tq // tk):
            st = qi * tq + d * tk
            st = st if nq == 1 else pl.multiple_of(st, tk)
            step(k_ref, vt_ref, st, tk, n_past + st, diag_mask(d))

    for p in range(n_pairs):
        a, b = acc_scr[2 * p], acc_scr[2 * p + 1]
        ot = jnp.where(low, a[:LANES] / a[LANES:LANES + 1], b[:LANES] / b[LANES:LANES + 1])
        o_ref[0, :, p * LANES:(p + 1) * LANES] = _dot_tn(ot.astype(BF16), eye_ref[...]).astype(o_ref.dtype)


def _attention(qt, k_new, vt_new, k_past, vt_past, bias, eye, *, n_pairs, shared, frame_causal, tq, tk, tkp):
    b, t = k_new.shape[0], k_new.shape[1]
    n_past = 0 if k_past is None else k_past.shape[1]
    nq = t // tq
    wq = qt.shape[1]
    args = [qt, k_new, vt_new]
    specs = [pl.BlockSpec((1, wq, tq), lambda bi, i: (bi, 0, i)),
             pl.BlockSpec((1, t, wq), lambda bi, i: (bi, 0, 0)),
             pl.BlockSpec((1, n_pairs * VT_ROWS, t), lambda bi, i: (bi, 0, 0))]
    if n_past:
        args += [k_past, vt_past]
        specs += [pl.BlockSpec((1, n_past, wq), lambda bi, i: (bi, 0, 0)),
                  pl.BlockSpec((1, n_pairs * VT_ROWS, n_past), lambda bi, i: (bi, 0, 0))]
    if bias is not None:
        cq, ck = bias
        args += [cq, ck]
        specs += [pl.BlockSpec((1, n_pairs, 2, tq), lambda bi, i: (bi, 0, 0, i)),
                  pl.BlockSpec((1, n_pairs, ck.shape[2], 2), lambda bi, i: (bi, 0, 0, 0))]
    args.append(eye)
    specs.append(pl.BlockSpec((LANES, LANES), lambda bi, i: (0, 0)))
    pipelined = n_past == 0 and nq > 1 and tq == tk
    kern = functools.partial(_attn_kernel, tq=tq, tk=tk, tkp=tkp, nq=nq, n_past=n_past, n_pairs=n_pairs,
                             shared=shared, frame_causal=frame_causal, has_bias=bias is not None,
                             pipelined=pipelined)
    nh = 2 * n_pairs
    score_bufs = [pltpu.VMEM((nh, tk, tq), F32)] * 2 if pipelined else []
    return pl.pallas_call(
        kern,
        grid=(b, nq),
        in_specs=specs,
        out_specs=pl.BlockSpec((1, tq, n_pairs * LANES), lambda bi, i: (bi, i, 0)),
        out_shape=jax.ShapeDtypeStruct((b, t, n_pairs * LANES), BF16),
        scratch_shapes=[pltpu.VMEM((nh, LANES, tq), BF16), pltpu.VMEM((nh, 1, tq), F32),
                        pltpu.VMEM((nh, VT_ROWS, tq), F32)] + score_bufs,
        compiler_params=_cparams(("parallel", "arbitrary")),
        name="attn_fox" if shared else "attn_mla",
    )(*args)


def _cumsum_kernel(x_ref, tri_ref, o_ref):
    rows, n = x_ref.shape
    tri = tri_ref[...]
    carry = jnp.zeros((rows, 1), F32)
    for t in range(n // LANES):
        xt = x_ref[:, t * LANES:(t + 1) * LANES]
        hi, mid, lo = _split3(xt)
        o_ref[:, t * LANES:(t + 1) * LANES] = (_dot(hi, tri) + _dot(mid, tri) + _dot(lo, tri) + carry) * LOG2E
        carry = carry + jnp.sum(xt, axis=1, keepdims=True)


def _cumsum_lanes(x):
    rows, n = x.shape
    tri = (jnp.arange(LANES)[:, None] <= jnp.arange(LANES)[None, :]).astype(BF16)
    return pl.pallas_call(
        _cumsum_kernel,
        out_shape=jax.ShapeDtypeStruct((rows, n), F32),
        compiler_params=pltpu.CompilerParams(vmem_limit_bytes=VMEM_LIMIT),
        name="cumsum",
    )(x, tri)


def _rwkv_tokens(pbv, prev_row, mu, w0, ww, a0, wa, wg, k_k, k_a, bd):
    rolled = pltpu.roll(pbv, 1, 0)
    shifted = jnp.where(_iota(pbv.shape, 0) == 0, prev_row, rolled)
    xs = pbv + (shifted - pbv) * mu
    r = xs[:, 0:D_B]
    kb = xs[:, D_B:2 * D_B]
    o3 = 3 * D_B
    wa_in = xs[:, o3:o3 + LANES]
    wa_in = jnp.where(_iota(wa_in.shape, 1) < W_LORA, jnp.tanh(wa_in), wa_in).astype(BF16)
    lw = -DECAY_SCALE * _sigmoid(w0 + _dot(wa_in, ww))
    a = _sigmoid(a0 + _dot(wa_in, wa))
    gate = _dot(_sigmoid(xs[:, o3 + LANES:o3 + 2 * LANES]).astype(BF16), wg)
    kk = kb * k_k
    kk = kk * lax.rsqrt(_dot_sel(kk * kk, bd) + 1e-12)
    return r, kk, kb * (1.0 + (a - 1.0) * k_a), kk * a, xs[:, 2 * D_B:3 * D_B], lw, gate


def _rwkv_chunk_kernel(pb_ref, pbprev_ref, prev_ref, mu_ref, w0_ref, ww_ref, a0_ref, wa_ref, wg_ref, kk_ref, ka_ref,
                       bd_ref, rt_ref, yl_ref, g_ref, h_ref, ro_ref, ko_ref, vo_ref, go_ref):
    s = pb_ref.shape[1]
    nc = s // CHUNK
    prev_row = jnp.where(pl.program_id(1) == 0, prev_ref[0], pbprev_ref[0, 7:8, :])
    r_all, kk_all, k_all, b_all, v_all, lw_all, gate = _rwkv_tokens(
        pb_ref[0], prev_row, mu_ref[...], w0_ref[...], ww_ref[...], a0_ref[...], wa_ref[...], wg_ref[...],
        kk_ref[...], ka_ref[...], bd_ref[...])
    ro_ref[0] = r_all.astype(BF16)
    ko_ref[0] = k_all.astype(BF16)
    vo_ref[0] = v_all.astype(BF16)
    go_ref[0] = gate.astype(BF16)
    row = _iota((s, LANES), 0)
    rin = jnp.bitwise_and(row, CHUNK - 1)
    ti = _iota((s, s), 0)
    si = _iota((s, s), 1)
    same = jnp.right_shift(ti, 6) == jnp.right_shift(si, 6)
    strict = jnp.logical_and(same, si < ti)
    incl = jnp.logical_and(same, si <= ti)
    eye = (ti == si).astype(F32)
    low = _iota((s, LANES), 1) < HEAD_DIM
    ji = _iota((LANES, LANES), 0)
    jj = _iota((LANES, LANES), 1)
    blockdiag = (ji < HEAD_DIM) == (jj < HEAD_DIM)
    npair = D_B // LANES
    pairs = range(npair)
    heads = [(p, x) for p in pairs for x in range(2)]

    tot, kkt, rt, khbh, kw, bw, v = [], [], [], [], [], [], []
    for p in pairs:
        sl = slice(p * LANES, (p + 1) * LANES)
        lw, kp, bp = lw_all[:, sl], k_all[:, sl], b_all[:, sl]
        cl = lw
        for sh in (1, 2, 4, 8, 16, 32):
            cl = cl + jnp.where(rin >= sh, pltpu.roll(cl, sh, 0), 0.0)
        tp = jnp.concatenate(
            [jnp.broadcast_to(cl[c * CHUNK + CHUNK - 1:(c + 1) * CHUNK, :], (CHUNK, LANES)) for c in range(nc)],
            axis=0)
        e_ncl = jnp.exp(-cl)
        e_rem = jnp.exp(tp - cl)
        tot.append(tp)
        kkt.append(kk_all[:, sl] * jnp.exp(cl - lw))
        rt.append(r_all[:, sl] * jnp.exp(cl))
        khbh.append(jnp.concatenate([kp * e_ncl, bp * e_ncl], axis=0).astype(BF16))
        kw.append(kp * e_rem)
        bw.append(bp * e_rem)
        v.append(v_all[:, sl])

    keep = lambda x: low if x == 0 else jnp.logical_not(low)
    kkt_x = [jnp.where(keep(x), kkt[p], 0.0) for p, x in heads]
    rt_x = [jnp.where(keep(x), rt[p], 0.0) for p, x in heads]
    akk, ark, arb, pw, tinv = [], [], [], [], []
    for i, (p, x) in enumerate(heads):
        p4 = _dot_nt(jnp.concatenate([kkt_x[i], rt_x[i]], axis=0).astype(BF16), khbh[p])
        akk.append(jnp.where(strict, p4[:s, :s], 0.0))
        ark.append(jnp.where(incl, p4[s:, :s], 0.0))
        arb.append(jnp.where(incl, p4[s:, s:], 0.0))
        pw.append(jnp.where(strict, -p4[:s, s:], 0.0))
        tinv.append(eye + pw[i])
    for _ in range(5):
        pw = [_mm(m, m) for m in pw]
        tinv = [t + _mm(t, m) for t, m in zip(tinv, pw)]
    av = [_mm(jnp.concatenate([akk[i], ark[i]], axis=0), v[p]) for i, (p, x) in enumerate(heads)]
    tx = [_mm(tinv[i], jnp.concatenate([av[i][:s], kkt_x[i]], axis=1)) for i in range(len(heads))]
    ax = [_mm(arb[i], tx[i]) for i in range(len(heads))]

    for p in pairs:
        sl = slice(p * LANES, (p + 1) * LANES)
        a, b2 = 2 * p, 2 * p + 1
        uloc2 = jnp.where(low, tx[a][:, :LANES], tx[b2][:, :LANES])
        kkt2 = jnp.where(low, tx[a][:, LANES:], tx[b2][:, LANES:])
        rt_ref[0, :, sl] = jnp.where(low, rt_x[a] - ax[a][:, LANES:], rt_x[b2] - ax[b2][:, LANES:]).astype(BF16)
        yl_ref[0, :, sl] = jnp.where(low, av[a][s:] - ax[a][:, :LANES], av[b2][s:] - ax[b2][:, :LANES]).astype(BF16)
        for c in range(nc):
            inc = jnp.right_shift(row, 6) == c
            bw_c = jnp.where(inc, bw[p], 0.0)
            kw_c = jnp.where(inc, kw[p], 0.0)
            e_tot = jnp.exp(tot[p][c * CHUNK:c * CHUNK + 1, :])
            gm = jnp.where(ji == jj, e_tot, 0.0) - _mm(bw_c, kkt2, _dot_tn)
            hm = _mm(kw_c, v[p], _dot_tn) - _mm(bw_c, uloc2, _dot_tn)
            g_ref[0, p, c] = jnp.where(blockdiag, gm, 0.0).astype(BF16)
            h_ref[0, p, c] = jnp.where(blockdiag, hm, 0.0).astype(BF16)


def _rwkv_chunk(pb3, prev, lw, s):
    bsz, t, _ = pb3.shape
    npair = H_B // 2
    nc = t // CHUNK
    sb = s // 8
    tok = pl.BlockSpec((1, s, D_B), lambda bi, i: (bi, i, 0))
    mat = pl.BlockSpec((1, npair, s // CHUNK, LANES, LANES), lambda bi, i: (bi, 0, i, 0, 0))
    params = [lw['rw_mu'], lw['rw_w0'], lw['rw_ww'], lw['rw_a0'], lw['rw_wa'], lw['rw_wg'], lw['rw_k_k'],
              lw['rw_k_a'], lw['bd384']]
    full = lambda a: pl.BlockSpec(a.shape, lambda bi, i: (0, 0))
    return pl.pallas_call(
        _rwkv_chunk_kernel,
        grid=(bsz, t // s),
        in_specs=[pl.BlockSpec((1, s, B_IN), lambda bi, i: (bi, i, 0)),
                  pl.BlockSpec((1, 8, B_IN), lambda bi, i: (bi, jnp.maximum(i * sb - 1, 0), 0)),
                  pl.BlockSpec((1, 1, B_IN), lambda bi, i: (bi, 0, 0))] + [full(a) for a in params],
        out_specs=[tok, tok, mat, mat] + [tok] * 4,
        out_shape=[jax.ShapeDtypeStruct((bsz, t, D_B), BF16)] * 2
                  + [jax.ShapeDtypeStruct((bsz, npair, nc, LANES, LANES), BF16)] * 2
                  + [jax.ShapeDtypeStruct((bsz, t, D_B), BF16)] * 4,
        compiler_params=_cparams(("parallel", "parallel")),
        name="rwkv_chunk",
    )(pb3, pb3, prev, *params)


def _rwkv_scan_kernel(s0_ref, g_ref, h_ref, rt_ref, yl_ref, r_ref, k_ref, v_ref, gate_ref,
                      lnw_ref, lnb_ref, rk_ref, bd_ref, o_ref, sfin_ref, st_scr, y_scr):
    npair, ncb = g_ref.shape[1], g_ref.shape[2]

    @pl.when(pl.program_id(1) == 0)
    def _():
        st_scr[...] = s0_ref[0]

    def times_state(a, st):
        hi, lo = _split2(st)
        return _dot(a, hi) + _dot(a, lo)

    sts = [st_scr[p] for p in range(npair)]
    for c in range(ncb):
        rows = slice(c * CHUNK, (c + 1) * CHUNK)
        for p in range(npair):
            sl = slice(p * LANES, (p + 1) * LANES)
            y_scr[rows, sl] = times_state(rt_ref[0, rows, sl], sts[p]) + yl_ref[0, rows, sl].astype(F32)
            sts[p] = times_state(g_ref[0, p, c], sts[p]) + h_ref[0, p, c].astype(F32)
    for p in range(npair):
        st_scr[p] = sts[p]
        sfin_ref[0, p] = sts[p]

    bd = bd_ref[...]
    y = y_scr[...]
    mu = _dot_sel(y, bd) * (1.0 / HEAD_DIM)
    d = y - mu
    var = _dot_sel(d * d, bd) * (1.0 / HEAD_DIM)
    yn = d * lax.rsqrt(var + GN_EPS) * lnw_ref[...] + lnb_ref[...]
    rk = r_ref[0].astype(F32) * k_ref[0].astype(F32) * rk_ref[...]
    bonus = _dot_sel(rk, bd) * v_ref[0].astype(F32)
    o_ref[0] = ((yn + bonus) * gate_ref[0].astype(F32)).astype(BF16)


def _rwkv_scan(s0, g, h, rt, yl, r, k, v, gate, lw, ncb):
    bsz, t, _ = rt.shape
    npair = H_B // 2
    nc = t // CHUNK
    st_spec = pl.BlockSpec((1, npair, LANES, LANES), lambda bi, i: (bi, 0, 0, 0))
    mat = pl.BlockSpec((1, npair, ncb, LANES, LANES), lambda bi, i: (bi, 0, i, 0, 0))
    tok = pl.BlockSpec((1, ncb * CHUNK, D_B), lambda bi, i: (bi, i, 0))
    params = [lw['rw_ln_w'], lw['rw_ln_b'], lw['rw_r_k'], lw['bd384']]
    full = lambda a: pl.BlockSpec(a.shape, lambda bi, i: (0, 0))
    return pl.pallas_call(
        _rwkv_scan_kernel,
        grid=(bsz, nc // ncb),
        in_specs=[st_spec, mat, mat] + [tok] * 6 + [full(a) for a in params],
        out_specs=[tok, st_spec],
        out_shape=[jax.ShapeDtypeStruct((bsz, t, D_B), BF16), jax.ShapeDtypeStruct((bsz, npair, LANES, LANES), F32)],
        scratch_shapes=[pltpu.VMEM((npair, LANES, LANES), F32), pltpu.VMEM((ncb * CHUNK, D_B), F32)],
        compiler_params=_cparams(("parallel", "arbitrary")),
        name="rwkv_scan",
    )(s0, g, h, rt, yl, r, k, v, gate, *params)


_L_EXP = 16


def _moe_kernel(x_ref, oa_ref, ob_ref, oc_ref, wo_ref, gffn_ref, wr_ref, br_ref, wg_ref, wu_ref, wd_ref,
                o_ref, acc_ref, h_ref, comb_ref):
    e = pl.program_id(1)

    @pl.when(e == 0)
    def _():
        da = H_A * HEAD_DIM
        x1 = (x_ref[...] + _dot(oa_ref[...], wo_ref[0:da, :]) + _dot(ob_ref[...], wo_ref[da:da + D_B, :])
              + _dot(oc_ref[...], wo_ref[da + D_B:, :]))
        acc_ref[...] = x1
        hf = x1 * lax.rsqrt(jnp.mean(x1 * x1, axis=-1, keepdims=True) + RMS_EPS) * gffn_ref[...]
        h_ref[...] = hf.astype(BF16)

        logit = _dot3(hf, wr_ref[...]) + br_ref[...]
        lane_i = _iota(logit.shape, 1)
        lane = lane_i.astype(F32)
        big = jnp.float32(3e38)
        is_g = lane_i < N_GROUPS
        gl = jnp.where(is_g, logit, -big)
        gmax = jnp.max(gl, axis=1, keepdims=True)
        pg_top = 1.0 / jnp.sum(jnp.where(is_g, jnp.exp(gl - gmax), 0.0), axis=1, keepdims=True)
        g_idx = jnp.min(jnp.where(jnp.logical_and(is_g, gl == gmax), lane, big), axis=1, keepdims=True)
        el = lane_i - _L_EXP
        in_e = jnp.logical_and(el >= 0, el < N_EXPERTS)
        sel = jnp.logical_and(in_e, jnp.right_shift(el, 2).astype(F32) == g_idx)
        l1 = jnp.where(sel, logit, -big)
        v1 = jnp.max(l1, axis=1, keepdims=True)
        i1 = jnp.min(jnp.where(jnp.logical_and(sel, l1 == v1), lane, big), axis=1, keepdims=True)
        sel2 = jnp.logical_and(sel, lane != i1)
        l2 = jnp.where(sel2, logit, -big)
        v2 = jnp.max(l2, axis=1, keepdims=True)
        i2 = jnp.min(jnp.where(jnp.logical_and(sel2, l2 == v2), lane, big), axis=1, keepdims=True)
        e2 = jnp.exp(v2 - v1)
        den = 1.0 / (1.0 + e2)
        comb_ref[...] = (jnp.where(lane == i1, den * pg_top, 0.0) + jnp.where(lane == i2, e2 * den * pg_top, 0.0))

    h = h_ref[...]
    comb = comb_ref[...]
    lane_c = _iota(comb.shape, 1)
    gates = [_dot(h, wg_ref[j]) for j in range(E_PER_GROUP)]
    ups = [_dot(h, wu_ref[j]) for j in range(E_PER_GROUP)]
    acts = []
    for j in range(E_PER_GROUP):
        ce = jnp.sum(jnp.where(lane_c == e * E_PER_GROUP + j + _L_EXP, comb, 0.0), axis=1, keepdims=True)
        acts.append((gates[j] * _sigmoid(gates[j]) * ups[j] * ce).astype(BF16))
    acc_ref[...] += _dot(jnp.concatenate(acts, axis=1), wd_ref[...])

    @pl.when(e == N_GROUPS - 1)
    def _():
        o_ref[...] = acc_ref[...]


def _outproj_moe(x2d, oa, ob, oc, lw, tm):
    n = x2d.shape[0]
    row = lambda w: pl.BlockSpec((tm, w), lambda i, e: (i, 0))
    full = lambda a: pl.BlockSpec(a.shape, lambda i, e: (0, 0))
    expert_in = pl.BlockSpec((E_PER_GROUP, D_MODEL, D_FF_E), lambda i, e: (e, 0, 0))
    return pl.pallas_call(
        _moe_kernel,
        grid=(n // tm, N_GROUPS),
        in_specs=[row(D_MODEL), row(H_A * HEAD_DIM), row(D_B), row(D_C), full(lw['w_out']), full(lw['g_ffn']),
                  full(lw['w_r']), full(lw['b_r']), expert_in, expert_in,
                  pl.BlockSpec((E_PER_GROUP * D_FF_E, D_MODEL), lambda i, e: (e, 0))],
        out_specs=row(D_MODEL),
        out_shape=jax.ShapeDtypeStruct((n, D_MODEL), F32),
        scratch_shapes=[pltpu.VMEM((tm, D_MODEL), F32), pltpu.VMEM((tm, D_MODEL), BF16), pltpu.VMEM((tm, LANES), F32)],
        compiler_params=_cparams(("parallel", "arbitrary")),
        name="outproj_moe",
    )(x2d, oa, ob, oc, lw['w_out'], lw['g_ffn'], lw['w_r'], lw['b_r'], lw['w_g'], lw['w_u'], lw['w_d'])


def _place(pieces, width):
    rows = pieces[0][1].shape[0]
    cols, at = [], 0
    for off, a in pieces:
        if off > at:
            cols.append(jnp.zeros((rows, off - at), F32))
        cols.append(a.astype(F32))
        at = off + a.shape[1]
    if width > at:
        cols.append(jnp.zeros((rows, width - at), F32))
    return jnp.concatenate(cols, axis=1)


def _row(v, width=None, off=0):
    v = v.reshape(1, -1).astype(F32)
    return v if width is None else _place([(off, v)], width)


def _block_diag_ones(n, blk):
    i = jnp.arange(n) // blk
    return (i[:, None] == i[None, :]).astype(BF16)


def _layer_weights(p, l):
    g = lambda name: p[name][l]
    w_in = g('w_in')
    o_b, o_c = A_IN, A_IN + B_IN
    w_in_p = _place([(_C_QL, w_in[:, :Q_LORA]), (_C_KV, w_in[:, Q_LORA:Q_LORA + KV_LORA]),
                     (_C_KR + NOPE, w_in[:, Q_LORA + KV_LORA:A_IN]), (_C_PB, w_in[:, o_b:o_c + 3 * D_C]),
                     (_C_F, w_in[:, o_c + 3 * D_C:])], _C_END).astype(BF16)
    w_uq = g('mla_w_uq').reshape(Q_LORA, H_A, QK_DIM)
    w_uq = jnp.pad(w_uq, ((0, 256 - Q_LORA), (0, 0), (0, LANES - QK_DIM))).reshape(256, H_A * LANES).astype(BF16)
    w_ukv = g('mla_w_ukv').reshape(KV_LORA, H_A, NOPE + HEAD_DIM)
    w_uk = jnp.pad(w_ukv[:, :, :NOPE], ((0, 0), (0, 0), (0, LANES - NOPE))).reshape(KV_LORA, H_A * LANES).astype(BF16)
    w_uv = w_ukv[:, :, NOPE:].reshape(KV_LORA, H_A * HEAD_DIM).astype(BF16)
    zeros_w = jnp.zeros((W_LORA, D_B), F32)
    w_r = _place([(0, g('moe_w_rg')), (_L_EXP, g('moe_w_re'))], LANES)
    b_r = _place([(0, g('moe_b_rg').reshape(1, -1)), (_L_EXP, g('moe_b_re').reshape(1, -1))], LANES)
    return dict(
        g_mix=_row(g('g_mix')), w_in=w_in_p, g_qa=_row(g('mla_g_qa'), 256), w_uq=w_uq, g_kva=_row(g('mla_g_kva')),
        g_qn=_row(g('mla_g_qn'), LANES), g_fq=_row(jnp.tile(g('fox_g_qn'), H_C)),
        g_fk=_row(jnp.tile(g('fox_g_kn'), H_C)), b_f=_row(g('fox_b_f'), LANES),
        ones128=jnp.ones((LANES, LANES), BF16), eye128=jnp.eye(LANES, dtype=BF16),
        bd256=_block_diag_ones(D_C, HEAD_DIM), bd384=_block_diag_ones(D_B, HEAD_DIM),
        w_uk=w_uk, w_uv=w_uv, g_kn=_row(g('mla_g_kn')[:NOPE], LANES), g_kr=_row(g('mla_g_kn')[NOPE:], LANES, NOPE),
        rw_mu=_row(g('rw_mu')), rw_w0=_row(g('rw_w0')), rw_a0=_row(g('rw_a0')),
        rw_ww=jnp.concatenate([g('rw_w_up'), zeros_w], axis=0).astype(BF16),
        rw_wa=jnp.concatenate([zeros_w, g('rw_a_up')], axis=0).astype(BF16),
        rw_wg=g('rw_g_up').astype(BF16), rw_k_k=_row(g('rw_k_k')), rw_k_a=_row(g('rw_k_a')),
        rw_r_k=_row(g('rw_r_k')), rw_ln_w=_row(g('rw_ln_w')), rw_ln_b=_row(g('rw_ln_b')),
        w_out=g('w_out').astype(BF16), g_ffn=_row(g('g_ffn')), w_r=w_r, b_r=b_r,
        w_g=g('moe_w_gate').astype(BF16), w_u=g('moe_w_up').astype(BF16),
        w_d=g('moe_w_down').astype(BF16).reshape(N_EXPERTS * D_FF_E, D_MODEL),
    )


def _rope_tables(pos):
    half = ROPE // 2
    inv = ROPE_BASE ** (-jnp.arange(half, dtype=F32) / half)
    ang = pos.astype(F32)[:, None] * inv[None, :]
    cos, sin = jnp.cos(ang), jnp.sin(ang)
    t = pos.shape[0]
    z = lambda w: jnp.zeros((t, w), F32)
    c = jnp.concatenate([jnp.ones((t, NOPE), F32), cos, cos, z(LANES - QK_DIM)], axis=1)
    s_left = jnp.concatenate([z(NOPE), -sin, z(half), z(LANES - QK_DIM)], axis=1)
    s_right = jnp.concatenate([z(NOPE), z(half), sin, z(LANES - QK_DIM)], axis=1)
    return c, s_left, s_right


def _pick(n, prefs):
    for t in prefs:
        if n % t == 0:
            return t
    return n


def _layer(x, lw, hist):
    b, t, _ = x.shape
    n = b * t
    past = 0 if hist is None else hist['ckv'].shape[1]
    x2d = x.reshape(n, D_MODEL)
    tm = _pick(t, (512, 256, 128, 64))
    tmp = _pick(past, (512, 256, 128, 64)) if past else 0
    eye = lw['eye128']

    q_tabs = _rope_tables(past + jnp.arange(t))
    ckv, kr128, kr, qt, pb, qct, kc, kcb, vc, vct, lf = _inproj(x2d, lw, q_tabs, b, t, tm)

    k_new, vt_new = _kvprep(ckv, kr128, lw, q_tabs, b, t, tm)
    if hist is None:
        kp = vpt = None
    else:
        ckv_past = hist['ckv'].reshape(b * past, KV_LORA)
        kr_past = jnp.pad(hist['krope'].reshape(b * past, ROPE), ((0, 0), (NOPE, LANES - QK_DIM)))
        kp, vpt = _kvprep(ckv_past, kr_past, lw, _rope_tables(jnp.arange(past)), b, past, tmp)
        kp = kp.reshape(b, past, -1)
    o_a = _attention(qt, k_new.reshape(b, t, -1), vt_new, kp, vpt, None, eye, n_pairs=H_A // 2, shared=False,
                     frame_causal=False, tq=tm, tk=min(tm, ATTN_TK), tkp=min(tmp, ATTN_TK))

    lf3 = lf.reshape(b, t, H_C)
    lf_all = lf3 if hist is None else jnp.concatenate([hist['flogf'].astype(F32), lf3], axis=1)
    ltot = past + t
    lpad = -(-ltot // LANES) * LANES
    lf_t = jnp.pad(jnp.swapaxes(lf_all, 1, 2), ((0, 0), (0, 0), (0, lpad - ltot))).reshape(b * H_C, lpad)
    c_all = _cumsum_lanes(lf_t).reshape(b, H_C // 2, 2, lpad)
    cq = c_all[..., past:past + t]
    ck = jnp.swapaxes(c_all, 2, 3)
    if hist is None:
        kcp = vcpt = None
    else:
        kcp = hist['fk'].reshape(b, past, D_C)
        vcpt = _transpose_cast(hist['fv'].reshape(b * past, D_C), lw, b, past, tmp)
    o_c = _attention(qct, kcb.reshape(b, t, D_C), vct, kcp, vcpt, (cq, ck), eye, n_pairs=H_C // 2, shared=True,
                     frame_causal=True, tq=tm, tk=min(tm, ATTN_TK), tkp=min(tmp, ATTN_TK))

    prev = jnp.zeros((b, 1, B_IN), F32) if hist is None else hist['shift'].astype(F32)
    pb3 = pb.reshape(b, t, B_IN)
    rt, yl, gm, hm, r, k, v, g = _rwkv_chunk(pb3, prev, lw, _pick(t, (256, 128, 64)))
    npair = H_B // 2
    if hist is None:
        s0 = jnp.zeros((b, npair, LANES, LANES), F32)
    else:
        st = jnp.swapaxes(hist['wkv'].astype(F32), -1, -2).reshape(b, npair, 2, HEAD_DIM, HEAD_DIM)
        s0 = jnp.zeros((b, npair, 2, HEAD_DIM, 2, HEAD_DIM), F32)
        s0 = s0.at[:, :, 0, :, 0, :].set(st[:, :, 0]).at[:, :, 1, :, 1, :].set(st[:, :, 1])
        s0 = s0.reshape(b, npair, LANES, LANES)
    o_b, sfin = _rwkv_scan(s0, gm, hm, rt, yl, r, k, v, g, lw, _pick(t // CHUNK, (8, 4, 2, 1)))
    sf = sfin.reshape(b, npair, 2, HEAD_DIM, 2, HEAD_DIM)
    s_fin = jnp.stack([sf[:, :, 0, :, 0, :], sf[:, :, 1, :, 1, :]], axis=2).reshape(b, H_B, HEAD_DIM, HEAD_DIM)
    s_fin = jnp.swapaxes(s_fin, -1, -2)

    tmm = _pick(n, (1024, 512, 256, 128, 64))
    x_out = _outproj_moe(x2d, o_a.reshape(n, -1), o_b.reshape(n, D_B), o_c.reshape(n, -1), lw, tmm)

    new = (ckv.reshape(b, t, KV_LORA), kr.reshape(b, t, ROPE), kc.reshape(b, t, H_C, HEAD_DIM),
           vc.reshape(b, t, H_C, HEAD_DIM), lf3, s_fin, pb3[:, -1:])
    return x_out.reshape(b, t, D_MODEL), new


def kernel(x_prompt, x_sample, cache_mla_latent, cache_mla_krope, cache_fox_k, cache_fox_v, cache_fox_logf,
           state_rwkv_wkv, state_rwkv_shift, g_mix, w_in, mla_g_qa, mla_w_uq, mla_g_kva, mla_w_ukv, mla_g_qn,
           mla_g_kn, rw_mu, rw_w0, rw_w_up, rw_a0, rw_a_up, rw_g_up, rw_k_k, rw_k_a, rw_r_k, rw_ln_w, rw_ln_b,
           fox_g_qn, fox_g_kn, fox_b_f, w_out, g_ffn, moe_w_rg, moe_b_rg, moe_w_re, moe_b_re, moe_w_gate,
           moe_w_up, moe_w_down):
    params = dict(g_mix=g_mix, w_in=w_in, mla_g_qa=mla_g_qa, mla_w_uq=mla_w_uq, mla_g_kva=mla_g_kva,
                  mla_w_ukv=mla_w_ukv, mla_g_qn=mla_g_qn, mla_g_kn=mla_g_kn, rw_mu=rw_mu, rw_w0=rw_w0,
                  rw_w_up=rw_w_up, rw_a0=rw_a0, rw_a_up=rw_a_up, rw_g_up=rw_g_up, rw_k_k=rw_k_k, rw_k_a=rw_k_a,
                  rw_r_k=rw_r_k, rw_ln_w=rw_ln_w, rw_ln_b=rw_ln_b, fox_g_qn=fox_g_qn, fox_g_kn=fox_g_kn,
                  fox_b_f=fox_b_f, w_out=w_out, g_ffn=g_ffn, moe_w_rg=moe_w_rg, moe_b_rg=moe_b_rg,
                  moe_w_re=moe_w_re, moe_b_re=moe_b_re, moe_w_gate=moe_w_gate, moe_w_up=moe_w_up,
                  moe_w_down=moe_w_down)
    depth = g_mix.shape[0]
    yp, ys = x_prompt, x_sample
    p_new, s_new = [], []
    for l in range(depth):
        lw = _layer_weights(params, l)
        hist = dict(ckv=cache_mla_latent[l], krope=cache_mla_krope[l], fk=cache_fox_k[l], fv=cache_fox_v[l],
                    flogf=cache_fox_logf[l], wkv=state_rwkv_wkv[l], shift=state_rwkv_shift[l])
        yp, np_l = _layer(yp, lw, None)
        ys, ns_l = _layer(ys, lw, hist)
        p_new.append(np_l)
        s_new.append(ns_l)
    p_out = tuple(jnp.stack(t) for t in zip(*p_new))
    s_out = tuple(jnp.stack(t) for t in zip(*s_new))
    return (yp, ys) + p_out + s_out
```

```python
import functools
import math

import jax
import jax.numpy as jnp
from jax import lax
from jax.experimental import pallas as pl
from jax.experimental.pallas import tpu as pltpu

F32 = jnp.float32
BF16 = jnp.bfloat16

D_MODEL = 1024
HEAD_DIM = 64
H_A, H_B, H_C = 6, 6, 4
Q_LORA, KV_LORA, NOPE, ROPE = 192, 128, 64, 32
QK_DIM = NOPE + ROPE
ROPE_BASE = 10000.0
A_IN = Q_LORA + KV_LORA + ROPE
D_B = H_B * HEAD_DIM
W_LORA, A_LORA, G_LORA = 64, 64, 128
B_IN = 3 * D_B + W_LORA + A_LORA + G_LORA
DECAY_SCALE = math.exp(-0.5)
GN_EPS = 64e-5
D_C = H_C * HEAD_DIM
C_IN = 3 * D_C + H_C
N_GROUPS, E_PER_GROUP = 4, 4
N_EXPERTS = N_GROUPS * E_PER_GROUP
D_FF_E = 256
NEG_INF = -1e30
RMS_EPS = 1e-6
CHUNK = 64
LOG2E = math.log2(math.e)

LANES = 128
BF16_ROWS = 16
VMEM_LIMIT = 56 * 1024 * 1024
ATTN_TK = 512
VT_ROWS = LANES + BF16_ROWS

_C_QL = 0
_C_KV = 256
_C_KR = 384
_C_PB = 512
_C_QC = _C_PB + B_IN
_C_KC = _C_QC + D_C
_C_VC = _C_KC + D_C
_C_F = _C_VC + D_C
_C_END = _C_F + LANES


def _cparams(sem):
    return pltpu.CompilerParams(dimension_semantics=sem, vmem_limit_bytes=VMEM_LIMIT)


def _dot(a, b):
    return jnp.dot(a, b, preferred_element_type=F32)


def _dot_nt(a, b):
    return lax.dot_general(a, b, (((1,), (1,)), ((), ())), preferred_element_type=F32)


def _dot_tn(a, b):
    return lax.dot_general(a, b, (((0,), (0,)), ((), ())), preferred_element_type=F32)


def _split2(x):
    hi = x.astype(BF16)
    lo = (x - hi.astype(F32)).astype(BF16)
    return hi, lo


def _split3(x):
    hi = x.astype(BF16)
    r = x - hi.astype(F32)
    mid = r.astype(BF16)
    lo = (r - mid.astype(F32)).astype(BF16)
    return hi, mid, lo


def _dot3(a, b, dot=_dot):
    ah, al = _split2(a)
    bh, bl = _split2(b)
    return dot(ah, bh) + (dot(ah, bl) + dot(al, bh))


def _mm(a, b, dot=_dot):
    return dot(a.astype(BF16), b.astype(BF16))


def _dot_sel(a, sel):
    return _dot(a.astype(BF16), sel)


def _iota(shape, dim):
    return lax.broadcasted_iota(jnp.int32, shape, dim)


def _rope(x, c, s_left, s_right):
    return x * c + pltpu.roll(x, 112, 1) * s_left + pltpu.roll(x, 16, 1) * s_right


def _sigmoid(x):
    return 1.0 / (1.0 + jnp.exp(-x))


def _log_sigmoid(x):
    return jnp.minimum(x, 0.0) - jnp.log(1.0 + jnp.exp(-jnp.abs(x)))


def _transpose_bf16(x, eye):
    return _dot_nt(eye, x.astype(BF16)).astype(BF16)


def _store_vt(vt_ref, pp, v_block, eye):
    base = pp * VT_ROWS
    vt_ref[0, base:base + LANES, :] = _transpose_bf16(v_block, eye)
    cols = vt_ref.shape[2]
    vt_ref[0, base + LANES:base + VT_ROWS, :] = jnp.where(_iota((BF16_ROWS, cols), 0) == 0, 1.0, 0.0).astype(BF16)


def _inproj_kernel(*refs, n_prev):
    it = iter(refs)
    (x_ref, gmix_ref, w_ref, gqa_ref, wuq_ref, gkva_ref, gqn_ref, gfq_ref, gfk_ref, bf_ref, ones_ref, bd_ref,
     eye_ref, wuk_ref, wuv_ref, gkn_ref, gkr_ref, c_ref, sl_ref, sr_ref) = (next(it) for _ in range(20))
    prev = [next(it) for _ in range(5)] if n_prev else []
    stacks = [next(it) for _ in range(5)]
    qt_ref, pb_ref, qct_ref, kcb_ref, vct_ref, km_ref, vtm_ref = (next(it) for _ in range(7))
    for s_ref, p_ref in zip(stacks, prev):
        s_ref[0:n_prev] = p_ref[...]
    ckv_ref, kr_ref, kc_ref, vc_ref, lf_ref = (s.at[n_prev] for s in stacks)

    x = x_ref[...]
    h = x * lax.rsqrt(jnp.mean(x * x, axis=-1, keepdims=True) + RMS_EPS) * gmix_ref[...]
    h = h.astype(BF16)
    eye = eye_ref[...]
    ones = ones_ref[...]
    bd = bd_ref[...]

    ql = _dot(h, w_ref[:, _C_QL:_C_QL + 256])
    kv = _dot(h, w_ref[:, _C_KV:_C_KV + KV_LORA])
    kr = _dot(h, w_ref[:, _C_KR:_C_KR + LANES])
    pb_ref[...] = _dot(h, w_ref[:, _C_PB:_C_PB + B_IN])
    qc = _dot(h, w_ref[:, _C_QC:_C_QC + D_C])
    kc = _dot(h, w_ref[:, _C_KC:_C_KC + D_C])
    vc = _dot(h, w_ref[:, _C_VC:_C_VC + D_C])
    f = _dot(h, w_ref[:, _C_F:_C_F + LANES]) + bf_ref[...]

    ql = ql * lax.rsqrt(jnp.sum(ql * ql, axis=-1, keepdims=True) * (1.0 / Q_LORA) + RMS_EPS) * gqa_ref[...]
    qh = _dot(ql.astype(BF16), wuq_ref[...])
    c, s_l, s_r = c_ref[...], sl_ref[...], sr_ref[...]
    qv = [qh[:, hh * LANES:(hh + 1) * LANES] for hh in range(H_A)]
    ss = [_dot_sel(q * q, ones) for q in qv]
    qn = [q * lax.rsqrt(s * (1.0 / QK_DIM) + RMS_EPS) * gqn_ref[...] for q, s in zip(qv, ss)]
    qr = [_rope(q, c, s_l, s_r) * (LOG2E * QK_DIM ** -0.5) for q in qn]
    for hh in range(H_A):
        qt_ref[0, hh * LANES:(hh + 1) * LANES, :] = _transpose_bf16(qr[hh], eye)

    ckv = kv * lax.rsqrt(jnp.mean(kv * kv, axis=-1, keepdims=True) + RMS_EPS) * gkva_ref[...]
    ckv_ref[...] = ckv
    kr_ref[...] = kr[:, NOPE:NOPE + ROPE]
    _mla_kv(ckv, kr, wuk_ref[...], wuv_ref[...], gkn_ref[...], gkr_ref[...], ones, eye, c, s_l, s_r, km_ref, vtm_ref)

    qc = qc * lax.rsqrt(_dot_sel(qc * qc, bd) * (1.0 / HEAD_DIM) + RMS_EPS) * gfq_ref[...]
    qc = qc * (LOG2E * HEAD_DIM ** -0.5)
    kc = kc * lax.rsqrt(_dot_sel(kc * kc, bd) * (1.0 / HEAD_DIM) + RMS_EPS) * gfk_ref[...]
    kc_ref[...] = kc
    kcb_ref[...] = kc.astype(BF16)
    vc_ref[...] = vc
    for pp in range(D_C // LANES):
        qct_ref[0, pp * LANES:(pp + 1) * LANES, :] = _transpose_bf16(qc[:, pp * LANES:(pp + 1) * LANES], eye)
        _store_vt(vct_ref, pp, vc[:, pp * LANES:(pp + 1) * LANES], eye)
    lf_ref[...] = _log_sigmoid(f)[:, :H_C]


_CACHE_WIDTHS = (KV_LORA, ROPE, D_C, D_C, H_C)


def _inproj(x2d, lw, tabs, bsz, t, tm, prev_stacks):
    n = x2d.shape[0]
    nt = t // tm
    n_prev = 0 if prev_stacks is None else prev_stacks[0].shape[0]
    row = lambda i: (i, 0)
    fixed = lambda i: (0, 0)
    tab = lambda i: (i % nt, 0)
    colmajor = lambda i: (i // nt, 0, i % nt)
    full = lambda a: pl.BlockSpec(a.shape, fixed)
    params = [lw['g_mix'], lw['w_in'], lw['g_qa'], lw['w_uq'], lw['g_kva'], lw['g_qn'], lw['g_fq'], lw['g_fk'],
              lw['b_f'], lw['ones128'], lw['bd256'], lw['eye128'], lw['w_uk'], lw['w_uv'], lw['g_kn'], lw['g_kr']]
    stack = lambda layers, w: pl.BlockSpec((layers, tm, w), lambda i: (0, i, 0))
    rowout = lambda w, dt: (pl.BlockSpec((tm, w), row), jax.ShapeDtypeStruct((n, w), dt))
    colout = lambda w: (pl.BlockSpec((1, w, tm), colmajor), jax.ShapeDtypeStruct((bsz, w, t), BF16))
    outs = [(stack(n_prev + 1, w), jax.ShapeDtypeStruct((n_prev + 1, n, w), F32)) for w in _CACHE_WIDTHS]
    outs += [colout(H_A * LANES), rowout(B_IN, F32), colout(D_C), rowout(D_C, BF16), colout(H_C // 2 * VT_ROWS),
             rowout(H_A * LANES, BF16), colout(H_A // 2 * VT_ROWS)]
    prev_args = [] if prev_stacks is None else list(prev_stacks)
    return pl.pallas_call(
        functools.partial(_inproj_kernel, n_prev=n_prev),
        grid=(n // tm,),
        in_specs=[pl.BlockSpec((tm, D_MODEL), row)] + [full(a) for a in params]
                 + [pl.BlockSpec((tm, LANES), tab)] * 3 + [stack(n_prev, w) for w in _CACHE_WIDTHS if n_prev],
        out_specs=[o[0] for o in outs],
        out_shape=[o[1] for o in outs],
        compiler_params=_cparams(("parallel",)),
        name="inproj",
    )(x2d, *params, *tabs, *prev_args)


def _mla_kv(ckv, kr, wuk, wuv, gkn, gkr, ones, eye, c, s_l, s_r, k_ref, vt_ref):
    cb = ckv.astype(BF16)
    kn = _dot(cb, wuk)
    v = _dot(cb, wuv)
    for pp in range(H_A // 2):
        _store_vt(vt_ref, pp, v[:, pp * LANES:(pp + 1) * LANES], eye)
    ssr = _dot_sel(kr * kr, ones)
    krg = _rope(kr * gkr, c, s_l, s_r)
    knh = [kn[:, hh * LANES:(hh + 1) * LANES] for hh in range(H_A)]
    ssn = [_dot_sel(k * k, ones) for k in knh]
    for hh in range(H_A):
        r = lax.rsqrt((ssn[hh] + ssr) * (1.0 / QK_DIM) + RMS_EPS)
        k_ref[:, hh * LANES:(hh + 1) * LANES] = ((knh[hh] * gkn + krg) * r).astype(BF16)


def _kvprep_kernel(ckv_ref, kr_ref, wuk_ref, wuv_ref, gkn_ref, gkr_ref, ones_ref, eye_ref, c_ref, sl_ref, sr_ref,
                   k_ref, vt_ref):
    _mla_kv(ckv_ref[...], kr_ref[...], wuk_ref[...], wuv_ref[...], gkn_ref[...], gkr_ref[...], ones_ref[...],
            eye_ref[...], c_ref[...], sl_ref[...], sr_ref[...], k_ref, vt_ref)


def _kvprep(ckv2d, kr128_2d, lw, tabs, bsz, t, tm):
    n = ckv2d.shape[0]
    nt = t // tm
    row = lambda i: (i, 0)
    fixed = lambda i: (0, 0)
    tab = lambda i: (i % nt, 0)
    full = lambda a: pl.BlockSpec(a.shape, fixed)
    params = [lw['w_uk'], lw['w_uv'], lw['g_kn'], lw['g_kr'], lw['ones128'], lw['eye128']]
    vt_rows = H_A // 2 * VT_ROWS
    return pl.pallas_call(
        _kvprep_kernel,
        grid=(n // tm,),
        in_specs=[pl.BlockSpec((tm, KV_LORA), row), pl.BlockSpec((tm, LANES), row)] + [full(a) for a in params]
                 + [pl.BlockSpec((tm, LANES), tab)] * 3,
        out_specs=[pl.BlockSpec((tm, H_A * LANES), row),
                   pl.BlockSpec((1, vt_rows, tm), lambda i: (i // nt, 0, i % nt))],
        out_shape=[jax.ShapeDtypeStruct((n, H_A * LANES), BF16),
                   jax.ShapeDtypeStruct((bsz, vt_rows, t), BF16)],
        compiler_params=_cparams(("parallel",)),
        name="kvprep",
    )(ckv2d, kr128_2d, *params, *tabs)


def _transpose_kernel(x_ref, eye_ref, o_ref):
    eye = eye_ref[...]
    for pp in range(x_ref.shape[1] // LANES):
        _store_vt(o_ref, pp, x_ref[:, pp * LANES:(pp + 1) * LANES], eye)


def _transpose_cast(x2d, lw, bsz, t, tm):
    n, w = x2d.shape
    nt = t // tm
    rows = w // LANES * VT_ROWS
    return pl.pallas_call(
        _transpose_kernel,
        grid=(n // tm,),
        in_specs=[pl.BlockSpec((tm, w), lambda i: (i, 0)), pl.BlockSpec((LANES, LANES), lambda i: (0, 0))],
        out_specs=pl.BlockSpec((1, rows, tm), lambda i: (i // nt, 0, i % nt)),
        out_shape=jax.ShapeDtypeStruct((bsz, rows, t), BF16),
        compiler_params=_cparams(("parallel",)),
        name="transpose_cast",
    )(x2d, lw['eye128'])


def _attn_kernel(*refs, tq, tk, tkp, nq, n_past, n_pairs, shared, frame_causal, has_bias, pipelined):
    it = iter(refs)
    q_ref, k_ref, vt_ref = next(it), next(it), next(it)
    if n_past:
        kp_ref, vpt_ref = next(it), next(it)
    if has_bias:
        cq_ref, ck_ref = next(it), next(it)
    eye_ref, o_ref = next(it), next(it)
    q_scr, m_scr, acc_scr = next(it), next(it), next(it)

    qi = 0 if nq == 1 else pl.program_id(1)
    low = _iota((LANES, tq), 0) < HEAD_DIM
    blk = lambda p, j: p if shared else 2 * p + j
    heads = [(p, j) for p in range(n_pairs) for j in range(2)]
    nh = len(heads)
    for p, j in heads:
        q = q_ref[0, blk(p, j) * LANES:(blk(p, j) + 1) * LANES, :]
        if shared:
            keep = low if j == 0 else jnp.logical_not(low)
            q = jnp.where(keep, q, jnp.zeros_like(q))
        q_scr[2 * p + j] = q
    m_scr[...] = jnp.full(m_scr.shape, -jnp.inf, F32)
    acc_scr[...] = jnp.zeros(acc_scr.shape, F32)

    def score(kr, st, width, p, j):
        kt = kr[0, pl.ds(st, width), blk(p, j) * LANES:(blk(p, j) + 1) * LANES].astype(BF16)
        return _dot(kt, q_scr[2 * p + j])

    def scores_to(buf, kr, st, width):
        for p, j in heads:
            buf[2 * p + j] = score(kr, st, width, p, j)

    def consume(get_s, vr, st, width, key_start, mask):
        ss = []
        for p, j in heads:
            s = get_s(p, j)
            if has_bias:
                s = s - ck_ref[0, p, pl.ds(key_start, width), j:j + 1]
            if mask is not None:
                s = jnp.where(mask, s, NEG_INF)
            ss.append(s)
        cqs = [cq_ref[0, p, j:j + 1, :] if has_bias else 0.0 for p, j in heads]
        m_prev = [m_scr[h] for h in range(nh)]
        m_next = [jnp.maximum(m_prev[h], jnp.max(ss[h], axis=0, keepdims=True) + cqs[h]) for h in range(nh)]
        prs = [jnp.exp2(ss[h] - (m_next[h] - cqs[h])).astype(BF16) for h in range(nh)]
        alphas = [jnp.exp2(m_prev[h] - m_next[h]) for h in range(nh)]
        pvs = [_dot(vr[0, p * VT_ROWS:(p + 1) * VT_ROWS, pl.ds(st, width)], prs[2 * p + j]) for p, j in heads]
        for h in range(nh):
            acc_scr[h] = acc_scr[h] * alphas[h] + pvs[h]
            m_scr[h] = m_next[h]

    def step(kr, vr, st, width, key_start, mask):
        consume(lambda p, j: score(kr, st, width, p, j), vr, st, width, key_start, mask)

    kidx = _iota((tk, tq), 0)
    qidx = _iota((tk, tq), 1)
    diag_mask = lambda d: ((kidx + d * tk <= qidx) if frame_causal else
                           (jnp.right_shift(kidx + d * tk, 6) <= jnp.right_shift(qidx, 6)))

    if pipelined:
        sa, sb = next(it), next(it)
        tile = lambda t: pl.multiple_of(t * tk, tk)
        from_a = lambda p, j: sa[2 * p + j]
        from_b = lambda p, j: sb[2 * p + j]
        scores_to(sa, k_ref, tile(0), tk)

        def pair_body(u, carry):
            t0 = 2 * u
            scores_to(sb, k_ref, tile(t0 + 1), tk)
            consume(from_a, vt_ref, tile(t0), tk, tile(t0), None)
            scores_to(sa, k_ref, tile(t0 + 2), tk)
            consume(from_b, vt_ref, tile(t0 + 1), tk, tile(t0 + 1), None)
            return carry
        lax.fori_loop(0, qi // 2, pair_body, 0)
        odd = lax.rem(qi, 2) == 1

        @pl.when(jnp.logical_not(odd))
        def _():
            consume(from_a, vt_ref, tile(qi), tk, tile(qi), diag_mask(0))

        @pl.when(odd)
        def _():
            scores_to(sb, k_ref, tile(qi), tk)
            consume(from_a, vt_ref, tile(qi - 1), tk, tile(qi - 1), None)
            consume(from_b, vt_ref, tile(qi), tk, tile(qi), diag_mask(0))
    else:
        if n_past:
            def past_body(t, carry):
                st = pl.multiple_of(t * tkp, tkp)
                step(kp_ref, vpt_ref, st, tkp, st, None)
                return carry
            lax.fori_loop(0, n_past // tkp, past_body, 0)

        def new_body(t, carry):
            st = pl.multiple_of(t * tk, tk)
            step(k_ref, vt_ref, st, tk, n_past + st, None)
            return carry
        if nq > 1:
            lax.fori_loop(0, qi * (tq // tk), new_body, 0)

        for d in range(tq // tk):
            st = qi * tq + d * tk
            st = st if nq == 1 else pl.multiple_of(st, tk)
            step(k_ref, vt_ref, st, tk, n_past + st, diag_mask(d))

    for p in range(n_pairs):
        a, b = acc_scr[2 * p], acc_scr[2 * p + 1]
        ot = jnp.where(low, a[:LANES] / a[LANES:LANES + 1], b[:LANES] / b[LANES:LANES + 1])
        o_ref[0, :, p * LANES:(p + 1) * LANES] = _dot_tn(ot.astype(BF16), eye_ref[...]).astype(o_ref.dtype)


def _attention(qt, k_new, vt_new, k_past, vt_past, bias, eye, *, n_pairs, shared, frame_causal, tq, tk, tkp):
    b, t = k_new.shape[0], k_new.shape[1]
    n_past = 0 if k_past is None else k_past.shape[1]
    nq = t // tq
    wq = qt.shape[1]
    args = [qt, k_new, vt_new]
    specs = [pl.BlockSpec((1, wq, tq), lambda bi, i: (bi, 0, i)),
             pl.BlockSpec((1, t, wq), lambda bi, i: (bi, 0, 0)),
             pl.BlockSpec((1, n_pairs * VT_ROWS, t), lambda bi, i: (bi, 0, 0))]
    if n_past:
        args += [k_past, vt_past]
        specs += [pl.BlockSpec((1, n_past, wq), lambda bi, i: (bi, 0, 0)),
                  pl.BlockSpec((1, n_pairs * VT_ROWS, n_past), lambda bi, i: (bi, 0, 0))]
    if bias is not None:
        cq, ck = bias
        args += [cq, ck]
        specs += [pl.BlockSpec((1, n_pairs, 2, tq), lambda bi, i: (bi, 0, 0, i)),
                  pl.BlockSpec((1, n_pairs, ck.shape[2], 2), lambda bi, i: (bi, 0, 0, 0))]
    args.append(eye)
    specs.append(pl.BlockSpec((LANES, LANES), lambda bi, i: (0, 0)))
    pipelined = n_past == 0 and nq > 1 and tq == tk
    kern = functools.partial(_attn_kernel, tq=tq, tk=tk, tkp=tkp, nq=nq, n_past=n_past, n_pairs=n_pairs,
                             shared=shared, frame_causal=frame_causal, has_bias=bias is not None,
                             pipelined=pipelined)
    nh = 2 * n_pairs
    score_bufs = [pltpu.VMEM((nh, tk, tq), F32)] * 2 if pipelined else []
    return pl.pallas_call(
        kern,
        grid=(b, nq),
        in_specs=specs,
        out_specs=pl.BlockSpec((1, tq, n_pairs * LANES), lambda bi, i: (bi, i, 0)),
        out_shape=jax.ShapeDtypeStruct((b, t, n_pairs * LANES), BF16),
        scratch_shapes=[pltpu.VMEM((nh, LANES, tq), BF16), pltpu.VMEM((nh, 1, tq), F32),
                        pltpu.VMEM((nh, VT_ROWS, tq), F32)] + score_bufs,
        compiler_params=_cparams(("parallel", "arbitrary")),
        name="attn_fox" if shared else "attn_mla",
    )(*args)


def _cumsum_kernel(x_ref, tri_ref, o_ref):
    rows, n = x_ref.shape
    tri = tri_ref[...]
    carry = jnp.zeros((rows, 1), F32)
    for t in range(n // LANES):
        xt = x_ref[:, t * LANES:(t + 1) * LANES]
        hi, mid, lo = _split3(xt)
        o_ref[:, t * LANES:(t + 1) * LANES] = (_dot(hi, tri) + _dot(mid, tri) + _dot(lo, tri) + carry) * LOG2E
        carry = carry + jnp.sum(xt, axis=1, keepdims=True)


def _cumsum_lanes(x):
    rows, n = x.shape
    tri = (jnp.arange(LANES)[:, None] <= jnp.arange(LANES)[None, :]).astype(BF16)
    return pl.pallas_call(
        _cumsum_kernel,
        out_shape=jax.ShapeDtypeStruct((rows, n), F32),
        compiler_params=pltpu.CompilerParams(vmem_limit_bytes=VMEM_LIMIT),
        name="cumsum",
    )(x, tri)


def _rwkv_tokens(pbv, prev_row, mu, w0, ww, a0, wa, wg, k_k, k_a, bd):
    rolled = pltpu.roll(pbv, 1, 0)
    shifted = jnp.where(_iota(pbv.shape, 0) == 0, prev_row, rolled)
    xs = pbv + (shifted - pbv) * mu
    r = xs[:, 0:D_B]
    kb = xs[:, D_B:2 * D_B]
    o3 = 3 * D_B
    wa_in = xs[:, o3:o3 + LANES]
    wa_in = jnp.where(_iota(wa_in.shape, 1) < W_LORA, jnp.tanh(wa_in), wa_in).astype(BF16)
    lw = -DECAY_SCALE * _sigmoid(w0 + _dot(wa_in, ww))
    a = _sigmoid(a0 + _dot(wa_in, wa))
    gate = _dot(_sigmoid(xs[:, o3 + LANES:o3 + 2 * LANES]).astype(BF16), wg)
    kk = kb * k_k
    kk = kk * lax.rsqrt(_dot_sel(kk * kk, bd) + 1e-12)
    return r, kk, kb * (1.0 + (a - 1.0) * k_a), kk * a, xs[:, 2 * D_B:3 * D_B], lw, gate


def _rwkv_chunk_kernel(pb_ref, pbprev_ref, prev_ref, mu_ref, w0_ref, ww_ref, a0_ref, wa_ref, wg_ref, kk_ref, ka_ref,
                       bd_ref, rt_ref, yl_ref, g_ref, h_ref, ro_ref, ko_ref, vo_ref, go_ref):
    s = pb_ref.shape[1]
    nc = s // CHUNK
    prev_row = jnp.where(pl.program_id(1) == 0, prev_ref[0], pbprev_ref[0, 7:8, :])
    r_all, kk_all, k_all, b_all, v_all, lw_all, gate = _rwkv_tokens(
        pb_ref[0], prev_row, mu_ref[...], w0_ref[...], ww_ref[...], a0_ref[...], wa_ref[...], wg_ref[...],
        kk_ref[...], ka_ref[...], bd_ref[...])
    ro_ref[0] = r_all.astype(BF16)
    ko_ref[0] = k_all.astype(BF16)
    vo_ref[0] = v_all.astype(BF16)
    go_ref[0] = gate.astype(BF16)
    row = _iota((s, LANES), 0)
    rin = jnp.bitwise_and(row, CHUNK - 1)
    ti = _iota((s, s), 0)
    si = _iota((s, s), 1)
    same = jnp.right_shift(ti, 6) == jnp.right_shift(si, 6)
    strict = jnp.logical_and(same, si < ti)
    incl = jnp.logical_and(same, si <= ti)
    eye = (ti == si).astype(F32)
    low = _iota((s, LANES), 1) < HEAD_DIM
    ji = _iota((LANES, LANES), 0)
    jj = _iota((LANES, LANES), 1)
    blockdiag = (ji < HEAD_DIM) == (jj < HEAD_DIM)
    npair = D_B // LANES
    pairs = range(npair)
    heads = [(p, x) for p in pairs for x in range(2)]

    tot, kkt, rt, khbh, kw, bw, v = [], [], [], [], [], [], []
    for p in pairs:
        sl = slice(p * LANES, (p + 1) * LANES)
        lw, kp, bp = lw_all[:, sl], k_all[:, sl], b_all[:, sl]
        cl = lw
        for sh in (1, 2, 4, 8, 16, 32):
            cl = cl + jnp.where(rin >= sh, pltpu.roll(cl, sh, 0), 0.0)
        tp = jnp.concatenate(
            [jnp.broadcast_to(cl[c * CHUNK + CHUNK - 1:(c + 1) * CHUNK, :], (CHUNK, LANES)) for c in range(nc)],
            axis=0)
        e_ncl = jnp.exp(-cl)
        e_rem = jnp.exp(tp - cl)
        tot.append(tp)
        kkt.append(kk_all[:, sl] * jnp.exp(cl - lw))
        rt.append(r_all[:, sl] * jnp.exp(cl))
        khbh.append(jnp.concatenate([kp * e_ncl, bp * e_ncl], axis=0).astype(BF16))
        kw.append(kp * e_rem)
        bw.append(bp * e_rem)
        v.append(v_all[:, sl])

    keep = lambda x: low if x == 0 else jnp.logical_not(low)
    kkt_x = [jnp.where(keep(x), kkt[p], 0.0) for p, x in heads]
    rt_x = [jnp.where(keep(x), rt[p], 0.0) for p, x in heads]
    akk, ark, arb, pw, tinv = [], [], [], [], []
    for i, (p, x) in enumerate(heads):
        p4 = _dot_nt(jnp.concatenate([kkt_x[i], rt_x[i]], axis=0).astype(BF16), khbh[p])
        akk.append(jnp.where(strict, p4[:s, :s], 0.0))
        ark.append(jnp.where(incl, p4[s:, :s], 0.0))
        arb.append(jnp.where(incl, p4[s:, s:], 0.0))
        pw.append(jnp.where(strict, -p4[:s, s:], 0.0))
        tinv.append(eye + pw[i])
    for _ in range(5):
        pw = [_mm(m, m) for m in pw]
        tinv = [t + _mm(t, m) for t, m in zip(tinv, pw)]
    av = [_mm(jnp.concatenate([akk[i], ark[i]], axis=0), v[p]) for i, (p, x) in enumerate(heads)]
    tx = [_mm(tinv[i], jnp.concatenate([av[i][:s], kkt_x[i]], axis=1)) for i in range(len(heads))]
    ax = [_mm(arb[i], tx[i]) for i in range(len(heads))]

    for p in pairs:
        sl = slice(p * LANES, (p + 1) * LANES)
        a, b2 = 2 * p, 2 * p + 1
        uloc2 = jnp.where(low, tx[a][:, :LANES], tx[b2][:, :LANES])
        kkt2 = jnp.where(low, tx[a][:, LANES:], tx[b2][:, LANES:])
        rt_ref[0, :, sl] = jnp.where(low, rt_x[a] - ax[a][:, LANES:], rt_x[b2] - ax[b2][:, LANES:]).astype(BF16)
        yl_ref[0, :, sl] = jnp.where(low, av[a][s:] - ax[a][:, :LANES], av[b2][s:] - ax[b2][:, :LANES]).astype(BF16)
        for c in range(nc):
            inc = jnp.right_shift(row, 6) == c
            bw_c = jnp.where(inc, bw[p], 0.0)
            kw_c = jnp.where(inc, kw[p], 0.0)
            e_tot = jnp.exp(tot[p][c * CHUNK:c * CHUNK + 1, :])
            gm = jnp.where(ji == jj, e_tot, 0.0) - _mm(bw_c, kkt2, _dot_tn)
            hm = _mm(kw_c, v[p], _dot_tn) - _mm(bw_c, uloc2, _dot_tn)
            g_ref[0, p, c] = jnp.where(blockdiag, gm, 0.0).astype(BF16)
            h_ref[0, p, c] = jnp.where(blockdiag, hm, 0.0).astype(BF16)


def _rwkv_chunk(pb3, prev, lw, s):
    bsz, t, _ = pb3.shape
    npair = H_B // 2
    nc = t // CHUNK
    sb = s // 8
    tok = pl.BlockSpec((1, s, D_B), lambda bi, i: (bi, i, 0))
    mat = pl.BlockSpec((1, npair, s // CHUNK, LANES, LANES), lambda bi, i: (bi, 0, i, 0, 0))
    params = [lw['rw_mu'], lw['rw_w0'], lw['rw_ww'], lw['rw_a0'], lw['rw_wa'], lw['rw_wg'], lw['rw_k_k'],
              lw['rw_k_a'], lw['bd384']]
    full = lambda a: pl.BlockSpec(a.shape, lambda bi, i: (0, 0))
    return pl.pallas_call(
        _rwkv_chunk_kernel,
        grid=(bsz, t // s),
        in_specs=[pl.BlockSpec((1, s, B_IN), lambda bi, i: (bi, i, 0)),
                  pl.BlockSpec((1, 8, B_IN), lambda bi, i: (bi, jnp.maximum(i * sb - 1, 0), 0)),
                  pl.BlockSpec((1, 1, B_IN), lambda bi, i: (bi, 0, 0))] + [full(a) for a in params],
        out_specs=[tok, tok, mat, mat] + [tok] * 4,
        out_shape=[jax.ShapeDtypeStruct((bsz, t, D_B), BF16)] * 2
                  + [jax.ShapeDtypeStruct((bsz, npair, nc, LANES, LANES), BF16)] * 2
                  + [jax.ShapeDtypeStruct((bsz, t, D_B), BF16)] * 4,
        compiler_params=_cparams(("parallel", "parallel")),
        name="rwkv_chunk",
    )(pb3, pb3, prev, *params)


def _rwkv_scan_kernel(s0_ref, g_ref, h_ref, rt_ref, yl_ref, r_ref, k_ref, v_ref, gate_ref,
                      lnw_ref, lnb_ref, rk_ref, bd_ref, o_ref, sfin_ref, st_scr, y_scr):
    npair, ncb = g_ref.shape[1], g_ref.shape[2]

    @pl.when(pl.program_id(1) == 0)
    def _():
        st_scr[...] = s0_ref[0]

    def times_state(a, st):
        hi, lo = _split2(st)
        return _dot(a, hi) + _dot(a, lo)

    sts = [st_scr[p] for p in range(npair)]
    for c in range(ncb):
        rows = slice(c * CHUNK, (c + 1) * CHUNK)
        for p in range(npair):
            sl = slice(p * LANES, (p + 1) * LANES)
            y_scr[rows, sl] = times_state(rt_ref[0, rows, sl], sts[p]) + yl_ref[0, rows, sl].astype(F32)
            sts[p] = times_state(g_ref[0, p, c], sts[p]) + h_ref[0, p, c].astype(F32)
    for p in range(npair):
        st_scr[p] = sts[p]
        sfin_ref[0, p] = sts[p]

    bd = bd_ref[...]
    y = y_scr[...]
    mu = _dot_sel(y, bd) * (1.0 / HEAD_DIM)
    d = y - mu
    var = _dot_sel(d * d, bd) * (1.0 / HEAD_DIM)
    yn = d * lax.rsqrt(var + GN_EPS) * lnw_ref[...] + lnb_ref[...]
    rk = r_ref[0].astype(F32) * k_ref[0].astype(F32) * rk_ref[...]
    bonus = _dot_sel(rk, bd) * v_ref[0].astype(F32)
    o_ref[0] = ((yn + bonus) * gate_ref[0].astype(F32)).astype(BF16)


def _rwkv_scan(s0, g, h, rt, yl, r, k, v, gate, lw, ncb):
    bsz, t, _ = rt.shape
    npair = H_B // 2
    nc = t // CHUNK
    st_spec = pl.BlockSpec((1, npair, LANES, LANES), lambda bi, i: (bi, 0, 0, 0))
    mat = pl.BlockSpec((1, npair, ncb, LANES, LANES), lambda bi, i: (bi, 0, i, 0, 0))
    tok = pl.BlockSpec((1, ncb * CHUNK, D_B), lambda bi, i: (bi, i, 0))
    params = [lw['rw_ln_w'], lw['rw_ln_b'], lw['rw_r_k'], lw['bd384']]
    full = lambda a: pl.BlockSpec(a.shape, lambda bi, i: (0, 0))
    return pl.pallas_call(
        _rwkv_scan_kernel,
        grid=(bsz, nc // ncb),
        in_specs=[st_spec, mat, mat] + [tok] * 6 + [full(a) for a in params],
        out_specs=[tok, st_spec],
        out_shape=[jax.ShapeDtypeStruct((bsz, t, D_B), BF16), jax.ShapeDtypeStruct((bsz, npair, LANES, LANES), F32)],
        scratch_shapes=[pltpu.VMEM((npair, LANES, LANES), F32), pltpu.VMEM((ncb * CHUNK, D_B), F32)],
        compiler_params=_cparams(("parallel", "arbitrary")),
        name="rwkv_scan",
    )(s0, g, h, rt, yl, r, k, v, gate, *params)


_L_EXP = 16


def _moe_kernel(x_ref, oa_ref, ob_ref, oc_ref, wo_ref, gffn_ref, wr_ref, br_ref, wg_ref, wu_ref, wd_ref,
                o_ref, acc_ref, h_ref, comb_ref):
    e = pl.program_id(1)

    @pl.when(e == 0)
    def _():
        da = H_A * HEAD_DIM
        x1 = (x_ref[...] + _dot(oa_ref[...], wo_ref[0:da, :]) + _dot(ob_ref[...], wo_ref[da:da + D_B, :])
              + _dot(oc_ref[...], wo_ref[da + D_B:, :]))
        acc_ref[...] = x1
        hf = x1 * lax.rsqrt(jnp.mean(x1 * x1, axis=-1, keepdims=True) + RMS_EPS) * gffn_ref[...]
        h_ref[...] = hf.astype(BF16)

        logit = _dot3(hf, wr_ref[...]) + br_ref[...]
        lane_i = _iota(logit.shape, 1)
        lane = lane_i.astype(F32)
        big = jnp.float32(3e38)
        is_g = lane_i < N_GROUPS
        gl = jnp.where(is_g, logit, -big)
        gmax = jnp.max(gl, axis=1, keepdims=True)
        pg_top = 1.0 / jnp.sum(jnp.where(is_g, jnp.exp(gl - gmax), 0.0), axis=1, keepdims=True)
        g_idx = jnp.min(jnp.where(jnp.logical_and(is_g, gl == gmax), lane, big), axis=1, keepdims=True)
        el = lane_i - _L_EXP
        in_e = jnp.logical_and(el >= 0, el < N_EXPERTS)
        sel = jnp.logical_and(in_e, jnp.right_shift(el, 2).astype(F32) == g_idx)
        l1 = jnp.where(sel, logit, -big)
        v1 = jnp.max(l1, axis=1, keepdims=True)
        i1 = jnp.min(jnp.where(jnp.logical_and(sel, l1 == v1), lane, big), axis=1, keepdims=True)
        sel2 = jnp.logical_and(sel, lane != i1)
        l2 = jnp.where(sel2, logit, -big)
        v2 = jnp.max(l2, axis=1, keepdims=True)
        i2 = jnp.min(jnp.where(jnp.logical_and(sel2, l2 == v2), lane, big), axis=1, keepdims=True)
        e2 = jnp.exp(v2 - v1)
        den = 1.0 / (1.0 + e2)
        comb_ref[...] = (jnp.where(lane == i1, den * pg_top, 0.0) + jnp.where(lane == i2, e2 * den * pg_top, 0.0))

    h = h_ref[...]
    comb = comb_ref[...]
    lane_c = _iota(comb.shape, 1)
    gates = [_dot(h, wg_ref[j]) for j in range(E_PER_GROUP)]
    ups = [_dot(h, wu_ref[j]) for j in range(E_PER_GROUP)]
    acts = []
    for j in range(E_PER_GROUP):
        ce = jnp.sum(jnp.where(lane_c == e * E_PER_GROUP + j + _L_EXP, comb, 0.0), axis=1, keepdims=True)
        acts.append((gates[j] * _sigmoid(gates[j]) * ups[j] * ce).astype(BF16))
    acc_ref[...] += _dot(jnp.concatenate(acts, axis=1), wd_ref[...])

    @pl.when(e == N_GROUPS - 1)
    def _():
        o_ref[...] = acc_ref[...]


def _outproj_moe(x2d, oa, ob, oc, lw, tm):
    n = x2d.shape[0]
    row = lambda w: pl.BlockSpec((tm, w), lambda i, e: (i, 0))
    full = lambda a: pl.BlockSpec(a.shape, lambda i, e: (0, 0))
    expert_in = pl.BlockSpec((E_PER_GROUP, D_MODEL, D_FF_E), lambda i, e: (e, 0, 0))
    return pl.pallas_call(
        _moe_kernel,
        grid=(n // tm, N_GROUPS),
        in_specs=[row(D_MODEL), row(H_A * HEAD_DIM), row(D_B), row(D_C), full(lw['w_out']), full(lw['g_ffn']),
                  full(lw['w_r']), full(lw['b_r']), expert_in, expert_in,
                  pl.BlockSpec((E_PER_GROUP * D_FF_E, D_MODEL), lambda i, e: (e, 0))],
        out_specs=row(D_MODEL),
        out_shape=jax.ShapeDtypeStruct((n, D_MODEL), F32),
        scratch_shapes=[pltpu.VMEM((tm, D_MODEL), F32), pltpu.VMEM((tm, D_MODEL), BF16), pltpu.VMEM((tm, LANES), F32)],
        compiler_params=_cparams(("parallel", "arbitrary")),
        name="outproj_moe",
    )(x2d, oa, ob, oc, lw['w_out'], lw['g_ffn'], lw['w_r'], lw['b_r'], lw['w_g'], lw['w_u'], lw['w_d'])


def _place(pieces, width):
    rows = pieces[0][1].shape[0]
    cols, at = [], 0
    for off, a in pieces:
        if off > at:
            cols.append(jnp.zeros((rows, off - at), F32))
        cols.append(a.astype(F32))
        at = off + a.shape[1]
    if width > at:
        cols.append(jnp.zeros((rows, width - at), F32))
    return jnp.concatenate(cols, axis=1)


def _row(v, width=None, off=0):
    v = v.reshape(1, -1).astype(F32)
    return v if width is None else _place([(off, v)], width)


def _block_diag_ones(n, blk):
    i = jnp.arange(n) // blk
    return (i[:, None] == i[None, :]).astype(BF16)


def _layer_weights(p, l):
    g = lambda name: p[name][l]
    w_in = g('w_in')
    o_b, o_c = A_IN, A_IN + B_IN
    w_in_p = _place([(_C_QL, w_in[:, :Q_LORA]), (_C_KV, w_in[:, Q_LORA:Q_LORA + KV_LORA]),
                     (_C_KR + NOPE, w_in[:, Q_LORA + KV_LORA:A_IN]), (_C_PB, w_in[:, o_b:o_c + 3 * D_C]),
                     (_C_F, w_in[:, o_c + 3 * D_C:])], _C_END).astype(BF16)
    w_uq = g('mla_w_uq').reshape(Q_LORA, H_A, QK_DIM)
    w_uq = jnp.pad(w_uq, ((0, 256 - Q_LORA), (0, 0), (0, LANES - QK_DIM))).reshape(256, H_A * LANES).astype(BF16)
    w_ukv = g('mla_w_ukv').reshape(KV_LORA, H_A, NOPE + HEAD_DIM)
    w_uk = jnp.pad(w_ukv[:, :, :NOPE], ((0, 0), (0, 0), (0, LANES - NOPE))).reshape(KV_LORA, H_A * LANES).astype(BF16)
    w_uv = w_ukv[:, :, NOPE:].reshape(KV_LORA, H_A * HEAD_DIM).astype(BF16)
    zeros_w = jnp.zeros((W_LORA, D_B), F32)
    w_r = _place([(0, g('moe_w_rg')), (_L_EXP, g('moe_w_re'))], LANES)
    b_r = _place([(0, g('moe_b_rg').reshape(1, -1)), (_L_EXP, g('moe_b_re').reshape(1, -1))], LANES)
    return dict(
        g_mix=_row(g('g_mix')), w_in=w_in_p, g_qa=_row(g('mla_g_qa'), 256), w_uq=w_uq, g_kva=_row(g('mla_g_kva')),
        g_qn=_row(g('mla_g_qn'), LANES), g_fq=_row(jnp.tile(g('fox_g_qn'), H_C)),
        g_fk=_row(jnp.tile(g('fox_g_kn'), H_C)), b_f=_row(g('fox_b_f'), LANES),
        ones128=jnp.ones((LANES, LANES), BF16), eye128=jnp.eye(LANES, dtype=BF16),
        bd256=_block_diag_ones(D_C, HEAD_DIM), bd384=_block_diag_ones(D_B, HEAD_DIM),
        w_uk=w_uk, w_uv=w_uv, g_kn=_row(g('mla_g_kn')[:NOPE], LANES), g_kr=_row(g('mla_g_kn')[NOPE:], LANES, NOPE),
        rw_mu=_row(g('rw_mu')), rw_w0=_row(g('rw_w0')), rw_a0=_row(g('rw_a0')),
        rw_ww=jnp.concatenate([g('rw_w_up'), zeros_w], axis=0).astype(BF16),
        rw_wa=jnp.concatenate([zeros_w, g('rw_a_up')], axis=0).astype(BF16),
        rw_wg=g('rw_g_up').astype(BF16), rw_k_k=_row(g('rw_k_k')), rw_k_a=_row(g('rw_k_a')),
        rw_r_k=_row(g('rw_r_k')), rw_ln_w=_row(g('rw_ln_w')), rw_ln_b=_row(g('rw_ln_b')),
        w_out=g('w_out').astype(BF16), g_ffn=_row(g('g_ffn')), w_r=w_r, b_r=b_r,
        w_g=g('moe_w_gate').astype(BF16), w_u=g('moe_w_up').astype(BF16),
        w_d=g('moe_w_down').astype(BF16).reshape(N_EXPERTS * D_FF_E, D_MODEL),
    )


def _rope_tables(pos):
    half = ROPE // 2
    inv = ROPE_BASE ** (-jnp.arange(half, dtype=F32) / half)
    ang = pos.astype(F32)[:, None] * inv[None, :]
    cos, sin = jnp.cos(ang), jnp.sin(ang)
    t = pos.shape[0]
    z = lambda w: jnp.zeros((t, w), F32)
    c = jnp.concatenate([jnp.ones((t, NOPE), F32), cos, cos, z(LANES - QK_DIM)], axis=1)
    s_left = jnp.concatenate([z(NOPE), -sin, z(half), z(LANES - QK_DIM)], axis=1)
    s_right = jnp.concatenate([z(NOPE), z(half), sin, z(LANES - QK_DIM)], axis=1)
    return c, s_left, s_right


def _pick(n, prefs):
    for t in prefs:
        if n % t == 0:
            return t
    return n


def _layer(x, lw, hist, stacks):
    b, t, _ = x.shape
    n = b * t
    past = 0 if hist is None else hist['ckv'].shape[1]
    x2d = x.reshape(n, D_MODEL)
    tm = _pick(t, (512, 256, 128, 64))
    tmp = _pick(past, (512, 256, 128, 64)) if past else 0
    eye = lw['eye128']

    q_tabs = _rope_tables(past + jnp.arange(t))
    outs = _inproj(x2d, lw, q_tabs, b, t, tm, stacks)
    stacks, (qt, pb, qct, kcb, vct, k_new, vt_new) = outs[:5], outs[5:]
    lf = stacks[4][-1]

    if hist is None:
        kp = vpt = None
    else:
        ckv_past = hist['ckv'].reshape(b * past, KV_LORA)
        kr_past = jnp.pad(hist['krope'].reshape(b * past, ROPE), ((0, 0), (NOPE, LANES - QK_DIM)))
        kp, vpt = _kvprep(ckv_past, kr_past, lw, _rope_tables(jnp.arange(past)), b, past, tmp)
        kp = kp.reshape(b, past, -1)
    o_a = _attention(qt, k_new.reshape(b, t, -1), vt_new, kp, vpt, None, eye, n_pairs=H_A // 2, shared=False,
                     frame_causal=False, tq=tm, tk=min(tm, ATTN_TK), tkp=min(tmp, ATTN_TK))

    lf3 = lf.reshape(b, t, H_C)
    lf_all = lf3 if hist is None else jnp.concatenate([hist['flogf'].astype(F32), lf3], axis=1)
    ltot = past + t
    lpad = -(-ltot // LANES) * LANES
    lf_t = jnp.pad(jnp.swapaxes(lf_all, 1, 2), ((0, 0), (0, 0), (0, lpad - ltot))).reshape(b * H_C, lpad)
    c_all = _cumsum_lanes(lf_t).reshape(b, H_C // 2, 2, lpad)
    cq = c_all[..., past:past + t]
    ck = jnp.swapaxes(c_all, 2, 3)
    if hist is None:
        kcp = vcpt = None
    else:
        kcp = hist['fk'].reshape(b, past, D_C)
        vcpt = _transpose_cast(hist['fv'].reshape(b * past, D_C), lw, b, past, tmp)
    o_c = _attention(qct, kcb.reshape(b, t, D_C), vct, kcp, vcpt, (cq, ck), eye, n_pairs=H_C // 2, shared=True,
                     frame_causal=True, tq=tm, tk=min(tm, ATTN_TK), tkp=min(tmp, ATTN_TK))

    prev = jnp.zeros((b, 1, B_IN), F32) if hist is None else hist['shift'].astype(F32)
    pb3 = pb.reshape(b, t, B_IN)
    rt, yl, gm, hm, r, k, v, g = _rwkv_chunk(pb3, prev, lw, _pick(t, (256, 128, 64)))
    npair = H_B // 2
    if hist is None:
        s0 = jnp.zeros((b, npair, LANES, LANES), F32)
    else:
        st = jnp.swapaxes(hist['wkv'].astype(F32), -1, -2).reshape(b, npair, 2, HEAD_DIM, HEAD_DIM)
        s0 = jnp.zeros((b, npair, 2, HEAD_DIM, 2, HEAD_DIM), F32)
        s0 = s0.at[:, :, 0, :, 0, :].set(st[:, :, 0]).at[:, :, 1, :, 1, :].set(st[:, :, 1])
        s0 = s0.reshape(b, npair, LANES, LANES)
    o_b, sfin = _rwkv_scan(s0, gm, hm, rt, yl, r, k, v, g, lw, _pick(t // CHUNK, (8, 4, 2, 1)))
    sf = sfin.reshape(b, npair, 2, HEAD_DIM, 2, HEAD_DIM)
    s_fin = jnp.stack([sf[:, :, 0, :, 0, :], sf[:, :, 1, :, 1, :]], axis=2).reshape(b, H_B, HEAD_DIM, HEAD_DIM)
    s_fin = jnp.swapaxes(s_fin, -1, -2)

    tmm = _pick(n, (1024, 512, 256, 128, 64))
    x_out = _outproj_moe(x2d, o_a.reshape(n, -1), o_b.reshape(n, D_B), o_c.reshape(n, -1), lw, tmm)

    return x_out.reshape(b, t, D_MODEL), stacks, (s_fin, pb3[:, -1:])


def _cache_outputs(stacks, b, t):
    ckv, kr, kc, vc, lf = stacks
    d = ckv.shape[0]
    return (ckv.reshape(d, b, t, KV_LORA), kr.reshape(d, b, t, ROPE), kc.reshape(d, b, t, H_C, HEAD_DIM),
            vc.reshape(d, b, t, H_C, HEAD_DIM), lf.reshape(d, b, t, H_C))


def kernel(x_prompt, x_sample, cache_mla_latent, cache_mla_krope, cache_fox_k, cache_fox_v, cache_fox_logf,
           state_rwkv_wkv, state_rwkv_shift, g_mix, w_in, mla_g_qa, mla_w_uq, mla_g_kva, mla_w_ukv, mla_g_qn,
           mla_g_kn, rw_mu, rw_w0, rw_w_up, rw_a0, rw_a_up, rw_g_up, rw_k_k, rw_k_a, rw_r_k, rw_ln_w, rw_ln_b,
           fox_g_qn, fox_g_kn, fox_b_f, w_out, g_ffn, moe_w_rg, moe_b_rg, moe_w_re, moe_b_re, moe_w_gate,
           moe_w_up, moe_w_down):
    params = dict(g_mix=g_mix, w_in=w_in, mla_g_qa=mla_g_qa, mla_w_uq=mla_w_uq, mla_g_kva=mla_g_kva,
                  mla_w_ukv=mla_w_ukv, mla_g_qn=mla_g_qn, mla_g_kn=mla_g_kn, rw_mu=rw_mu, rw_w0=rw_w0,
                  rw_w_up=rw_w_up, rw_a0=rw_a0, rw_a_up=rw_a_up, rw_g_up=rw_g_up, rw_k_k=rw_k_k, rw_k_a=rw_k_a,
                  rw_r_k=rw_r_k, rw_ln_w=rw_ln_w, rw_ln_b=rw_ln_b, fox_g_qn=fox_g_qn, fox_g_kn=fox_g_kn,
                  fox_b_f=fox_b_f, w_out=w_out, g_ffn=g_ffn, moe_w_rg=moe_w_rg, moe_b_rg=moe_b_rg,
                  moe_w_re=moe_w_re, moe_b_re=moe_b_re, moe_w_gate=moe_w_gate, moe_w_up=moe_w_up,
                  moe_w_down=moe_w_down)
    depth = g_mix.shape[0]
    yp, ys = x_prompt, x_sample
    p_stacks = s_stacks = None
    p_small, s_small = [], []
    for l in range(depth):
        lw = _layer_weights(params, l)
        hist = dict(ckv=cache_mla_latent[l], krope=cache_mla_krope[l], fk=cache_fox_k[l], fv=cache_fox_v[l],
                    flogf=cache_fox_logf[l], wkv=state_rwkv_wkv[l], shift=state_rwkv_shift[l])
        yp, p_stacks, sm = _layer(yp, lw, None, p_stacks)
        p_small.append(sm)
        ys, s_stacks, sm = _layer(ys, lw, hist, s_stacks)
        s_small.append(sm)
    p_out = _cache_outputs(p_stacks, *x_prompt.shape[:2]) + tuple(jnp.stack(t) for t in zip(*p_small))
    s_out = _cache_outputs(s_stacks, *x_sample.shape[:2]) + tuple(jnp.stack(t) for t in zip(*s_small))
    return (yp, ys) + p_out + s_out
```

```python
import functools
import math

import jax
import jax.numpy as jnp
from jax import lax
from jax.experimental import pallas as pl
from jax.experimental.pallas import tpu as pltpu

F32 = jnp.float32
BF16 = jnp.bfloat16

D_MODEL = 1024
HEAD_DIM = 64
H_A, H_B, H_C = 6, 6, 4
Q_LORA, KV_LORA, NOPE, ROPE = 192, 128, 64, 32
QK_DIM = NOPE + ROPE
ROPE_BASE = 10000.0
A_IN = Q_LORA + KV_LORA + ROPE
D_B = H_B * HEAD_DIM
W_LORA, A_LORA, G_LORA = 64, 64, 128
B_IN = 3 * D_B + W_LORA + A_LORA + G_LORA
DECAY_SCALE = math.exp(-0.5)
GN_EPS = 64e-5
D_C = H_C * HEAD_DIM
C_IN = 3 * D_C + H_C
N_GROUPS, E_PER_GROUP = 4, 4
N_EXPERTS = N_GROUPS * E_PER_GROUP
D_FF_E = 256
NEG_INF = -1e30
RMS_EPS = 1e-6
CHUNK = 64
LOG2E = math.log2(math.e)

LANES = 128
BF16_ROWS = 16
VMEM_LIMIT = 56 * 1024 * 1024
ATTN_TK = 512
ATTN_TQ = 1024
VT_ROWS = LANES + BF16_ROWS

_C_QL = 0
_C_KV = 256
_C_KR = 384
_C_PB = 512
_C_QC = _C_PB + B_IN
_C_KC = _C_QC + D_C
_C_VC = _C_KC + D_C
_C_F = _C_VC + D_C
_C_END = _C_F + LANES


def _cparams(sem):
    return pltpu.CompilerParams(dimension_semantics=sem, vmem_limit_bytes=VMEM_LIMIT)


def _dot(a, b):
    return jnp.dot(a, b, preferred_element_type=F32)


def _dot_nt(a, b):
    return lax.dot_general(a, b, (((1,), (1,)), ((), ())), preferred_element_type=F32)


def _dot_tn(a, b):
    return lax.dot_general(a, b, (((0,), (0,)), ((), ())), preferred_element_type=F32)


def _split2(x):
    hi = x.astype(BF16)
    lo = (x - hi.astype(F32)).astype(BF16)
    return hi, lo


def _split3(x):
    hi = x.astype(BF16)
    r = x - hi.astype(F32)
    mid = r.astype(BF16)
    lo = (r - mid.astype(F32)).astype(BF16)
    return hi, mid, lo


def _dot3(a, b, dot=_dot):
    ah, al = _split2(a)
    bh, bl = _split2(b)
    return dot(ah, bh) + (dot(ah, bl) + dot(al, bh))


def _mm(a, b, dot=_dot):
    return dot(a.astype(BF16), b.astype(BF16))


def _dot_sel(a, sel):
    return _dot(a.astype(BF16), sel)


def _iota(shape, dim):
    return lax.broadcasted_iota(jnp.int32, shape, dim)


def _rope(x, c, s_left, s_right):
    return x * c + pltpu.roll(x, 112, 1) * s_left + pltpu.roll(x, 16, 1) * s_right


def _sigmoid(x):
    return 1.0 / (1.0 + jnp.exp(-x))


def _log_sigmoid(x):
    return jnp.minimum(x, 0.0) - jnp.log(1.0 + jnp.exp(-jnp.abs(x)))


def _transpose_bf16(x, eye):
    return _dot_nt(eye, x.astype(BF16)).astype(BF16)


def _store_vt(vt_ref, pp, v_block, eye):
    base = pp * VT_ROWS
    vt_ref[0, base:base + LANES, :] = _transpose_bf16(v_block, eye)
    cols = vt_ref.shape[2]
    vt_ref[0, base + LANES:base + VT_ROWS, :] = jnp.where(_iota((BF16_ROWS, cols), 0) == 0, 1.0, 0.0).astype(BF16)


def _inproj_kernel(*refs, n_prev):
    it = iter(refs)
    (x_ref, gmix_ref, w_ref, gqa_ref, wuq_ref, gkva_ref, gqn_ref, gfq_ref, gfk_ref, bf_ref, ones_ref, bd_ref,
     eye_ref, wuk_ref, wuv_ref, gkn_ref, gkr_ref, c_ref, sl_ref, sr_ref) = (next(it) for _ in range(20))
    prev = [next(it) for _ in range(5)] if n_prev else []
    stacks = [next(it) for _ in range(5)]
    qt_ref, pb_ref, qct_ref, kcb_ref, vct_ref, km_ref, vtm_ref = (next(it) for _ in range(7))
    for s_ref, p_ref in zip(stacks, prev):
        s_ref[0:n_prev] = p_ref[...]
    ckv_ref, kr_ref, kc_ref, vc_ref, lf_ref = (s.at[n_prev] for s in stacks)

    x = x_ref[...]
    h = x * lax.rsqrt(jnp.mean(x * x, axis=-1, keepdims=True) + RMS_EPS) * gmix_ref[...]
    h = h.astype(BF16)
    eye = eye_ref[...]
    ones = ones_ref[...]
    bd = bd_ref[...]

    ql = _dot(h, w_ref[:, _C_QL:_C_QL + 256])
    kv = _dot(h, w_ref[:, _C_KV:_C_KV + KV_LORA])
    kr = _dot(h, w_ref[:, _C_KR:_C_KR + LANES])
    pb_ref[...] = _dot(h, w_ref[:, _C_PB:_C_PB + B_IN])
    qc = _dot(h, w_ref[:, _C_QC:_C_QC + D_C])
    kc = _dot(h, w_ref[:, _C_KC:_C_KC + D_C])
    vc = _dot(h, w_ref[:, _C_VC:_C_VC + D_C])
    f = _dot(h, w_ref[:, _C_F:_C_F + LANES]) + bf_ref[...]

    ql = ql * lax.rsqrt(jnp.sum(ql * ql, axis=-1, keepdims=True) * (1.0 / Q_LORA) + RMS_EPS) * gqa_ref[...]
    qh = _dot(ql.astype(BF16), wuq_ref[...])
    c, s_l, s_r = c_ref[...], sl_ref[...], sr_ref[...]
    qv = [qh[:, hh * LANES:(hh + 1) * LANES] for hh in range(H_A)]
    ss = [_dot_sel(q * q, ones) for q in qv]
    qn = [q * lax.rsqrt(s * (1.0 / QK_DIM) + RMS_EPS) * gqn_ref[...] for q, s in zip(qv, ss)]
    qr = [_rope(q, c, s_l, s_r) * (LOG2E * QK_DIM ** -0.5) for q in qn]
    for hh in range(H_A):
        qt_ref[0, hh * LANES:(hh + 1) * LANES, :] = _transpose_bf16(qr[hh], eye)

    ckv = kv * lax.rsqrt(jnp.mean(kv * kv, axis=-1, keepdims=True) + RMS_EPS) * gkva_ref[...]
    ckv_ref[...] = ckv
    kr_ref[...] = kr[:, NOPE:NOPE + ROPE]
    _mla_kv(ckv, kr, wuk_ref[...], wuv_ref[...], gkn_ref[...], gkr_ref[...], ones, eye, c, s_l, s_r, km_ref, vtm_ref)

    qc = qc * lax.rsqrt(_dot_sel(qc * qc, bd) * (1.0 / HEAD_DIM) + RMS_EPS) * gfq_ref[...]
    qc = qc * (LOG2E * HEAD_DIM ** -0.5)
    kc = kc * lax.rsqrt(_dot_sel(kc * kc, bd) * (1.0 / HEAD_DIM) + RMS_EPS) * gfk_ref[...]
    kc_ref[...] = kc
    kcb_ref[...] = kc.astype(BF16)
    vc_ref[...] = vc
    for pp in range(D_C // LANES):
        qct_ref[0, pp * LANES:(pp + 1) * LANES, :] = _transpose_bf16(qc[:, pp * LANES:(pp + 1) * LANES], eye)
        _store_vt(vct_ref, pp, vc[:, pp * LANES:(pp + 1) * LANES], eye)
    lf_ref[...] = _log_sigmoid(f)[:, :H_C]


_CACHE_WIDTHS = (KV_LORA, ROPE, D_C, D_C, H_C)


def _inproj(x2d, lw, tabs, bsz, t, tm, prev_stacks):
    n = x2d.shape[0]
    nt = t // tm
    n_prev = 0 if prev_stacks is None else prev_stacks[0].shape[0]
    row = lambda i: (i, 0)
    fixed = lambda i: (0, 0)
    tab = lambda i: (i % nt, 0)
    colmajor = lambda i: (i // nt, 0, i % nt)
    full = lambda a: pl.BlockSpec(a.shape, fixed)
    params = [lw['g_mix'], lw['w_in'], lw['g_qa'], lw['w_uq'], lw['g_kva'], lw['g_qn'], lw['g_fq'], lw['g_fk'],
              lw['b_f'], lw['ones128'], lw['bd256'], lw['eye128'], lw['w_uk'], lw['w_uv'], lw['g_kn'], lw['g_kr']]
    stack = lambda layers, w: pl.BlockSpec((layers, tm, w), lambda i: (0, i, 0))
    rowout = lambda w, dt: (pl.BlockSpec((tm, w), row), jax.ShapeDtypeStruct((n, w), dt))
    colout = lambda w: (pl.BlockSpec((1, w, tm), colmajor), jax.ShapeDtypeStruct((bsz, w, t), BF16))
    outs = [(stack(n_prev + 1, w), jax.ShapeDtypeStruct((n_prev + 1, n, w), F32)) for w in _CACHE_WIDTHS]
    outs += [colout(H_A * LANES), rowout(B_IN, F32), colout(D_C), rowout(D_C, BF16), colout(H_C // 2 * VT_ROWS),
             rowout(H_A * LANES, BF16), colout(H_A // 2 * VT_ROWS)]
    prev_args = [] if prev_stacks is None else list(prev_stacks)
    return pl.pallas_call(
        functools.partial(_inproj_kernel, n_prev=n_prev),
        grid=(n // tm,),
        in_specs=[pl.BlockSpec((tm, D_MODEL), row)] + [full(a) for a in params]
                 + [pl.BlockSpec((tm, LANES), tab)] * 3 + [stack(n_prev, w) for w in _CACHE_WIDTHS if n_prev],
        out_specs=[o[0] for o in outs],
        out_shape=[o[1] for o in outs],
        compiler_params=_cparams(("parallel",)),
        name="inproj",
    )(x2d, *params, *tabs, *prev_args)


def _mla_kv(ckv, kr, wuk, wuv, gkn, gkr, ones, eye, c, s_l, s_r, k_ref, vt_ref):
    cb = ckv.astype(BF16)
    kn = _dot(cb, wuk)
    v = _dot(cb, wuv)
    for pp in range(H_A // 2):
        _store_vt(vt_ref, pp, v[:, pp * LANES:(pp + 1) * LANES], eye)
    ssr = _dot_sel(kr * kr, ones)
    krg = _rope(kr * gkr, c, s_l, s_r)
    knh = [kn[:, hh * LANES:(hh + 1) * LANES] for hh in range(H_A)]
    ssn = [_dot_sel(k * k, ones) for k in knh]
    for hh in range(H_A):
        r = lax.rsqrt((ssn[hh] + ssr) * (1.0 / QK_DIM) + RMS_EPS)
        k_ref[:, hh * LANES:(hh + 1) * LANES] = ((knh[hh] * gkn + krg) * r).astype(BF16)


def _kvprep_kernel(ckv_ref, kr_ref, wuk_ref, wuv_ref, gkn_ref, gkr_ref, ones_ref, eye_ref, c_ref, sl_ref, sr_ref,
                   k_ref, vt_ref):
    _mla_kv(ckv_ref[...], kr_ref[...], wuk_ref[...], wuv_ref[...], gkn_ref[...], gkr_ref[...], ones_ref[...],
            eye_ref[...], c_ref[...], sl_ref[...], sr_ref[...], k_ref, vt_ref)


def _kvprep(ckv2d, kr128_2d, lw, tabs, bsz, t, tm):
    n = ckv2d.shape[0]
    nt = t // tm
    row = lambda i: (i, 0)
    fixed = lambda i: (0, 0)
    tab = lambda i: (i % nt, 0)
    full = lambda a: pl.BlockSpec(a.shape, fixed)
    params = [lw['w_uk'], lw['w_uv'], lw['g_kn'], lw['g_kr'], lw['ones128'], lw['eye128']]
    vt_rows = H_A // 2 * VT_ROWS
    return pl.pallas_call(
        _kvprep_kernel,
        grid=(n // tm,),
        in_specs=[pl.BlockSpec((tm, KV_LORA), row), pl.BlockSpec((tm, LANES), row)] + [full(a) for a in params]
                 + [pl.BlockSpec((tm, LANES), tab)] * 3,
        out_specs=[pl.BlockSpec((tm, H_A * LANES), row),
                   pl.BlockSpec((1, vt_rows, tm), lambda i: (i // nt, 0, i % nt))],
        out_shape=[jax.ShapeDtypeStruct((n, H_A * LANES), BF16),
                   jax.ShapeDtypeStruct((bsz, vt_rows, t), BF16)],
        compiler_params=_cparams(("parallel",)),
        name="kvprep",
    )(ckv2d, kr128_2d, *params, *tabs)


def _transpose_kernel(x_ref, eye_ref, o_ref):
    eye = eye_ref[...]
    for pp in range(x_ref.shape[1] // LANES):
        _store_vt(o_ref, pp, x_ref[:, pp * LANES:(pp + 1) * LANES], eye)


def _transpose_cast(x2d, lw, bsz, t, tm):
    n, w = x2d.shape
    nt = t // tm
    rows = w // LANES * VT_ROWS
    return pl.pallas_call(
        _transpose_kernel,
        grid=(n // tm,),
        in_specs=[pl.BlockSpec((tm, w), lambda i: (i, 0)), pl.BlockSpec((LANES, LANES), lambda i: (0, 0))],
        out_specs=pl.BlockSpec((1, rows, tm), lambda i: (i // nt, 0, i % nt)),
        out_shape=jax.ShapeDtypeStruct((bsz, rows, t), BF16),
        compiler_params=_cparams(("parallel",)),
        name="transpose_cast",
    )(x2d, lw['eye128'])


def _attn_kernel(*refs, tq, tk, tkp, nq, n_past, n_pairs, shared, frame_causal, has_bias, pipelined):
    it = iter(refs)
    q_ref, k_ref, vt_ref = next(it), next(it), next(it)
    if n_past:
        kp_ref, vpt_ref = next(it), next(it)
    if has_bias:
        cq_ref, ck_ref = next(it), next(it)
    eye_ref, o_ref = next(it), next(it)
    q_scr, m_scr, acc_scr = next(it), next(it), next(it)

    qi = 0 if nq == 1 else pl.program_id(1)
    low = _iota((LANES, tq), 0) < HEAD_DIM
    blk = lambda p, j: p if shared else 2 * p + j
    heads = [(p, j) for p in range(n_pairs) for j in range(2)]
    nh = len(heads)
    for p, j in heads:
        q = q_ref[0, blk(p, j) * LANES:(blk(p, j) + 1) * LANES, :]
        if shared:
            keep = low if j == 0 else jnp.logical_not(low)
            q = jnp.where(keep, q, jnp.zeros_like(q))
        q_scr[2 * p + j] = q
    m_scr[...] = jnp.full(m_scr.shape, -jnp.inf, F32)
    acc_scr[...] = jnp.zeros(acc_scr.shape, F32)

    def score(kr, st, width, p, j):
        kt = kr[0, pl.ds(st, width), blk(p, j) * LANES:(blk(p, j) + 1) * LANES].astype(BF16)
        return _dot(kt, q_scr[2 * p + j])

    def scores_to(buf, kr, st, width):
        for p, j in heads:
            buf[2 * p + j] = score(kr, st, width, p, j)

    def consume(get_s, vr, st, width, key_start, mask):
        ss = []
        for p, j in heads:
            s = get_s(p, j)
            if has_bias:
                s = s - ck_ref[0, pl.ds(key_start, width), 2 * p + j:2 * p + j + 1]
            if mask is not None:
                s = jnp.where(mask, s, NEG_INF)
            ss.append(s)
        cqs = [cq_ref[0, p, j:j + 1, :] if has_bias else 0.0 for p, j in heads]
        m_prev = [m_scr[h] for h in range(nh)]
        m_next = [jnp.maximum(m_prev[h], jnp.max(ss[h], axis=0, keepdims=True) + cqs[h]) for h in range(nh)]
        prs = [jnp.exp2(ss[h] - (m_next[h] - cqs[h])).astype(BF16) for h in range(nh)]
        alphas = [jnp.exp2(m_prev[h] - m_next[h]) for h in range(nh)]
        pvs = [_dot(vr[0, p * VT_ROWS:(p + 1) * VT_ROWS, pl.ds(st, width)], prs[2 * p + j]) for p, j in heads]
        for h in range(nh):
            acc_scr[h] = acc_scr[h] * alphas[h] + pvs[h]
            m_scr[h] = m_next[h]

    def step(kr, vr, st, width, key_start, mask):
        consume(lambda p, j: score(kr, st, width, p, j), vr, st, width, key_start, mask)

    kidx = _iota((tk, tq), 0)
    qidx = _iota((tk, tq), 1)
    diag_mask = lambda d: ((kidx + d * tk <= qidx) if frame_causal else
                           (jnp.right_shift(kidx + d * tk, 6) <= jnp.right_shift(qidx, 6)))

    if pipelined:
        sa, sb = next(it), next(it)
        tile = lambda t: pl.multiple_of(t * tk, tk)
        from_a = lambda p, j: sa[2 * p + j]
        from_b = lambda p, j: sb[2 * p + j]
        scores_to(sa, k_ref, tile(0), tk)

        def pair_body(u, carry):
            t0 = 2 * u
            scores_to(sb, k_ref, tile(t0 + 1), tk)
            consume(from_a, vt_ref, tile(t0), tk, tile(t0), None)
            scores_to(sa, k_ref, tile(t0 + 2), tk)
            consume(from_b, vt_ref, tile(t0 + 1), tk, tile(t0 + 1), None)
            return carry
        if tq == 2 * tk:
            first = 2 * qi
            lax.fori_loop(0, qi, pair_body, 0)
            scores_to(sb, k_ref, tile(first + 1), tk)
            consume(from_a, vt_ref, tile(first), tk, tile(first), diag_mask(0))
            consume(from_b, vt_ref, tile(first + 1), tk, tile(first + 1), diag_mask(1))
        else:
            lax.fori_loop(0, qi // 2, pair_body, 0)
            odd = lax.rem(qi, 2) == 1

            @pl.when(jnp.logical_not(odd))
            def _():
                consume(from_a, vt_ref, tile(qi), tk, tile(qi), diag_mask(0))

            @pl.when(odd)
            def _():
                scores_to(sb, k_ref, tile(qi), tk)
                consume(from_a, vt_ref, tile(qi - 1), tk, tile(qi - 1), None)
                consume(from_b, vt_ref, tile(qi), tk, tile(qi), diag_mask(0))
    else:
        if n_past:
            def past_body(t, carry):
                st = pl.multiple_of(t * tkp, tkp)
                step(kp_ref, vpt_ref, st, tkp, st, None)
                return carry
            lax.fori_loop(0, n_past // tkp, past_body, 0)

        def new_body(t, carry):
            st = pl.multiple_of(t * tk, tk)
            step(k_ref, vt_ref, st, tk, n_past + st, None)
            return carry
        if nq > 1:
            lax.fori_loop(0, qi * (tq // tk), new_body, 0)

        for d in range(tq // tk):
            st = qi * tq + d * tk
            st = st if nq == 1 else pl.multiple_of(st, tk)
            step(k_ref, vt_ref, st, tk, n_past + st, diag_mask(d))

    for p in range(n_pairs):
        a, b = acc_scr[2 * p], acc_scr[2 * p + 1]
        ot = jnp.where(low, a[:LANES] / a[LANES:LANES + 1], b[:LANES] / b[LANES:LANES + 1])
        o_ref[0, :, p * LANES:(p + 1) * LANES] = _dot_tn(ot.astype(BF16), eye_ref[...]).astype(o_ref.dtype)


def _attention(qt, k_new, vt_new, k_past, vt_past, bias, eye, *, n_pairs, shared, frame_causal, tq, tk, tkp):
    b, t = k_new.shape[0], k_new.shape[1]
    n_past = 0 if k_past is None else k_past.shape[1]
    nq = t // tq
    wq = qt.shape[1]
    args = [qt, k_new, vt_new]
    specs = [pl.BlockSpec((1, wq, tq), lambda bi, i: (bi, 0, i)),
             pl.BlockSpec((1, t, wq), lambda bi, i: (bi, 0, 0)),
             pl.BlockSpec((1, n_pairs * VT_ROWS, t), lambda bi, i: (bi, 0, 0))]
    if n_past:
        args += [k_past, vt_past]
        specs += [pl.BlockSpec((1, n_past, wq), lambda bi, i: (bi, 0, 0)),
                  pl.BlockSpec((1, n_pairs * VT_ROWS, n_past), lambda bi, i: (bi, 0, 0))]
    if bias is not None:
        cq, ck = bias
        args += [cq, ck]
        specs += [pl.BlockSpec((1, n_pairs, 2, tq), lambda bi, i: (bi, 0, 0, i)),
                  pl.BlockSpec((1,) + ck.shape[1:], lambda bi, i: (bi, 0, 0))]
    args.append(eye)
    specs.append(pl.BlockSpec((LANES, LANES), lambda bi, i: (0, 0)))
    pipelined = n_past == 0 and nq > 1 and tq in (tk, 2 * tk)
    kern = functools.partial(_attn_kernel, tq=tq, tk=tk, tkp=tkp, nq=nq, n_past=n_past, n_pairs=n_pairs,
                             shared=shared, frame_causal=frame_causal, has_bias=bias is not None,
                             pipelined=pipelined)
    nh = 2 * n_pairs
    score_bufs = [pltpu.VMEM((nh, tk, tq), F32)] * 2 if pipelined else []
    return pl.pallas_call(
        kern,
        grid=(b, nq),
        in_specs=specs,
        out_specs=pl.BlockSpec((1, tq, n_pairs * LANES), lambda bi, i: (bi, i, 0)),
        out_shape=jax.ShapeDtypeStruct((b, t, n_pairs * LANES), BF16),
        scratch_shapes=[pltpu.VMEM((nh, LANES, tq), BF16), pltpu.VMEM((nh, 1, tq), F32),
                        pltpu.VMEM((nh, VT_ROWS, tq), F32)] + score_bufs,
        compiler_params=_cparams(("parallel", "arbitrary")),
        name="attn_fox" if shared else "attn_mla",
    )(*args)


def _cumsum_kernel(x_ref, tri_ref, o_ref):
    rows, n = x_ref.shape
    tri = tri_ref[...]
    carry = jnp.zeros((rows, 1), F32)
    for t in range(n // LANES):
        xt = x_ref[:, t * LANES:(t + 1) * LANES]
        hi, mid, lo = _split3(xt)
        o_ref[:, t * LANES:(t + 1) * LANES] = (_dot(hi, tri) + _dot(mid, tri) + _dot(lo, tri) + carry) * LOG2E
        carry = carry + jnp.sum(xt, axis=1, keepdims=True)


def _cumsum_lanes(x):
    rows, n = x.shape
    tri = (jnp.arange(LANES)[:, None] <= jnp.arange(LANES)[None, :]).astype(BF16)
    return pl.pallas_call(
        _cumsum_kernel,
        out_shape=jax.ShapeDtypeStruct((rows, n), F32),
        compiler_params=pltpu.CompilerParams(vmem_limit_bytes=VMEM_LIMIT),
        name="cumsum",
    )(x, tri)


def _rwkv_tokens(pbv, prev_row, mu, w0, ww, a0, wa, wg, k_k, k_a, bd):
    rolled = pltpu.roll(pbv, 1, 0)
    shifted = jnp.where(_iota(pbv.shape, 0) == 0, prev_row, rolled)
    xs = pbv + (shifted - pbv) * mu
    r = xs[:, 0:D_B]
    kb = xs[:, D_B:2 * D_B]
    o3 = 3 * D_B
    wa_in = xs[:, o3:o3 + LANES]
    wa_in = jnp.where(_iota(wa_in.shape, 1) < W_LORA, jnp.tanh(wa_in), wa_in).astype(BF16)
    lw = -DECAY_SCALE * _sigmoid(w0 + _dot(wa_in, ww))
    a = _sigmoid(a0 + _dot(wa_in, wa))
    gate = _dot(_sigmoid(xs[:, o3 + LANES:o3 + 2 * LANES]).astype(BF16), wg)
    kk = kb * k_k
    kk = kk * lax.rsqrt(_dot_sel(kk * kk, bd) + 1e-12)
    return r, kk, kb * (1.0 + (a - 1.0) * k_a), kk * a, xs[:, 2 * D_B:3 * D_B], lw, gate


def _rwkv_chunk_kernel(pb_ref, pbprev_ref, prev_ref, mu_ref, w0_ref, ww_ref, a0_ref, wa_ref, wg_ref, kk_ref, ka_ref,
                       bd_ref, rt_ref, yl_ref, g_ref, h_ref, ro_ref, ko_ref, vo_ref, go_ref):
    s = pb_ref.shape[1]
    nc = s // CHUNK
    prev_row = jnp.where(pl.program_id(1) == 0, prev_ref[0], pbprev_ref[0, 7:8, :])
    r_all, kk_all, k_all, b_all, v_all, lw_all, gate = _rwkv_tokens(
        pb_ref[0], prev_row, mu_ref[...], w0_ref[...], ww_ref[...], a0_ref[...], wa_ref[...], wg_ref[...],
        kk_ref[...], ka_ref[...], bd_ref[...])
    ro_ref[0] = r_all.astype(BF16)
    ko_ref[0] = k_all.astype(BF16)
    vo_ref[0] = v_all.astype(BF16)
    go_ref[0] = gate.astype(BF16)
    row = _iota((s, LANES), 0)
    rin = jnp.bitwise_and(row, CHUNK - 1)
    ti = _iota((s, s), 0)
    si = _iota((s, s), 1)
    same = jnp.right_shift(ti, 6) == jnp.right_shift(si, 6)
    strict = jnp.logical_and(same, si < ti)
    incl = jnp.logical_and(same, si <= ti)
    eye = (ti == si).astype(F32)
    low = _iota((s, LANES), 1) < HEAD_DIM
    ji = _iota((LANES, LANES), 0)
    jj = _iota((LANES, LANES), 1)
    blockdiag = (ji < HEAD_DIM) == (jj < HEAD_DIM)
    npair = D_B // LANES
    pairs = range(npair)
    heads = [(p, x) for p in pairs for x in range(2)]

    tot, kkt, rt, khbh, kw, bw, v = [], [], [], [], [], [], []
    for p in pairs:
        sl = slice(p * LANES, (p + 1) * LANES)
        lw, kp, bp = lw_all[:, sl], k_all[:, sl], b_all[:, sl]
        cl = lw
        for sh in (1, 2, 4, 8, 16, 32):
            cl = cl + jnp.where(rin >= sh, pltpu.roll(cl, sh, 0), 0.0)
        tp = jnp.concatenate(
            [jnp.broadcast_to(cl[c * CHUNK + CHUNK - 1:(c + 1) * CHUNK, :], (CHUNK, LANES)) for c in range(nc)],
            axis=0)
        e_ncl = jnp.exp(-cl)
        e_rem = jnp.exp(tp - cl)
        tot.append(tp)
        kkt.append(kk_all[:, sl] * jnp.exp(cl - lw))
        rt.append(r_all[:, sl] * jnp.exp(cl))
        khbh.append(jnp.concatenate([kp * e_ncl, bp * e_ncl], axis=0).astype(BF16))
        kw.append(kp * e_rem)
        bw.append(bp * e_rem)
        v.append(v_all[:, sl])

    keep = lambda x: low if x == 0 else jnp.logical_not(low)
    kkt_x = [jnp.where(keep(x), kkt[p], 0.0) for p, x in heads]
    rt_x = [jnp.where(keep(x), rt[p], 0.0) for p, x in heads]
    akk, ark, arb, pw, tinv = [], [], [], [], []
    for i, (p, x) in enumerate(heads):
        p4 = _dot_nt(jnp.concatenate([kkt_x[i], rt_x[i]], axis=0).astype(BF16), khbh[p])
        akk.append(jnp.where(strict, p4[:s, :s], 0.0))
        ark.append(jnp.where(incl, p4[s:, :s], 0.0))
        arb.append(jnp.where(incl, p4[s:, s:], 0.0))
        pw.append(jnp.where(strict, -p4[:s, s:], 0.0))
        tinv.append(eye + pw[i])
    for _ in range(5):
        pw = [_mm(m, m) for m in pw]
        tinv = [t + _mm(t, m) for t, m in zip(tinv, pw)]
    av = [_mm(jnp.concatenate([akk[i], ark[i]], axis=0), v[p]) for i, (p, x) in enumerate(heads)]
    tx = [_mm(tinv[i], jnp.concatenate([av[i][:s], kkt_x[i]], axis=1)) for i in range(len(heads))]
    ax = [_mm(arb[i], tx[i]) for i in range(len(heads))]

    for p in pairs:
        sl = slice(p * LANES, (p + 1) * LANES)
        a, b2 = 2 * p, 2 * p + 1
        uloc2 = jnp.where(low, tx[a][:, :LANES], tx[b2][:, :LANES])
        kkt2 = jnp.where(low, tx[a][:, LANES:], tx[b2][:, LANES:])
        rt_ref[0, :, sl] = jnp.where(low, rt_x[a] - ax[a][:, LANES:], rt_x[b2] - ax[b2][:, LANES:]).astype(BF16)
        yl_ref[0, :, sl] = jnp.where(low, av[a][s:] - ax[a][:, :LANES], av[b2][s:] - ax[b2][:, :LANES]).astype(BF16)
        for c in range(nc):
            inc = jnp.right_shift(row, 6) == c
            bw_c = jnp.where(inc, bw[p], 0.0)
            kw_c = jnp.where(inc, kw[p], 0.0)
            e_tot = jnp.exp(tot[p][c * CHUNK:c * CHUNK + 1, :])
            gm = jnp.where(ji == jj, e_tot, 0.0) - _mm(bw_c, kkt2, _dot_tn)
            hm = _mm(kw_c, v[p], _dot_tn) - _mm(bw_c, uloc2, _dot_tn)
            g_ref[0, p, c] = jnp.where(blockdiag, gm, 0.0).astype(BF16)
            h_ref[0, p, c] = jnp.where(blockdiag, hm, 0.0).astype(BF16)


def _rwkv_chunk(pb3, prev, lw, s):
    bsz, t, _ = pb3.shape
    npair = H_B // 2
    nc = t // CHUNK
    sb = s // 8
    tok = pl.BlockSpec((1, s, D_B), lambda bi, i: (bi, i, 0))
    mat = pl.BlockSpec((1, npair, s // CHUNK, LANES, LANES), lambda bi, i: (bi, 0, i, 0, 0))
    params = [lw['rw_mu'], lw['rw_w0'], lw['rw_ww'], lw['rw_a0'], lw['rw_wa'], lw['rw_wg'], lw['rw_k_k'],
              lw['rw_k_a'], lw['bd384']]
    full = lambda a: pl.BlockSpec(a.shape, lambda bi, i: (0, 0))
    return pl.pallas_call(
        _rwkv_chunk_kernel,
        grid=(bsz, t // s),
        in_specs=[pl.BlockSpec((1, s, B_IN), lambda bi, i: (bi, i, 0)),
                  pl.BlockSpec((1, 8, B_IN), lambda bi, i: (bi, jnp.maximum(i * sb - 1, 0), 0)),
                  pl.BlockSpec((1, 1, B_IN), lambda bi, i: (bi, 0, 0))] + [full(a) for a in params],
        out_specs=[tok, tok, mat, mat] + [tok] * 4,
        out_shape=[jax.ShapeDtypeStruct((bsz, t, D_B), BF16)] * 2
                  + [jax.ShapeDtypeStruct((bsz, npair, nc, LANES, LANES), BF16)] * 2
                  + [jax.ShapeDtypeStruct((bsz, t, D_B), BF16)] * 4,
        compiler_params=_cparams(("parallel", "parallel")),
        name="rwkv_chunk",
    )(pb3, pb3, prev, *params)


def _rwkv_scan_kernel(s0_ref, g_ref, h_ref, rt_ref, yl_ref, r_ref, k_ref, v_ref, gate_ref,
                      lnw_ref, lnb_ref, rk_ref, bd_ref, o_ref, sfin_ref, st_scr, y_scr):
    npair, ncb = g_ref.shape[1], g_ref.shape[2]

    @pl.when(pl.program_id(1) == 0)
    def _():
        st_scr[...] = s0_ref[0]

    def times_state(a, st):
        hi, lo = _split2(st)
        return _dot(a, hi) + _dot(a, lo)

    sts = [st_scr[p] for p in range(npair)]
    for c in range(ncb):
        rows = slice(c * CHUNK, (c + 1) * CHUNK)
        for p in range(npair):
            sl = slice(p * LANES, (p + 1) * LANES)
            y_scr[rows, sl] = times_state(rt_ref[0, rows, sl], sts[p]) + yl_ref[0, rows, sl].astype(F32)
            sts[p] = times_state(g_ref[0, p, c], sts[p]) + h_ref[0, p, c].astype(F32)
    for p in range(npair):
        st_scr[p] = sts[p]
        sfin_ref[0, p] = sts[p]

    bd = bd_ref[...]
    y = y_scr[...]
    mu = _dot_sel(y, bd) * (1.0 / HEAD_DIM)
    d = y - mu
    var = _dot_sel(d * d, bd) * (1.0 / HEAD_DIM)
    yn = d * lax.rsqrt(var + GN_EPS) * lnw_ref[...] + lnb_ref[...]
    rk = r_ref[0].astype(F32) * k_ref[0].astype(F32) * rk_ref[...]
    bonus = _dot_sel(rk, bd) * v_ref[0].astype(F32)
    o_ref[0] = ((yn + bonus) * gate_ref[0].astype(F32)).astype(BF16)


def _rwkv_scan(s0, g, h, rt, yl, r, k, v, gate, lw, ncb):
    bsz, t, _ = rt.shape
    npair = H_B // 2
    nc = t // CHUNK
    st_spec = pl.BlockSpec((1, npair, LANES, LANES), lambda bi, i: (bi, 0, 0, 0))
    mat = pl.BlockSpec((1, npair, ncb, LANES, LANES), lambda bi, i: (bi, 0, i, 0, 0))
    tok = pl.BlockSpec((1, ncb * CHUNK, D_B), lambda bi, i: (bi, i, 0))
    params = [lw['rw_ln_w'], lw['rw_ln_b'], lw['rw_r_k'], lw['bd384']]
    full = lambda a: pl.BlockSpec(a.shape, lambda bi, i: (0, 0))
    return pl.pallas_call(
        _rwkv_scan_kernel,
        grid=(bsz, nc // ncb),
        in_specs=[st_spec, mat, mat] + [tok] * 6 + [full(a) for a in params],
        out_specs=[tok, st_spec],
        out_shape=[jax.ShapeDtypeStruct((bsz, t, D_B), BF16), jax.ShapeDtypeStruct((bsz, npair, LANES, LANES), F32)],
        scratch_shapes=[pltpu.VMEM((npair, LANES, LANES), F32), pltpu.VMEM((ncb * CHUNK, D_B), F32)],
        compiler_params=_cparams(("parallel", "arbitrary")),
        name="rwkv_scan",
    )(s0, g, h, rt, yl, r, k, v, gate, *params)


_L_EXP = 16


def _moe_kernel(x_ref, oa_ref, ob_ref, oc_ref, wo_ref, gffn_ref, wr_ref, br_ref, wg_ref, wu_ref, wd_ref,
                o_ref, acc_ref, h_ref, comb_ref):
    e = pl.program_id(1)

    @pl.when(e == 0)
    def _():
        da = H_A * HEAD_DIM
        x1 = (x_ref[...] + _dot(oa_ref[...], wo_ref[0:da, :]) + _dot(ob_ref[...], wo_ref[da:da + D_B, :])
              + _dot(oc_ref[...], wo_ref[da + D_B:, :]))
        acc_ref[...] = x1
        hf = x1 * lax.rsqrt(jnp.mean(x1 * x1, axis=-1, keepdims=True) + RMS_EPS) * gffn_ref[...]
        h_ref[...] = hf.astype(BF16)

        logit = _dot3(hf, wr_ref[...]) + br_ref[...]
        lane_i = _iota(logit.shape, 1)
        lane = lane_i.astype(F32)
        big = jnp.float32(3e38)
        is_g = lane_i < N_GROUPS
        gl = jnp.where(is_g, logit, -big)
        gmax = jnp.max(gl, axis=1, keepdims=True)
        pg_top = 1.0 / jnp.sum(jnp.where(is_g, jnp.exp(gl - gmax), 0.0), axis=1, keepdims=True)
        g_idx = jnp.min(jnp.where(jnp.logical_and(is_g, gl == gmax), lane, big), axis=1, keepdims=True)
        el = lane_i - _L_EXP
        in_e = jnp.logical_and(el >= 0, el < N_EXPERTS)
        sel = jnp.logical_and(in_e, jnp.right_shift(el, 2).astype(F32) == g_idx)
        l1 = jnp.where(sel, logit, -big)
        v1 = jnp.max(l1, axis=1, keepdims=True)
        i1 = jnp.min(jnp.where(jnp.logical_and(sel, l1 == v1), lane, big), axis=1, keepdims=True)
        sel2 = jnp.logical_and(sel, lane != i1)
        l2 = jnp.where(sel2, logit, -big)
        v2 = jnp.max(l2, axis=1, keepdims=True)
        i2 = jnp.min(jnp.where(jnp.logical_and(sel2, l2 == v2), lane, big), axis=1, keepdims=True)
        e2 = jnp.exp(v2 - v1)
        den = 1.0 / (1.0 + e2)
        comb_ref[...] = (jnp.where(lane == i1, den * pg_top, 0.0) + jnp.where(lane == i2, e2 * den * pg_top, 0.0))

    h = h_ref[...]
    comb = comb_ref[...]
    lane_c = _iota(comb.shape, 1)
    gates = [_dot(h, wg_ref[j]) for j in range(E_PER_GROUP)]
    ups = [_dot(h, wu_ref[j]) for j in range(E_PER_GROUP)]
    acts = []
    for j in range(E_PER_GROUP):
        ce = jnp.sum(jnp.where(lane_c == e * E_PER_GROUP + j + _L_EXP, comb, 0.0), axis=1, keepdims=True)
        acts.append((gates[j] * _sigmoid(gates[j]) * ups[j] * ce).astype(BF16))
    acc_ref[...] += _dot(jnp.concatenate(acts, axis=1), wd_ref[...])

    @pl.when(e == N_GROUPS - 1)
    def _():
        o_ref[...] = acc_ref[...]


def _outproj_moe(x2d, oa, ob, oc, lw, tm):
    n = x2d.shape[0]
    row = lambda w: pl.BlockSpec((tm, w), lambda i, e: (i, 0))
    full = lambda a: pl.BlockSpec(a.shape, lambda i, e: (0, 0))
    expert_in = pl.BlockSpec((E_PER_GROUP, D_MODEL, D_FF_E), lambda i, e: (e, 0, 0))
    return pl.pallas_call(
        _moe_kernel,
        grid=(n // tm, N_GROUPS),
        in_specs=[row(D_MODEL), row(H_A * HEAD_DIM), row(D_B), row(D_C), full(lw['w_out']), full(lw['g_ffn']),
                  full(lw['w_r']), full(lw['b_r']), expert_in, expert_in,
                  pl.BlockSpec((E_PER_GROUP * D_FF_E, D_MODEL), lambda i, e: (e, 0))],
        out_specs=row(D_MODEL),
        out_shape=jax.ShapeDtypeStruct((n, D_MODEL), F32),
        scratch_shapes=[pltpu.VMEM((tm, D_MODEL), F32), pltpu.VMEM((tm, D_MODEL), BF16), pltpu.VMEM((tm, LANES), F32)],
        compiler_params=_cparams(("parallel", "arbitrary")),
        name="outproj_moe",
    )(x2d, oa, ob, oc, lw['w_out'], lw['g_ffn'], lw['w_r'], lw['b_r'], lw['w_g'], lw['w_u'], lw['w_d'])


def _place(pieces, width):
    rows = pieces[0][1].shape[0]
    cols, at = [], 0
    for off, a in pieces:
        if off > at:
            cols.append(jnp.zeros((rows, off - at), F32))
        cols.append(a.astype(F32))
        at = off + a.shape[1]
    if width > at:
        cols.append(jnp.zeros((rows, width - at), F32))
    return jnp.concatenate(cols, axis=1)


def _row(v, width=None, off=0):
    v = v.reshape(1, -1).astype(F32)
    return v if width is None else _place([(off, v)], width)


def _block_diag_ones(n, blk):
    i = jnp.arange(n) // blk
    return (i[:, None] == i[None, :]).astype(BF16)


def _layer_weights(p, l):
    g = lambda name: p[name][l]
    w_in = g('w_in')
    o_b, o_c = A_IN, A_IN + B_IN
    w_in_p = _place([(_C_QL, w_in[:, :Q_LORA]), (_C_KV, w_in[:, Q_LORA:Q_LORA + KV_LORA]),
                     (_C_KR + NOPE, w_in[:, Q_LORA + KV_LORA:A_IN]), (_C_PB, w_in[:, o_b:o_c + 3 * D_C]),
                     (_C_F, w_in[:, o_c + 3 * D_C:])], _C_END).astype(BF16)
    w_uq = g('mla_w_uq').reshape(Q_LORA, H_A, QK_DIM)
    w_uq = jnp.pad(w_uq, ((0, 256 - Q_LORA), (0, 0), (0, LANES - QK_DIM))).reshape(256, H_A * LANES).astype(BF16)
    w_ukv = g('mla_w_ukv').reshape(KV_LORA, H_A, NOPE + HEAD_DIM)
    w_uk = jnp.pad(w_ukv[:, :, :NOPE], ((0, 0), (0, 0), (0, LANES - NOPE))).reshape(KV_LORA, H_A * LANES).astype(BF16)
    w_uv = w_ukv[:, :, NOPE:].reshape(KV_LORA, H_A * HEAD_DIM).astype(BF16)
    zeros_w = jnp.zeros((W_LORA, D_B), F32)
    w_r = _place([(0, g('moe_w_rg')), (_L_EXP, g('moe_w_re'))], LANES)
    b_r = _place([(0, g('moe_b_rg').reshape(1, -1)), (_L_EXP, g('moe_b_re').reshape(1, -1))], LANES)
    return dict(
        g_mix=_row(g('g_mix')), w_in=w_in_p, g_qa=_row(g('mla_g_qa'), 256), w_uq=w_uq, g_kva=_row(g('mla_g_kva')),
        g_qn=_row(g('mla_g_qn'), LANES), g_fq=_row(jnp.tile(g('fox_g_qn'), H_C)),
        g_fk=_row(jnp.tile(g('fox_g_kn'), H_C)), b_f=_row(g('fox_b_f'), LANES),
        ones128=jnp.ones((LANES, LANES), BF16), eye128=jnp.eye(LANES, dtype=BF16),
        bd256=_block_diag_ones(D_C, HEAD_DIM), bd384=_block_diag_ones(D_B, HEAD_DIM),
        w_uk=w_uk, w_uv=w_uv, g_kn=_row(g('mla_g_kn')[:NOPE], LANES), g_kr=_row(g('mla_g_kn')[NOPE:], LANES, NOPE),
        rw_mu=_row(g('rw_mu')), rw_w0=_row(g('rw_w0')), rw_a0=_row(g('rw_a0')),
        rw_ww=jnp.concatenate([g('rw_w_up'), zeros_w], axis=0).astype(BF16),
        rw_wa=jnp.concatenate([zeros_w, g('rw_a_up')], axis=0).astype(BF16),
        rw_wg=g('rw_g_up').astype(BF16), rw_k_k=_row(g('rw_k_k')), rw_k_a=_row(g('rw_k_a')),
        rw_r_k=_row(g('rw_r_k')), rw_ln_w=_row(g('rw_ln_w')), rw_ln_b=_row(g('rw_ln_b')),
        w_out=g('w_out').astype(BF16), g_ffn=_row(g('g_ffn')), w_r=w_r, b_r=b_r,
        w_g=g('moe_w_gate').astype(BF16), w_u=g('moe_w_up').astype(BF16),
        w_d=g('moe_w_down').astype(BF16).reshape(N_EXPERTS * D_FF_E, D_MODEL),
    )


def _rope_tables(pos):
    half = ROPE // 2
    inv = ROPE_BASE ** (-jnp.arange(half, dtype=F32) / half)
    ang = pos.astype(F32)[:, None] * inv[None, :]
    cos, sin = jnp.cos(ang), jnp.sin(ang)
    t = pos.shape[0]
    z = lambda w: jnp.zeros((t, w), F32)
    c = jnp.concatenate([jnp.ones((t, NOPE), F32), cos, cos, z(LANES - QK_DIM)], axis=1)
    s_left = jnp.concatenate([z(NOPE), -sin, z(half), z(LANES - QK_DIM)], axis=1)
    s_right = jnp.concatenate([z(NOPE), z(half), sin, z(LANES - QK_DIM)], axis=1)
    return c, s_left, s_right


def _pick(n, prefs):
    for t in prefs:
        if n % t == 0:
            return t
    return n


def _layer(x, lw, hist, stacks):
    b, t, _ = x.shape
    n = b * t
    past = 0 if hist is None else hist['ckv'].shape[1]
    x2d = x.reshape(n, D_MODEL)
    tm = _pick(t, (512, 256, 128, 64))
    tmp = _pick(past, (512, 256, 128, 64)) if past else 0
    tqa = _pick(t, (ATTN_TQ, 512, 256, 128, 64))
    eye = lw['eye128']

    q_tabs = _rope_tables(past + jnp.arange(t))
    outs = _inproj(x2d, lw, q_tabs, b, t, tm, stacks)
    stacks, (qt, pb, qct, kcb, vct, k_new, vt_new) = outs[:5], outs[5:]
    lf = stacks[4][-1]

    if hist is None:
        kp = vpt = None
    else:
        ckv_past = hist['ckv'].reshape(b * past, KV_LORA)
        kr_past = jnp.pad(hist['krope'].reshape(b * past, ROPE), ((0, 0), (NOPE, LANES - QK_DIM)))
        kp, vpt = _kvprep(ckv_past, kr_past, lw, _rope_tables(jnp.arange(past)), b, past, tmp)
        kp = kp.reshape(b, past, -1)
    o_a = _attention(qt, k_new.reshape(b, t, -1), vt_new, kp, vpt, None, eye, n_pairs=H_A // 2, shared=False,
                     frame_causal=False, tq=tm, tk=min(tm, ATTN_TK), tkp=min(tmp, ATTN_TK))

    lf3 = lf.reshape(b, t, H_C)
    lf_all = lf3 if hist is None else jnp.concatenate([hist['flogf'].astype(F32), lf3], axis=1)
    ltot = past + t
    lpad = -(-ltot // LANES) * LANES
    lf_t = jnp.pad(jnp.swapaxes(lf_all, 1, 2), ((0, 0), (0, 0), (0, lpad - ltot))).reshape(b * H_C, lpad)
    c_all = _cumsum_lanes(lf_t).reshape(b, H_C // 2, 2, lpad)
    cq = c_all[..., past:past + t]
    ck = jnp.swapaxes(c_all.reshape(b, H_C, lpad), 1, 2)
    if hist is None:
        kcp = vcpt = None
    else:
        kcp = hist['fk'].reshape(b, past, D_C)
        vcpt = _transpose_cast(hist['fv'].reshape(b * past, D_C), lw, b, past, tmp)
    o_c = _attention(qct, kcb.reshape(b, t, D_C), vct, kcp, vcpt, (cq, ck), eye, n_pairs=H_C // 2, shared=True,
                     frame_causal=True, tq=tqa, tk=min(tqa, ATTN_TK), tkp=min(tmp, ATTN_TK))

    prev = jnp.zeros((b, 1, B_IN), F32) if hist is None else hist['shift'].astype(F32)
    pb3 = pb.reshape(b, t, B_IN)
    rt, yl, gm, hm, r, k, v, g = _rwkv_chunk(pb3, prev, lw, _pick(t, (256, 128, 64)))
    npair = H_B // 2
    if hist is None:
        s0 = jnp.zeros((b, npair, LANES, LANES), F32)
    else:
        st = jnp.swapaxes(hist['wkv'].astype(F32), -1, -2).reshape(b, npair, 2, HEAD_DIM, HEAD_DIM)
        s0 = jnp.zeros((b, npair, 2, HEAD_DIM, 2, HEAD_DIM), F32)
        s0 = s0.at[:, :, 0, :, 0, :].set(st[:, :, 0]).at[:, :, 1, :, 1, :].set(st[:, :, 1])
        s0 = s0.reshape(b, npair, LANES, LANES)
    o_b, sfin = _rwkv_scan(s0, gm, hm, rt, yl, r, k, v, g, lw, _pick(t // CHUNK, (8, 4, 2, 1)))
    sf = sfin.reshape(b, npair, 2, HEAD_DIM, 2, HEAD_DIM)
    s_fin = jnp.stack([sf[:, :, 0, :, 0, :], sf[:, :, 1, :, 1, :]], axis=2).reshape(b, H_B, HEAD_DIM, HEAD_DIM)
    s_fin = jnp.swapaxes(s_fin, -1, -2)

    tmm = _pick(n, (1024, 512, 256, 128, 64))
    x_out = _outproj_moe(x2d, o_a.reshape(n, -1), o_b.reshape(n, D_B), o_c.reshape(n, -1), lw, tmm)

    return x_out.reshape(b, t, D_MODEL), stacks, (s_fin, pb3[:, -1:])


def _cache_outputs(stacks, b, t):
    ckv, kr, kc, vc, lf = stacks
    d = ckv.shape[0]
    return (ckv.reshape(d, b, t, KV_LORA), kr.reshape(d, b, t, ROPE), kc.reshape(d, b, t, H_C, HEAD_DIM),
            vc.reshape(d, b, t, H_C, HEAD_DIM), lf.reshape(d, b, t, H_C))


def kernel(x_prompt, x_sample, cache_mla_latent, cache_mla_krope, cache_fox_k, cache_fox_v, cache_fox_logf,
           state_rwkv_wkv, state_rwkv_shift, g_mix, w_in, mla_g_qa, mla_w_uq, mla_g_kva, mla_w_ukv, mla_g_qn,
           mla_g_kn, rw_mu, rw_w0, rw_w_up, rw_a0, rw_a_up, rw_g_up, rw_k_k, rw_k_a, rw_r_k, rw_ln_w, rw_ln_b,
           fox_g_qn, fox_g_kn, fox_b_f, w_out, g_ffn, moe_w_rg, moe_b_rg, moe_w_re, moe_b_re, moe_w_gate,
           moe_w_up, moe_w_down):
    params = dict(g_mix=g_mix, w_in=w_in, mla_g_qa=mla_g_qa, mla_w_uq=mla_w_uq, mla_g_kva=mla_g_kva,
                  mla_w_ukv=mla_w_ukv, mla_g_qn=mla_g_qn, mla_g_kn=mla_g_kn, rw_mu=rw_mu, rw_w0=rw_w0,
                  rw_w_up=rw_w_up, rw_a0=rw_a0, rw_a_up=rw_a_up, rw_g_up=rw_g_up, rw_k_k=rw_k_k, rw_k_a=rw_k_a,
                  rw_r_k=rw_r_k, rw_ln_w=rw_ln_w, rw_ln_b=rw_ln_b, fox_g_qn=fox_g_qn, fox_g_kn=fox_g_kn,
                  fox_b_f=fox_b_f, w_out=w_out, g_ffn=g_ffn, moe_w_rg=moe_w_rg, moe_b_rg=moe_b_rg,
                  moe_w_re=moe_w_re, moe_b_re=moe_b_re, moe_w_gate=moe_w_gate, moe_w_up=moe_w_up,
                  moe_w_down=moe_w_down)
    depth = g_mix.shape[0]
    yp, ys = x_prompt, x_sample
    p_stacks = s_stacks = None
    p_small, s_small = [], []
    for l in range(depth):
        lw = _layer_weights(params, l)
        hist = dict(ckv=cache_mla_latent[l], krope=cache_mla_krope[l], fk=cache_fox_k[l], fv=cache_fox_v[l],
                    flogf=cache_fox_logf[l], wkv=state_rwkv_wkv[l], shift=state_rwkv_shift[l])
        yp, p_stacks, sm = _layer(yp, lw, None, p_stacks)
        p_small.append(sm)
        ys, s_stacks, sm = _layer(ys, lw, hist, s_stacks)
        s_small.append(sm)
    p_out = _cache_outputs(p_stacks, *x_prompt.shape[:2]) + tuple(jnp.stack(t) for t in zip(*p_small))
    s_out = _cache_outputs(s_stacks, *x_sample.shape[:2]) + tuple(jnp.stack(t) for t in zip(*s_small))
    return (yp, ys) + p_out + s_out
```

```python
import functools
import math

import jax
import jax.numpy as jnp
from jax import lax
from jax.experimental import pallas as pl
from jax.experimental.pallas import tpu as pltpu

F32 = jnp.float32
BF16 = jnp.bfloat16

D_MODEL = 1024
HEAD_DIM = 64
H_A, H_B, H_C = 6, 6, 4
Q_LORA, KV_LORA, NOPE, ROPE = 192, 128, 64, 32
QK_DIM = NOPE + ROPE
ROPE_BASE = 10000.0
A_IN = Q_LORA + KV_LORA + ROPE
D_B = H_B * HEAD_DIM
W_LORA, A_LORA, G_LORA = 64, 64, 128
B_IN = 3 * D_B + W_LORA + A_LORA + G_LORA
DECAY_SCALE = math.exp(-0.5)
GN_EPS = 64e-5
D_C = H_C * HEAD_DIM
C_IN = 3 * D_C + H_C
N_GROUPS, E_PER_GROUP = 4, 4
N_EXPERTS = N_GROUPS * E_PER_GROUP
D_FF_E = 256
NEG_INF = -1e30
RMS_EPS = 1e-6
CHUNK = 64
LOG2E = math.log2(math.e)

LANES = 128
BF16_ROWS = 16
VMEM_LIMIT = 56 * 1024 * 1024
ATTN_TK = 512
ATTN_TQ = 1024
VT_ROWS = LANES + BF16_ROWS

_C_QL = 0
_C_KV = 256
_C_KR = 384
_C_PB = 512
_C_QC = _C_PB + B_IN
_C_KC = _C_QC + D_C
_C_VC = _C_KC + D_C
_C_F = _C_VC + D_C
_C_END = _C_F + LANES


def _cparams(sem):
    return pltpu.CompilerParams(dimension_semantics=sem, vmem_limit_bytes=VMEM_LIMIT)


def _dot(a, b):
    return jnp.dot(a, b, preferred_element_type=F32)


def _dot_nt(a, b):
    return lax.dot_general(a, b, (((1,), (1,)), ((), ())), preferred_element_type=F32)


def _dot_tn(a, b):
    return lax.dot_general(a, b, (((0,), (0,)), ((), ())), preferred_element_type=F32)


def _split2(x):
    hi = x.astype(BF16)
    lo = (x - hi.astype(F32)).astype(BF16)
    return hi, lo


def _split3(x):
    hi = x.astype(BF16)
    r = x - hi.astype(F32)
    mid = r.astype(BF16)
    lo = (r - mid.astype(F32)).astype(BF16)
    return hi, mid, lo


def _dot3(a, b, dot=_dot):
    ah, al = _split2(a)
    bh, bl = _split2(b)
    return dot(ah, bh) + (dot(ah, bl) + dot(al, bh))


def _mm(a, b, dot=_dot):
    return dot(a.astype(BF16), b.astype(BF16))


def _dot_sel(a, sel):
    return _dot(a.astype(BF16), sel)


def _iota(shape, dim):
    return lax.broadcasted_iota(jnp.int32, shape, dim)


def _rope(x, c, s_left, s_right):
    return x * c + pltpu.roll(x, 112, 1) * s_left + pltpu.roll(x, 16, 1) * s_right


def _sigmoid(x):
    return 1.0 / (1.0 + jnp.exp(-x))


def _log_sigmoid(x):
    return jnp.minimum(x, 0.0) - jnp.log(1.0 + jnp.exp(-jnp.abs(x)))


def _transpose_bf16(x, eye):
    return _dot_nt(eye, x.astype(BF16)).astype(BF16)


def _store_vt(vt_ref, pp, v_block, eye):
    base = pp * VT_ROWS
    vt_ref[0, base:base + LANES, :] = _transpose_bf16(v_block, eye)
    cols = vt_ref.shape[2]
    vt_ref[0, base + LANES:base + VT_ROWS, :] = jnp.where(_iota((BF16_ROWS, cols), 0) == 0, 1.0, 0.0).astype(BF16)


def _inproj_kernel(*refs, n_prev):
    it = iter(refs)
    (x_ref, gmix_ref, w_ref, gqa_ref, wuq_ref, gkva_ref, gqn_ref, gfq_ref, gfk_ref, bf_ref, ones_ref, bd_ref,
     eye_ref, wuk_ref, wuv_ref, gkn_ref, gkr_ref, c_ref, sl_ref, sr_ref) = (next(it) for _ in range(20))
    prev = [next(it) for _ in range(5)] if n_prev else []
    stacks = [next(it) for _ in range(5)]
    qt_ref, pb_ref, qct_ref, kcb_ref, vct_ref, km_ref, vtm_ref = (next(it) for _ in range(7))
    for s_ref, p_ref in zip(stacks, prev):
        s_ref[0:n_prev] = p_ref[...]
    ckv_ref, kr_ref, kc_ref, vc_ref, lf_ref = (s.at[n_prev] for s in stacks)

    x = x_ref[...]
    h = x * lax.rsqrt(jnp.mean(x * x, axis=-1, keepdims=True) + RMS_EPS) * gmix_ref[...]
    h = h.astype(BF16)
    eye = eye_ref[...]
    ones = ones_ref[...]
    bd = bd_ref[...]

    ql = _dot(h, w_ref[:, _C_QL:_C_QL + 256])
    kv = _dot(h, w_ref[:, _C_KV:_C_KV + KV_LORA])
    kr = _dot(h, w_ref[:, _C_KR:_C_KR + LANES])
    pb_ref[...] = _dot(h, w_ref[:, _C_PB:_C_PB + B_IN])
    qc = _dot(h, w_ref[:, _C_QC:_C_QC + D_C])
    kc = _dot(h, w_ref[:, _C_KC:_C_KC + D_C])
    vc = _dot(h, w_ref[:, _C_VC:_C_VC + D_C])
    f = _dot(h, w_ref[:, _C_F:_C_F + LANES]) + bf_ref[...]

    ql = ql * lax.rsqrt(jnp.sum(ql * ql, axis=-1, keepdims=True) * (1.0 / Q_LORA) + RMS_EPS) * gqa_ref[...]
    qh = _dot(ql.astype(BF16), wuq_ref[...])
    c, s_l, s_r = c_ref[...], sl_ref[...], sr_ref[...]
    qv = [qh[:, hh * LANES:(hh + 1) * LANES] for hh in range(H_A)]
    ss = [_dot_sel(q * q, ones) for q in qv]
    qn = [q * lax.rsqrt(s * (1.0 / QK_DIM) + RMS_EPS) * gqn_ref[...] for q, s in zip(qv, ss)]
    qr = [_rope(q, c, s_l, s_r) * (LOG2E * QK_DIM ** -0.5) for q in qn]
    for hh in range(H_A):
        qt_ref[0, hh * LANES:(hh + 1) * LANES, :] = _transpose_bf16(qr[hh], eye)

    ckv = kv * lax.rsqrt(jnp.mean(kv * kv, axis=-1, keepdims=True) + RMS_EPS) * gkva_ref[...]
    ckv_ref[...] = ckv
    kr_ref[...] = kr[:, NOPE:NOPE + ROPE]
    _mla_kv(ckv, kr, wuk_ref[...], wuv_ref[...], gkn_ref[...], gkr_ref[...], ones, eye, c, s_l, s_r, km_ref, vtm_ref)

    qc = qc * lax.rsqrt(_dot_sel(qc * qc, bd) * (1.0 / HEAD_DIM) + RMS_EPS) * gfq_ref[...]
    qc = qc * (LOG2E * HEAD_DIM ** -0.5)
    kc = kc * lax.rsqrt(_dot_sel(kc * kc, bd) * (1.0 / HEAD_DIM) + RMS_EPS) * gfk_ref[...]
    kc_ref[...] = kc
    kcb_ref[...] = kc.astype(BF16)
    vc_ref[...] = vc
    for pp in range(D_C // LANES):
        qct_ref[0, pp * LANES:(pp + 1) * LANES, :] = _transpose_bf16(qc[:, pp * LANES:(pp + 1) * LANES], eye)
        _store_vt(vct_ref, pp, vc[:, pp * LANES:(pp + 1) * LANES], eye)
    lf_ref[...] = _log_sigmoid(f)[:, :H_C]


_CACHE_WIDTHS = (KV_LORA, ROPE, D_C, D_C, H_C)


def _inproj(x2d, lw, tabs, bsz, t, tm, prev_stacks):
    n = x2d.shape[0]
    nt = t // tm
    n_prev = 0 if prev_stacks is None else prev_stacks[0].shape[0]
    row = lambda i: (i, 0)
    fixed = lambda i: (0, 0)
    tab = lambda i: (i % nt, 0)
    colmajor = lambda i: (i // nt, 0, i % nt)
    full = lambda a: pl.BlockSpec(a.shape, fixed)
    params = [lw['g_mix'], lw['w_in'], lw['g_qa'], lw['w_uq'], lw['g_kva'], lw['g_qn'], lw['g_fq'], lw['g_fk'],
              lw['b_f'], lw['ones128'], lw['bd256'], lw['eye128'], lw['w_uk'], lw['w_uv'], lw['g_kn'], lw['g_kr']]
    stack = lambda layers, w: pl.BlockSpec((layers, tm, w), lambda i: (0, i, 0))
    rowout = lambda w, dt: (pl.BlockSpec((tm, w), row), jax.ShapeDtypeStruct((n, w), dt))
    colout = lambda w: (pl.BlockSpec((1, w, tm), colmajor), jax.ShapeDtypeStruct((bsz, w, t), BF16))
    outs = [(stack(n_prev + 1, w), jax.ShapeDtypeStruct((n_prev + 1, n, w), F32)) for w in _CACHE_WIDTHS]
    outs += [colout(H_A * LANES), rowout(B_IN, F32), colout(D_C), rowout(D_C, BF16), colout(H_C // 2 * VT_ROWS),
             rowout(H_A * LANES, BF16), colout(H_A // 2 * VT_ROWS)]
    prev_args = [] if prev_stacks is None else list(prev_stacks)
    return pl.pallas_call(
        functools.partial(_inproj_kernel, n_prev=n_prev),
        grid=(n // tm,),
        in_specs=[pl.BlockSpec((tm, D_MODEL), row)] + [full(a) for a in params]
                 + [pl.BlockSpec((tm, LANES), tab)] * 3 + [stack(n_prev, w) for w in _CACHE_WIDTHS if n_prev],
        out_specs=[o[0] for o in outs],
        out_shape=[o[1] for o in outs],
        compiler_params=_cparams(("parallel",)),
        name="inproj",
    )(x2d, *params, *tabs, *prev_args)


def _mla_kv(ckv, kr, wuk, wuv, gkn, gkr, ones, eye, c, s_l, s_r, k_ref, vt_ref):
    cb = ckv.astype(BF16)
    kn = _dot(cb, wuk)
    v = _dot(cb, wuv)
    for pp in range(H_A // 2):
        _store_vt(vt_ref, pp, v[:, pp * LANES:(pp + 1) * LANES], eye)
    ssr = _dot_sel(kr * kr, ones)
    krg = _rope(kr * gkr, c, s_l, s_r)
    knh = [kn[:, hh * LANES:(hh + 1) * LANES] for hh in range(H_A)]
    ssn = [_dot_sel(k * k, ones) for k in knh]
    for hh in range(H_A):
        r = lax.rsqrt((ssn[hh] + ssr) * (1.0 / QK_DIM) + RMS_EPS)
        k_ref[:, hh * LANES:(hh + 1) * LANES] = ((knh[hh] * gkn + krg) * r).astype(BF16)


def _kvprep_kernel(ckv_ref, kr_ref, wuk_ref, wuv_ref, gkn_ref, gkr_ref, ones_ref, eye_ref, c_ref, sl_ref, sr_ref,
                   k_ref, vt_ref):
    _mla_kv(ckv_ref[...], kr_ref[...], wuk_ref[...], wuv_ref[...], gkn_ref[...], gkr_ref[...], ones_ref[...],
            eye_ref[...], c_ref[...], sl_ref[...], sr_ref[...], k_ref, vt_ref)


def _kvprep(ckv3d, kr128_3d, layer, lw, tabs, bsz, t, tm):
    n = ckv3d.shape[1]
    nt = t // tm
    row = lambda i: (i, 0)
    fixed = lambda i: (0, 0)
    tab = lambda i: (i % nt, 0)
    lrow = lambda i: (layer, i, 0)
    full = lambda a: pl.BlockSpec(a.shape, fixed)
    params = [lw['w_uk'], lw['w_uv'], lw['g_kn'], lw['g_kr'], lw['ones128'], lw['eye128']]
    vt_rows = H_A // 2 * VT_ROWS
    return pl.pallas_call(
        _kvprep_kernel,
        grid=(n // tm,),
        in_specs=[pl.BlockSpec((None, tm, KV_LORA), lrow), pl.BlockSpec((None, tm, LANES), lrow)]
                 + [full(a) for a in params] + [pl.BlockSpec((tm, LANES), tab)] * 3,
        out_specs=[pl.BlockSpec((tm, H_A * LANES), row),
                   pl.BlockSpec((1, vt_rows, tm), lambda i: (i // nt, 0, i % nt))],
        out_shape=[jax.ShapeDtypeStruct((n, H_A * LANES), BF16),
                   jax.ShapeDtypeStruct((bsz, vt_rows, t), BF16)],
        compiler_params=_cparams(("parallel",)),
        name="kvprep",
    )(ckv3d, kr128_3d, *params, *tabs)


def _transpose_kernel(x_ref, eye_ref, o_ref):
    eye = eye_ref[...]
    for pp in range(x_ref.shape[1] // LANES):
        _store_vt(o_ref, pp, x_ref[:, pp * LANES:(pp + 1) * LANES], eye)


def _transpose_cast(x3d, layer, lw, bsz, t, tm):
    _, n, w = x3d.shape
    nt = t // tm
    rows = w // LANES * VT_ROWS
    return pl.pallas_call(
        _transpose_kernel,
        grid=(n // tm,),
        in_specs=[pl.BlockSpec((None, tm, w), lambda i: (layer, i, 0)),
                  pl.BlockSpec((LANES, LANES), lambda i: (0, 0))],
        out_specs=pl.BlockSpec((1, rows, tm), lambda i: (i // nt, 0, i % nt)),
        out_shape=jax.ShapeDtypeStruct((bsz, rows, t), BF16),
        compiler_params=_cparams(("parallel",)),
        name="transpose_cast",
    )(x3d, lw['eye128'])


def _attn_kernel(*refs, tq, tk, tkp, nq, n_past, n_pairs, shared, frame_causal, has_bias, pipelined):
    it = iter(refs)
    q_ref, k_ref, vt_ref = next(it), next(it), next(it)
    if n_past:
        kp_ref, vpt_ref = next(it), next(it)
    if has_bias:
        cq_ref, ck_ref = next(it), next(it)
    eye_ref, o_ref = next(it), next(it)
    q_scr, m_scr, acc_scr = next(it), next(it), next(it)

    qi = 0 if nq == 1 else pl.program_id(1)
    low = _iota((LANES, tq), 0) < HEAD_DIM
    blk = lambda p, j: p if shared else 2 * p + j
    heads = [(p, j) for p in range(n_pairs) for j in range(2)]
    nh = len(heads)
    for p, j in heads:
        q = q_ref[0, blk(p, j) * LANES:(blk(p, j) + 1) * LANES, :]
        if shared:
            keep = low if j == 0 else jnp.logical_not(low)
            q = jnp.where(keep, q, jnp.zeros_like(q))
        q_scr[2 * p + j] = q
    m_scr[...] = jnp.full(m_scr.shape, -jnp.inf, F32)
    acc_scr[...] = jnp.zeros(acc_scr.shape, F32)

    def score(kr, st, width, p, j):
        kt = kr[0, pl.ds(st, width), blk(p, j) * LANES:(blk(p, j) + 1) * LANES].astype(BF16)
        return _dot(kt, q_scr[2 * p + j])

    def scores_to(buf, kr, st, width):
        for p, j in heads:
            buf[2 * p + j] = score(kr, st, width, p, j)

    def consume(get_s, vr, st, width, key_start, mask):
        ss = []
        for p, j in heads:
            s = get_s(p, j)
            if has_bias:
                s = s - ck_ref[0, pl.ds(key_start, width), 2 * p + j:2 * p + j + 1]
            if mask is not None:
                s = jnp.where(mask, s, NEG_INF)
            ss.append(s)
        cqs = [cq_ref[0, p, j:j + 1, :] if has_bias else 0.0 for p, j in heads]
        m_prev = [m_scr[h] for h in range(nh)]
        m_next = [jnp.maximum(m_prev[h], jnp.max(ss[h], axis=0, keepdims=True) + cqs[h]) for h in range(nh)]
        prs = [jnp.exp2(ss[h] - (m_next[h] - cqs[h])).astype(BF16) for h in range(nh)]
        alphas = [jnp.exp2(m_prev[h] - m_next[h]) for h in range(nh)]
        pvs = [_dot(vr[0, p * VT_ROWS:(p + 1) * VT_ROWS, pl.ds(st, width)], prs[2 * p + j]) for p, j in heads]
        for h in range(nh):
            acc_scr[h] = acc_scr[h] * alphas[h] + pvs[h]
            m_scr[h] = m_next[h]

    def step(kr, vr, st, width, key_start, mask):
        consume(lambda p, j: score(kr, st, width, p, j), vr, st, width, key_start, mask)

    kidx = _iota((tk, tq), 0)
    qidx = _iota((tk, tq), 1)
    diag_mask = lambda d: ((kidx + d * tk <= qidx) if frame_causal else
                           (jnp.right_shift(kidx + d * tk, 6) <= jnp.right_shift(qidx, 6)))

    if pipelined:
        sa, sb = next(it), next(it)
        tile = lambda t: pl.multiple_of(t * tk, tk)
        from_a = lambda p, j: sa[2 * p + j]
        from_b = lambda p, j: sb[2 * p + j]
        scores_to(sa, k_ref, tile(0), tk)

        def pair_body(u, carry):
            t0 = 2 * u
            scores_to(sb, k_ref, tile(t0 + 1), tk)
            consume(from_a, vt_ref, tile(t0), tk, tile(t0), None)
            scores_to(sa, k_ref, tile(t0 + 2), tk)
            consume(from_b, vt_ref, tile(t0 + 1), tk, tile(t0 + 1), None)
            return carry
        if tq == 2 * tk:
            first = 2 * qi
            lax.fori_loop(0, qi, pair_body, 0)
            scores_to(sb, k_ref, tile(first + 1), tk)
            consume(from_a, vt_ref, tile(first), tk, tile(first), diag_mask(0))
            consume(from_b, vt_ref, tile(first + 1), tk, tile(first + 1), diag_mask(1))
        else:
            lax.fori_loop(0, qi // 2, pair_body, 0)
            odd = lax.rem(qi, 2) == 1

            @pl.when(jnp.logical_not(odd))
            def _():
                consume(from_a, vt_ref, tile(qi), tk, tile(qi), diag_mask(0))

            @pl.when(odd)
            def _():
                scores_to(sb, k_ref, tile(qi), tk)
                consume(from_a, vt_ref, tile(qi - 1), tk, tile(qi - 1), None)
                consume(from_b, vt_ref, tile(qi), tk, tile(qi), diag_mask(0))
    else:
        if n_past:
            def past_body(t, carry):
                st = pl.multiple_of(t * tkp, tkp)
                step(kp_ref, vpt_ref, st, tkp, st, None)
                return carry
            lax.fori_loop(0, n_past // tkp, past_body, 0)

        def new_body(t, carry):
            st = pl.multiple_of(t * tk, tk)
            step(k_ref, vt_ref, st, tk, n_past + st, None)
            return carry
        if nq > 1:
            lax.fori_loop(0, qi * (tq // tk), new_body, 0)

        for d in range(tq // tk):
            st = qi * tq + d * tk
            st = st if nq == 1 else pl.multiple_of(st, tk)
            step(k_ref, vt_ref, st, tk, n_past + st, diag_mask(d))

    for p in range(n_pairs):
        a, b = acc_scr[2 * p], acc_scr[2 * p + 1]
        ot = jnp.where(low, a[:LANES] / a[LANES:LANES + 1], b[:LANES] / b[LANES:LANES + 1])
        o_ref[0, :, p * LANES:(p + 1) * LANES] = _dot_tn(ot.astype(BF16), eye_ref[...]).astype(o_ref.dtype)


def _attention(qt, k_new, vt_new, k_past, vt_past, bias, eye, *, n_pairs, shared, frame_causal, tq, tk, tkp,
               past_layer=0):
    b, t = k_new.shape[0], k_new.shape[1]
    n_past = 0 if k_past is None else k_past.shape[-2]
    nq = t // tq
    wq = qt.shape[1]
    args = [qt, k_new, vt_new]
    specs = [pl.BlockSpec((1, wq, tq), lambda bi, i: (bi, 0, i)),
             pl.BlockSpec((1, t, wq), lambda bi, i: (bi, 0, 0)),
             pl.BlockSpec((1, n_pairs * VT_ROWS, t), lambda bi, i: (bi, 0, 0))]
    if n_past:
        args += [k_past, vt_past]
        if k_past.ndim == 4:
            kspec = pl.BlockSpec((None, 1, n_past, wq), lambda bi, i: (past_layer, bi, 0, 0))
        else:
            kspec = pl.BlockSpec((1, n_past, wq), lambda bi, i: (bi, 0, 0))
        specs += [kspec, pl.BlockSpec((1, n_pairs * VT_ROWS, n_past), lambda bi, i: (bi, 0, 0))]
    if bias is not None:
        cq, ck = bias
        args += [cq, ck]
        specs += [pl.BlockSpec((1, n_pairs, 2, tq), lambda bi, i: (bi, 0, 0, i)),
                  pl.BlockSpec((1,) + ck.shape[1:], lambda bi, i: (bi, 0, 0))]
    args.append(eye)
    specs.append(pl.BlockSpec((LANES, LANES), lambda bi, i: (0, 0)))
    pipelined = n_past == 0 and nq > 1 and tq in (tk, 2 * tk)
    kern = functools.partial(_attn_kernel, tq=tq, tk=tk, tkp=tkp, nq=nq, n_past=n_past, n_pairs=n_pairs,
                             shared=shared, frame_causal=frame_causal, has_bias=bias is not None,
                             pipelined=pipelined)
    nh = 2 * n_pairs
    score_bufs = [pltpu.VMEM((nh, tk, tq), F32)] * 2 if pipelined else []
    return pl.pallas_call(
        kern,
        grid=(b, nq),
        in_specs=specs,
        out_specs=pl.BlockSpec((1, tq, n_pairs * LANES), lambda bi, i: (bi, i, 0)),
        out_shape=jax.ShapeDtypeStruct((b, t, n_pairs * LANES), BF16),
        scratch_shapes=[pltpu.VMEM((nh, LANES, tq), BF16), pltpu.VMEM((nh, 1, tq), F32),
                        pltpu.VMEM((nh, VT_ROWS, tq), F32)] + score_bufs,
        compiler_params=_cparams(("parallel", "arbitrary")),
        name="attn_fox" if shared else "attn_mla",
    )(*args)


def _cumsum_kernel(x_ref, tri_ref, o_ref):
    rows, n = x_ref.shape
    tri = tri_ref[...]
    carry = jnp.zeros((rows, 1), F32)
    for t in range(n // LANES):
        xt = x_ref[:, t * LANES:(t + 1) * LANES]
        hi, mid, lo = _split3(xt)
        o_ref[:, t * LANES:(t + 1) * LANES] = (_dot(hi, tri) + _dot(mid, tri) + _dot(lo, tri) + carry) * LOG2E
        carry = carry + jnp.sum(xt, axis=1, keepdims=True)


def _cumsum_lanes(x):
    rows, n = x.shape
    tri = (jnp.arange(LANES)[:, None] <= jnp.arange(LANES)[None, :]).astype(BF16)
    return pl.pallas_call(
        _cumsum_kernel,
        out_shape=jax.ShapeDtypeStruct((rows, n), F32),
        compiler_params=pltpu.CompilerParams(vmem_limit_bytes=VMEM_LIMIT),
        name="cumsum",
    )(x, tri)


def _rwkv_tokens(pbv, prev_row, mu, w0, ww, a0, wa, wg, k_k, k_a, bd):
    rolled = pltpu.roll(pbv, 1, 0)
    shifted = jnp.where(_iota(pbv.shape, 0) == 0, prev_row, rolled)
    xs = pbv + (shifted - pbv) * mu
    r = xs[:, 0:D_B]
    kb = xs[:, D_B:2 * D_B]
    o3 = 3 * D_B
    wa_in = xs[:, o3:o3 + LANES]
    wa_in = jnp.where(_iota(wa_in.shape, 1) < W_LORA, jnp.tanh(wa_in), wa_in).astype(BF16)
    lw = -DECAY_SCALE * _sigmoid(w0 + _dot(wa_in, ww))
    a = _sigmoid(a0 + _dot(wa_in, wa))
    gate = _dot(_sigmoid(xs[:, o3 + LANES:o3 + 2 * LANES]).astype(BF16), wg)
    kk = kb * k_k
    kk = kk * lax.rsqrt(_dot_sel(kk * kk, bd) + 1e-12)
    return r, kk, kb * (1.0 + (a - 1.0) * k_a), kk * a, xs[:, 2 * D_B:3 * D_B], lw, gate


def _rwkv_chunk_kernel(pb_ref, pbprev_ref, prev_ref, mu_ref, w0_ref, ww_ref, a0_ref, wa_ref, wg_ref, kk_ref, ka_ref,
                       bd_ref, rt_ref, yl_ref, g_ref, h_ref, ro_ref, ko_ref, vo_ref, go_ref):
    s = pb_ref.shape[1]
    nc = s // CHUNK
    prev_row = jnp.where(pl.program_id(1) == 0, prev_ref[0], pbprev_ref[0, 7:8, :])
    r_all, kk_all, k_all, b_all, v_all, lw_all, gate = _rwkv_tokens(
        pb_ref[0], prev_row, mu_ref[...], w0_ref[...], ww_ref[...], a0_ref[...], wa_ref[...], wg_ref[...],
        kk_ref[...], ka_ref[...], bd_ref[...])
    ro_ref[0] = r_all.astype(BF16)
    ko_ref[0] = k_all.astype(BF16)
    vo_ref[0] = v_all.astype(BF16)
    go_ref[0] = gate.astype(BF16)
    row = _iota((s, LANES), 0)
    rin = jnp.bitwise_and(row, CHUNK - 1)
    ti = _iota((s, s), 0)
    si = _iota((s, s), 1)
    same = jnp.right_shift(ti, 6) == jnp.right_shift(si, 6)
    strict = jnp.logical_and(same, si < ti)
    incl = jnp.logical_and(same, si <= ti)
    eye = (ti == si).astype(F32)
    low = _iota((s, LANES), 1) < HEAD_DIM
    ji = _iota((LANES, LANES), 0)
    jj = _iota((LANES, LANES), 1)
    blockdiag = (ji < HEAD_DIM) == (jj < HEAD_DIM)
    npair = D_B // LANES
    pairs = range(npair)
    heads = [(p, x) for p in pairs for x in range(2)]

    tot, kkt, rt, khbh, kw, bw, v = [], [], [], [], [], [], []
    for p in pairs:
        sl = slice(p * LANES, (p + 1) * LANES)
        lw, kp, bp = lw_all[:, sl], k_all[:, sl], b_all[:, sl]
        cl = lw
        for sh in (1, 2, 4, 8, 16, 32):
            cl = cl + jnp.where(rin >= sh, pltpu.roll(cl, sh, 0), 0.0)
        tp = jnp.concatenate(
            [jnp.broadcast_to(cl[c * CHUNK + CHUNK - 1:(c + 1) * CHUNK, :], (CHUNK, LANES)) for c in range(nc)],
            axis=0)
        e_ncl = jnp.exp(-cl)
        e_rem = jnp.exp(tp - cl)
        tot.append(tp)
        kkt.append(kk_all[:, sl] * jnp.exp(cl - lw))
        rt.append(r_all[:, sl] * jnp.exp(cl))
        khbh.append(jnp.concatenate([kp * e_ncl, bp * e_ncl], axis=0).astype(BF16))
        kw.append(kp * e_rem)
        bw.append(bp * e_rem)
        v.append(v_all[:, sl])

    keep = lambda x: low if x == 0 else jnp.logical_not(low)
    kkt_x = [jnp.where(keep(x), kkt[p], 0.0) for p, x in heads]
    rt_x = [jnp.where(keep(x), rt[p], 0.0) for p, x in heads]
    akk, ark, arb, pw, tinv = [], [], [], [], []
    for i, (p, x) in enumerate(heads):
        p4 = _dot_nt(jnp.concatenate([kkt_x[i], rt_x[i]], axis=0).astype(BF16), khbh[p])
        akk.append(jnp.where(strict, p4[:s, :s], 0.0))
        ark.append(jnp.where(incl, p4[s:, :s], 0.0))
        arb.append(jnp.where(incl, p4[s:, s:], 0.0))
        pw.append(jnp.where(strict, -p4[:s, s:], 0.0))
        tinv.append(eye + pw[i])
    for _ in range(5):
        pw = [_mm(m, m) for m in pw]
        tinv = [t + _mm(t, m) for t, m in zip(tinv, pw)]
    av = [_mm(jnp.concatenate([akk[i], ark[i]], axis=0), v[p]) for i, (p, x) in enumerate(heads)]
    tx = [_mm(tinv[i], jnp.concatenate([av[i][:s], kkt_x[i]], axis=1)) for i in range(len(heads))]
    ax = [_mm(arb[i], tx[i]) for i in range(len(heads))]

    for p in pairs:
        sl = slice(p * LANES, (p + 1) * LANES)
        a, b2 = 2 * p, 2 * p + 1
        uloc2 = jnp.where(low, tx[a][:, :LANES], tx[b2][:, :LANES])
        kkt2 = jnp.where(low, tx[a][:, LANES:], tx[b2][:, LANES:])
        rt_ref[0, :, sl] = jnp.where(low, rt_x[a] - ax[a][:, LANES:], rt_x[b2] - ax[b2][:, LANES:]).astype(BF16)
        yl_ref[0, :, sl] = jnp.where(low, av[a][s:] - ax[a][:, :LANES], av[b2][s:] - ax[b2][:, :LANES]).astype(BF16)
        for c in range(nc):
            inc = jnp.right_shift(row, 6) == c
            bw_c = jnp.where(inc, bw[p], 0.0)
            kw_c = jnp.where(inc, kw[p], 0.0)
            e_tot = jnp.exp(tot[p][c * CHUNK:c * CHUNK + 1, :])
            gm = jnp.where(ji == jj, e_tot, 0.0) - _mm(bw_c, kkt2, _dot_tn)
            hm = _mm(kw_c, v[p], _dot_tn) - _mm(bw_c, uloc2, _dot_tn)
            g_ref[0, p, c] = jnp.where(blockdiag, gm, 0.0).astype(BF16)
            h_ref[0, p, c] = jnp.where(blockdiag, hm, 0.0).astype(BF16)


def _rwkv_chunk(pb3, prev, lw, s):
    bsz, t, _ = pb3.shape
    npair = H_B // 2
    nc = t // CHUNK
    sb = s // 8
    tok = pl.BlockSpec((1, s, D_B), lambda bi, i: (bi, i, 0))
    mat = pl.BlockSpec((1, npair, s // CHUNK, LANES, LANES), lambda bi, i: (bi, 0, i, 0, 0))
    params = [lw['rw_mu'], lw['rw_w0'], lw['rw_ww'], lw['rw_a0'], lw['rw_wa'], lw['rw_wg'], lw['rw_k_k'],
              lw['rw_k_a'], lw['bd384']]
    full = lambda a: pl.BlockSpec(a.shape, lambda bi, i: (0, 0))
    return pl.pallas_call(
        _rwkv_chunk_kernel,
        grid=(bsz, t // s),
        in_specs=[pl.BlockSpec((1, s, B_IN), lambda bi, i: (bi, i, 0)),
                  pl.BlockSpec((1, 8, B_IN), lambda bi, i: (bi, jnp.maximum(i * sb - 1, 0), 0)),
                  pl.BlockSpec((1, 1, B_IN), lambda bi, i: (bi, 0, 0))] + [full(a) for a in params],
        out_specs=[tok, tok, mat, mat] + [tok] * 4,
        out_shape=[jax.ShapeDtypeStruct((bsz, t, D_B), BF16)] * 2
                  + [jax.ShapeDtypeStruct((bsz, npair, nc, LANES, LANES), BF16)] * 2
                  + [jax.ShapeDtypeStruct((bsz, t, D_B), BF16)] * 4,
        compiler_params=_cparams(("parallel", "parallel")),
        name="rwkv_chunk",
    )(pb3, pb3, prev, *params)


def _rwkv_scan_kernel(s0_ref, g_ref, h_ref, rt_ref, yl_ref, r_ref, k_ref, v_ref, gate_ref,
                      lnw_ref, lnb_ref, rk_ref, bd_ref, o_ref, sfin_ref, st_scr, y_scr):
    npair, ncb = g_ref.shape[1], g_ref.shape[2]

    @pl.when(pl.program_id(1) == 0)
    def _():
        st_scr[...] = s0_ref[0]

    def times_state(a, st):
        hi, lo = _split2(st)
        return _dot(a, hi) + _dot(a, lo)

    sts = [st_scr[p] for p in range(npair)]
    for c in range(ncb):
        rows = slice(c * CHUNK, (c + 1) * CHUNK)
        for p in range(npair):
            sl = slice(p * LANES, (p + 1) * LANES)
            y_scr[rows, sl] = times_state(rt_ref[0, rows, sl], sts[p]) + yl_ref[0, rows, sl].astype(F32)
            sts[p] = times_state(g_ref[0, p, c], sts[p]) + h_ref[0, p, c].astype(F32)
    for p in range(npair):
        st_scr[p] = sts[p]
        sfin_ref[0, p] = sts[p]

    bd = bd_ref[...]
    y = y_scr[...]
    mu = _dot_sel(y, bd) * (1.0 / HEAD_DIM)
    d = y - mu
    var = _dot_sel(d * d, bd) * (1.0 / HEAD_DIM)
    yn = d * lax.rsqrt(var + GN_EPS) * lnw_ref[...] + lnb_ref[...]
    rk = r_ref[0].astype(F32) * k_ref[0].astype(F32) * rk_ref[...]
    bonus = _dot_sel(rk, bd) * v_ref[0].astype(F32)
    o_ref[0] = ((yn + bonus) * gate_ref[0].astype(F32)).astype(BF16)


def _rwkv_scan(s0, g, h, rt, yl, r, k, v, gate, lw, ncb):
    bsz, t, _ = rt.shape
    npair = H_B // 2
    nc = t // CHUNK
    st_spec = pl.BlockSpec((1, npair, LANES, LANES), lambda bi, i: (bi, 0, 0, 0))
    mat = pl.BlockSpec((1, npair, ncb, LANES, LANES), lambda bi, i: (bi, 0, i, 0, 0))
    tok = pl.BlockSpec((1, ncb * CHUNK, D_B), lambda bi, i: (bi, i, 0))
    params = [lw['rw_ln_w'], lw['rw_ln_b'], lw['rw_r_k'], lw['bd384']]
    full = lambda a: pl.BlockSpec(a.shape, lambda bi, i: (0, 0))
    return pl.pallas_call(
        _rwkv_scan_kernel,
        grid=(bsz, nc // ncb),
        in_specs=[st_spec, mat, mat] + [tok] * 6 + [full(a) for a in params],
        out_specs=[tok, st_spec],
        out_shape=[jax.ShapeDtypeStruct((bsz, t, D_B), BF16), jax.ShapeDtypeStruct((bsz, npair, LANES, LANES), F32)],
        scratch_shapes=[pltpu.VMEM((npair, LANES, LANES), F32), pltpu.VMEM((ncb * CHUNK, D_B), F32)],
        compiler_params=_cparams(("parallel", "arbitrary")),
        name="rwkv_scan",
    )(s0, g, h, rt, yl, r, k, v, gate, *params)


_L_EXP = 16


def _moe_kernel(x_ref, oa_ref, ob_ref, oc_ref, wo_ref, gffn_ref, wr_ref, br_ref, wg_ref, wu_ref, wd_ref,
                o_ref, acc_ref, h_ref, comb_ref):
    e = pl.program_id(1)

    @pl.when(e == 0)
    def _():
        da = H_A * HEAD_DIM
        x1 = (x_ref[...] + _dot(oa_ref[...], wo_ref[0:da, :]) + _dot(ob_ref[...], wo_ref[da:da + D_B, :])
              + _dot(oc_ref[...], wo_ref[da + D_B:, :]))
        acc_ref[...] = x1
        hf = x1 * lax.rsqrt(jnp.mean(x1 * x1, axis=-1, keepdims=True) + RMS_EPS) * gffn_ref[...]
        h_ref[...] = hf.astype(BF16)

        logit = _dot3(hf, wr_ref[...]) + br_ref[...]
        lane_i = _iota(logit.shape, 1)
        lane = lane_i.astype(F32)
        big = jnp.float32(3e38)
        is_g = lane_i < N_GROUPS
        gl = jnp.where(is_g, logit, -big)
        gmax = jnp.max(gl, axis=1, keepdims=True)
        pg_top = 1.0 / jnp.sum(jnp.where(is_g, jnp.exp(gl - gmax), 0.0), axis=1, keepdims=True)
        g_idx = jnp.min(jnp.where(jnp.logical_and(is_g, gl == gmax), lane, big), axis=1, keepdims=True)
        el = lane_i - _L_EXP
        in_e = jnp.logical_and(el >= 0, el < N_EXPERTS)
        sel = jnp.logical_and(in_e, jnp.right_shift(el, 2).astype(F32) == g_idx)
        l1 = jnp.where(sel, logit, -big)
        v1 = jnp.max(l1, axis=1, keepdims=True)
        i1 = jnp.min(jnp.where(jnp.logical_and(sel, l1 == v1), lane, big), axis=1, keepdims=True)
        sel2 = jnp.logical_and(sel, lane != i1)
        l2 = jnp.where(sel2, logit, -big)
        v2 = jnp.max(l2, axis=1, keepdims=True)
        i2 = jnp.min(jnp.where(jnp.logical_and(sel2, l2 == v2), lane, big), axis=1, keepdims=True)
        e2 = jnp.exp(v2 - v1)
        den = 1.0 / (1.0 + e2)
        comb_ref[...] = (jnp.where(lane == i1, den * pg_top, 0.0) + jnp.where(lane == i2, e2 * den * pg_top, 0.0))

    h = h_ref[...]
    comb = comb_ref[...]
    lane_c = _iota(comb.shape, 1)
    gates = [_dot(h, wg_ref[j]) for j in range(E_PER_GROUP)]
    ups = [_dot(h, wu_ref[j]) for j in range(E_PER_GROUP)]
    acts = []
    for j in range(E_PER_GROUP):
        ce = jnp.sum(jnp.where(lane_c == e * E_PER_GROUP + j + _L_EXP, comb, 0.0), axis=1, keepdims=True)
        acts.append((gates[j] * _sigmoid(gates[j]) * ups[j] * ce).astype(BF16))
    acc_ref[...] += _dot(jnp.concatenate(acts, axis=1), wd_ref[...])

    @pl.when(e == N_GROUPS - 1)
    def _():
        o_ref[...] = acc_ref[...]


def _outproj_moe(x2d, oa, ob, oc, lw, tm):
    n = x2d.shape[0]
    row = lambda w: pl.BlockSpec((tm, w), lambda i, e: (i, 0))
    full = lambda a: pl.BlockSpec(a.shape, lambda i, e: (0, 0))
    expert_in = pl.BlockSpec((E_PER_GROUP, D_MODEL, D_FF_E), lambda i, e: (e, 0, 0))
    return pl.pallas_call(
        _moe_kernel,
        grid=(n // tm, N_GROUPS),
        in_specs=[row(D_MODEL), row(H_A * HEAD_DIM), row(D_B), row(D_C), full(lw['w_out']), full(lw['g_ffn']),
                  full(lw['w_r']), full(lw['b_r']), expert_in, expert_in,
                  pl.BlockSpec((E_PER_GROUP * D_FF_E, D_MODEL), lambda i, e: (e, 0))],
        out_specs=row(D_MODEL),
        out_shape=jax.ShapeDtypeStruct((n, D_MODEL), F32),
        scratch_shapes=[pltpu.VMEM((tm, D_MODEL), F32), pltpu.VMEM((tm, D_MODEL), BF16), pltpu.VMEM((tm, LANES), F32)],
        compiler_params=_cparams(("parallel", "arbitrary")),
        name="outproj_moe",
    )(x2d, oa, ob, oc, lw['w_out'], lw['g_ffn'], lw['w_r'], lw['b_r'], lw['w_g'], lw['w_u'], lw['w_d'])


def _place(pieces, width):
    rows = pieces[0][1].shape[0]
    cols, at = [], 0
    for off, a in pieces:
        if off > at:
            cols.append(jnp.zeros((rows, off - at), F32))
        cols.append(a.astype(F32))
        at = off + a.shape[1]
    if width > at:
        cols.append(jnp.zeros((rows, width - at), F32))
    return jnp.concatenate(cols, axis=1)


def _row(v, width=None, off=0):
    v = v.reshape(1, -1).astype(F32)
    return v if width is None else _place([(off, v)], width)


def _block_diag_ones(n, blk):
    i = jnp.arange(n) // blk
    return (i[:, None] == i[None, :]).astype(BF16)


def _layer_weights(p, l):
    g = lambda name: p[name][l]
    w_in = g('w_in')
    o_b, o_c = A_IN, A_IN + B_IN
    w_in_p = _place([(_C_QL, w_in[:, :Q_LORA]), (_C_KV, w_in[:, Q_LORA:Q_LORA + KV_LORA]),
                     (_C_KR + NOPE, w_in[:, Q_LORA + KV_LORA:A_IN]), (_C_PB, w_in[:, o_b:o_c + 3 * D_C]),
                     (_C_F, w_in[:, o_c + 3 * D_C:])], _C_END).astype(BF16)
    w_uq = g('mla_w_uq').reshape(Q_LORA, H_A, QK_DIM)
    w_uq = jnp.pad(w_uq, ((0, 256 - Q_LORA), (0, 0), (0, LANES - QK_DIM))).reshape(256, H_A * LANES).astype(BF16)
    w_ukv = g('mla_w_ukv').reshape(KV_LORA, H_A, NOPE + HEAD_DIM)
    w_uk = jnp.pad(w_ukv[:, :, :NOPE], ((0, 0), (0, 0), (0, LANES - NOPE))).reshape(KV_LORA, H_A * LANES).astype(BF16)
    w_uv = w_ukv[:, :, NOPE:].reshape(KV_LORA, H_A * HEAD_DIM).astype(BF16)
    zeros_w = jnp.zeros((W_LORA, D_B), F32)
    w_r = _place([(0, g('moe_w_rg')), (_L_EXP, g('moe_w_re'))], LANES)
    b_r = _place([(0, g('moe_b_rg').reshape(1, -1)), (_L_EXP, g('moe_b_re').reshape(1, -1))], LANES)
    return dict(
        g_mix=_row(g('g_mix')), w_in=w_in_p, g_qa=_row(g('mla_g_qa'), 256), w_uq=w_uq, g_kva=_row(g('mla_g_kva')),
        g_qn=_row(g('mla_g_qn'), LANES), g_fq=_row(jnp.tile(g('fox_g_qn'), H_C)),
        g_fk=_row(jnp.tile(g('fox_g_kn'), H_C)), b_f=_row(g('fox_b_f'), LANES),
        ones128=jnp.ones((LANES, LANES), BF16), eye128=jnp.eye(LANES, dtype=BF16),
        bd256=_block_diag_ones(D_C, HEAD_DIM), bd384=_block_diag_ones(D_B, HEAD_DIM),
        w_uk=w_uk, w_uv=w_uv, g_kn=_row(g('mla_g_kn')[:NOPE], LANES), g_kr=_row(g('mla_g_kn')[NOPE:], LANES, NOPE),
        rw_mu=_row(g('rw_mu')), rw_w0=_row(g('rw_w0')), rw_a0=_row(g('rw_a0')),
        rw_ww=jnp.concatenate([g('rw_w_up'), zeros_w], axis=0).astype(BF16),
        rw_wa=jnp.concatenate([zeros_w, g('rw_a_up')], axis=0).astype(BF16),
        rw_wg=g('rw_g_up').astype(BF16), rw_k_k=_row(g('rw_k_k')), rw_k_a=_row(g('rw_k_a')),
        rw_r_k=_row(g('rw_r_k')), rw_ln_w=_row(g('rw_ln_w')), rw_ln_b=_row(g('rw_ln_b')),
        w_out=g('w_out').astype(BF16), g_ffn=_row(g('g_ffn')), w_r=w_r, b_r=b_r,
        w_g=g('moe_w_gate').astype(BF16), w_u=g('moe_w_up').astype(BF16),
        w_d=g('moe_w_down').astype(BF16).reshape(N_EXPERTS * D_FF_E, D_MODEL),
    )


def _rope_tables(pos):
    half = ROPE // 2
    inv = ROPE_BASE ** (-jnp.arange(half, dtype=F32) / half)
    ang = pos.astype(F32)[:, None] * inv[None, :]
    cos, sin = jnp.cos(ang), jnp.sin(ang)
    t = pos.shape[0]
    z = lambda w: jnp.zeros((t, w), F32)
    c = jnp.concatenate([jnp.ones((t, NOPE), F32), cos, cos, z(LANES - QK_DIM)], axis=1)
    s_left = jnp.concatenate([z(NOPE), -sin, z(half), z(LANES - QK_DIM)], axis=1)
    s_right = jnp.concatenate([z(NOPE), z(half), sin, z(LANES - QK_DIM)], axis=1)
    return c, s_left, s_right


def _pick(n, prefs):
    for t in prefs:
        if n % t == 0:
            return t
    return n


def _layer(x, lw, hist, stacks):
    b, t, _ = x.shape
    n = b * t
    past = 0 if hist is None else hist['past']
    layer = 0 if hist is None else hist['layer']
    x2d = x.reshape(n, D_MODEL)
    tm = _pick(t, (512, 256, 128, 64))
    tmp = _pick(past, (512, 256, 128, 64)) if past else 0
    tqa = _pick(t, (ATTN_TQ, 512, 256, 128, 64))
    eye = lw['eye128']

    q_tabs = _rope_tables(past + jnp.arange(t))
    outs = _inproj(x2d, lw, q_tabs, b, t, tm, stacks)
    stacks, (qt, pb, qct, kcb, vct, k_new, vt_new) = outs[:5], outs[5:]
    lf = stacks[4][-1]

    if hist is None:
        kp = vpt = None
    else:
        kp, vpt = _kvprep(hist['ckv'], hist['krope'], layer, lw, _rope_tables(jnp.arange(past)), b, past, tmp)
        kp = kp.reshape(b, past, -1)
    o_a = _attention(qt, k_new.reshape(b, t, -1), vt_new, kp, vpt, None, eye, n_pairs=H_A // 2, shared=False,
                     frame_causal=False, tq=tm, tk=min(tm, ATTN_TK), tkp=min(tmp, ATTN_TK))

    lf3 = lf.reshape(b, t, H_C)
    lf_all = lf3 if hist is None else jnp.concatenate([hist['flogf'].astype(F32), lf3], axis=1)
    ltot = past + t
    lpad = -(-ltot // LANES) * LANES
    lf_t = jnp.pad(jnp.swapaxes(lf_all, 1, 2), ((0, 0), (0, 0), (0, lpad - ltot))).reshape(b * H_C, lpad)
    c_all = _cumsum_lanes(lf_t).reshape(b, H_C // 2, 2, lpad)
    cq = c_all[..., past:past + t]
    ck = jnp.swapaxes(c_all.reshape(b, H_C, lpad), 1, 2)
    if hist is None:
        kcp = vcpt = None
    else:
        kcp = hist['fk']
        vcpt = _transpose_cast(hist['fv'], layer, lw, b, past, tmp)
    o_c = _attention(qct, kcb.reshape(b, t, D_C), vct, kcp, vcpt, (cq, ck), eye, n_pairs=H_C // 2, shared=True,
                     frame_causal=True, tq=tqa, tk=min(tqa, ATTN_TK), tkp=min(tmp, ATTN_TK), past_layer=layer)

    prev = jnp.zeros((b, 1, B_IN), F32) if hist is None else hist['shift'].astype(F32)
    pb3 = pb.reshape(b, t, B_IN)
    rt, yl, gm, hm, r, k, v, g = _rwkv_chunk(pb3, prev, lw, _pick(t, (256, 128, 64)))
    npair = H_B // 2
    if hist is None:
        s0 = jnp.zeros((b, npair, LANES, LANES), F32)
    else:
        st = jnp.swapaxes(hist['wkv'].astype(F32), -1, -2).reshape(b, npair, 2, HEAD_DIM, HEAD_DIM)
        s0 = jnp.zeros((b, npair, 2, HEAD_DIM, 2, HEAD_DIM), F32)
        s0 = s0.at[:, :, 0, :, 0, :].set(st[:, :, 0]).at[:, :, 1, :, 1, :].set(st[:, :, 1])
        s0 = s0.reshape(b, npair, LANES, LANES)
    o_b, sfin = _rwkv_scan(s0, gm, hm, rt, yl, r, k, v, g, lw, _pick(t // CHUNK, (8, 4, 2, 1)))
    sf = sfin.reshape(b, npair, 2, HEAD_DIM, 2, HEAD_DIM)
    s_fin = jnp.stack([sf[:, :, 0, :, 0, :], sf[:, :, 1, :, 1, :]], axis=2).reshape(b, H_B, HEAD_DIM, HEAD_DIM)
    s_fin = jnp.swapaxes(s_fin, -1, -2)

    tmm = _pick(n, (1024, 512, 256, 128, 64))
    x_out = _outproj_moe(x2d, o_a.reshape(n, -1), o_b.reshape(n, D_B), o_c.reshape(n, -1), lw, tmm)

    return x_out.reshape(b, t, D_MODEL), stacks, (s_fin, pb3[:, -1:])


def _cache_outputs(stacks, b, t):
    ckv, kr, kc, vc, lf = stacks
    d = ckv.shape[0]
    return (ckv.reshape(d, b, t, KV_LORA), kr.reshape(d, b, t, ROPE), kc.reshape(d, b, t, H_C, HEAD_DIM),
            vc.reshape(d, b, t, H_C, HEAD_DIM), lf.reshape(d, b, t, H_C))


def kernel(x_prompt, x_sample, cache_mla_latent, cache_mla_krope, cache_fox_k, cache_fox_v, cache_fox_logf,
           state_rwkv_wkv, state_rwkv_shift, g_mix, w_in, mla_g_qa, mla_w_uq, mla_g_kva, mla_w_ukv, mla_g_qn,
           mla_g_kn, rw_mu, rw_w0, rw_w_up, rw_a0, rw_a_up, rw_g_up, rw_k_k, rw_k_a, rw_r_k, rw_ln_w, rw_ln_b,
           fox_g_qn, fox_g_kn, fox_b_f, w_out, g_ffn, moe_w_rg, moe_b_rg, moe_w_re, moe_b_re, moe_w_gate,
           moe_w_up, moe_w_down):
    params = dict(g_mix=g_mix, w_in=w_in, mla_g_qa=mla_g_qa, mla_w_uq=mla_w_uq, mla_g_kva=mla_g_kva,
                  mla_w_ukv=mla_w_ukv, mla_g_qn=mla_g_qn, mla_g_kn=mla_g_kn, rw_mu=rw_mu, rw_w0=rw_w0,
                  rw_w_up=rw_w_up, rw_a0=rw_a0, rw_a_up=rw_a_up, rw_g_up=rw_g_up, rw_k_k=rw_k_k, rw_k_a=rw_k_a,
                  rw_r_k=rw_r_k, rw_ln_w=rw_ln_w, rw_ln_b=rw_ln_b, fox_g_qn=fox_g_qn, fox_g_kn=fox_g_kn,
                  fox_b_f=fox_b_f, w_out=w_out, g_ffn=g_ffn, moe_w_rg=moe_w_rg, moe_b_rg=moe_b_rg,
                  moe_w_re=moe_w_re, moe_b_re=moe_b_re, moe_w_gate=moe_w_gate, moe_w_up=moe_w_up,
                  moe_w_down=moe_w_down)
    depth = g_mix.shape[0]
    yp, ys = x_prompt, x_sample
    p_stacks = s_stacks = None
    p_small, s_small = [], []
    db, past = cache_mla_latent.shape[1], cache_mla_latent.shape[2]
    ckv_all = cache_mla_latent.reshape(depth, db * past, KV_LORA)
    kr_all = jnp.pad(cache_mla_krope.reshape(depth, db * past, ROPE), ((0, 0), (0, 0), (NOPE, LANES - QK_DIM)))
    fk_all = cache_fox_k.reshape(depth, db, past, D_C)
    fv_all = cache_fox_v.reshape(depth, db * past, D_C)
    for l in range(depth):
        lw = _layer_weights(params, l)
        hist = dict(past=past, layer=l, ckv=ckv_all, krope=kr_all, fk=fk_all, fv=fv_all,
                    flogf=cache_fox_logf[l], wkv=state_rwkv_wkv[l], shift=state_rwkv_shift[l])
        yp, p_stacks, sm = _layer(yp, lw, None, p_stacks)
        p_small.append(sm)
        ys, s_stacks, sm = _layer(ys, lw, hist, s_stacks)
        s_small.append(sm)
    p_out = _cache_outputs(p_stacks, *x_prompt.shape[:2]) + tuple(jnp.stack(t) for t in zip(*p_small))
    s_out = _cache_outputs(s_stacks, *x_sample.shape[:2]) + tuple(jnp.stack(t) for t in zip(*s_small))
    return (yp, ys) + p_out + s_out
```

```python
import functools
import math

import jax
import jax.numpy as jnp
from jax import lax
from jax.experimental import pallas as pl
from jax.experimental.pallas import tpu as pltpu

F32 = jnp.float32
BF16 = jnp.bfloat16

D_MODEL = 1024
HEAD_DIM = 64
H_A, H_B, H_C = 6, 6, 4
Q_LORA, KV_LORA, NOPE, ROPE = 192, 128, 64, 32
QK_DIM = NOPE + ROPE
ROPE_BASE = 10000.0
A_IN = Q_LORA + KV_LORA + ROPE
D_B = H_B * HEAD_DIM
W_LORA, A_LORA, G_LORA = 64, 64, 128
B_IN = 3 * D_B + W_LORA + A_LORA + G_LORA
DECAY_SCALE = math.exp(-0.5)
GN_EPS = 64e-5
D_C = H_C * HEAD_DIM
C_IN = 3 * D_C + H_C
N_GROUPS, E_PER_GROUP = 4, 4
N_EXPERTS = N_GROUPS * E_PER_GROUP
D_FF_E = 256
NEG_INF = -1e30
RMS_EPS = 1e-6
CHUNK = 64
LOG2E = math.log2(math.e)

LANES = 128
BF16_ROWS = 16
VMEM_LIMIT = 56 * 1024 * 1024
ATTN_TK = 512
ATTN_TKP = 2048
ATTN_TQ = 1024
VT_ROWS = LANES + BF16_ROWS

_C_QL = 0
_C_KV = 256
_C_KR = 384
_C_PB = 512
_C_QC = _C_PB + B_IN
_C_KC = _C_QC + D_C
_C_VC = _C_KC + D_C
_C_F = _C_VC + D_C
_C_END = _C_F + LANES


def _cparams(sem):
    return pltpu.CompilerParams(dimension_semantics=sem, vmem_limit_bytes=VMEM_LIMIT)


def _dot(a, b):
    return jnp.dot(a, b, preferred_element_type=F32)


def _dot_nt(a, b):
    return lax.dot_general(a, b, (((1,), (1,)), ((), ())), preferred_element_type=F32)


def _dot_tn(a, b):
    return lax.dot_general(a, b, (((0,), (0,)), ((), ())), preferred_element_type=F32)


def _split2(x):
    hi = x.astype(BF16)
    lo = (x - hi.astype(F32)).astype(BF16)
    return hi, lo


def _split3(x):
    hi = x.astype(BF16)
    r = x - hi.astype(F32)
    mid = r.astype(BF16)
    lo = (r - mid.astype(F32)).astype(BF16)
    return hi, mid, lo


def _dot3(a, b, dot=_dot):
    ah, al = _split2(a)
    bh, bl = _split2(b)
    return dot(ah, bh) + (dot(ah, bl) + dot(al, bh))


def _mm(a, b, dot=_dot):
    return dot(a.astype(BF16), b.astype(BF16))


def _dot_sel(a, sel):
    return _dot(a.astype(BF16), sel)


def _iota(shape, dim):
    return lax.broadcasted_iota(jnp.int32, shape, dim)


def _rope(x, c, s_left, s_right):
    return x * c + pltpu.roll(x, 112, 1) * s_left + pltpu.roll(x, 16, 1) * s_right


def _sigmoid(x):
    return 1.0 / (1.0 + jnp.exp(-x))


def _log_sigmoid(x):
    return jnp.minimum(x, 0.0) - jnp.log(1.0 + jnp.exp(-jnp.abs(x)))


def _transpose_bf16(x, eye):
    return _dot_nt(eye, x.astype(BF16)).astype(BF16)


def _store_vt(vt_ref, pp, v_block, eye):
    base = pp * VT_ROWS
    vt_ref[0, base:base + LANES, :] = _transpose_bf16(v_block, eye)
    cols = vt_ref.shape[2]
    vt_ref[0, base + LANES:base + VT_ROWS, :] = jnp.where(_iota((BF16_ROWS, cols), 0) == 0, 1.0, 0.0).astype(BF16)


def _inproj_kernel(*refs, n_prev):
    it = iter(refs)
    (x_ref, gmix_ref, w_ref, gqa_ref, wuq_ref, gkva_ref, gqn_ref, gfq_ref, gfk_ref, bf_ref, ones_ref, bd_ref,
     eye_ref, wuk_ref, wuv_ref, gkn_ref, gkr_ref, c_ref, sl_ref, sr_ref) = (next(it) for _ in range(20))
    prev = [next(it) for _ in range(5)] if n_prev else []
    stacks = [next(it) for _ in range(5)]
    qt_ref, pb_ref, qct_ref, kcb_ref, vct_ref, km_ref, vtm_ref = (next(it) for _ in range(7))
    for s_ref, p_ref in zip(stacks, prev):
        s_ref[0:n_prev] = p_ref[...]
    ckv_ref, kr_ref, kc_ref, vc_ref, lf_ref = (s.at[n_prev] for s in stacks)

    x = x_ref[...]
    h = x * lax.rsqrt(jnp.mean(x * x, axis=-1, keepdims=True) + RMS_EPS) * gmix_ref[...]
    h = h.astype(BF16)
    eye = eye_ref[...]
    ones = ones_ref[...]
    bd = bd_ref[...]

    ql = _dot(h, w_ref[:, _C_QL:_C_QL + 256])
    kv = _dot(h, w_ref[:, _C_KV:_C_KV + KV_LORA])
    kr = _dot(h, w_ref[:, _C_KR:_C_KR + LANES])
    pb_ref[...] = _dot(h, w_ref[:, _C_PB:_C_PB + B_IN])
    qc = _dot(h, w_ref[:, _C_QC:_C_QC + D_C])
    kc = _dot(h, w_ref[:, _C_KC:_C_KC + D_C])
    vc = _dot(h, w_ref[:, _C_VC:_C_VC + D_C])
    f = _dot(h, w_ref[:, _C_F:_C_F + LANES]) + bf_ref[...]

    ql = ql * lax.rsqrt(jnp.sum(ql * ql, axis=-1, keepdims=True) * (1.0 / Q_LORA) + RMS_EPS) * gqa_ref[...]
    qh = _dot(ql.astype(BF16), wuq_ref[...])
    c, s_l, s_r = c_ref[...], sl_ref[...], sr_ref[...]
    qv = [qh[:, hh * LANES:(hh + 1) * LANES] for hh in range(H_A)]
    ss = [_dot_sel(q * q, ones) for q in qv]
    qn = [q * lax.rsqrt(s * (1.0 / QK_DIM) + RMS_EPS) * gqn_ref[...] for q, s in zip(qv, ss)]
    qr = [_rope(q, c, s_l, s_r) * (LOG2E * QK_DIM ** -0.5) for q in qn]
    for hh in range(H_A):
        qt_ref[0, hh * LANES:(hh + 1) * LANES, :] = _transpose_bf16(qr[hh], eye)

    ckv = kv * lax.rsqrt(jnp.mean(kv * kv, axis=-1, keepdims=True) + RMS_EPS) * gkva_ref[...]
    ckv_ref[...] = ckv
    kr_ref[...] = kr[:, NOPE:NOPE + ROPE]
    _mla_kv(ckv, kr, wuk_ref[...], wuv_ref[...], gkn_ref[...], gkr_ref[...], ones, eye, c, s_l, s_r, km_ref, vtm_ref)

    qc = qc * lax.rsqrt(_dot_sel(qc * qc, bd) * (1.0 / HEAD_DIM) + RMS_EPS) * gfq_ref[...]
    qc = qc * (LOG2E * HEAD_DIM ** -0.5)
    kc = kc * lax.rsqrt(_dot_sel(kc * kc, bd) * (1.0 / HEAD_DIM) + RMS_EPS) * gfk_ref[...]
    kc_ref[...] = kc
    kcb_ref[...] = kc.astype(BF16)
    vc_ref[...] = vc
    for pp in range(D_C // LANES):
        qct_ref[0, pp * LANES:(pp + 1) * LANES, :] = _transpose_bf16(qc[:, pp * LANES:(pp + 1) * LANES], eye)
        _store_vt(vct_ref, pp, vc[:, pp * LANES:(pp + 1) * LANES], eye)
    lf_ref[...] = _log_sigmoid(f)[:, :H_C]


_CACHE_WIDTHS = (KV_LORA, ROPE, D_C, D_C, H_C)


def _inproj(x2d, lw, tabs, bsz, t, tm, prev_stacks):
    n = x2d.shape[0]
    nt = t // tm
    n_prev = 0 if prev_stacks is None else prev_stacks[0].shape[0]
    row = lambda i: (i, 0)
    fixed = lambda i: (0, 0)
    tab = lambda i: (i % nt, 0)
    colmajor = lambda i: (i // nt, 0, i % nt)
    full = lambda a: pl.BlockSpec(a.shape, fixed)
    params = [lw['g_mix'], lw['w_in'], lw['g_qa'], lw['w_uq'], lw['g_kva'], lw['g_qn'], lw['g_fq'], lw['g_fk'],
              lw['b_f'], lw['ones128'], lw['bd256'], lw['eye128'], lw['w_uk'], lw['w_uv'], lw['g_kn'], lw['g_kr']]
    stack = lambda layers, w: pl.BlockSpec((layers, tm, w), lambda i: (0, i, 0))
    rowout = lambda w, dt: (pl.BlockSpec((tm, w), row), jax.ShapeDtypeStruct((n, w), dt))
    colout = lambda w: (pl.BlockSpec((1, w, tm), colmajor), jax.ShapeDtypeStruct((bsz, w, t), BF16))
    outs = [(stack(n_prev + 1, w), jax.ShapeDtypeStruct((n_prev + 1, n, w), F32)) for w in _CACHE_WIDTHS]
    outs += [colout(H_A * LANES), rowout(B_IN, F32), colout(D_C), rowout(D_C, BF16), colout(H_C // 2 * VT_ROWS),
             rowout(H_A * LANES, BF16), colout(H_A // 2 * VT_ROWS)]
    prev_args = [] if prev_stacks is None else list(prev_stacks)
    return pl.pallas_call(
        functools.partial(_inproj_kernel, n_prev=n_prev),
        grid=(n // tm,),
        in_specs=[pl.BlockSpec((tm, D_MODEL), row)] + [full(a) for a in params]
                 + [pl.BlockSpec((tm, LANES), tab)] * 3 + [stack(n_prev, w) for w in _CACHE_WIDTHS if n_prev],
        out_specs=[o[0] for o in outs],
        out_shape=[o[1] for o in outs],
        compiler_params=_cparams(("parallel",)),
        name="inproj",
    )(x2d, *params, *tabs, *prev_args)


def _mla_kv(ckv, kr, wuk, wuv, gkn, gkr, ones, eye, c, s_l, s_r, k_ref, vt_ref):
    cb = ckv.astype(BF16)
    kn = _dot(cb, wuk)
    v = _dot(cb, wuv)
    for pp in range(H_A // 2):
        _store_vt(vt_ref, pp, v[:, pp * LANES:(pp + 1) * LANES], eye)
    ssr = _dot_sel(kr * kr, ones)
    krg = _rope(kr * gkr, c, s_l, s_r)
    knh = [kn[:, hh * LANES:(hh + 1) * LANES] for hh in range(H_A)]
    ssn = [_dot_sel(k * k, ones) for k in knh]
    for hh in range(H_A):
        r = lax.rsqrt((ssn[hh] + ssr) * (1.0 / QK_DIM) + RMS_EPS)
        k_ref[:, hh * LANES:(hh + 1) * LANES] = ((knh[hh] * gkn + krg) * r).astype(BF16)


def _kvprep_kernel(ckv_ref, kr_ref, wuk_ref, wuv_ref, gkn_ref, gkr_ref, ones_ref, eye_ref, c_ref, sl_ref, sr_ref,
                   k_ref, vt_ref):
    _mla_kv(ckv_ref[...], kr_ref[...], wuk_ref[...], wuv_ref[...], gkn_ref[...], gkr_ref[...], ones_ref[...],
            eye_ref[...], c_ref[...], sl_ref[...], sr_ref[...], k_ref, vt_ref)


def _kvprep(ckv3d, kr128_3d, layer, lw, tabs, bsz, t, tm):
    n = ckv3d.shape[1]
    nt = t // tm
    row = lambda i: (i, 0)
    fixed = lambda i: (0, 0)
    tab = lambda i: (i % nt, 0)
    lrow = lambda i: (layer, i, 0)
    full = lambda a: pl.BlockSpec(a.shape, fixed)
    params = [lw['w_uk'], lw['w_uv'], lw['g_kn'], lw['g_kr'], lw['ones128'], lw['eye128']]
    vt_rows = H_A // 2 * VT_ROWS
    return pl.pallas_call(
        _kvprep_kernel,
        grid=(n // tm,),
        in_specs=[pl.BlockSpec((None, tm, KV_LORA), lrow), pl.BlockSpec((None, tm, LANES), lrow)]
                 + [full(a) for a in params] + [pl.BlockSpec((tm, LANES), tab)] * 3,
        out_specs=[pl.BlockSpec((tm, H_A * LANES), row),
                   pl.BlockSpec((1, vt_rows, tm), lambda i: (i // nt, 0, i % nt))],
        out_shape=[jax.ShapeDtypeStruct((n, H_A * LANES), BF16),
                   jax.ShapeDtypeStruct((bsz, vt_rows, t), BF16)],
        compiler_params=_cparams(("parallel",)),
        name="kvprep",
    )(ckv3d, kr128_3d, *params, *tabs)


def _transpose_kernel(x_ref, eye_ref, o_ref):
    eye = eye_ref[...]
    for pp in range(x_ref.shape[1] // LANES):
        _store_vt(o_ref, pp, x_ref[:, pp * LANES:(pp + 1) * LANES], eye)


def _transpose_cast(x3d, layer, lw, bsz, t, tm):
    _, n, w = x3d.shape
    nt = t // tm
    rows = w // LANES * VT_ROWS
    return pl.pallas_call(
        _transpose_kernel,
        grid=(n // tm,),
        in_specs=[pl.BlockSpec((None, tm, w), lambda i: (layer, i, 0)),
                  pl.BlockSpec((LANES, LANES), lambda i: (0, 0))],
        out_specs=pl.BlockSpec((1, rows, tm), lambda i: (i // nt, 0, i % nt)),
        out_shape=jax.ShapeDtypeStruct((bsz, rows, t), BF16),
        compiler_params=_cparams(("parallel",)),
        name="transpose_cast",
    )(x3d, lw['eye128'])


def _attn_kernel(*refs, tq, tk, tkp, nq, n_past, n_pairs, shared, frame_causal, has_bias, pipelined):
    it = iter(refs)
    q_ref, k_ref, vt_ref = next(it), next(it), next(it)
    if n_past:
        kp_ref, vpt_ref = next(it), next(it)
    if has_bias:
        cq_ref, ck_ref = next(it), next(it)
    eye_ref, o_ref = next(it), next(it)
    q_scr, m_scr, acc_scr = next(it), next(it), next(it)

    qi = 0 if nq == 1 else pl.program_id(1)
    low = _iota((LANES, tq), 0) < HEAD_DIM
    blk = lambda p, j: p if shared else 2 * p + j
    heads = [(p, j) for p in range(n_pairs) for j in range(2)]
    nh = len(heads)
    for p, j in heads:
        q = q_ref[0, blk(p, j) * LANES:(blk(p, j) + 1) * LANES, :]
        if shared:
            keep = low if j == 0 else jnp.logical_not(low)
            q = jnp.where(keep, q, jnp.zeros_like(q))
        q_scr[2 * p + j] = q
    m_scr[...] = jnp.full(m_scr.shape, -jnp.inf, F32)
    acc_scr[...] = jnp.zeros(acc_scr.shape, F32)

    def score(kr, st, width, p, j):
        kt = kr[0, pl.ds(st, width), blk(p, j) * LANES:(blk(p, j) + 1) * LANES].astype(BF16)
        return _dot(kt, q_scr[2 * p + j])

    def scores_to(buf, kr, st, width):
        for p, j in heads:
            buf[2 * p + j] = score(kr, st, width, p, j)

    def consume(get_s, vr, st, width, key_start, mask):
        ss = []
        for p, j in heads:
            s = get_s(p, j)
            if has_bias:
                s = s - ck_ref[0, pl.ds(key_start, width), 2 * p + j:2 * p + j + 1]
            if mask is not None:
                s = jnp.where(mask, s, NEG_INF)
            ss.append(s)
        cqs = [cq_ref[0, p, j:j + 1, :] if has_bias else 0.0 for p, j in heads]
        m_prev = [m_scr[h] for h in range(nh)]
        m_next = [jnp.maximum(m_prev[h], jnp.max(ss[h], axis=0, keepdims=True) + cqs[h]) for h in range(nh)]
        prs = [jnp.exp2(ss[h] - (m_next[h] - cqs[h])).astype(BF16) for h in range(nh)]
        alphas = [jnp.exp2(m_prev[h] - m_next[h]) for h in range(nh)]
        pvs = [_dot(vr[0, p * VT_ROWS:(p + 1) * VT_ROWS, pl.ds(st, width)], prs[2 * p + j]) for p, j in heads]
        for h in range(nh):
            acc_scr[h] = acc_scr[h] * alphas[h] + pvs[h]
            m_scr[h] = m_next[h]

    def step(kr, vr, st, width, key_start, mask):
        consume(lambda p, j: score(kr, st, width, p, j), vr, st, width, key_start, mask)

    kidx = _iota((tk, tq), 0)
    qidx = _iota((tk, tq), 1)
    diag_mask = lambda d: ((kidx + d * tk <= qidx) if frame_causal else
                           (jnp.right_shift(kidx + d * tk, 6) <= jnp.right_shift(qidx, 6)))

    if pipelined:
        sa, sb = next(it), next(it)
        tile = lambda t: pl.multiple_of(t * tk, tk)
        from_a = lambda p, j: sa[2 * p + j]
        from_b = lambda p, j: sb[2 * p + j]
        scores_to(sa, k_ref, tile(0), tk)

        def pair_body(u, carry):
            t0 = 2 * u
            scores_to(sb, k_ref, tile(t0 + 1), tk)
            consume(from_a, vt_ref, tile(t0), tk, tile(t0), None)
            scores_to(sa, k_ref, tile(t0 + 2), tk)
            consume(from_b, vt_ref, tile(t0 + 1), tk, tile(t0 + 1), None)
            return carry
        if tq == 2 * tk:
            first = 2 * qi
            lax.fori_loop(0, qi, pair_body, 0)
            scores_to(sb, k_ref, tile(first + 1), tk)
            consume(from_a, vt_ref, tile(first), tk, tile(first), diag_mask(0))
            consume(from_b, vt_ref, tile(first + 1), tk, tile(first + 1), diag_mask(1))
        else:
            lax.fori_loop(0, qi // 2, pair_body, 0)
            odd = lax.rem(qi, 2) == 1

            @pl.when(jnp.logical_not(odd))
            def _():
                consume(from_a, vt_ref, tile(qi), tk, tile(qi), diag_mask(0))

            @pl.when(odd)
            def _():
                scores_to(sb, k_ref, tile(qi), tk)
                consume(from_a, vt_ref, tile(qi - 1), tk, tile(qi - 1), None)
                consume(from_b, vt_ref, tile(qi), tk, tile(qi), diag_mask(0))
    else:
        if n_past:
            def past_body(t, carry):
                st = pl.multiple_of(t * tkp, tkp)
                step(kp_ref, vpt_ref, st, tkp, st, None)
                return carry
            lax.fori_loop(0, n_past // tkp, past_body, 0)

        def new_body(t, carry):
            st = pl.multiple_of(t * tk, tk)
            step(k_ref, vt_ref, st, tk, n_past + st, None)
            return carry
        if nq > 1:
            lax.fori_loop(0, qi * (tq // tk), new_body, 0)

        for d in range(tq // tk):
            st = qi * tq + d * tk
            st = st if nq == 1 else pl.multiple_of(st, tk)
            step(k_ref, vt_ref, st, tk, n_past + st, diag_mask(d))

    for p in range(n_pairs):
        a, b = acc_scr[2 * p], acc_scr[2 * p + 1]
        ot = jnp.where(low, a[:LANES] / a[LANES:LANES + 1], b[:LANES] / b[LANES:LANES + 1])
        o_ref[0, :, p * LANES:(p + 1) * LANES] = _dot_tn(ot.astype(BF16), eye_ref[...]).astype(o_ref.dtype)


def _attention(qt, k_new, vt_new, k_past, vt_past, bias, eye, *, n_pairs, shared, frame_causal, tq, tk, tkp,
               past_layer=0):
    b, t = k_new.shape[0], k_new.shape[1]
    n_past = 0 if k_past is None else k_past.shape[-2]
    nq = t // tq
    wq = qt.shape[1]
    args = [qt, k_new, vt_new]
    specs = [pl.BlockSpec((1, wq, tq), lambda bi, i: (bi, 0, i)),
             pl.BlockSpec((1, t, wq), lambda bi, i: (bi, 0, 0)),
             pl.BlockSpec((1, n_pairs * VT_ROWS, t), lambda bi, i: (bi, 0, 0))]
    if n_past:
        args += [k_past, vt_past]
        if k_past.ndim == 4:
            kspec = pl.BlockSpec((None, 1, n_past, wq), lambda bi, i: (past_layer, bi, 0, 0))
        else:
            kspec = pl.BlockSpec((1, n_past, wq), lambda bi, i: (bi, 0, 0))
        specs += [kspec, pl.BlockSpec((1, n_pairs * VT_ROWS, n_past), lambda bi, i: (bi, 0, 0))]
    if bias is not None:
        cq, ck = bias
        args += [cq, ck]
        specs += [pl.BlockSpec((1, n_pairs, 2, tq), lambda bi, i: (bi, 0, 0, i)),
                  pl.BlockSpec((1,) + ck.shape[1:], lambda bi, i: (bi, 0, 0))]
    args.append(eye)
    specs.append(pl.BlockSpec((LANES, LANES), lambda bi, i: (0, 0)))
    pipelined = n_past == 0 and nq > 1 and tq in (tk, 2 * tk)
    kern = functools.partial(_attn_kernel, tq=tq, tk=tk, tkp=tkp, nq=nq, n_past=n_past, n_pairs=n_pairs,
                             shared=shared, frame_causal=frame_causal, has_bias=bias is not None,
                             pipelined=pipelined)
    nh = 2 * n_pairs
    score_bufs = [pltpu.VMEM((nh, tk, tq), F32)] * 2 if pipelined else []
    return pl.pallas_call(
        kern,
        grid=(b, nq),
        in_specs=specs,
        out_specs=pl.BlockSpec((1, tq, n_pairs * LANES), lambda bi, i: (bi, i, 0)),
        out_shape=jax.ShapeDtypeStruct((b, t, n_pairs * LANES), BF16),
        scratch_shapes=[pltpu.VMEM((nh, LANES, tq), BF16), pltpu.VMEM((nh, 1, tq), F32),
                        pltpu.VMEM((nh, VT_ROWS, tq), F32)] + score_bufs,
        compiler_params=_cparams(("parallel", "arbitrary")),
        name="attn_fox" if shared else "attn_mla",
    )(*args)


def _cumsum_kernel(x_ref, tri_ref, o_ref):
    rows, n = x_ref.shape
    tri = tri_ref[...]
    carry = jnp.zeros((rows, 1), F32)
    for t in range(n // LANES):
        xt = x_ref[:, t * LANES:(t + 1) * LANES]
        hi, mid, lo = _split3(xt)
        o_ref[:, t * LANES:(t + 1) * LANES] = (_dot(hi, tri) + _dot(mid, tri) + _dot(lo, tri) + carry) * LOG2E
        carry = carry + jnp.sum(xt, axis=1, keepdims=True)


def _cumsum_lanes(x):
    rows, n = x.shape
    tri = (jnp.arange(LANES)[:, None] <= jnp.arange(LANES)[None, :]).astype(BF16)
    return pl.pallas_call(
        _cumsum_kernel,
        out_shape=jax.ShapeDtypeStruct((rows, n), F32),
        compiler_params=pltpu.CompilerParams(vmem_limit_bytes=VMEM_LIMIT),
        name="cumsum",
    )(x, tri)


def _rwkv_tokens(pbv, prev_row, mu, w0, ww, a0, wa, wg, k_k, k_a, bd):
    rolled = pltpu.roll(pbv, 1, 0)
    shifted = jnp.where(_iota(pbv.shape, 0) == 0, prev_row, rolled)
    xs = pbv + (shifted - pbv) * mu
    r = xs[:, 0:D_B]
    kb = xs[:, D_B:2 * D_B]
    o3 = 3 * D_B
    wa_in = xs[:, o3:o3 + LANES]
    wa_in = jnp.where(_iota(wa_in.shape, 1) < W_LORA, jnp.tanh(wa_in), wa_in).astype(BF16)
    lw = -DECAY_SCALE * _sigmoid(w0 + _dot(wa_in, ww))
    a = _sigmoid(a0 + _dot(wa_in, wa))
    gate = _dot(_sigmoid(xs[:, o3 + LANES:o3 + 2 * LANES]).astype(BF16), wg)
    kk = kb * k_k
    kk = kk * lax.rsqrt(_dot_sel(kk * kk, bd) + 1e-12)
    return r, kk, kb * (1.0 + (a - 1.0) * k_a), kk * a, xs[:, 2 * D_B:3 * D_B], lw, gate


def _rwkv_chunk_kernel(pb_ref, pbprev_ref, prev_ref, mu_ref, w0_ref, ww_ref, a0_ref, wa_ref, wg_ref, kk_ref, ka_ref,
                       bd_ref, rt_ref, yl_ref, g_ref, h_ref, ro_ref, ko_ref, vo_ref, go_ref):
    s = pb_ref.shape[1]
    nc = s // CHUNK
    prev_row = jnp.where(pl.program_id(1) == 0, prev_ref[0], pbprev_ref[0, 7:8, :])
    r_all, kk_all, k_all, b_all, v_all, lw_all, gate = _rwkv_tokens(
        pb_ref[0], prev_row, mu_ref[...], w0_ref[...], ww_ref[...], a0_ref[...], wa_ref[...], wg_ref[...],
        kk_ref[...], ka_ref[...], bd_ref[...])
    ro_ref[0] = r_all.astype(BF16)
    ko_ref[0] = k_all.astype(BF16)
    vo_ref[0] = v_all.astype(BF16)
    go_ref[0] = gate.astype(BF16)
    row = _iota((s, LANES), 0)
    rin = jnp.bitwise_and(row, CHUNK - 1)
    ti = _iota((s, s), 0)
    si = _iota((s, s), 1)
    same = jnp.right_shift(ti, 6) == jnp.right_shift(si, 6)
    strict = jnp.logical_and(same, si < ti)
    incl = jnp.logical_and(same, si <= ti)
    eye = (ti == si).astype(F32)
    low = _iota((s, LANES), 1) < HEAD_DIM
    ji = _iota((LANES, LANES), 0)
    jj = _iota((LANES, LANES), 1)
    blockdiag = (ji < HEAD_DIM) == (jj < HEAD_DIM)
    npair = D_B // LANES
    pairs = range(npair)
    heads = [(p, x) for p in pairs for x in range(2)]

    tot, kkt, rt, khbh, kw, bw, v = [], [], [], [], [], [], []
    for p in pairs:
        sl = slice(p * LANES, (p + 1) * LANES)
        lw, kp, bp = lw_all[:, sl], k_all[:, sl], b_all[:, sl]
        cl = lw
        for sh in (1, 2, 4, 8, 16, 32):
            cl = cl + jnp.where(rin >= sh, pltpu.roll(cl, sh, 0), 0.0)
        tp = jnp.concatenate(
            [jnp.broadcast_to(cl[c * CHUNK + CHUNK - 1:(c + 1) * CHUNK, :], (CHUNK, LANES)) for c in range(nc)],
            axis=0)
        e_ncl = jnp.exp(-cl)
        e_rem = jnp.exp(tp - cl)
        tot.append(tp)
        kkt.append(kk_all[:, sl] * jnp.exp(cl - lw))
        rt.append(r_all[:, sl] * jnp.exp(cl))
        khbh.append(jnp.concatenate([kp * e_ncl, bp * e_ncl], axis=0).astype(BF16))
        kw.append(kp * e_rem)
        bw.append(bp * e_rem)
        v.append(v_all[:, sl])

    keep = lambda x: low if x == 0 else jnp.logical_not(low)
    kkt_x = [jnp.where(keep(x), kkt[p], 0.0) for p, x in heads]
    rt_x = [jnp.where(keep(x), rt[p], 0.0) for p, x in heads]
    akk, ark, arb, pw, tinv = [], [], [], [], []
    for i, (p, x) in enumerate(heads):
        p4 = _dot_nt(jnp.concatenate([kkt_x[i], rt_x[i]], axis=0).astype(BF16), khbh[p])
        akk.append(jnp.where(strict, p4[:s, :s], 0.0))
        ark.append(jnp.where(incl, p4[s:, :s], 0.0))
        arb.append(jnp.where(incl, p4[s:, s:], 0.0))
        pw.append(jnp.where(strict, -p4[:s, s:], 0.0))
        tinv.append(eye + pw[i])
    for _ in range(5):
        pw = [_mm(m, m) for m in pw]
        tinv = [t + _mm(t, m) for t, m in zip(tinv, pw)]
    av = [_mm(jnp.concatenate([akk[i], ark[i]], axis=0), v[p]) for i, (p, x) in enumerate(heads)]
    tx = [_mm(tinv[i], jnp.concatenate([av[i][:s], kkt_x[i]], axis=1)) for i in range(len(heads))]
    ax = [_mm(arb[i], tx[i]) for i in range(len(heads))]

    for p in pairs:
        sl = slice(p * LANES, (p + 1) * LANES)
        a, b2 = 2 * p, 2 * p + 1
        uloc2 = jnp.where(low, tx[a][:, :LANES], tx[b2][:, :LANES])
        kkt2 = jnp.where(low, tx[a][:, LANES:], tx[b2][:, LANES:])
        rt_ref[0, :, sl] = jnp.where(low, rt_x[a] - ax[a][:, LANES:], rt_x[b2] - ax[b2][:, LANES:]).astype(BF16)
        yl_ref[0, :, sl] = jnp.where(low, av[a][s:] - ax[a][:, :LANES], av[b2][s:] - ax[b2][:, :LANES]).astype(BF16)
        for c in range(nc):
            inc = jnp.right_shift(row, 6) == c
            bw_c = jnp.where(inc, bw[p], 0.0)
            kw_c = jnp.where(inc, kw[p], 0.0)
            e_tot = jnp.exp(tot[p][c * CHUNK:c * CHUNK + 1, :])
            gm = jnp.where(ji == jj, e_tot, 0.0) - _mm(bw_c, kkt2, _dot_tn)
            hm = _mm(kw_c, v[p], _dot_tn) - _mm(bw_c, uloc2, _dot_tn)
            g_ref[0, p, c] = jnp.where(blockdiag, gm, 0.0).astype(BF16)
            h_ref[0, p, c] = jnp.where(blockdiag, hm, 0.0).astype(BF16)


def _rwkv_chunk(pb3, prev, lw, s):
    bsz, t, _ = pb3.shape
    npair = H_B // 2
    nc = t // CHUNK
    sb = s // 8
    tok = pl.BlockSpec((1, s, D_B), lambda bi, i: (bi, i, 0))
    mat = pl.BlockSpec((1, npair, s // CHUNK, LANES, LANES), lambda bi, i: (bi, 0, i, 0, 0))
    params = [lw['rw_mu'], lw['rw_w0'], lw['rw_ww'], lw['rw_a0'], lw['rw_wa'], lw['rw_wg'], lw['rw_k_k'],
              lw['rw_k_a'], lw['bd384']]
    full = lambda a: pl.BlockSpec(a.shape, lambda bi, i: (0, 0))
    return pl.pallas_call(
        _rwkv_chunk_kernel,
        grid=(bsz, t // s),
        in_specs=[pl.BlockSpec((1, s, B_IN), lambda bi, i: (bi, i, 0)),
                  pl.BlockSpec((1, 8, B_IN), lambda bi, i: (bi, jnp.maximum(i * sb - 1, 0), 0)),
                  pl.BlockSpec((1, 1, B_IN), lambda bi, i: (bi, 0, 0))] + [full(a) for a in params],
        out_specs=[tok, tok, mat, mat] + [tok] * 4,
        out_shape=[jax.ShapeDtypeStruct((bsz, t, D_B), BF16)] * 2
                  + [jax.ShapeDtypeStruct((bsz, npair, nc, LANES, LANES), BF16)] * 2
                  + [jax.ShapeDtypeStruct((bsz, t, D_B), BF16)] * 4,
        compiler_params=_cparams(("parallel", "parallel")),
        name="rwkv_chunk",
    )(pb3, pb3, prev, *params)


def _rwkv_scan_kernel(s0_ref, g_ref, h_ref, rt_ref, yl_ref, r_ref, k_ref, v_ref, gate_ref,
                      lnw_ref, lnb_ref, rk_ref, bd_ref, o_ref, sfin_ref, st_scr, y_scr):
    npair, ncb = g_ref.shape[1], g_ref.shape[2]

    @pl.when(pl.program_id(1) == 0)
    def _():
        st_scr[...] = s0_ref[0]

    def times_state(a, st):
        hi, lo = _split2(st)
        return _dot(a, hi) + _dot(a, lo)

    sts = [st_scr[p] for p in range(npair)]
    for c in range(ncb):
        rows = slice(c * CHUNK, (c + 1) * CHUNK)
        for p in range(npair):
            sl = slice(p * LANES, (p + 1) * LANES)
            y_scr[rows, sl] = times_state(rt_ref[0, rows, sl], sts[p]) + yl_ref[0, rows, sl].astype(F32)
            sts[p] = times_state(g_ref[0, p, c], sts[p]) + h_ref[0, p, c].astype(F32)
    for p in range(npair):
        st_scr[p] = sts[p]
        sfin_ref[0, p] = sts[p]

    bd = bd_ref[...]
    y = y_scr[...]
    mu = _dot_sel(y, bd) * (1.0 / HEAD_DIM)
    d = y - mu
    var = _dot_sel(d * d, bd) * (1.0 / HEAD_DIM)
    yn = d * lax.rsqrt(var + GN_EPS) * lnw_ref[...] + lnb_ref[...]
    rk = r_ref[0].astype(F32) * k_ref[0].astype(F32) * rk_ref[...]
    bonus = _dot_sel(rk, bd) * v_ref[0].astype(F32)
    o_ref[0] = ((yn + bonus) * gate_ref[0].astype(F32)).astype(BF16)


def _rwkv_scan(s0, g, h, rt, yl, r, k, v, gate, lw, ncb):
    bsz, t, _ = rt.shape
    npair = H_B // 2
    nc = t // CHUNK
    st_spec = pl.BlockSpec((1, npair, LANES, LANES), lambda bi, i: (bi, 0, 0, 0))
    mat = pl.BlockSpec((1, npair, ncb, LANES, LANES), lambda bi, i: (bi, 0, i, 0, 0))
    tok = pl.BlockSpec((1, ncb * CHUNK, D_B), lambda bi, i: (bi, i, 0))
    params = [lw['rw_ln_w'], lw['rw_ln_b'], lw['rw_r_k'], lw['bd384']]
    full = lambda a: pl.BlockSpec(a.shape, lambda bi, i: (0, 0))
    return pl.pallas_call(
        _rwkv_scan_kernel,
        grid=(bsz, nc // ncb),
        in_specs=[st_spec, mat, mat] + [tok] * 6 + [full(a) for a in params],
        out_specs=[tok, st_spec],
        out_shape=[jax.ShapeDtypeStruct((bsz, t, D_B), BF16), jax.ShapeDtypeStruct((bsz, npair, LANES, LANES), F32)],
        scratch_shapes=[pltpu.VMEM((npair, LANES, LANES), F32), pltpu.VMEM((ncb * CHUNK, D_B), F32)],
        compiler_params=_cparams(("parallel", "arbitrary")),
        name="rwkv_scan",
    )(s0, g, h, rt, yl, r, k, v, gate, *params)


_L_EXP = 16


def _moe_kernel(x_ref, oa_ref, ob_ref, oc_ref, wo_ref, gffn_ref, wr_ref, br_ref, wg_ref, wu_ref, wd_ref,
                o_ref, acc_ref, h_ref, comb_ref):
    e = pl.program_id(1)

    @pl.when(e == 0)
    def _():
        da = H_A * HEAD_DIM
        x1 = (x_ref[...] + _dot(oa_ref[...], wo_ref[0:da, :]) + _dot(ob_ref[...], wo_ref[da:da + D_B, :])
              + _dot(oc_ref[...], wo_ref[da + D_B:, :]))
        acc_ref[...] = x1
        hf = x1 * lax.rsqrt(jnp.mean(x1 * x1, axis=-1, keepdims=True) + RMS_EPS) * gffn_ref[...]
        h_ref[...] = hf.astype(BF16)

        logit = _dot3(hf, wr_ref[...]) + br_ref[...]
        lane_i = _iota(logit.shape, 1)
        lane = lane_i.astype(F32)
        big = jnp.float32(3e38)
        is_g = lane_i < N_GROUPS
        gl = jnp.where(is_g, logit, -big)
        gmax = jnp.max(gl, axis=1, keepdims=True)
        pg_top = 1.0 / jnp.sum(jnp.where(is_g, jnp.exp(gl - gmax), 0.0), axis=1, keepdims=True)
        g_idx = jnp.min(jnp.where(jnp.logical_and(is_g, gl == gmax), lane, big), axis=1, keepdims=True)
        el = lane_i - _L_EXP
        in_e = jnp.logical_and(el >= 0, el < N_EXPERTS)
        sel = jnp.logical_and(in_e, jnp.right_shift(el, 2).astype(F32) == g_idx)
        l1 = jnp.where(sel, logit, -big)
        v1 = jnp.max(l1, axis=1, keepdims=True)
        i1 = jnp.min(jnp.where(jnp.logical_and(sel, l1 == v1), lane, big), axis=1, keepdims=True)
        sel2 = jnp.logical_and(sel, lane != i1)
        l2 = jnp.where(sel2, logit, -big)
        v2 = jnp.max(l2, axis=1, keepdims=True)
        i2 = jnp.min(jnp.where(jnp.logical_and(sel2, l2 == v2), lane, big), axis=1, keepdims=True)
        e2 = jnp.exp(v2 - v1)
        den = 1.0 / (1.0 + e2)
        comb_ref[...] = (jnp.where(lane == i1, den * pg_top, 0.0) + jnp.where(lane == i2, e2 * den * pg_top, 0.0))

    h = h_ref[...]
    comb = comb_ref[...]
    lane_c = _iota(comb.shape, 1)
    gates = [_dot(h, wg_ref[j]) for j in range(E_PER_GROUP)]
    ups = [_dot(h, wu_ref[j]) for j in range(E_PER_GROUP)]
    acts = []
    for j in range(E_PER_GROUP):
        ce = jnp.sum(jnp.where(lane_c == e * E_PER_GROUP + j + _L_EXP, comb, 0.0), axis=1, keepdims=True)
        acts.append((gates[j] * _sigmoid(gates[j]) * ups[j] * ce).astype(BF16))
    acc_ref[...] += _dot(jnp.concatenate(acts, axis=1), wd_ref[...])

    @pl.when(e == N_GROUPS - 1)
    def _():
        o_ref[...] = acc_ref[...]


def _outproj_moe(x2d, oa, ob, oc, lw, tm):
    n = x2d.shape[0]
    row = lambda w: pl.BlockSpec((tm, w), lambda i, e: (i, 0))
    full = lambda a: pl.BlockSpec(a.shape, lambda i, e: (0, 0))
    expert_in = pl.BlockSpec((E_PER_GROUP, D_MODEL, D_FF_E), lambda i, e: (e, 0, 0))
    return pl.pallas_call(
        _moe_kernel,
        grid=(n // tm, N_GROUPS),
        in_specs=[row(D_MODEL), row(H_A * HEAD_DIM), row(D_B), row(D_C), full(lw['w_out']), full(lw['g_ffn']),
                  full(lw['w_r']), full(lw['b_r']), expert_in, expert_in,
                  pl.BlockSpec((E_PER_GROUP * D_FF_E, D_MODEL), lambda i, e: (e, 0))],
        out_specs=row(D_MODEL),
        out_shape=jax.ShapeDtypeStruct((n, D_MODEL), F32),
        scratch_shapes=[pltpu.VMEM((tm, D_MODEL), F32), pltpu.VMEM((tm, D_MODEL), BF16), pltpu.VMEM((tm, LANES), F32)],
        compiler_params=_cparams(("parallel", "arbitrary")),
        name="outproj_moe",
    )(x2d, oa, ob, oc, lw['w_out'], lw['g_ffn'], lw['w_r'], lw['b_r'], lw['w_g'], lw['w_u'], lw['w_d'])


def _place(pieces, width):
    rows = pieces[0][1].shape[0]
    cols, at = [], 0
    for off, a in pieces:
        if off > at:
            cols.append(jnp.zeros((rows, off - at), F32))
        cols.append(a.astype(F32))
        at = off + a.shape[1]
    if width > at:
        cols.append(jnp.zeros((rows, width - at), F32))
    return jnp.concatenate(cols, axis=1)


def _row(v, width=None, off=0):
    v = v.reshape(1, -1).astype(F32)
    return v if width is None else _place([(off, v)], width)


def _block_diag_ones(n, blk):
    i = jnp.arange(n) // blk
    return (i[:, None] == i[None, :]).astype(BF16)


def _layer_weights(p, l):
    g = lambda name: p[name][l]
    w_in = g('w_in')
    o_b, o_c = A_IN, A_IN + B_IN
    w_in_p = _place([(_C_QL, w_in[:, :Q_LORA]), (_C_KV, w_in[:, Q_LORA:Q_LORA + KV_LORA]),
                     (_C_KR + NOPE, w_in[:, Q_LORA + KV_LORA:A_IN]), (_C_PB, w_in[:, o_b:o_c + 3 * D_C]),
                     (_C_F, w_in[:, o_c + 3 * D_C:])], _C_END).astype(BF16)
    w_uq = g('mla_w_uq').reshape(Q_LORA, H_A, QK_DIM)
    w_uq = jnp.pad(w_uq, ((0, 256 - Q_LORA), (0, 0), (0, LANES - QK_DIM))).reshape(256, H_A * LANES).astype(BF16)
    w_ukv = g('mla_w_ukv').reshape(KV_LORA, H_A, NOPE + HEAD_DIM)
    w_uk = jnp.pad(w_ukv[:, :, :NOPE], ((0, 0), (0, 0), (0, LANES - NOPE))).reshape(KV_LORA, H_A * LANES).astype(BF16)
    w_uv = w_ukv[:, :, NOPE:].reshape(KV_LORA, H_A * HEAD_DIM).astype(BF16)
    zeros_w = jnp.zeros((W_LORA, D_B), F32)
    w_r = _place([(0, g('moe_w_rg')), (_L_EXP, g('moe_w_re'))], LANES)
    b_r = _place([(0, g('moe_b_rg').reshape(1, -1)), (_L_EXP, g('moe_b_re').reshape(1, -1))], LANES)
    return dict(
        g_mix=_row(g('g_mix')), w_in=w_in_p, g_qa=_row(g('mla_g_qa'), 256), w_uq=w_uq, g_kva=_row(g('mla_g_kva')),
        g_qn=_row(g('mla_g_qn'), LANES), g_fq=_row(jnp.tile(g('fox_g_qn'), H_C)),
        g_fk=_row(jnp.tile(g('fox_g_kn'), H_C)), b_f=_row(g('fox_b_f'), LANES),
        ones128=jnp.ones((LANES, LANES), BF16), eye128=jnp.eye(LANES, dtype=BF16),
        bd256=_block_diag_ones(D_C, HEAD_DIM), bd384=_block_diag_ones(D_B, HEAD_DIM),
        w_uk=w_uk, w_uv=w_uv, g_kn=_row(g('mla_g_kn')[:NOPE], LANES), g_kr=_row(g('mla_g_kn')[NOPE:], LANES, NOPE),
        rw_mu=_row(g('rw_mu')), rw_w0=_row(g('rw_w0')), rw_a0=_row(g('rw_a0')),
        rw_ww=jnp.concatenate([g('rw_w_up'), zeros_w], axis=0).astype(BF16),
        rw_wa=jnp.concatenate([zeros_w, g('rw_a_up')], axis=0).astype(BF16),
        rw_wg=g('rw_g_up').astype(BF16), rw_k_k=_row(g('rw_k_k')), rw_k_a=_row(g('rw_k_a')),
        rw_r_k=_row(g('rw_r_k')), rw_ln_w=_row(g('rw_ln_w')), rw_ln_b=_row(g('rw_ln_b')),
        w_out=g('w_out').astype(BF16), g_ffn=_row(g('g_ffn')), w_r=w_r, b_r=b_r,
        w_g=g('moe_w_gate').astype(BF16), w_u=g('moe_w_up').astype(BF16),
        w_d=g('moe_w_down').astype(BF16).reshape(N_EXPERTS * D_FF_E, D_MODEL),
    )


def _rope_tables(pos):
    half = ROPE // 2
    inv = ROPE_BASE ** (-jnp.arange(half, dtype=F32) / half)
    ang = pos.astype(F32)[:, None] * inv[None, :]
    cos, sin = jnp.cos(ang), jnp.sin(ang)
    t = pos.shape[0]
    z = lambda w: jnp.zeros((t, w), F32)
    c = jnp.concatenate([jnp.ones((t, NOPE), F32), cos, cos, z(LANES - QK_DIM)], axis=1)
    s_left = jnp.concatenate([z(NOPE), -sin, z(half), z(LANES - QK_DIM)], axis=1)
    s_right = jnp.concatenate([z(NOPE), z(half), sin, z(LANES - QK_DIM)], axis=1)
    return c, s_left, s_right


def _pick(n, prefs):
    for t in prefs:
        if n % t == 0:
            return t
    return n


def _layer(x, lw, hist, stacks):
    b, t, _ = x.shape
    n = b * t
    past = 0 if hist is None else hist['past']
    layer = 0 if hist is None else hist['layer']
    x2d = x.reshape(n, D_MODEL)
    tm = _pick(t, (512, 256, 128, 64))
    tmp = _pick(past, (512, 256, 128, 64)) if past else 0
    tqa = _pick(t, (ATTN_TQ, 512, 256, 128, 64))
    tkp = _pick(past, (ATTN_TKP, 1024, 512, 256, 128, 64)) if past else 0
    eye = lw['eye128']

    q_tabs = _rope_tables(past + jnp.arange(t))
    outs = _inproj(x2d, lw, q_tabs, b, t, tm, stacks)
    stacks, (qt, pb, qct, kcb, vct, k_new, vt_new) = outs[:5], outs[5:]
    lf = stacks[4][-1]

    if hist is None:
        kp = vpt = None
    else:
        kp, vpt = _kvprep(hist['ckv'], hist['krope'], layer, lw, _rope_tables(jnp.arange(past)), b, past, tmp)
        kp = kp.reshape(b, past, -1)
    o_a = _attention(qt, k_new.reshape(b, t, -1), vt_new, kp, vpt, None, eye, n_pairs=H_A // 2, shared=False,
                     frame_causal=False, tq=tm, tk=min(tm, ATTN_TK), tkp=tkp)

    lf3 = lf.reshape(b, t, H_C)
    lf_all = lf3 if hist is None else jnp.concatenate([hist['flogf'].astype(F32), lf3], axis=1)
    ltot = past + t
    lpad = -(-ltot // LANES) * LANES
    lf_t = jnp.pad(jnp.swapaxes(lf_all, 1, 2), ((0, 0), (0, 0), (0, lpad - ltot))).reshape(b * H_C, lpad)
    c_all = _cumsum_lanes(lf_t).reshape(b, H_C // 2, 2, lpad)
    cq = c_all[..., past:past + t]
    ck = jnp.swapaxes(c_all.reshape(b, H_C, lpad), 1, 2)
    if hist is None:
        kcp = vcpt = None
    else:
        kcp = hist['fk']
        vcpt = _transpose_cast(hist['fv'], layer, lw, b, past, tmp)
    o_c = _attention(qct, kcb.reshape(b, t, D_C), vct, kcp, vcpt, (cq, ck), eye, n_pairs=H_C // 2, shared=True,
                     frame_causal=True, tq=tqa, tk=min(tqa, ATTN_TK), tkp=tkp, past_layer=layer)

    prev = jnp.zeros((b, 1, B_IN), F32) if hist is None else hist['shift'].astype(F32)
    pb3 = pb.reshape(b, t, B_IN)
    rt, yl, gm, hm, r, k, v, g = _rwkv_chunk(pb3, prev, lw, _pick(t, (256, 128, 64)))
    npair = H_B // 2
    if hist is None:
        s0 = jnp.zeros((b, npair, LANES, LANES), F32)
    else:
        st = jnp.swapaxes(hist['wkv'].astype(F32), -1, -2).reshape(b, npair, 2, HEAD_DIM, HEAD_DIM)
        s0 = jnp.zeros((b, npair, 2, HEAD_DIM, 2, HEAD_DIM), F32)
        s0 = s0.at[:, :, 0, :, 0, :].set(st[:, :, 0]).at[:, :, 1, :, 1, :].set(st[:, :, 1])
        s0 = s0.reshape(b, npair, LANES, LANES)
    o_b, sfin = _rwkv_scan(s0, gm, hm, rt, yl, r, k, v, g, lw, _pick(t // CHUNK, (8, 4, 2, 1)))
    sf = sfin.reshape(b, npair, 2, HEAD_DIM, 2, HEAD_DIM)
    s_fin = jnp.stack([sf[:, :, 0, :, 0, :], sf[:, :, 1, :, 1, :]], axis=2).reshape(b, H_B, HEAD_DIM, HEAD_DIM)
    s_fin = jnp.swapaxes(s_fin, -1, -2)

    tmm = _pick(n, (1024, 512, 256, 128, 64))
    x_out = _outproj_moe(x2d, o_a.reshape(n, -1), o_b.reshape(n, D_B), o_c.reshape(n, -1), lw, tmm)

    return x_out.reshape(b, t, D_MODEL), stacks, (s_fin, pb3[:, -1:])


def _cache_outputs(stacks, b, t):
    ckv, kr, kc, vc, lf = stacks
    d = ckv.shape[0]
    return (ckv.reshape(d, b, t, KV_LORA), kr.reshape(d, b, t, ROPE), kc.reshape(d, b, t, H_C, HEAD_DIM),
            vc.reshape(d, b, t, H_C, HEAD_DIM), lf.reshape(d, b, t, H_C))


def kernel(x_prompt, x_sample, cache_mla_latent, cache_mla_krope, cache_fox_k, cache_fox_v, cache_fox_logf,
           state_rwkv_wkv, state_rwkv_shift, g_mix, w_in, mla_g_qa, mla_w_uq, mla_g_kva, mla_w_ukv, mla_g_qn,
           mla_g_kn, rw_mu, rw_w0, rw_w_up, rw_a0, rw_a_up, rw_g_up, rw_k_k, rw_k_a, rw_r_k, rw_ln_w, rw_ln_b,
           fox_g_qn, fox_g_kn, fox_b_f, w_out, g_ffn, moe_w_rg, moe_b_rg, moe_w_re, moe_b_re, moe_w_gate,
           moe_w_up, moe_w_down):
    params = dict(g_mix=g_mix, w_in=w_in, mla_g_qa=mla_g_qa, mla_w_uq=mla_w_uq, mla_g_kva=mla_g_kva,
                  mla_w_ukv=mla_w_ukv, mla_g_qn=mla_g_qn, mla_g_kn=mla_g_kn, rw_mu=rw_mu, rw_w0=rw_w0,
                  rw_w_up=rw_w_up, rw_a0=rw_a0, rw_a_up=rw_a_up, rw_g_up=rw_g_up, rw_k_k=rw_k_k, rw_k_a=rw_k_a,
                  rw_r_k=rw_r_k, rw_ln_w=rw_ln_w, rw_ln_b=rw_ln_b, fox_g_qn=fox_g_qn, fox_g_kn=fox_g_kn,
                  fox_b_f=fox_b_f, w_out=w_out, g_ffn=g_ffn, moe_w_rg=moe_w_rg, moe_b_rg=moe_b_rg,
                  moe_w_re=moe_w_re, moe_b_re=moe_b_re, moe_w_gate=moe_w_gate, moe_w_up=moe_w_up,
                  moe_w_down=moe_w_down)
    depth = g_mix.shape[0]
    yp, ys = x_prompt, x_sample
    p_stacks = s_stacks = None
    p_small, s_small = [], []
    db, past = cache_mla_latent.shape[1], cache_mla_latent.shape[2]
    ckv_all = cache_mla_latent.reshape(depth, db * past, KV_LORA)
    kr_all = jnp.pad(cache_mla_krope.reshape(depth, db * past, ROPE), ((0, 0), (0, 0), (NOPE, LANES - QK_DIM)))
    fk_all = cache_fox_k.reshape(depth, db, past, D_C)
    fv_all = cache_fox_v.reshape(depth, db * past, D_C)
    for l in range(depth):
        lw = _layer_weights(params, l)
        hist = dict(past=past, layer=l, ckv=ckv_all, krope=kr_all, fk=fk_all, fv=fv_all,
                    flogf=cache_fox_logf[l], wkv=state_rwkv_wkv[l], shift=state_rwkv_shift[l])
        yp, p_stacks, sm = _layer(yp, lw, None, p_stacks)
        p_small.append(sm)
        ys, s_stacks, sm = _layer(ys, lw, hist, s_stacks)
        s_small.append(sm)
    p_out = _cache_outputs(p_stacks, *x_prompt.shape[:2]) + tuple(jnp.stack(t) for t in zip(*p_small))
    s_out = _cache_outputs(s_stacks, *x_sample.shape[:2]) + tuple(jnp.stack(t) for t in zip(*s_small))
    return (yp, ys) + p_out + s_out
```
